```python
import jax, jax.numpy as jnp
from jax import lax
import numpy as np

D_MODEL = 1024
BATCH = 8
SEQ = 4096
DEPTH = 2

N_HEADS = 16
HEAD_DIM = D_MODEL // N_HEADS
MOBA_HEADS = 4
NSA_HEADS = 6
SB_HEADS = N_HEADS - MOBA_HEADS - NSA_HEADS
MOBA_W = MOBA_HEADS * HEAD_DIM
NSA_W = NSA_HEADS * HEAD_DIM
SB_W = SB_HEADS * HEAD_DIM
ROPE_DIM = HEAD_DIM // 4
ROPE_THETA = 500000.0

MOBA_BLOCK = 256
MOBA_TOPK = 3
MOBA_CHUNK = 32

NSA_CMP_LEN = 32
NSA_CMP_STRIDE = 16
NSA_SLC_BLOCK = 64
NSA_SLC_TOPK = 16
NSA_WINDOW = 512
NSA_CHUNK = 64
NSA_N_KV = 6

SB_CHUNK = 128

N_EXPERTS = 32
TOP_K = 4
D_EXPERT = D_MODEL
SWIGLU_LIMIT = 7.0
SWIGLU_ALPHA = 1.702
MOE_BLOCK = 128

RMS_EPS = 1e-6
NEG_INF = -1e30
SEL_FORCE = 1e4

IN_WIDTHS = (MOBA_W, MOBA_W, MOBA_W, NSA_W, NSA_N_KV * HEAD_DIM, 3 * NSA_HEADS, SB_W, SB_W, SB_W)
IN_W = sum(IN_WIDTHS)

kernel_name = 'hybrid_moba_nsa_stickbreak_moe_layer'


def rms_norm(x, g):
    xf = x.astype(jnp.float32)
    y = xf * lax.rsqrt(jnp.mean(xf * xf, axis=-1, keepdims=True) + RMS_EPS)
    return (y * g.astype(jnp.float32)).astype(x.dtype)


def rope_tables(seq):
    inv_freq = ROPE_THETA ** (-jnp.arange(0, ROPE_DIM, 2, dtype=jnp.float32) / ROPE_DIM)
    ang = jnp.arange(seq, dtype=jnp.float32)[:, None] * inv_freq[None, :]
    return jnp.cos(ang), jnp.sin(ang)


def partial_rope(x, cos, sin):
    half = ROPE_DIM // 2
    xf = x.astype(jnp.float32)
    x1, x2, rest = xf[..., :half], xf[..., half:ROPE_DIM], xf[..., ROPE_DIM:]
    c = cos[None, :, None, :]
    s = sin[None, :, None, :]
    return jnp.concatenate([x1 * c - x2 * s, x2 * c + x1 * s, rest], axis=-1).astype(x.dtype)


def masked_softmax(s, mask):
    s = jnp.where(mask, s, NEG_INF)
    m = jnp.max(s, axis=-1, keepdims=True)
    e = jnp.where(mask, jnp.exp(s - m), 0.0)
    return e / jnp.maximum(jnp.sum(e, axis=-1, keepdims=True), 1e-30)


def moba_attention(q, k, v):
    b, h, s, dh = q.shape
    nb = -(-s // MOBA_BLOCK)
    pad = nb * MOBA_BLOCK - s
    kb = jnp.pad(k, ((0, 0), (0, 0), (0, pad), (0, 0))).reshape(b, h, nb, MOBA_BLOCK, dh)
    vb = jnp.pad(v, ((0, 0), (0, 0), (0, pad), (0, 0))).reshape(b, h, nb, MOBA_BLOCK, dh)
    kmean = jnp.mean(kb.astype(jnp.float32), axis=3)
    n_sel = min(MOBA_TOPK, nb - 1)
    scale = dh ** -0.5
    bi = jnp.arange(b)[:, None, None, None]
    hi = jnp.arange(h)[None, :, None, None]
    blk = jnp.arange(nb)
    koff = jnp.arange(MOBA_BLOCK)

    def chunk(ci):
        t0 = ci * MOBA_CHUNK
        tq = t0 + jnp.arange(MOBA_CHUNK)
        qc = lax.dynamic_slice_in_dim(q, t0, MOBA_CHUNK, axis=2)
        own = t0 // MOBA_BLOCK
        ko = lax.dynamic_index_in_dim(kb, own, axis=2, keepdims=False)
        vo = lax.dynamic_index_in_dim(vb, own, axis=2, keepdims=False)
        s_own = jnp.einsum('bhqd,bhkd->bhqk', qc, ko, preferred_element_type=jnp.float32) * scale
        m_own = jnp.broadcast_to((own * MOBA_BLOCK + koff)[None, :] <= tq[:, None], s_own.shape)
        if n_sel == 0:
            p = masked_softmax(s_own, m_own)
            return jnp.einsum('bhqk,bhkd->bhqd', p.astype(v.dtype), vo)
        gate = jnp.einsum('bhqd,bhnd->bhqn', qc.astype(jnp.float32), kmean)
        gate = jnp.where(blk < own, gate, NEG_INF)
        g_val, g_idx = lax.top_k(gate, n_sel)
        ks = kb[bi, hi, g_idx]
        vs = vb[bi, hi, g_idx]
        s_sel = jnp.einsum('bhqd,bhqnkd->bhqnk', qc, ks, preferred_element_type=jnp.float32) * scale
        m_sel = jnp.broadcast_to((g_val > 0.5 * NEG_INF)[..., None], s_sel.shape)
        cl = n_sel * MOBA_BLOCK
        p = masked_softmax(
            jnp.concatenate([s_sel.reshape(b, h, MOBA_CHUNK, cl), s_own], axis=-1),
            jnp.concatenate([m_sel.reshape(b, h, MOBA_CHUNK, cl), m_own], axis=-1)).astype(v.dtype)
        o_sel = jnp.einsum('bhqnk,bhqnkd->bhqd', p[..., :cl].reshape(b, h, MOBA_CHUNK, n_sel, MOBA_BLOCK), vs)
        return o_sel + jnp.einsum('bhqk,bhkd->bhqd', p[..., cl:], vo)

    out = lax.map(chunk, jnp.arange(s // MOBA_CHUNK))
    return out.transpose(1, 2, 0, 3, 4).reshape(b, h, s, dh)


def nsa_attention(q, k_cmp, v_cmp, k_slc, v_slc, k_win, v_win, gates, cmp_pe, cmp_w1, cmp_w2):
    b, s, hq, dh = q.shape
    scale = dh ** -0.5
    n_cmp = (s - NSA_CMP_LEN) // NSA_CMP_STRIDE + 1
    cstart = jnp.arange(n_cmp) * NSA_CMP_STRIDE
    cend = cstart + NSA_CMP_LEN - 1
    cidx = cstart[:, None] + jnp.arange(NSA_CMP_LEN)[None, :]

    def compress(t, i):
        blocks = (t[:, cidx] + cmp_pe[i]).reshape(b, n_cmp, NSA_CMP_LEN * dh)
        return jax.nn.silu(blocks @ cmp_w1[i]) @ cmp_w2[i]

    kc = compress(k_cmp, 0)
    vc = compress(v_cmp, 1)
    n_slc = s // NSA_SLC_BLOCK
    n_sel = min(NSA_SLC_TOPK, n_slc)
    sstart = jnp.arange(n_slc) * NSA_SLC_BLOCK
    overlap = ((cstart[:, None] <= sstart[None, :] + NSA_SLC_BLOCK - 1)
               & (cend[:, None] >= sstart[None, :])).astype(jnp.float32)
    ks_b = k_slc.reshape(b, n_slc, NSA_SLC_BLOCK, dh)
    vs_b = v_slc.reshape(b, n_slc, NSA_SLC_BLOCK, dh)
    kw_p = jnp.pad(k_win, ((0, 0), (NSA_WINDOW, 0), (0, 0)))
    vw_p = jnp.pad(v_win, ((0, 0), (NSA_WINDOW, 0), (0, 0)))
    bi = jnp.arange(b)[:, None, None]
    slc_off = jnp.arange(NSA_SLC_BLOCK)
    win_off = jnp.arange(NSA_WINDOW + NSA_CHUNK)
    blk = jnp.arange(n_slc)
    ls = n_sel * NSA_SLC_BLOCK

    def chunk(ci):
        t0 = ci * NSA_CHUNK
        tq = t0 + jnp.arange(NSA_CHUNK)
        qc = lax.dynamic_slice_in_dim(q, t0, NSA_CHUNK, axis=1)
        gc = lax.dynamic_slice_in_dim(gates, t0, NSA_CHUNK, axis=1)
        s_c = jnp.einsum('bqhd,bnd->bhqn', qc, kc, preferred_element_type=jnp.float32) * scale
        p_c = masked_softmax(s_c, cend[None, :] <= tq[:, None])
        o_c = jnp.einsum('bhqn,bnd->bqhd', p_c.astype(vc.dtype), vc)
        imp = jnp.einsum('bhqn,nm->bqm', p_c, overlap)
        own_b = tq // NSA_SLC_BLOCK
        forced = (blk[None, :] == 0) | (blk[None, :] == own_b[:, None]) | (blk[None, :] == own_b[:, None] - 1)
        imp = jnp.where(forced, SEL_FORCE, imp)
        imp = jnp.where(sstart[None, :] <= tq[:, None], imp, NEG_INF)
        sel_val, sel_idx = lax.top_k(imp, n_sel)
        ks = ks_b[bi, sel_idx]
        vs = vs_b[bi, sel_idx]
        kpos = sel_idx[..., None] * NSA_SLC_BLOCK + slc_off
        m_s = (sel_val[..., None] > 0.5 * NEG_INF) & (kpos <= tq[None, :, None, None])
        s_s = jnp.einsum('bqhd,bqnkd->bhqnk', qc, ks, preferred_element_type=jnp.float32) * scale
        p_s = masked_softmax(s_s.reshape(b, hq, NSA_CHUNK, ls), m_s.reshape(b, 1, NSA_CHUNK, ls))
        o_s = jnp.einsum('bhqnk,bqnkd->bqhd',
                         p_s.reshape(b, hq, NSA_CHUNK, n_sel, NSA_SLC_BLOCK).astype(vs.dtype), vs)
        kw = lax.dynamic_slice_in_dim(kw_p, t0, NSA_WINDOW + NSA_CHUNK, axis=1)
        vw = lax.dynamic_slice_in_dim(vw_p, t0, NSA_WINDOW + NSA_CHUNK, axis=1)
        wpos = t0 - NSA_WINDOW + win_off
        m_w = ((wpos[None, :] <= tq[:, None]) & (wpos[None, :] > tq[:, None] - NSA_WINDOW)
               & (wpos[None, :] >= 0))
        s_w = jnp.einsum('bqhd,bkd->bhqk', qc, kw, preferred_element_type=jnp.float32) * scale
        p_w = masked_softmax(s_w, m_w)
        o_w = jnp.einsum('bhqk,bkd->bqhd', p_w.astype(vw.dtype), vw)
        return gc[..., 0:1] * o_c + gc[..., 1:2] * o_s + gc[..., 2:3] * o_w

    out = lax.map(chunk, jnp.arange(s // NSA_CHUNK))
    return out.transpose(1, 0, 2, 3, 4).reshape(b, s, hq, dh)


def stick_breaking_attention(q, k, v):
    b, h, s, dh = q.shape
    scale = dh ** -0.5
    kpos = jnp.arange(s)

    def chunk(ci):
        t0 = ci * SB_CHUNK
        tq = t0 + jnp.arange(SB_CHUNK)
        qc = lax.dynamic_slice_in_dim(q, t0, SB_CHUNK, axis=2)
        z = jnp.einsum('bhqd,bhkd->bhqk', qc, k, preferred_element_type=jnp.float32) * scale
        mask = kpos[None, :] < tq[:, None]
        log_keep = jnp.where(mask, jax.nn.log_sigmoid(-z), 0.0)
        later = lax.cumsum(log_keep, axis=3, reverse=True) - log_keep
        a = jnp.where(mask, jnp.exp(jax.nn.log_sigmoid(z) + later), 0.0)
        return jnp.einsum('bhqk,bhkd->bhqd', a.astype(v.dtype), v)

    out = lax.map(chunk, jnp.arange(s // SB_CHUNK))
    return out.transpose(1, 2, 0, 3, 4).reshape(b, h, s, dh)


def token_mixer(h, w_in, cmp_pe, cmp_w1, cmp_w2, grp_g, w_out, cos, sin):
    b, s, _ = h.shape
    proj = h @ w_in
    offs = [int(o) for o in np.cumsum(IN_WIDTHS)[:-1]]
    qa, ka, va, qb, kvb, gb, qc, kc, vc = jnp.split(proj, offs, axis=-1)
    hd = lambda t, n: t.reshape(b, s, n, HEAD_DIM)
    bhsd = lambda t: t.transpose(0, 2, 1, 3)
    oa = moba_attention(bhsd(partial_rope(hd(qa, MOBA_HEADS), cos, sin)),
                        bhsd(partial_rope(hd(ka, MOBA_HEADS), cos, sin)),
                        bhsd(hd(va, MOBA_HEADS)))
    oa = bhsd(oa).reshape(b, s, MOBA_W)
    kvb = hd(kvb, NSA_N_KV)
    keys = partial_rope(kvb[:, :, 0::2], cos, sin)
    vals = kvb[:, :, 1::2]
    gates = jax.nn.sigmoid(gb).reshape(b, s, NSA_HEADS, 3)
    ob = nsa_attention(partial_rope(hd(qb, NSA_HEADS), cos, sin),
                       keys[:, :, 0], vals[:, :, 0], keys[:, :, 1], vals[:, :, 1],
                       keys[:, :, 2], vals[:, :, 2], gates, cmp_pe, cmp_w1, cmp_w2)
    ob = ob.reshape(b, s, NSA_W)
    oc = stick_breaking_attention(bhsd(hd(qc, SB_HEADS)), bhsd(hd(kc, SB_HEADS)), bhsd(hd(vc, SB_HEADS)))
    oc = bhsd(oc).reshape(b, s, SB_W)
    g_a, g_b, g_c = jnp.split(grp_g, [MOBA_W, MOBA_W + NSA_W])
    y = jnp.concatenate([rms_norm(oa, g_a), rms_norm(ob, g_b), rms_norm(oc, g_c)], axis=-1)
    return y @ w_out


def clamped_swiglu(hgu):
    x_glu, x_lin = jnp.split(hgu, 2, axis=-1)
    x_glu = jnp.minimum(x_glu, SWIGLU_LIMIT)
    x_lin = jnp.clip(x_lin, -SWIGLU_LIMIT, SWIGLU_LIMIT)
    return x_glu * jax.nn.sigmoid(SWIGLU_ALPHA * x_glu) * (x_lin + 1.0)


def moe_ffn(h, router_w, router_b, w_gu, b_gu, w_dn, b_dn):
    b, s, d = h.shape
    n = b * s
    xt = h.reshape(n, d)
    logits = jnp.dot(xt, router_w, preferred_element_type=jnp.float32) + router_b.astype(jnp.float32)
    top_val, top_idx = lax.top_k(logits, TOP_K)
    gate = jax.nn.softmax(top_val, axis=-1)
    e = top_idx.reshape(-1)
    tok = jnp.arange(n * TOP_K, dtype=jnp.int32) // TOP_K
    onehot = (e[:, None] == jnp.arange(N_EXPERTS, dtype=e.dtype)[None, :]).astype(jnp.int32)
    rank = jnp.take_along_axis(jnp.cumsum(onehot, axis=0), e[:, None], axis=1)[:, 0] - 1
    counts = jnp.sum(onehot, axis=0)
    padded = (counts + MOE_BLOCK - 1) // MOE_BLOCK * MOE_BLOCK
    pad_end = jnp.cumsum(padded)
    dest = (pad_end - padded)[e] + rank
    n_rows = -(-(n * TOP_K + N_EXPERTS * (MOE_BLOCK - 1)) // MOE_BLOCK) * MOE_BLOCK
    n_blk = n_rows // MOE_BLOCK
    row_tok = jnp.zeros((n_rows,), jnp.int32).at[dest].set(tok)
    blk_exp = jnp.minimum(jnp.searchsorted(pad_end, jnp.arange(n_blk, dtype=jnp.int32) * MOE_BLOCK,
                                           side='right'), N_EXPERTS - 1)

    def expert_block(args):
        rows, eid = args
        xb = xt[rows]
        hgu = xb @ w_gu[eid] + b_gu[eid]
        return clamped_swiglu(hgu) @ w_dn[eid] + b_dn[eid]

    y_rows = lax.map(expert_block, (row_tok.reshape(n_blk, MOE_BLOCK), blk_exp)).reshape(n_rows, d)
    y = jnp.sum(y_rows[dest].reshape(n, TOP_K, d) * gate[..., None].astype(h.dtype), axis=1)
    return y.reshape(b, s, d)


def setup_inputs(seed: int = 0) -> dict:
    key = jax.random.key(seed)
    ks = jax.random.split(key, 17)
    f32 = jnp.float32
    nrm = lambda k, shape, sc: jax.random.normal(k, shape, f32) * sc
    L = DEPTH
    D = D_MODEL
    return {
        'x': nrm(ks[0], (BATCH, SEQ, D), 1.0),
        'c': nrm(ks[1], (BATCH, D), 1.0),
        'ada_w': nrm(ks[2], (L, D, 6 * D), 0.5 * D ** -0.5),
        'ada_b': nrm(ks[3], (L, 6 * D), 0.02),
        'norm_g': 1.0 + nrm(ks[4], (L, 4, D), 0.05),
        'w_in': nrm(ks[5], (L, D, IN_W), D ** -0.5),
        'nsa_cmp_pe': nrm(ks[6], (L, 2, NSA_CMP_LEN, HEAD_DIM), 0.02),
        'nsa_cmp_w1': nrm(ks[7], (L, 2, NSA_CMP_LEN * HEAD_DIM, HEAD_DIM), (NSA_CMP_LEN * HEAD_DIM) ** -0.5),
        'nsa_cmp_w2': nrm(ks[8], (L, 2, HEAD_DIM, HEAD_DIM), HEAD_DIM ** -0.5),
        'mix_out_g': 1.0 + nrm(ks[9], (L, D), 0.05),
        'w_out': nrm(ks[10], (L, D, D), D ** -0.5),
        'router_w': nrm(ks[11], (L, D, N_EXPERTS), D ** -0.5),
        'router_b': nrm(ks[12], (L, N_EXPERTS), 0.01),
        'exp_w_gu': nrm(ks[13], (L, N_EXPERTS, D, 2 * D_EXPERT), D ** -0.5),
        'exp_b_gu': nrm(ks[14], (L, N_EXPERTS, 2 * D_EXPERT), 0.01),
        'exp_w_dn': nrm(ks[15], (L, N_EXPERTS, D_EXPERT, D), D_EXPERT ** -0.5),
        'exp_b_dn': nrm(ks[16], (L, N_EXPERTS, D), 0.01),
    }


def reference(x, c, ada_w, ada_b, norm_g, w_in, nsa_cmp_pe, nsa_cmp_w1, nsa_cmp_w2, mix_out_g,
              w_out, router_w, router_b, exp_w_gu, exp_b_gu, exp_w_dn, exp_b_dn):
    cos, sin = rope_tables(x.shape[1])
    c_act = jax.nn.silu(c)
    for l in range(DEPTH):
        mod = c_act @ ada_w[l] + ada_b[l]
        sh1, sc1, g1, sh2, sc2, g2 = (m[:, None, :] for m in jnp.split(mod, 6, axis=-1))
        hm = rms_norm(x, norm_g[l, 0]) * (1.0 + sc1) + sh1
        y = token_mixer(hm, w_in[l], nsa_cmp_pe[l], nsa_cmp_w1[l], nsa_cmp_w2[l], mix_out_g[l],
                        w_out[l], cos, sin)
        x = x + g1 * rms_norm(y, norm_g[l, 1])
        hf = rms_norm(x, norm_g[l, 2]) * (1.0 + sc2) + sh2
        y = moe_ffn(hf, router_w[l], router_b[l], exp_w_gu[l], exp_b_gu[l], exp_w_dn[l], exp_b_dn[l])
        x = x + g2 * rms_norm(y, norm_g[l, 3])
    return x
```

```python
import functools

import numpy as np
import jax
import jax.numpy as jnp
from jax import lax
from jax.experimental import pallas as pl
from jax.experimental.pallas import tpu as pltpu

F32 = jnp.float32
BF16 = jnp.bfloat16
HI = lax.Precision.HIGHEST

D_MODEL = 1024
N_HEADS = 16
HEAD_DIM = 64
MOBA_HEADS = 4
NSA_HEADS = 6
SB_HEADS = 6
MOBA_W = MOBA_HEADS * HEAD_DIM
NSA_W = NSA_HEADS * HEAD_DIM
SB_W = SB_HEADS * HEAD_DIM
ROPE_DIM = 16
ROPE_THETA = 500000.0
MOBA_BLOCK = 256
MOBA_TOPK = 3
NSA_CMP_LEN = 32
NSA_CMP_STRIDE = 16
NSA_SLC_BLOCK = 64
NSA_SLC_TOPK = 16
NSA_WINDOW = 512
N_EXPERTS = 32
TOP_K = 4
SWIGLU_LIMIT = 7.0
SWIGLU_ALPHA = 1.702
RMS_EPS = 1e-6
NEG_INF = -1e30
SEL_FORCE = 1e4
SCALE = HEAD_DIM ** -0.5

LANES = 128
VMEM_LIMIT = 48 * 1024 * 1024


def _cparams(sem):
    return pltpu.CompilerParams(dimension_semantics=sem, vmem_limit_bytes=VMEM_LIMIT)


def _dot(a, b):
    return jnp.dot(a, b, preferred_element_type=F32)


def _dot_nt(a, b):
    return lax.dot_general(a, b, (((1,), (1,)), ((), ())), preferred_element_type=F32)


SB_TILE = 256


def _sb_kernel(q_ref, k_ref, v_ref, o_ref, ks_ref, vs_ref):
    T = SB_TILE
    qi = pl.program_id(2)

    @pl.when(qi == 0)
    def _():
        for h in range(2):
            ks_ref[h] = k_ref[0, :, h * HEAD_DIM:(h + 1) * HEAD_DIM].astype(BF16)
            vs_ref[h] = v_ref[0, :, h * HEAD_DIM:(h + 1) * HEAD_DIM].astype(BF16)

    row = lax.broadcasted_iota(jnp.int32, (T, T), 0)
    col = lax.broadcasted_iota(jnp.int32, (T, T), 1)
    upper = (row > col).astype(BF16)
    diag_mask = col < row

    outs = []
    for h in range(2):
        q = (q_ref[0, :, h * HEAD_DIM:(h + 1) * HEAD_DIM] * SCALE).astype(BF16)

        def tile(j, carry, masked):
            run, acc = carry
            start = pl.multiple_of(j * T, T)
            kt = ks_ref[h, pl.ds(start, T), :]
            vt = vs_ref[h, pl.ds(start, T), :]
            z = _dot_nt(q, kt)
            lk = -(jnp.maximum(z, 0.0) + jnp.log1p(jnp.exp(-jnp.abs(z))))
            if masked:
                lk = jnp.where(diag_mask, lk, 0.0)
            hi = lk.astype(BF16)
            lo = (lk - hi.astype(F32)).astype(BF16)
            later = _dot(hi, upper) + _dot(lo, upper)
            a = jnp.exp(z + lk + later + run)
            if masked:
                a = jnp.where(diag_mask, a, 0.0)
            acc = acc + _dot(a.astype(BF16), vt)
            run = run + later[:, 0:1] + lk[:, 0:1]
            return run, acc

        carry = (jnp.zeros((T, 1), F32), jnp.zeros((T, HEAD_DIM), F32))
        carry = tile(qi, carry, True)
        carry = lax.fori_loop(0, qi, lambda i, c: tile(qi - 1 - i, c, False), carry)
        outs.append(carry[1])
    o_ref[0] = jnp.concatenate(outs, axis=1)


def sb_attention(q, k, v):
    b, s, w = q.shape
    T = SB_TILE
    qspec = pl.BlockSpec((1, T, LANES), lambda bi, p, i: (bi, i, p))
    kvspec = pl.BlockSpec((1, s, LANES), lambda bi, p, i: (bi, 0, p))
    return pl.pallas_call(
        _sb_kernel,
        grid=(b, w // LANES, s // T),
        in_specs=[qspec, kvspec, kvspec],
        out_specs=qspec,
        out_shape=jax.ShapeDtypeStruct((b, s, w), F32),
        scratch_shapes=[pltpu.VMEM((2, s, HEAD_DIM), BF16), pltpu.VMEM((2, s, HEAD_DIM), BF16)],
        compiler_params=_cparams(("arbitrary", "arbitrary", "arbitrary")),
        name="sb_attention",
    )(q, k, v)


def _rank_before(vals, n):
    idx = lax.broadcasted_iota(jnp.int32, vals.shape, 1)
    rank = jnp.zeros(vals.shape, F32)
    for j2 in range(n):
        other = vals[:, j2:j2 + 1]
        ahead = (other > vals) | ((other == vals) & (idx > j2))
        rank = rank + jnp.where(ahead, 1.0, 0.0)
    return rank


def _moba_kernel(q_ref, k_ref, v_ref, o_ref, ks_ref, vs_ref, km_ref, bias_ref):
    T = MOBA_BLOCK
    nb = km_ref.shape[1]
    qi = pl.program_id(2)

    @pl.when(qi == 0)
    def _():
        for h in range(2):
            kh = k_ref[0, :, h * HEAD_DIM:(h + 1) * HEAD_DIM]
            ks_ref[h] = kh.astype(BF16)
            vs_ref[h] = v_ref[0, :, h * HEAD_DIM:(h + 1) * HEAD_DIM].astype(BF16)
            km_ref[h] = jnp.mean(kh.reshape(nb, T, HEAD_DIM), axis=1)

    row = lax.broadcasted_iota(jnp.int32, (T, T), 0)
    col = lax.broadcasted_iota(jnp.int32, (T, T), 1)
    causal = col <= row
    blk = lax.broadcasted_iota(jnp.int32, (T, nb), 1)

    outs = []
    for h in range(2):
        qf = q_ref[0, :, h * HEAD_DIM:(h + 1) * HEAD_DIM]
        q = (qf * SCALE).astype(BF16)
        gate = lax.dot_general(qf, km_ref[h], (((1,), (1,)), ((), ())),
                               precision=HI, preferred_element_type=F32)
        gate = jnp.where(blk < qi, gate, NEG_INF)
        sel = (_rank_before(gate, nb) < float(MOBA_TOPK)) & (gate > 0.5 * NEG_INF)
        selb = jnp.where(sel, 0.0, NEG_INF)
        for j in range(nb):
            bias_ref[j] = jnp.broadcast_to(selb[:, j:j + 1], (T, LANES))

        start = pl.multiple_of(qi * T, T)
        s = _dot_nt(q, ks_ref[h, pl.ds(start, T), :])
        s = jnp.where(causal, s, NEG_INF)
        m = jnp.max(s, axis=1, keepdims=True)
        p = jnp.where(causal, jnp.exp(s - m), 0.0)
        l = jnp.sum(p, axis=1, keepdims=True)
        acc = _dot(p.astype(BF16), vs_ref[h, pl.ds(start, T), :])

        def past(j, carry):
            m, l, acc = carry
            st = pl.multiple_of(j * T, T)
            b = bias_ref[j]
            s = _dot_nt(q, ks_ref[h, pl.ds(st, T), :]) + jnp.concatenate([b, b], axis=1)
            m_new = jnp.maximum(m, jnp.max(s, axis=1, keepdims=True))
            alpha = jnp.exp(m - m_new)
            p = jnp.exp(s - m_new)
            l = alpha * l + jnp.sum(p, axis=1, keepdims=True)
            acc = alpha * acc + _dot(p.astype(BF16), vs_ref[h, pl.ds(st, T), :])
            return m_new, l, acc

        m, l, acc = lax.fori_loop(0, qi, past, (m, l, acc))
        outs.append(acc / jnp.maximum(l, 1e-30))
    o_ref[0] = jnp.concatenate(outs, axis=1)


def moba_attention(q, k, v):
    b, s, w = q.shape
    T = MOBA_BLOCK
    nb = s // T
    qspec = pl.BlockSpec((1, T, LANES), lambda bi, p, i: (bi, i, p))
    kvspec = pl.BlockSpec((1, s, LANES), lambda bi, p, i: (bi, 0, p))
    return pl.pallas_call(
        _moba_kernel,
        grid=(b, w // LANES, nb),
        in_specs=[qspec, kvspec, kvspec],
        out_specs=qspec,
        out_shape=jax.ShapeDtypeStruct((b, s, w), F32),
        scratch_shapes=[pltpu.VMEM((2, s, HEAD_DIM), BF16), pltpu.VMEM((2, s, HEAD_DIM), BF16),
                        pltpu.VMEM((2, nb, HEAD_DIM), F32), pltpu.VMEM((nb, T, LANES), F32)],
        compiler_params=_cparams(("arbitrary", "arbitrary", "arbitrary")),
        name="moba_attention",
    )(q, k, v)


def _nsa_compress_kernel(x_ref, pe_ref, w1_ref, w2_ref, o_ref):
    nc = x_ref.shape[2]
    half = NSA_CMP_STRIDE * HEAD_DIM
    x = x_ref[0, 0]
    w1 = w1_ref[0]
    first = jnp.dot(x, w1[:half], precision=HI, preferred_element_type=F32)
    second = jnp.dot(x, w1[half:], precision=HI, preferred_element_type=F32)
    pe = jnp.broadcast_to(pe_ref[0], (8, 2 * half))
    peb = jnp.dot(pe, w1, precision=HI, preferred_element_type=F32)[0:1]
    pre = first + pltpu.roll(second, nc - 1, 0) + peb
    hid = pre * jax.nn.sigmoid(pre)
    o_ref[0, 0] = jnp.dot(hid, w2_ref[0], precision=HI, preferred_element_type=F32)


def nsa_compress(kv16, pe, w1, w2):
    b, _, nc, wide = kv16.shape
    return pl.pallas_call(
        _nsa_compress_kernel,
        grid=(b, 2),
        in_specs=[pl.BlockSpec((1, 1, nc, wide), lambda bi, i: (bi, i, 0, 0)),
                  pl.BlockSpec((1, 1, 2 * wide), lambda bi, i: (i, 0, 0)),
                  pl.BlockSpec((1, 2 * wide, HEAD_DIM), lambda bi, i: (i, 0, 0)),
                  pl.BlockSpec((1, HEAD_DIM, HEAD_DIM), lambda bi, i: (i, 0, 0))],
        out_specs=pl.BlockSpec((1, 1, nc, HEAD_DIM), lambda bi, i: (bi, i, 0, 0)),
        out_shape=jax.ShapeDtypeStruct((b, 2, nc, HEAD_DIM), F32),
        compiler_params=_cparams(("arbitrary", "arbitrary")),
        name="nsa_compress",
    )(kv16, pe, w1, w2)


NSA_TQ = 128
NSA_KT = 256
NSA_SPAN = NSA_WINDOW + NSA_TQ


def _softmax_rows(s, mask):
    s = jnp.where(mask, s, NEG_INF)
    m = jnp.max(s, axis=-1, keepdims=True)
    e = jnp.where(mask, jnp.exp(s - m), 0.0)
    return e / jnp.maximum(jnp.sum(e, axis=-1, keepdims=True), 1e-30)


def _nsa_kernel(q_ref, cmp_ref, slc_ref, win_ref, g_ref, ov_ref, o_ref,
                ksl_ref, vsl_ref, kw_ref, vw_ref, bias_ref):
    TQ, KT, H = NSA_TQ, NSA_KT, NSA_HEADS
    nc, ns = ov_ref.shape
    nkt = bias_ref.shape[0]
    per = KT // NSA_SLC_BLOCK
    qi = pl.program_id(1)
    t0 = qi * TQ

    @pl.when(qi == 0)
    def _():
        ksl_ref[...] = slc_ref[0, :, :HEAD_DIM].astype(BF16)
        vsl_ref[...] = slc_ref[0, :, HEAD_DIM:].astype(BF16)
        kw_ref[...] = win_ref[0, :, :HEAD_DIM].astype(BF16)
        vw_ref[...] = win_ref[0, :, HEAD_DIM:].astype(BF16)

    qf = jnp.concatenate([q_ref[0, :, h * HEAD_DIM:(h + 1) * HEAD_DIM] for h in range(H)],
                         axis=0) * SCALE
    q = qf.astype(BF16)

    tq_c = t0 + lax.broadcasted_iota(jnp.int32, (TQ, nc), 0)
    n_c = lax.broadcasted_iota(jnp.int32, (TQ, nc), 1)
    mask_c = (n_c * NSA_CMP_STRIDE + (NSA_CMP_LEN - 1) <= tq_c) & (n_c < nc - 1)
    s_c = lax.dot_general(qf, cmp_ref[0, 0], (((1,), (1,)), ((), ())),
                          precision=HI, preferred_element_type=F32)
    p_c = _softmax_rows(s_c.reshape(H, TQ, nc), mask_c[None])
    o_c = _dot(p_c.reshape(H * TQ, nc).astype(BF16), cmp_ref[0, 1].astype(BF16))

    imp = jnp.dot(jnp.sum(p_c, axis=0), ov_ref[...], precision=HI, preferred_element_type=F32)
    tq_s = t0 + lax.broadcasted_iota(jnp.int32, (TQ, ns), 0)
    blk = lax.broadcasted_iota(jnp.int32, (TQ, ns), 1)
    own = tq_s // NSA_SLC_BLOCK
    forced = (blk == 0) | (blk == own) | (blk == own - 1)
    imp = jnp.where(forced, SEL_FORCE, imp)
    imp = jnp.where(blk <= own, imp, NEG_INF)
    sel = (_rank_before(imp, ns) < float(min(NSA_SLC_TOPK, ns))) & (imp > 0.5 * NEG_INF)
    selb = jnp.where(sel, 0.0, NEG_INF)
    for j in range(nkt):
        bias_ref[j] = jnp.concatenate(
            [jnp.broadcast_to(selb[:, j * per + c:j * per + c + 1], (TQ, NSA_SLC_BLOCK))
             for c in range(per)], axis=1)

    jd = t0 // KT
    start = pl.multiple_of(jd * KT, KT)
    kpos = start + lax.broadcasted_iota(jnp.int32, (TQ, KT), 1)
    tq_k = t0 + lax.broadcasted_iota(jnp.int32, (TQ, KT), 0)
    s = _dot_nt(q, ksl_ref[pl.ds(start, KT), :]).reshape(H, TQ, KT)
    s = jnp.where((kpos <= tq_k)[None], s + bias_ref[jd][None], NEG_INF)
    m = jnp.max(s, axis=-1, keepdims=True)
    p = jnp.exp(s - m)
    l = jnp.sum(p, axis=-1, keepdims=True)
    acc = _dot(p.reshape(H * TQ, KT).astype(BF16), vsl_ref[pl.ds(start, KT), :])

    def past(j, carry):
        m, l, acc = carry
        st = pl.multiple_of(j * KT, KT)
        s = _dot_nt(q, ksl_ref[pl.ds(st, KT), :]).reshape(H, TQ, KT) + bias_ref[j][None]
        m_new = jnp.maximum(m, jnp.max(s, axis=-1, keepdims=True))
        alpha = jnp.exp(m - m_new)
        p = jnp.exp(s - m_new)
        l = alpha * l + jnp.sum(p, axis=-1, keepdims=True)
        pv = _dot(p.reshape(H * TQ, KT).astype(BF16), vsl_ref[pl.ds(st, KT), :])
        acc = alpha.reshape(H * TQ, 1) * acc + pv
        return m_new, l, acc

    m, l, acc = lax.fori_loop(0, jd, past, (m, l, acc))
    o_s = acc / jnp.maximum(l, 1e-30).reshape(H * TQ, 1)

    w0 = pl.multiple_of(jnp.maximum(t0 - NSA_WINDOW, 0), TQ)
    wpos = w0 + lax.broadcasted_iota(jnp.int32, (TQ, NSA_SPAN), 1)
    tq_w = t0 + lax.broadcasted_iota(jnp.int32, (TQ, NSA_SPAN), 0)
    mask_w = (wpos <= tq_w) & (wpos > tq_w - NSA_WINDOW)
    s_w = _dot_nt(q, kw_ref[pl.ds(w0, NSA_SPAN), :]).reshape(H, TQ, NSA_SPAN)
    p_w = _softmax_rows(s_w, mask_w[None])
    o_w = _dot(p_w.reshape(H * TQ, NSA_SPAN).astype(BF16), vw_ref[pl.ds(w0, NSA_SPAN), :])

    g = g_ref[0]
    outs = []
    for h in range(H):
        rows = slice(h * TQ, (h + 1) * TQ)
        outs.append(g[:, 3 * h:3 * h + 1] * o_c[rows] + g[:, 3 * h + 1:3 * h + 2] * o_s[rows]
                    + g[:, 3 * h + 2:3 * h + 3] * o_w[rows])
    o_ref[0] = jnp.concatenate(outs, axis=1)


def nsa_attention(q, kv, gates, cmp_kv, overlap):
    b, s, w = q.shape
    nc, ns = overlap.shape
    TQ = NSA_TQ
    return pl.pallas_call(
        _nsa_kernel,
        grid=(b, s // TQ),
        in_specs=[pl.BlockSpec((1, TQ, w), lambda bi, i: (bi, i, 0)),
                  pl.BlockSpec((1, 2, nc, HEAD_DIM), lambda bi, i: (bi, 0, 0, 0)),
                  pl.BlockSpec((1, s, LANES), lambda bi, i: (bi, 0, 1)),
                  pl.BlockSpec((1, s, LANES), lambda bi, i: (bi, 0, 2)),
                  pl.BlockSpec((1, TQ, LANES), lambda bi, i: (bi, i, 0)),
                  pl.BlockSpec((nc, ns), lambda bi, i: (0, 0))],
        out_specs=pl.BlockSpec((1, TQ, w), lambda bi, i: (bi, i, 0)),
        out_shape=jax.ShapeDtypeStruct((b, s, w), F32),
        scratch_shapes=[pltpu.VMEM((s, HEAD_DIM), BF16) for _ in range(4)]
        + [pltpu.VMEM((s // NSA_KT, TQ, NSA_KT), F32)],
        compiler_params=_cparams(("arbitrary", "arbitrary")),
        name="nsa_attention",
    )(q, cmp_kv, kv, kv, gates, overlap)


def _nsa_overlap(s):
    nc = s // NSA_CMP_STRIDE
    ns = s // NSA_SLC_BLOCK
    cstart = np.arange(nc) * NSA_CMP_STRIDE
    cend = cstart + NSA_CMP_LEN - 1
    sstart = np.arange(ns) * NSA_SLC_BLOCK
    ov = (cstart[:, None] <= sstart[None, :] + NSA_SLC_BLOCK - 1) & (cend[:, None] >= sstart[None, :])
    ov[nc - 1] = False
    return jnp.asarray(ov.astype(np.float32))


def _mod_kernel(c_ref, w_ref, b_ref, o_ref):
    c = c_ref[...]
    act = c * jax.nn.sigmoid(c)
    o_ref[0] = jnp.dot(act, w_ref[0], precision=HI, preferred_element_type=F32) + b_ref[0]


def ada_modulation(c, ada_w, ada_b):
    nl, d, wide = ada_w.shape
    b = c.shape[0]
    tn = D_MODEL
    return pl.pallas_call(
        _mod_kernel,
        grid=(nl, wide // tn),
        in_specs=[pl.BlockSpec((b, d), lambda l, j: (0, 0)),
                  pl.BlockSpec((1, d, tn), lambda l, j: (l, 0, j)),
                  pl.BlockSpec((1, 1, tn), lambda l, j: (l, 0, j))],
        out_specs=pl.BlockSpec((1, b, tn), lambda l, j: (l, 0, j)),
        out_shape=jax.ShapeDtypeStruct((nl, b, wide), F32),
        compiler_params=_cparams(("arbitrary", "arbitrary")),
        name="ada_modulation",
    )(c, ada_w, ada_b.reshape(nl, 1, wide))


def _rms(x, g):
    return x * lax.rsqrt(jnp.mean(x * x, axis=-1, keepdims=True) + RMS_EPS) * g


_GATE_PAD = LANES
_COLS = {}
_off = 0
for _name, _w in (("qa", MOBA_W), ("ka", MOBA_W), ("va", MOBA_W), ("qb", NSA_W), ("kvb", NSA_W),
                  ("gb", _GATE_PAD), ("qc", SB_W), ("kc", SB_W), ("vc", SB_W)):
    _COLS[_name] = (_off, _w)
    _off += _w
IN_W_PACKED = _off
PROJ_TM = 256


def _rope_block(p, cs, sm, sp):
    return p * cs + pltpu.roll(p, LANES - ROPE_DIM // 2, 1) * sm + pltpu.roll(p, ROPE_DIM // 2, 1) * sp


def _in_proj_kernel(x_ref, sc_ref, sh_ref, g_ref, w_ref, cqk_ref, mqk_ref, pqk_ref,
                    ckv_ref, mkv_ref, pkv_ref,
                    qa_ref, ka_ref, va_ref, qb_ref, kvb_ref, gb_ref, qc_ref, kc_ref, vc_ref):
    hm = _rms(x_ref[0], g_ref[...]) * (1.0 + sc_ref[0]) + sh_ref[0]
    p = _dot(hm.astype(BF16), w_ref[...])
    qk = (cqk_ref[...], mqk_ref[...], pqk_ref[...])
    kv = (ckv_ref[...], mkv_ref[...], pkv_ref[...])

    def emit(ref, name, tabs):
        off, w = _COLS[name]
        for j in range(w // LANES):
            blk = p[:, off + j * LANES: off + (j + 1) * LANES]
            if tabs is not None:
                blk = _rope_block(blk, *tabs)
            ref[0, :, j * LANES:(j + 1) * LANES] = blk

    emit(qa_ref, "qa", qk)
    emit(ka_ref, "ka", qk)
    emit(va_ref, "va", None)
    emit(qb_ref, "qb", qk)
    emit(kvb_ref, "kvb", kv)
    emit(qc_ref, "qc", None)
    emit(kc_ref, "kc", None)
    emit(vc_ref, "vc", None)
    off, w = _COLS["gb"]
    gb_ref[0] = jax.nn.sigmoid(p[:, off:off + w])


def in_projection(x, sc, sh, g, w_packed, tabs):
    b, s, d = x.shape
    tm = PROJ_TM
    row = lambda w: pl.BlockSpec((1, tm, w), lambda bi, i: (bi, i, 0))
    vec = pl.BlockSpec((1, 1, d), lambda bi, i: (bi, 0, 0))
    tab = pl.BlockSpec((tm, LANES), lambda bi, i: (i, 0))
    names = ("qa", "ka", "va", "qb", "kvb", "gb", "qc", "kc", "vc")
    return pl.pallas_call(
        _in_proj_kernel,
        grid=(b, s // tm),
        in_specs=[row(d), vec, vec, pl.BlockSpec((1, d), lambda bi, i: (0, 0)),
                  pl.BlockSpec((d, IN_W_PACKED), lambda bi, i: (0, 0))] + [tab] * 6,
        out_specs=[row(_COLS[n][1]) for n in names],
        out_shape=[jax.ShapeDtypeStruct((b, s, _COLS[n][1]), F32) for n in names],
        compiler_params=_cparams(("arbitrary", "arbitrary")),
        name="in_projection",
    )(x, sc, sh, g, w_packed, *tabs)


def _pack_w_in(w_in):
    widths = (MOBA_W, MOBA_W, MOBA_W, NSA_W, NSA_W, 3 * NSA_HEADS, SB_W, SB_W, SB_W)
    offs = np.cumsum((0,) + widths)
    parts = []
    for i, w in enumerate(widths):
        blk = w_in[:, offs[i]:offs[i + 1]]
        if w == 3 * NSA_HEADS:
            blk = jnp.pad(blk, ((0, 0), (0, _GATE_PAD - w)))
        parts.append(blk)
    return jnp.concatenate(parts, axis=1).astype(BF16)


def _rope_tables(s):
    half = ROPE_DIM // 2
    inv_freq = ROPE_THETA ** (-jnp.arange(0, ROPE_DIM, 2, dtype=F32) / ROPE_DIM)
    ang = jnp.arange(s, dtype=F32)[:, None] * inv_freq[None, :]
    cos, sin = jnp.cos(ang), jnp.sin(ang)
    zeros = jnp.zeros((s, HEAD_DIM - ROPE_DIM), F32)
    z8 = jnp.zeros((s, half), F32)
    cs_h = jnp.concatenate([cos, cos, zeros + 1.0], axis=1)
    sm_h = jnp.concatenate([-sin, z8, zeros], axis=1)
    sp_h = jnp.concatenate([z8, sin, zeros], axis=1)
    ident = (jnp.ones((s, HEAD_DIM), F32), jnp.zeros((s, HEAD_DIM), F32), jnp.zeros((s, HEAD_DIM), F32))
    qk = tuple(jnp.concatenate([t, t], axis=1) for t in (cs_h, sm_h, sp_h))
    kv = tuple(jnp.concatenate([t, i], axis=1) for t, i in zip((cs_h, sm_h, sp_h), ident))
    return qk + kv


OUT_TM = 256


def _out_proj_kernel(oa_ref, ob_ref, oc_ref, x_ref, gg_ref, w_ref, g1_ref, n1_ref, n2_ref,
                     sc_ref, sh_ref, rw_ref, rb_ref, x1_ref, hf_ref, lg_ref):
    gg = gg_ref[...]
    y = jnp.concatenate([_rms(oa_ref[0], gg[:, :MOBA_W]),
                         _rms(ob_ref[0], gg[:, MOBA_W:MOBA_W + NSA_W]),
                         _rms(oc_ref[0], gg[:, MOBA_W + NSA_W:])], axis=1)
    y = _dot(y.astype(BF16), w_ref[...])
    x1 = x_ref[0] + g1_ref[0] * _rms(y, n1_ref[...])
    x1_ref[0] = x1
    hf = _rms(x1, n2_ref[...]) * (1.0 + sc_ref[0]) + sh_ref[0]
    hf_ref[0] = hf.astype(BF16)
    lg_ref[0] = jnp.dot(hf, rw_ref[...], precision=HI, preferred_element_type=F32) + rb_ref[...]


def out_projection(oa, ob, oc, x, grp_g, w_out, g1, n1, n2, sc2, sh2, rw, rb):
    b, s, d = x.shape
    tm = OUT_TM
    row = lambda w: pl.BlockSpec((1, tm, w), lambda bi, i: (bi, i, 0))
    vec = pl.BlockSpec((1, 1, d), lambda bi, i: (bi, 0, 0))
    cst = lambda r, w: pl.BlockSpec((r, w), lambda bi, i: (0, 0))
    return pl.pallas_call(
        _out_proj_kernel,
        grid=(b, s // tm),
        in_specs=[row(MOBA_W), row(NSA_W), row(SB_W), row(d), cst(1, d), cst(d, d), vec,
                  cst(1, d), cst(1, d), vec, vec, cst(d, LANES), cst(1, LANES)],
        out_specs=[row(d), row(d), row(LANES)],
        out_shape=[jax.ShapeDtypeStruct((b, s, d), F32), jax.ShapeDtypeStruct((b, s, d), BF16),
                   jax.ShapeDtypeStruct((b, s, LANES), F32)],
        compiler_params=_cparams(("arbitrary", "arbitrary")),
        name="out_projection",
    )(oa, ob, oc, x, grp_g, w_out, g1, n1, n2, sc2, sh2, rw, rb)


MOE_TM = 256


def _expert_kernel(be_ref, nu_ref, xs_ref, wgu_ref, bgu_ref, wdn_ref, bdn_ref, y_ref):
    i = pl.program_id(0)

    @pl.when(i < nu_ref[0])
    def _():
        hgu = _dot(xs_ref[...], wgu_ref[0]) + bgu_ref[0]
        de = hgu.shape[1] // 2
        glu = jnp.minimum(hgu[:, :de], SWIGLU_LIMIT)
        lin = jnp.clip(hgu[:, de:], -SWIGLU_LIMIT, SWIGLU_LIMIT)
        act = glu * jax.nn.sigmoid(SWIGLU_ALPHA * glu) * (lin + 1.0)
        y_ref[...] = _dot(act.astype(BF16), wdn_ref[0]) + bdn_ref[0]

    @pl.when(i >= nu_ref[0])
    def _():
        y_ref[...] = jnp.zeros(y_ref.shape, y_ref.dtype)


def expert_ffn(blk_exp, n_used, xs, w_gu, b_gu, w_dn, b_dn):
    r, d = xs.shape
    ne, _, wide = w_gu.shape
    tm = MOE_TM
    grid_spec = pltpu.PrefetchScalarGridSpec(
        num_scalar_prefetch=2,
        grid=(r // tm,),
        in_specs=[pl.BlockSpec((tm, d), lambda i, be, nu: (i, 0)),
                  pl.BlockSpec((1, d, wide), lambda i, be, nu: (be[i], 0, 0)),
                  pl.BlockSpec((1, 1, wide), lambda i, be, nu: (be[i], 0, 0)),
                  pl.BlockSpec((1, wide // 2, d), lambda i, be, nu: (be[i], 0, 0)),
                  pl.BlockSpec((1, 1, d), lambda i, be, nu: (be[i], 0, 0))],
        out_specs=pl.BlockSpec((tm, d), lambda i, be, nu: (i, 0)),
    )
    return pl.pallas_call(
        _expert_kernel,
        grid_spec=grid_spec,
        out_shape=jax.ShapeDtypeStruct((r, d), F32),
        compiler_params=_cparams(("arbitrary",)),
        name="expert_ffn",
    )(blk_exp, n_used, xs, w_gu, b_gu.reshape(ne, 1, wide), w_dn, b_dn.reshape(ne, 1, d))


COMB_TM = 256


def _combine_kernel(yg_ref, gate_ref, x_ref, g2_ref, n3_ref, o_ref):
    gate = gate_ref[0]
    y = gate[:, 0:1] * yg_ref[0, 0]
    for k in range(1, TOP_K):
        y = y + gate[:, k:k + 1] * yg_ref[0, k]
    o_ref[0] = x_ref[0] + g2_ref[0] * _rms(y, n3_ref[...])


def moe_combine(yg, gate, x, g2, n3):
    b, s, d = x.shape
    tm = COMB_TM
    return pl.pallas_call(
        _combine_kernel,
        grid=(b, s // tm),
        in_specs=[pl.BlockSpec((1, TOP_K, tm, d), lambda bi, i: (bi, 0, i, 0)),
                  pl.BlockSpec((1, tm, LANES), lambda bi, i: (bi, i, 0)),
                  pl.BlockSpec((1, tm, d), lambda bi, i: (bi, i, 0)),
                  pl.BlockSpec((1, 1, d), lambda bi, i: (bi, 0, 0)),
                  pl.BlockSpec((1, d), lambda bi, i: (0, 0))],
        out_specs=pl.BlockSpec((1, tm, d), lambda bi, i: (bi, i, 0)),
        out_shape=jax.ShapeDtypeStruct((b, s, d), F32),
        compiler_params=_cparams(("arbitrary", "arbitrary")),
        name="moe_combine",
    )(yg, gate, x, g2, n3)


def _route(logits):
    n = logits.shape[0]
    tm = MOE_TM
    top_val, top_idx = lax.top_k(logits, TOP_K)
    gate = jax.nn.softmax(top_val, axis=-1)
    e = top_idx.reshape(-1)
    onehot = (e[:, None] == jnp.arange(N_EXPERTS, dtype=e.dtype)[None, :]).astype(jnp.int32)
    rank = jnp.take_along_axis(jnp.cumsum(onehot, axis=0), e[:, None], axis=1)[:, 0] - 1
    counts = jnp.sum(onehot, axis=0)
    padded = (counts + tm - 1) // tm * tm
    pad_end = jnp.cumsum(padded)
    dest = (pad_end - padded)[e] + rank
    n_rows = -(-(n * TOP_K + N_EXPERTS * (tm - 1)) // tm) * tm
    n_blk = n_rows // tm
    tok = jnp.arange(n * TOP_K, dtype=jnp.int32) // TOP_K
    row_tok = jnp.zeros((n_rows,), jnp.int32).at[dest].set(tok)
    blk_exp = jnp.minimum(jnp.searchsorted(pad_end, jnp.arange(n_blk, dtype=jnp.int32) * tm,
                                           side='right'), N_EXPERTS - 1).astype(jnp.int32)
    n_used = (pad_end[-1] // tm).astype(jnp.int32).reshape(1)
    return gate, dest.reshape(n, TOP_K), row_tok, blk_exp, n_used


def _layer(x, mod, norm_g, w_in, cmp_pe, cmp_w1, cmp_w2, grp_g, w_out, router_w, router_b,
           w_gu, b_gu, w_dn, b_dn, tabs, overlap):
    b, s, d = x.shape
    sh1, sc1, g1, sh2, sc2, g2 = (m.reshape(b, 1, d) for m in jnp.split(mod, 6, axis=-1))
    ng = norm_g.reshape(4, 1, d)

    qa, ka, va, qb, kvb, gb, qc, kc, vc = in_projection(x, sc1, sh1, ng[0], _pack_w_in(w_in), tabs)
    oa = moba_attention(qa, ka, va)
    nc = s // NSA_CMP_STRIDE
    kv16 = jnp.stack([kvb[:, :, :HEAD_DIM], kvb[:, :, HEAD_DIM:2 * HEAD_DIM]], axis=1)
    kv16 = kv16.reshape(b, 2, nc, NSA_CMP_STRIDE * HEAD_DIM)
    cmp_kv = nsa_compress(kv16, cmp_pe.reshape(2, 1, NSA_CMP_LEN * HEAD_DIM), cmp_w1, cmp_w2)
    ob = nsa_attention(qb, kvb, gb, cmp_kv, overlap)
    oc = sb_attention(qc, kc, vc)

    rw = jnp.pad(router_w, ((0, 0), (0, LANES - N_EXPERTS)))
    rb = jnp.pad(router_b, (0, LANES - N_EXPERTS)).reshape(1, LANES)
    x1, hf, logits = out_projection(oa, ob, oc, x, grp_g.reshape(1, d), w_out.astype(BF16), g1,
                                    ng[1], ng[2], sc2, sh2, rw, rb)

    n = b * s
    gate, dest, row_tok, blk_exp, n_used = _route(logits.reshape(n, LANES)[:, :N_EXPERTS])
    xs = jnp.take(hf.reshape(n, d), row_tok, axis=0)
    y_rows = expert_ffn(blk_exp, n_used, xs, w_gu.astype(BF16), b_gu, w_dn.astype(BF16), b_dn)
    yg = jnp.take(y_rows, dest.reshape(b, s, TOP_K).transpose(0, 2, 1), axis=0)
    gate = jnp.pad(gate, ((0, 0), (0, LANES - TOP_K))).reshape(b, s, LANES)
    return moe_combine(yg, gate, x1, g2, ng[3])


def kernel(x, c, ada_w, ada_b, norm_g, w_in, nsa_cmp_pe, nsa_cmp_w1, nsa_cmp_w2, mix_out_g,
           w_out, router_w, router_b, exp_w_gu, exp_b_gu, exp_w_dn, exp_b_dn):
    s = x.shape[1]
    tabs = _rope_tables(s)
    overlap = _nsa_overlap(s)
    mod = ada_modulation(c, ada_w, ada_b)
    for l in range(ada_w.shape[0]):
        x = _layer(x, mod[l], norm_g[l], w_in[l], nsa_cmp_pe[l], nsa_cmp_w1[l], nsa_cmp_w2[l],
                   mix_out_g[l], w_out[l], router_w[l], router_b[l], exp_w_gu[l], exp_b_gu[l],
                   exp_w_dn[l], exp_b_dn[l], tabs, overlap)
    return x
```

```python
import functools

import numpy as np
import jax
import jax.numpy as jnp
from jax import lax
from jax.experimental import pallas as pl
from jax.experimental.pallas import tpu as pltpu

F32 = jnp.float32
BF16 = jnp.bfloat16
HI = lax.Precision.HIGHEST

D_MODEL = 1024
N_HEADS = 16
HEAD_DIM = 64
MOBA_HEADS = 4
NSA_HEADS = 6
SB_HEADS = 6
MOBA_W = MOBA_HEADS * HEAD_DIM
NSA_W = NSA_HEADS * HEAD_DIM
SB_W = SB_HEADS * HEAD_DIM
ROPE_DIM = 16
ROPE_THETA = 500000.0
MOBA_BLOCK = 256
MOBA_TOPK = 3
NSA_CMP_LEN = 32
NSA_CMP_STRIDE = 16
NSA_SLC_BLOCK = 64
NSA_SLC_TOPK = 16
NSA_WINDOW = 512
N_EXPERTS = 32
TOP_K = 4
SWIGLU_LIMIT = 7.0
SWIGLU_ALPHA = 1.702
RMS_EPS = 1e-6
NEG_INF = -1e30
SEL_FORCE = 1e4
SCALE = HEAD_DIM ** -0.5
LOG2E = 1.4426950408889634

LANES = 128
VMEM_LIMIT = 48 * 1024 * 1024


def _cparams(sem):
    return pltpu.CompilerParams(dimension_semantics=sem, vmem_limit_bytes=VMEM_LIMIT)


def _dot(a, b):
    return jnp.dot(a, b, preferred_element_type=F32)


def _dot_nt(a, b):
    return lax.dot_general(a, b, (((1,), (1,)), ((), ())), preferred_element_type=F32)


SB_TILE = 256


SB_GROUP_W = SB_W


def _sb_kernel(q_ref, k_ref, v_ref, o_ref, ks_ref, vs_ref):
    T = SB_TILE
    H = ks_ref.shape[0]
    qi = pl.program_id(2)

    @pl.when(qi == 0)
    def _():
        for h in range(H):
            ks_ref[h] = k_ref[0, :, h * HEAD_DIM:(h + 1) * HEAD_DIM].astype(BF16)
            vs_ref[h] = v_ref[0, :, h * HEAD_DIM:(h + 1) * HEAD_DIM].astype(BF16)

    row = lax.broadcasted_iota(jnp.int32, (T, T), 0)
    col = lax.broadcasted_iota(jnp.int32, (T, T), 1)
    incl = (row >= col).astype(BF16)
    sum_rhs = jnp.concatenate([incl, incl], axis=0)
    diag_mask = col < row
    qs = [(q_ref[0, :, h * HEAD_DIM:(h + 1) * HEAD_DIM] * (SCALE * LOG2E)).astype(BF16)
          for h in range(H)]
    sign = jnp.uint32(0x80000000)

    def scores(h, j):
        return _dot_nt(qs[h], ks_ref[h, pl.ds(pl.multiple_of(j * T, T), T), :])

    def neg_log2_keep(z, masked):
        neg_abs = lax.bitcast_convert_type(lax.bitcast_convert_type(z, jnp.uint32) | sign, F32)
        nlk = jnp.maximum(z, 0.0) + jnp.log2(1.0 + jnp.exp2(neg_abs))
        return jnp.where(diag_mask, nlk, 0.0) if masked else nlk

    def suffix_sums(nlk):
        hi = nlk.astype(BF16)
        lo = (nlk - hi.astype(F32)).astype(BF16)
        return _dot(jnp.concatenate([hi, lo], axis=1), sum_rhs)

    def weights(z, sums, run, masked):
        a = jnp.exp2(z - sums - run)
        return (jnp.where(diag_mask, a, 0.0) if masked else a).astype(BF16)

    def pv(h, p, j):
        return _dot(p, vs_ref[h, pl.ds(pl.multiple_of(j * T, T), T), :])

    def step(zs, runs, masked):
        sums = [suffix_sums(neg_log2_keep(zs[h], masked)) for h in range(H)]
        ps = [weights(zs[h], sums[h], runs[h], masked) for h in range(H)]
        return ps, [runs[h] + sums[h][:, 0:1] for h in range(H)]

    zeros = jnp.zeros((T, 1), F32)
    ps, runs = step([scores(h, qi) for h in range(H)], [zeros] * H, True)
    accs = [pv(h, ps[h], qi) for h in range(H)]

    def past(i, carry):
        runs, accs = carry
        j = qi - 1 - i
        ps, runs = step([scores(h, j) for h in range(H)], runs, False)
        return runs, [accs[h] + pv(h, ps[h], j) for h in range(H)]

    runs, accs = lax.fori_loop(0, qi, past, (runs, accs))
    o_ref[0] = jnp.concatenate(accs, axis=1)


def sb_attention(q, k, v):
    b, s, w = q.shape
    T = SB_TILE
    gw = SB_GROUP_W
    nh = gw // HEAD_DIM
    qspec = pl.BlockSpec((1, T, gw), lambda bi, p, i: (bi, i, p))
    kvspec = pl.BlockSpec((1, s, gw), lambda bi, p, i: (bi, 0, p))
    return pl.pallas_call(
        _sb_kernel,
        grid=(b, w // gw, s // T),
        in_specs=[qspec, kvspec, kvspec],
        out_specs=qspec,
        out_shape=jax.ShapeDtypeStruct((b, s, w), F32),
        scratch_shapes=[pltpu.VMEM((nh, s, HEAD_DIM), BF16), pltpu.VMEM((nh, s, HEAD_DIM), BF16)],
        compiler_params=_cparams(("arbitrary", "arbitrary", "arbitrary")),
        name="sb_attention",
    )(q, k, v)


def _rank_before_t(vals, n):
    idx = lax.broadcasted_iota(jnp.int32, vals.shape, 0)
    rank = jnp.zeros(vals.shape, F32)
    for j2 in range(n):
        other = vals[j2:j2 + 1, :]
        ahead = (other > vals) | ((other == vals) & (idx > j2))
        rank = rank + jnp.where(ahead, 1.0, 0.0)
    return rank


MOBA_KT = 2 * MOBA_BLOCK


def _moba_kernel(q_ref, k_ref, v_ref, o_ref, ks_ref, vs_ref, km_ref):
    T, KT = MOBA_BLOCK, MOBA_KT
    nb = km_ref.shape[1]
    s_len = ks_ref.shape[1]
    qi = pl.program_id(2)

    @pl.when(qi == 0)
    def _():
        key_blk = lax.broadcasted_iota(jnp.int32, (s_len, HEAD_DIM), 0) // T
        onehot = (lax.broadcasted_iota(jnp.int32, (s_len, HEAD_DIM), 1) == key_blk).astype(F32)
        for h in range(2):
            kh = k_ref[0, :, h * HEAD_DIM:(h + 1) * HEAD_DIM]
            ks_ref[h] = jnp.concatenate([kh, onehot], axis=1).astype(BF16)
            vs_ref[h] = v_ref[0, :, h * HEAD_DIM:(h + 1) * HEAD_DIM].astype(BF16)
            km_ref[h] = jnp.mean(kh.reshape(nb, T, HEAD_DIM), axis=1)

    row = lax.broadcasted_iota(jnp.int32, (T, T), 0)
    col = lax.broadcasted_iota(jnp.int32, (T, T), 1)
    causal = col <= row
    blk_t = lax.broadcasted_iota(jnp.int32, (nb, T), 0)
    start = pl.multiple_of(qi * T, T)

    q_aug, state = [], []
    for h in range(2):
        qf = q_ref[0, :, h * HEAD_DIM:(h + 1) * HEAD_DIM]
        qs = qf * SCALE
        gate = lax.dot_general(km_ref[h], qf, (((1,), (1,)), ((), ())),
                               precision=HI, preferred_element_type=F32)
        gate = jnp.where(blk_t < qi, gate, NEG_INF)
        sel = (_rank_before_t(gate, nb) < float(MOBA_TOPK)) & (gate > 0.5 * NEG_INF)
        selb = jnp.where(sel, 0.0, NEG_INF)
        selb = jnp.concatenate([selb, jnp.full((LANES - nb, T), NEG_INF, F32)], axis=0).T
        q_aug.append(jnp.concatenate([qs, selb[:, :HEAD_DIM]], axis=1).astype(BF16))

        s = _dot_nt(qs.astype(BF16), ks_ref[h, pl.ds(start, T), 0:HEAD_DIM])
        s = jnp.where(causal, s, NEG_INF)
        m = jnp.max(s, axis=1, keepdims=True)
        p = jnp.where(causal, jnp.exp(s - m), 0.0)
        l = jnp.sum(p, axis=1, keepdims=True)
        acc = _dot(p.astype(BF16), vs_ref[h, pl.ds(start, T), :])
        state += [m, l, acc]

    def past(i, carry):
        st = pl.multiple_of(i * KT, KT)
        out = []
        for h in range(2):
            m, l, acc = carry[3 * h:3 * h + 3]
            s = _dot_nt(q_aug[h], ks_ref[h, pl.ds(st, KT), :])
            m_new = jnp.maximum(m, jnp.max(s, axis=1, keepdims=True))
            alpha = jnp.exp(m - m_new)
            p = jnp.exp(s - m_new)
            l = alpha * l + jnp.sum(p, axis=1, keepdims=True)
            acc = alpha * acc + _dot(p.astype(BF16), vs_ref[h, pl.ds(st, KT), :])
            out += [m_new, l, acc]
        return tuple(out)

    state = lax.fori_loop(0, (qi + 1) // 2, past, tuple(state))
    outs = [state[3 * h + 2] / jnp.maximum(state[3 * h + 1], 1e-30) for h in range(2)]
    o_ref[0] = jnp.concatenate(outs, axis=1)


def moba_attention(q, k, v):
    b, s, w = q.shape
    T = MOBA_BLOCK
    nb = s // T
    assert nb <= HEAD_DIM and s % MOBA_KT == 0
    qspec = pl.BlockSpec((1, T, LANES), lambda bi, p, i: (bi, i, p))
    kvspec = pl.BlockSpec((1, s, LANES), lambda bi, p, i: (bi, 0, p))
    return pl.pallas_call(
        _moba_kernel,
        grid=(b, w // LANES, nb),
        in_specs=[qspec, kvspec, kvspec],
        out_specs=qspec,
        out_shape=jax.ShapeDtypeStruct((b, s, w), F32),
        scratch_shapes=[pltpu.VMEM((2, s, LANES), BF16), pltpu.VMEM((2, s, HEAD_DIM), BF16),
                        pltpu.VMEM((2, nb, HEAD_DIM), F32)],
        compiler_params=_cparams(("arbitrary", "arbitrary", "arbitrary")),
        name="moba_attention",
    )(q, k, v)


def _nsa_compress_kernel(x_ref, pe_ref, w1_ref, w2_ref, o_ref):
    nc = x_ref.shape[2]
    half = NSA_CMP_STRIDE * HEAD_DIM
    x = x_ref[0, 0]
    w1 = w1_ref[0]
    first = jnp.dot(x, w1[:half], precision=HI, preferred_element_type=F32)
    second = jnp.dot(x, w1[half:], precision=HI, preferred_element_type=F32)
    pe = jnp.broadcast_to(pe_ref[0], (8, 2 * half))
    peb = jnp.dot(pe, w1, precision=HI, preferred_element_type=F32)[0:1]
    pre = first + pltpu.roll(second, nc - 1, 0) + peb
    hid = pre * jax.nn.sigmoid(pre)
    o_ref[0, 0] = jnp.dot(hid, w2_ref[0], precision=HI, preferred_element_type=F32)


def nsa_compress(kv16, pe, w1, w2):
    b, _, nc, wide = kv16.shape
    return pl.pallas_call(
        _nsa_compress_kernel,
        grid=(b, 2),
        in_specs=[pl.BlockSpec((1, 1, nc, wide), lambda bi, i: (bi, i, 0, 0)),
                  pl.BlockSpec((1, 1, 2 * wide), lambda bi, i: (i, 0, 0)),
                  pl.BlockSpec((1, 2 * wide, HEAD_DIM), lambda bi, i: (i, 0, 0)),
                  pl.BlockSpec((1, HEAD_DIM, HEAD_DIM), lambda bi, i: (i, 0, 0))],
        out_specs=pl.BlockSpec((1, 1, nc, HEAD_DIM), lambda bi, i: (bi, i, 0, 0)),
        out_shape=jax.ShapeDtypeStruct((b, 2, nc, HEAD_DIM), F32),
        compiler_params=_cparams(("arbitrary", "arbitrary")),
        name="nsa_compress",
    )(kv16, pe, w1, w2)


NSA_TQ = 128
NSA_KT = 256
NSA_SPAN = NSA_WINDOW + NSA_TQ


def _softmax_rows(s, mask):
    s = jnp.where(mask, s, NEG_INF)
    m = jnp.max(s, axis=-1, keepdims=True)
    e = jnp.where(mask, jnp.exp(s - m), 0.0)
    return e / jnp.maximum(jnp.sum(e, axis=-1, keepdims=True), 1e-30)


def _nsa_kernel(q_ref, cmp_ref, slc_ref, win_ref, g_ref, ov_ref, o_ref,
                ksl_ref, vsl_ref, kw_ref, vw_ref):
    TQ, KT, H = NSA_TQ, NSA_KT, NSA_HEADS
    ns, nc = ov_ref.shape
    s_len = ksl_ref.shape[0]
    qi = pl.program_id(1)
    t0 = qi * TQ

    @pl.when(qi == 0)
    def _():
        key_blk = lax.broadcasted_iota(jnp.int32, (s_len, HEAD_DIM), 0) // NSA_SLC_BLOCK
        onehot = (lax.broadcasted_iota(jnp.int32, (s_len, HEAD_DIM), 1) == key_blk).astype(F32)
        ksl_ref[...] = jnp.concatenate([slc_ref[0, :, :HEAD_DIM], onehot], axis=1).astype(BF16)
        vsl_ref[...] = slc_ref[0, :, HEAD_DIM:].astype(BF16)
        kw_ref[...] = win_ref[0, :, :HEAD_DIM].astype(BF16)
        vw_ref[...] = win_ref[0, :, HEAD_DIM:].astype(BF16)

    qf = jnp.concatenate([q_ref[0, :, h * HEAD_DIM:(h + 1) * HEAD_DIM] for h in range(H)],
                         axis=0) * SCALE
    q = qf.astype(BF16)

    tq_c = t0 + lax.broadcasted_iota(jnp.int32, (TQ, nc), 0)
    n_c = lax.broadcasted_iota(jnp.int32, (TQ, nc), 1)
    mask_c = (n_c * NSA_CMP_STRIDE + (NSA_CMP_LEN - 1) <= tq_c) & (n_c < nc - 1)
    s_c = lax.dot_general(qf, cmp_ref[0, 0], (((1,), (1,)), ((), ())),
                          precision=HI, preferred_element_type=F32)
    p_c = _softmax_rows(s_c.reshape(H, TQ, nc), mask_c[None])
    o_c = _dot(p_c.reshape(H * TQ, nc).astype(BF16), cmp_ref[0, 1].astype(BF16))

    imp = lax.dot_general(ov_ref[...], jnp.sum(p_c, axis=0), (((1,), (1,)), ((), ())),
                          precision=HI, preferred_element_type=F32)
    tq_s = t0 + lax.broadcasted_iota(jnp.int32, (ns, TQ), 1)
    blk = lax.broadcasted_iota(jnp.int32, (ns, TQ), 0)
    own = tq_s // NSA_SLC_BLOCK
    forced = (blk == 0) | (blk == own) | (blk == own - 1)
    imp = jnp.where(forced, SEL_FORCE, imp)
    imp = jnp.where(blk <= own, imp, NEG_INF)
    sel = (_rank_before_t(imp, ns) < float(min(NSA_SLC_TOPK, ns))) & (imp > 0.5 * NEG_INF)
    selb = jnp.where(sel, 0.0, NEG_INF)
    if ns < LANES:
        selb = jnp.concatenate([selb, jnp.full((LANES - ns, TQ), NEG_INF, F32)], axis=0)
    selb = selb.T[:, :HEAD_DIM]
    q_aug = jnp.concatenate([qf, jnp.concatenate([selb] * H, axis=0)], axis=1).astype(BF16)

    jd = t0 // KT
    start = pl.multiple_of(jd * KT, KT)
    kpos = start + lax.broadcasted_iota(jnp.int32, (TQ, KT), 1)
    tq_k = t0 + lax.broadcasted_iota(jnp.int32, (TQ, KT), 0)
    s = _dot_nt(q_aug, ksl_ref[pl.ds(start, KT), :]).reshape(H, TQ, KT)
    s = jnp.where((kpos <= tq_k)[None], s, NEG_INF)
    m = jnp.max(s, axis=-1, keepdims=True)
    p = jnp.exp(s - m)
    l = jnp.sum(p, axis=-1, keepdims=True)
    acc = _dot(p.reshape(H * TQ, KT).astype(BF16), vsl_ref[pl.ds(start, KT), :])

    def past(j, carry):
        m, l, acc = carry
        st = pl.multiple_of(j * KT, KT)
        s = _dot_nt(q_aug, ksl_ref[pl.ds(st, KT), :]).reshape(H, TQ, KT)
        m_new = jnp.maximum(m, jnp.max(s, axis=-1, keepdims=True))
        alpha = jnp.exp(m - m_new)
        p = jnp.exp(s - m_new)
        l = alpha * l + jnp.sum(p, axis=-1, keepdims=True)
        pv = _dot(p.reshape(H * TQ, KT).astype(BF16), vsl_ref[pl.ds(st, KT), :])
        acc = alpha.reshape(H * TQ, 1) * acc + pv
        return m_new, l, acc

    m, l, acc = lax.fori_loop(0, jd, past, (m, l, acc))
    o_s = acc / jnp.maximum(l, 1e-30).reshape(H * TQ, 1)

    w0 = pl.multiple_of(jnp.maximum(t0 - NSA_WINDOW, 0), TQ)
    wpos = w0 + lax.broadcasted_iota(jnp.int32, (TQ, NSA_SPAN), 1)
    tq_w = t0 + lax.broadcasted_iota(jnp.int32, (TQ, NSA_SPAN), 0)
    mask_w = (wpos <= tq_w) & (wpos > tq_w - NSA_WINDOW)
    s_w = _dot_nt(q, kw_ref[pl.ds(w0, NSA_SPAN), :]).reshape(H, TQ, NSA_SPAN)
    p_w = _softmax_rows(s_w, mask_w[None])
    o_w = _dot(p_w.reshape(H * TQ, NSA_SPAN).astype(BF16), vw_ref[pl.ds(w0, NSA_SPAN), :])

    g = g_ref[0]
    outs = []
    for h in range(H):
        rows = slice(h * TQ, (h + 1) * TQ)
        outs.append(g[:, 3 * h:3 * h + 1] * o_c[rows] + g[:, 3 * h + 1:3 * h + 2] * o_s[rows]
                    + g[:, 3 * h + 2:3 * h + 3] * o_w[rows])
    o_ref[0] = jnp.concatenate(outs, axis=1)


def nsa_attention(q, kv, gates, cmp_kv, overlap):
    b, s, w = q.shape
    ns, nc = overlap.shape
    assert ns <= HEAD_DIM
    TQ = NSA_TQ
    return pl.pallas_call(
        _nsa_kernel,
        grid=(b, s // TQ),
        in_specs=[pl.BlockSpec((1, TQ, w), lambda bi, i: (bi, i, 0)),
                  pl.BlockSpec((1, 2, nc, HEAD_DIM), lambda bi, i: (bi, 0, 0, 0)),
                  pl.BlockSpec((1, s, LANES), lambda bi, i: (bi, 0, 1)),
                  pl.BlockSpec((1, s, LANES), lambda bi, i: (bi, 0, 2)),
                  pl.BlockSpec((1, TQ, LANES), lambda bi, i: (bi, i, 0)),
                  pl.BlockSpec((ns, nc), lambda bi, i: (0, 0))],
        out_specs=pl.BlockSpec((1, TQ, w), lambda bi, i: (bi, i, 0)),
        out_shape=jax.ShapeDtypeStruct((b, s, w), F32),
        scratch_shapes=[pltpu.VMEM((s, LANES), BF16)] + [pltpu.VMEM((s, HEAD_DIM), BF16)] * 3,
        compiler_params=_cparams(("arbitrary", "arbitrary")),
        name="nsa_attention",
    )(q, cmp_kv, kv, kv, gates, overlap)


def _nsa_overlap(s):
    nc = s // NSA_CMP_STRIDE
    ns = s // NSA_SLC_BLOCK
    cstart = np.arange(nc) * NSA_CMP_STRIDE
    cend = cstart + NSA_CMP_LEN - 1
    sstart = np.arange(ns) * NSA_SLC_BLOCK
    ov = (cstart[:, None] <= sstart[None, :] + NSA_SLC_BLOCK - 1) & (cend[:, None] >= sstart[None, :])
    ov[nc - 1] = False
    return jnp.asarray(ov.T.astype(np.float32))


def _mod_kernel(c_ref, w_ref, b_ref, o_ref):
    c = c_ref[...]
    act = c * jax.nn.sigmoid(c)
    o_ref[0] = jnp.dot(act, w_ref[0], precision=HI, preferred_element_type=F32) + b_ref[0]


def ada_modulation(c, ada_w, ada_b):
    nl, d, wide = ada_w.shape
    b = c.shape[0]
    tn = D_MODEL
    return pl.pallas_call(
        _mod_kernel,
        grid=(nl, wide // tn),
        in_specs=[pl.BlockSpec((b, d), lambda l, j: (0, 0)),
                  pl.BlockSpec((1, d, tn), lambda l, j: (l, 0, j)),
                  pl.BlockSpec((1, 1, tn), lambda l, j: (l, 0, j))],
        out_specs=pl.BlockSpec((1, b, tn), lambda l, j: (l, 0, j)),
        out_shape=jax.ShapeDtypeStruct((nl, b, wide), F32),
        compiler_params=_cparams(("arbitrary", "arbitrary")),
        name="ada_modulation",
    )(c, ada_w, ada_b.reshape(nl, 1, wide))


def _rms(x, g):
    return x * lax.rsqrt(jnp.mean(x * x, axis=-1, keepdims=True) + RMS_EPS) * g


_GATE_PAD = LANES
_COLS = {}
_off = 0
for _name, _w in (("qa", MOBA_W), ("ka", MOBA_W), ("va", MOBA_W), ("qb", NSA_W), ("kvb", NSA_W),
                  ("gb", _GATE_PAD), ("qc", SB_W), ("kc", SB_W), ("vc", SB_W)):
    _COLS[_name] = (_off, _w)
    _off += _w
IN_W_PACKED = _off
PROJ_TM = 256


def _rope_block(p, cs, sm, sp):
    return p * cs + pltpu.roll(p, LANES - ROPE_DIM // 2, 1) * sm + pltpu.roll(p, ROPE_DIM // 2, 1) * sp


def _in_proj_kernel(x_ref, sc_ref, sh_ref, g_ref, w_ref, cqk_ref, mqk_ref, pqk_ref,
                    ckv_ref, mkv_ref, pkv_ref,
                    qa_ref, ka_ref, va_ref, qb_ref, kvb_ref, gb_ref, qc_ref, kc_ref, vc_ref):
    hm = _rms(x_ref[0], g_ref[...]) * (1.0 + sc_ref[0]) + sh_ref[0]
    p = _dot(hm.astype(BF16), w_ref[...])
    qk = (cqk_ref[...], mqk_ref[...], pqk_ref[...])
    kv = (ckv_ref[...], mkv_ref[...], pkv_ref[...])

    def emit(ref, name, tabs):
        off, w = _COLS[name]
        for j in range(w // LANES):
            blk = p[:, off + j * LANES: off + (j + 1) * LANES]
            if tabs is not None:
                blk = _rope_block(blk, *tabs)
            ref[0, :, j * LANES:(j + 1) * LANES] = blk

    emit(qa_ref, "qa", qk)
    emit(ka_ref, "ka", qk)
    emit(va_ref, "va", None)
    emit(qb_ref, "qb", qk)
    emit(kvb_ref, "kvb", kv)
    emit(qc_ref, "qc", None)
    emit(kc_ref, "kc", None)
    emit(vc_ref, "vc", None)
    off, w = _COLS["gb"]
    gb_ref[0] = jax.nn.sigmoid(p[:, off:off + w])


def in_projection(x, sc, sh, g, w_packed, tabs):
    b, s, d = x.shape
    tm = PROJ_TM
    row = lambda w: pl.BlockSpec((1, tm, w), lambda bi, i: (bi, i, 0))
    vec = pl.BlockSpec((1, 1, d), lambda bi, i: (bi, 0, 0))
    tab = pl.BlockSpec((tm, LANES), lambda bi, i: (i, 0))
    names = ("qa", "ka", "va", "qb", "kvb", "gb", "qc", "kc", "vc")
    return pl.pallas_call(
        _in_proj_kernel,
        grid=(b, s // tm),
        in_specs=[row(d), vec, vec, pl.BlockSpec((1, d), lambda bi, i: (0, 0)),
                  pl.BlockSpec((d, IN_W_PACKED), lambda bi, i: (0, 0))] + [tab] * 6,
        out_specs=[row(_COLS[n][1]) for n in names],
        out_shape=[jax.ShapeDtypeStruct((b, s, _COLS[n][1]), F32) for n in names],
        compiler_params=_cparams(("arbitrary", "arbitrary")),
        name="in_projection",
    )(x, sc, sh, g, w_packed, *tabs)


def _pack_w_in(w_in):
    widths = (MOBA_W, MOBA_W, MOBA_W, NSA_W, NSA_W, 3 * NSA_HEADS, SB_W, SB_W, SB_W)
    offs = np.cumsum((0,) + widths)
    parts = []
    for i, w in enumerate(widths):
        blk = w_in[:, offs[i]:offs[i + 1]]
        if w == 3 * NSA_HEADS:
            blk = jnp.pad(blk, ((0, 0), (0, _GATE_PAD - w)))
        parts.append(blk)
    return jnp.concatenate(parts, axis=1).astype(BF16)


def _rope_tables(s):
    half = ROPE_DIM // 2
    inv_freq = ROPE_THETA ** (-jnp.arange(0, ROPE_DIM, 2, dtype=F32) / ROPE_DIM)
    ang = jnp.arange(s, dtype=F32)[:, None] * inv_freq[None, :]
    cos, sin = jnp.cos(ang), jnp.sin(ang)
    zeros = jnp.zeros((s, HEAD_DIM - ROPE_DIM), F32)
    z8 = jnp.zeros((s, half), F32)
    cs_h = jnp.concatenate([cos, cos, zeros + 1.0], axis=1)
    sm_h = jnp.concatenate([-sin, z8, zeros], axis=1)
    sp_h = jnp.concatenate([z8, sin, zeros], axis=1)
    ident = (jnp.ones((s, HEAD_DIM), F32), jnp.zeros((s, HEAD_DIM), F32), jnp.zeros((s, HEAD_DIM), F32))
    qk = tuple(jnp.concatenate([t, t], axis=1) for t in (cs_h, sm_h, sp_h))
    kv = tuple(jnp.concatenate([t, i], axis=1) for t, i in zip((cs_h, sm_h, sp_h), ident))
    return qk + kv


OUT_TM = 256


def _out_proj_kernel(oa_ref, ob_ref, oc_ref, x_ref, gg_ref, w_ref, g1_ref, n1_ref, n2_ref,
                     sc_ref, sh_ref, rw_ref, rb_ref, x1_ref, hf_ref, lg_ref):
    gg = gg_ref[...]
    y = jnp.concatenate([_rms(oa_ref[0], gg[:, :MOBA_W]),
                         _rms(ob_ref[0], gg[:, MOBA_W:MOBA_W + NSA_W]),
                         _rms(oc_ref[0], gg[:, MOBA_W + NSA_W:])], axis=1)
    y = _dot(y.astype(BF16), w_ref[...])
    x1 = x_ref[0] + g1_ref[0] * _rms(y, n1_ref[...])
    x1_ref[0] = x1
    hf = _rms(x1, n2_ref[...]) * (1.0 + sc_ref[0]) + sh_ref[0]
    hf_ref[0] = hf.astype(BF16)
    lg_ref[0] = jnp.dot(hf, rw_ref[...], precision=HI, preferred_element_type=F32) + rb_ref[...]


def out_projection(oa, ob, oc, x, grp_g, w_out, g1, n1, n2, sc2, sh2, rw, rb):
    b, s, d = x.shape
    tm = OUT_TM
    row = lambda w: pl.BlockSpec((1, tm, w), lambda bi, i: (bi, i, 0))
    vec = pl.BlockSpec((1, 1, d), lambda bi, i: (bi, 0, 0))
    cst = lambda r, w: pl.BlockSpec((r, w), lambda bi, i: (0, 0))
    return pl.pallas_call(
        _out_proj_kernel,
        grid=(b, s // tm),
        in_specs=[row(MOBA_W), row(NSA_W), row(SB_W), row(d), cst(1, d), cst(d, d), vec,
                  cst(1, d), cst(1, d), vec, vec, cst(d, LANES), cst(1, LANES)],
        out_specs=[row(d), row(d), row(LANES)],
        out_shape=[jax.ShapeDtypeStruct((b, s, d), F32), jax.ShapeDtypeStruct((b, s, d), BF16),
                   jax.ShapeDtypeStruct((b, s, LANES), F32)],
        compiler_params=_cparams(("arbitrary", "arbitrary")),
        name="out_projection",
    )(oa, ob, oc, x, grp_g, w_out, g1, n1, n2, sc2, sh2, rw, rb)


MOE_TM = 256


def _expert_kernel(be_ref, nu_ref, xs_ref, wgu_ref, bgu_ref, wdn_ref, bdn_ref, y_ref):
    i = pl.program_id(0)

    @pl.when(i < nu_ref[0])
    def _():
        hgu = _dot(xs_ref[...], wgu_ref[0]) + bgu_ref[0]
        de = hgu.shape[1] // 2
        glu = jnp.minimum(hgu[:, :de], SWIGLU_LIMIT)
        lin = jnp.clip(hgu[:, de:], -SWIGLU_LIMIT, SWIGLU_LIMIT)
        act = glu * jax.nn.sigmoid(SWIGLU_ALPHA * glu) * (lin + 1.0)
        y_ref[...] = _dot(act.astype(BF16), wdn_ref[0]) + bdn_ref[0]

    @pl.when(i >= nu_ref[0])
    def _():
        y_ref[...] = jnp.zeros(y_ref.shape, y_ref.dtype)


def expert_ffn(blk_exp, n_used, xs, w_gu, b_gu, w_dn, b_dn):
    r, d = xs.shape
    ne, _, wide = w_gu.shape
    tm = MOE_TM
    grid_spec = pltpu.PrefetchScalarGridSpec(
        num_scalar_prefetch=2,
        grid=(r // tm,),
        in_specs=[pl.BlockSpec((tm, d), lambda i, be, nu: (i, 0)),
                  pl.BlockSpec((1, d, wide), lambda i, be, nu: (be[i], 0, 0)),
                  pl.BlockSpec((1, 1, wide), lambda i, be, nu: (be[i], 0, 0)),
                  pl.BlockSpec((1, wide // 2, d), lambda i, be, nu: (be[i], 0, 0)),
                  pl.BlockSpec((1, 1, d), lambda i, be, nu: (be[i], 0, 0))],
        out_specs=pl.BlockSpec((tm, d), lambda i, be, nu: (i, 0)),
    )
    return pl.pallas_call(
        _expert_kernel,
        grid_spec=grid_spec,
        out_shape=jax.ShapeDtypeStruct((r, d), F32),
        compiler_params=_cparams(("arbitrary",)),
        name="expert_ffn",
    )(blk_exp, n_used, xs, w_gu, b_gu.reshape(ne, 1, wide), w_dn, b_dn.reshape(ne, 1, d))


COMB_TM = 256


def _combine_kernel(yg_ref, gate_ref, x_ref, g2_ref, n3_ref, o_ref):
    gate = gate_ref[0]
    y = gate[:, 0:1] * yg_ref[0, 0]
    for k in range(1, TOP_K):
        y = y + gate[:, k:k + 1] * yg_ref[0, k]
    o_ref[0] = x_ref[0] + g2_ref[0] * _rms(y, n3_ref[...])


def moe_combine(yg, gate, x, g2, n3):
    b, s, d = x.shape
    tm = COMB_TM
    return pl.pallas_call(
        _combine_kernel,
        grid=(b, s // tm),
        in_specs=[pl.BlockSpec((1, TOP_K, tm, d), lambda bi, i: (bi, 0, i, 0)),
                  pl.BlockSpec((1, tm, LANES), lambda bi, i: (bi, i, 0)),
                  pl.BlockSpec((1, tm, d), lambda bi, i: (bi, i, 0)),
                  pl.BlockSpec((1, 1, d), lambda bi, i: (bi, 0, 0)),
                  pl.BlockSpec((1, d), lambda bi, i: (0, 0))],
        out_specs=pl.BlockSpec((1, tm, d), lambda bi, i: (bi, i, 0)),
        out_shape=jax.ShapeDtypeStruct((b, s, d), F32),
        compiler_params=_cparams(("arbitrary", "arbitrary")),
        name="moe_combine",
    )(yg, gate, x, g2, n3)


def _route(logits):
    n = logits.shape[0]
    tm = MOE_TM
    top_val, top_idx = lax.top_k(logits, TOP_K)
    gate = jax.nn.softmax(top_val, axis=-1)
    e = top_idx.reshape(-1)
    onehot = (e[:, None] == jnp.arange(N_EXPERTS, dtype=e.dtype)[None, :]).astype(jnp.int32)
    rank = jnp.take_along_axis(jnp.cumsum(onehot, axis=0), e[:, None], axis=1)[:, 0] - 1
    counts = jnp.sum(onehot, axis=0)
    padded = (counts + tm - 1) // tm * tm
    pad_end = jnp.cumsum(padded)
    dest = (pad_end - padded)[e] + rank
    n_rows = -(-(n * TOP_K + N_EXPERTS * (tm - 1)) // tm) * tm
    n_blk = n_rows // tm
    tok = jnp.arange(n * TOP_K, dtype=jnp.int32) // TOP_K
    row_tok = jnp.zeros((n_rows,), jnp.int32).at[dest].set(tok)
    blk_exp = jnp.minimum(jnp.searchsorted(pad_end, jnp.arange(n_blk, dtype=jnp.int32) * tm,
                                           side='right'), N_EXPERTS - 1).astype(jnp.int32)
    n_used = (pad_end[-1] // tm).astype(jnp.int32).reshape(1)
    return gate, dest.reshape(n, TOP_K), row_tok, blk_exp, n_used


def _layer(x, mod, norm_g, w_in, cmp_pe, cmp_w1, cmp_w2, grp_g, w_out, router_w, router_b,
           w_gu, b_gu, w_dn, b_dn, tabs, overlap):
    b, s, d = x.shape
    sh1, sc1, g1, sh2, sc2, g2 = (m.reshape(b, 1, d) for m in jnp.split(mod, 6, axis=-1))
    ng = norm_g.reshape(4, 1, d)

    qa, ka, va, qb, kvb, gb, qc, kc, vc = in_projection(x, sc1, sh1, ng[0], _pack_w_in(w_in), tabs)
    oa = moba_attention(qa, ka, va)
    nc = s // NSA_CMP_STRIDE
    kv16 = jnp.stack([kvb[:, :, :HEAD_DIM], kvb[:, :, HEAD_DIM:2 * HEAD_DIM]], axis=1)
    kv16 = kv16.reshape(b, 2, nc, NSA_CMP_STRIDE * HEAD_DIM)
    cmp_kv = nsa_compress(kv16, cmp_pe.reshape(2, 1, NSA_CMP_LEN * HEAD_DIM), cmp_w1, cmp_w2)
    ob = nsa_attention(qb, kvb, gb, cmp_kv, overlap)
    oc = sb_attention(qc, kc, vc)

    rw = jnp.pad(router_w, ((0, 0), (0, LANES - N_EXPERTS)))
    rb = jnp.pad(router_b, (0, LANES - N_EXPERTS)).reshape(1, LANES)
    x1, hf, logits = out_projection(oa, ob, oc, x, grp_g.reshape(1, d), w_out.astype(BF16), g1,
                                    ng[1], ng[2], sc2, sh2, rw, rb)

    n = b * s
    gate, dest, row_tok, blk_exp, n_used = _route(logits.reshape(n, LANES)[:, :N_EXPERTS])
    xs = jnp.take(hf.reshape(n, d), row_tok, axis=0)
    y_rows = expert_ffn(blk_exp, n_used, xs, w_gu.astype(BF16), b_gu, w_dn.astype(BF16), b_dn)
    yg = jnp.take(y_rows, dest.reshape(b, s, TOP_K).transpose(0, 2, 1), axis=0)
    gate = jnp.pad(gate, ((0, 0), (0, LANES - TOP_K))).reshape(b, s, LANES)
    return moe_combine(yg, gate, x1, g2, ng[3])


def kernel(x, c, ada_w, ada_b, norm_g, w_in, nsa_cmp_pe, nsa_cmp_w1, nsa_cmp_w2, mix_out_g,
           w_out, router_w, router_b, exp_w_gu, exp_b_gu, exp_w_dn, exp_b_dn):
    s = x.shape[1]
    tabs = _rope_tables(s)
    overlap = _nsa_overlap(s)
    mod = ada_modulation(c, ada_w, ada_b)
    for l in range(ada_w.shape[0]):
        x = _layer(x, mod[l], norm_g[l], w_in[l], nsa_cmp_pe[l], nsa_cmp_w1[l], nsa_cmp_w2[l],
                   mix_out_g[l], w_out[l], router_w[l], router_b[l], exp_w_gu[l], exp_b_gu[l],
                   exp_w_dn[l], exp_b_dn[l], tabs, overlap)
    return x
```

```python
import functools

import numpy as np
import jax
import jax.numpy as jnp
from jax import lax
from jax.experimental import pallas as pl
from jax.experimental.pallas import tpu as pltpu

F32 = jnp.float32
BF16 = jnp.bfloat16
HI = lax.Precision.HIGHEST

D_MODEL = 1024
N_HEADS = 16
HEAD_DIM = 64
MOBA_HEADS = 4
NSA_HEADS = 6
SB_HEADS = 6
MOBA_W = MOBA_HEADS * HEAD_DIM
NSA_W = NSA_HEADS * HEAD_DIM
SB_W = SB_HEADS * HEAD_DIM
ROPE_DIM = 16
ROPE_THETA = 500000.0
MOBA_BLOCK = 256
MOBA_TOPK = 3
NSA_CMP_LEN = 32
NSA_CMP_STRIDE = 16
NSA_SLC_BLOCK = 64
NSA_SLC_TOPK = 16
NSA_WINDOW = 512
N_EXPERTS = 32
TOP_K = 4
SWIGLU_LIMIT = 7.0
SWIGLU_ALPHA = 1.702
RMS_EPS = 1e-6
NEG_INF = -1e30
SEL_FORCE = 1e4
SCALE = HEAD_DIM ** -0.5
LOG2E = 1.4426950408889634

LANES = 128
VMEM_LIMIT = 48 * 1024 * 1024


def _cparams(sem):
    return pltpu.CompilerParams(dimension_semantics=sem, vmem_limit_bytes=VMEM_LIMIT)


def _dot(a, b):
    return jnp.dot(a, b, preferred_element_type=F32)


def _dot_nt(a, b):
    return lax.dot_general(a, b, (((1,), (1,)), ((), ())), preferred_element_type=F32)


SB_TILE = 256


SB_GROUP_W = SB_W


def _sb_kernel(q_ref, k_ref, v_ref, o_ref, ks_ref, vs_ref):
    T = SB_TILE
    H = ks_ref.shape[0]
    qi = pl.program_id(2)

    @pl.when(qi == 0)
    def _():
        for h in range(H):
            ks_ref[h] = k_ref[0, :, h * HEAD_DIM:(h + 1) * HEAD_DIM].astype(BF16)
            vs_ref[h] = v_ref[0, :, h * HEAD_DIM:(h + 1) * HEAD_DIM].astype(BF16)

    row = lax.broadcasted_iota(jnp.int32, (T, T), 0)
    col = lax.broadcasted_iota(jnp.int32, (T, T), 1)
    incl = (row >= col).astype(BF16)
    sum_rhs = jnp.concatenate([incl, incl], axis=0)
    diag_mask = col < row
    qs = [(q_ref[0, :, h * HEAD_DIM:(h + 1) * HEAD_DIM] * (SCALE * LOG2E)).astype(BF16)
          for h in range(H)]
    sign = jnp.uint32(0x80000000)

    def scores(h, j):
        return _dot_nt(qs[h], ks_ref[h, pl.ds(pl.multiple_of(j * T, T), T), :])

    def neg_log2_keep(z, masked):
        neg_abs = lax.bitcast_convert_type(lax.bitcast_convert_type(z, jnp.uint32) | sign, F32)
        nlk = jnp.maximum(z, 0.0) + jnp.log2(1.0 + jnp.exp2(neg_abs))
        return jnp.where(diag_mask, nlk, 0.0) if masked else nlk

    def suffix_sums(nlk):
        hi = nlk.astype(BF16)
        lo = (nlk - hi.astype(F32)).astype(BF16)
        return _dot(jnp.concatenate([hi, lo], axis=1), sum_rhs)

    def weights(z, sums, run, masked):
        a = jnp.exp2(z - sums - run)
        return (jnp.where(diag_mask, a, 0.0) if masked else a).astype(BF16)

    def pv(h, p, j):
        return _dot(p, vs_ref[h, pl.ds(pl.multiple_of(j * T, T), T), :])

    def step(zs, runs, masked):
        sums = [suffix_sums(neg_log2_keep(zs[h], masked)) for h in range(H)]
        ps = [weights(zs[h], sums[h], runs[h], masked) for h in range(H)]
        return ps, [runs[h] + sums[h][:, 0:1] for h in range(H)]

    zeros = jnp.zeros((T, 1), F32)
    ps, runs = step([scores(h, qi) for h in range(H)], [zeros] * H, True)
    accs = [pv(h, ps[h], qi) for h in range(H)]

    def past(i, carry):
        runs, accs = carry
        j = qi - 1 - i
        ps, runs = step([scores(h, j) for h in range(H)], runs, False)
        return runs, [accs[h] + pv(h, ps[h], j) for h in range(H)]

    runs, accs = lax.fori_loop(0, qi, past, (runs, accs))
    o_ref[0] = jnp.concatenate(accs, axis=1)


def sb_attention(q, k, v):
    b, s, w = q.shape
    T = SB_TILE
    gw = SB_GROUP_W
    nh = gw // HEAD_DIM
    qspec = pl.BlockSpec((1, T, gw), lambda bi, p, i: (bi, i, p))
    kvspec = pl.BlockSpec((1, s, gw), lambda bi, p, i: (bi, 0, p))
    return pl.pallas_call(
        _sb_kernel,
        grid=(b, w // gw, s // T),
        in_specs=[qspec, kvspec, kvspec],
        out_specs=qspec,
        out_shape=jax.ShapeDtypeStruct((b, s, w), F32),
        scratch_shapes=[pltpu.VMEM((nh, s, HEAD_DIM), BF16), pltpu.VMEM((nh, s, HEAD_DIM), BF16)],
        compiler_params=_cparams(("arbitrary", "arbitrary", "arbitrary")),
        name="sb_attention",
    )(q, k, v)


def _rank_before_t(vals, n):
    idx = lax.broadcasted_iota(jnp.int32, vals.shape, 0)
    rank = jnp.zeros(vals.shape, F32)
    for j2 in range(n):
        other = vals[j2:j2 + 1, :]
        ahead = (other > vals) | ((other == vals) & (idx > j2))
        rank = rank + jnp.where(ahead, 1.0, 0.0)
    return rank


MOBA_KT = 2 * MOBA_BLOCK


def _moba_kernel(q_ref, k_ref, v_ref, o_ref, ks_ref, vs_ref, km_ref):
    T, KT = MOBA_BLOCK, MOBA_KT
    nb = km_ref.shape[1]
    s_len = ks_ref.shape[1]
    qi = pl.program_id(2)

    @pl.when(qi == 0)
    def _():
        key_blk = lax.broadcasted_iota(jnp.int32, (s_len, HEAD_DIM), 0) // T
        onehot = (lax.broadcasted_iota(jnp.int32, (s_len, HEAD_DIM), 1) == key_blk).astype(F32)
        for h in range(2):
            kh = k_ref[0, :, h * HEAD_DIM:(h + 1) * HEAD_DIM]
            ks_ref[h] = jnp.concatenate([kh, onehot], axis=1).astype(BF16)
            vs_ref[h] = v_ref[0, :, h * HEAD_DIM:(h + 1) * HEAD_DIM].astype(BF16)
            km_ref[h] = jnp.mean(kh.reshape(nb, T, HEAD_DIM), axis=1)

    row = lax.broadcasted_iota(jnp.int32, (T, T), 0)
    col = lax.broadcasted_iota(jnp.int32, (T, T), 1)
    causal = col <= row
    blk_t = lax.broadcasted_iota(jnp.int32, (nb, T), 0)
    start = pl.multiple_of(qi * T, T)

    q_aug, state = [], []
    for h in range(2):
        qf = q_ref[0, :, h * HEAD_DIM:(h + 1) * HEAD_DIM]
        qs = qf * SCALE
        gate = lax.dot_general(km_ref[h], qf, (((1,), (1,)), ((), ())),
                               precision=HI, preferred_element_type=F32)
        gate = jnp.where(blk_t < qi, gate, NEG_INF)
        sel = (_rank_before_t(gate, nb) < float(MOBA_TOPK)) & (gate > 0.5 * NEG_INF)
        selb = jnp.where(sel, 0.0, NEG_INF)
        selb = jnp.concatenate([selb, jnp.full((LANES - nb, T), NEG_INF, F32)], axis=0).T
        q_aug.append(jnp.concatenate([qs, selb[:, :HEAD_DIM]], axis=1).astype(BF16))

        s = _dot_nt(qs.astype(BF16), ks_ref[h, pl.ds(start, T), 0:HEAD_DIM])
        s = jnp.where(causal, s, NEG_INF)
        m = jnp.max(s, axis=1, keepdims=True)
        p = jnp.where(causal, jnp.exp(s - m), 0.0)
        l = jnp.sum(p, axis=1, keepdims=True)
        acc = _dot(p.astype(BF16), vs_ref[h, pl.ds(start, T), :])
        state += [m, l, acc]

    def past(i, carry):
        st = pl.multiple_of(i * KT, KT)
        out = []
        for h in range(2):
            m, l, acc = carry[3 * h:3 * h + 3]
            s = _dot_nt(q_aug[h], ks_ref[h, pl.ds(st, KT), :])
            m_new = jnp.maximum(m, jnp.max(s, axis=1, keepdims=True))
            alpha = jnp.exp(m - m_new)
            p = jnp.exp(s - m_new)
            l = alpha * l + jnp.sum(p, axis=1, keepdims=True)
            acc = alpha * acc + _dot(p.astype(BF16), vs_ref[h, pl.ds(st, KT), :])
            out += [m_new, l, acc]
        return tuple(out)

    state = lax.fori_loop(0, (qi + 1) // 2, past, tuple(state))
    outs = [state[3 * h + 2] / jnp.maximum(state[3 * h + 1], 1e-30) for h in range(2)]
    o_ref[0] = jnp.concatenate(outs, axis=1)


def moba_attention(q, k, v):
    b, s, w = q.shape
    T = MOBA_BLOCK
    nb = s // T
    assert nb <= HEAD_DIM and s % MOBA_KT == 0
    qspec = pl.BlockSpec((1, T, LANES), lambda bi, p, i: (bi, i, p))
    kvspec = pl.BlockSpec((1, s, LANES), lambda bi, p, i: (bi, 0, p))
    return pl.pallas_call(
        _moba_kernel,
        grid=(b, w // LANES, nb),
        in_specs=[qspec, kvspec, kvspec],
        out_specs=qspec,
        out_shape=jax.ShapeDtypeStruct((b, s, w), F32),
        scratch_shapes=[pltpu.VMEM((2, s, LANES), BF16), pltpu.VMEM((2, s, HEAD_DIM), BF16),
                        pltpu.VMEM((2, nb, HEAD_DIM), F32)],
        compiler_params=_cparams(("arbitrary", "arbitrary", "arbitrary")),
        name="moba_attention",
    )(q, k, v)


def _nsa_compress_kernel(x_ref, pe_ref, w1_ref, w2_ref, o_ref):
    nc = x_ref.shape[2]
    half = NSA_CMP_STRIDE * HEAD_DIM
    x = x_ref[0, 0]
    w1 = w1_ref[0]
    first = jnp.dot(x, w1[:half], precision=HI, preferred_element_type=F32)
    second = jnp.dot(x, w1[half:], precision=HI, preferred_element_type=F32)
    pe = jnp.broadcast_to(pe_ref[0], (8, 2 * half))
    peb = jnp.dot(pe, w1, precision=HI, preferred_element_type=F32)[0:1]
    pre = first + pltpu.roll(second, nc - 1, 0) + peb
    hid = pre * jax.nn.sigmoid(pre)
    o_ref[0, 0] = jnp.dot(hid, w2_ref[0], precision=HI, preferred_element_type=F32)


def nsa_compress(kv16, pe, w1, w2):
    b, _, nc, wide = kv16.shape
    return pl.pallas_call(
        _nsa_compress_kernel,
        grid=(b, 2),
        in_specs=[pl.BlockSpec((1, 1, nc, wide), lambda bi, i: (bi, i, 0, 0)),
                  pl.BlockSpec((1, 1, 2 * wide), lambda bi, i: (i, 0, 0)),
                  pl.BlockSpec((1, 2 * wide, HEAD_DIM), lambda bi, i: (i, 0, 0)),
                  pl.BlockSpec((1, HEAD_DIM, HEAD_DIM), lambda bi, i: (i, 0, 0))],
        out_specs=pl.BlockSpec((1, 1, nc, HEAD_DIM), lambda bi, i: (bi, i, 0, 0)),
        out_shape=jax.ShapeDtypeStruct((b, 2, nc, HEAD_DIM), F32),
        compiler_params=_cparams(("arbitrary", "arbitrary")),
        name="nsa_compress",
    )(kv16, pe, w1, w2)


NSA_TQ = 128
NSA_KT = 256
NSA_SPAN = NSA_WINDOW + NSA_TQ


def _softmax_rows(s, mask):
    s = jnp.where(mask, s, NEG_INF)
    m = jnp.max(s, axis=-1, keepdims=True)
    e = jnp.where(mask, jnp.exp(s - m), 0.0)
    return e / jnp.maximum(jnp.sum(e, axis=-1, keepdims=True), 1e-30)


def _nsa_kernel(q_ref, cmp_ref, slc_ref, win_ref, g_ref, ov_ref, o_ref,
                ksl_ref, vsl_ref, kw_ref, vw_ref):
    TQ, KT, H = NSA_TQ, NSA_KT, NSA_HEADS
    ns, nc = ov_ref.shape
    s_len = ksl_ref.shape[0]
    qi = pl.program_id(1)
    t0 = qi * TQ

    @pl.when(qi == 0)
    def _():
        key_blk = lax.broadcasted_iota(jnp.int32, (s_len, HEAD_DIM), 0) // NSA_SLC_BLOCK
        onehot = (lax.broadcasted_iota(jnp.int32, (s_len, HEAD_DIM), 1) == key_blk).astype(F32)
        ksl_ref[...] = jnp.concatenate([slc_ref[0, :, :HEAD_DIM], onehot], axis=1).astype(BF16)
        vsl_ref[...] = slc_ref[0, :, HEAD_DIM:].astype(BF16)
        kw_ref[...] = win_ref[0, :, :HEAD_DIM].astype(BF16)
        vw_ref[...] = win_ref[0, :, HEAD_DIM:].astype(BF16)

    qf = jnp.concatenate([q_ref[0, :, h * HEAD_DIM:(h + 1) * HEAD_DIM] for h in range(H)],
                         axis=0) * SCALE
    q = qf.astype(BF16)

    tq_c = t0 + lax.broadcasted_iota(jnp.int32, (TQ, nc), 0)
    n_c = lax.broadcasted_iota(jnp.int32, (TQ, nc), 1)
    mask_c = (n_c * NSA_CMP_STRIDE + (NSA_CMP_LEN - 1) <= tq_c) & (n_c < nc - 1)
    s_c = lax.dot_general(qf, cmp_ref[0, 0], (((1,), (1,)), ((), ())),
                          precision=HI, preferred_element_type=F32)
    p_c = _softmax_rows(s_c.reshape(H, TQ, nc), mask_c[None])
    o_c = _dot(p_c.reshape(H * TQ, nc).astype(BF16), cmp_ref[0, 1].astype(BF16))

    imp = lax.dot_general(ov_ref[...], jnp.sum(p_c, axis=0), (((1,), (1,)), ((), ())),
                          precision=HI, preferred_element_type=F32)
    tq_s = t0 + lax.broadcasted_iota(jnp.int32, (ns, TQ), 1)
    blk = lax.broadcasted_iota(jnp.int32, (ns, TQ), 0)
    own = tq_s // NSA_SLC_BLOCK
    forced = (blk == 0) | (blk == own) | (blk == own - 1)
    imp = jnp.where(forced, SEL_FORCE, imp)
    imp = jnp.where(blk <= own, imp, NEG_INF)
    sel = (_rank_before_t(imp, ns) < float(min(NSA_SLC_TOPK, ns))) & (imp > 0.5 * NEG_INF)
    selb = jnp.where(sel, 0.0, NEG_INF)
    if ns < LANES:
        selb = jnp.concatenate([selb, jnp.full((LANES - ns, TQ), NEG_INF, F32)], axis=0)
    selb = selb.T[:, :HEAD_DIM]
    q_aug = jnp.concatenate([qf, jnp.concatenate([selb] * H, axis=0)], axis=1).astype(BF16)

    jd = t0 // KT
    start = pl.multiple_of(jd * KT, KT)
    kpos = start + lax.broadcasted_iota(jnp.int32, (TQ, KT), 1)
    tq_k = t0 + lax.broadcasted_iota(jnp.int32, (TQ, KT), 0)
    s = _dot_nt(q_aug, ksl_ref[pl.ds(start, KT), :]).reshape(H, TQ, KT)
    s = jnp.where((kpos <= tq_k)[None], s, NEG_INF)
    m = jnp.max(s, axis=-1, keepdims=True)
    p = jnp.exp(s - m)
    l = jnp.sum(p, axis=-1, keepdims=True)
    acc = _dot(p.reshape(H * TQ, KT).astype(BF16), vsl_ref[pl.ds(start, KT), :])

    def past(j, carry):
        m, l, acc = carry
        st = pl.multiple_of(j * KT, KT)
        s = _dot_nt(q_aug, ksl_ref[pl.ds(st, KT), :]).reshape(H, TQ, KT)
        m_new = jnp.maximum(m, jnp.max(s, axis=-1, keepdims=True))
        alpha = jnp.exp(m - m_new)
        p = jnp.exp(s - m_new)
        l = alpha * l + jnp.sum(p, axis=-1, keepdims=True)
        pv = _dot(p.reshape(H * TQ, KT).astype(BF16), vsl_ref[pl.ds(st, KT), :])
        acc = alpha.reshape(H * TQ, 1) * acc + pv
        return m_new, l, acc

    m, l, acc = lax.fori_loop(0, jd, past, (m, l, acc))
    o_s = acc / jnp.maximum(l, 1e-30).reshape(H * TQ, 1)

    w0 = pl.multiple_of(jnp.maximum(t0 - NSA_WINDOW, 0), TQ)
    wpos = w0 + lax.broadcasted_iota(jnp.int32, (TQ, NSA_SPAN), 1)
    tq_w = t0 + lax.broadcasted_iota(jnp.int32, (TQ, NSA_SPAN), 0)
    mask_w = (wpos <= tq_w) & (wpos > tq_w - NSA_WINDOW)
    s_w = _dot_nt(q, kw_ref[pl.ds(w0, NSA_SPAN), :]).reshape(H, TQ, NSA_SPAN)
    p_w = _softmax_rows(s_w, mask_w[None])
    o_w = _dot(p_w.reshape(H * TQ, NSA_SPAN).astype(BF16), vw_ref[pl.ds(w0, NSA_SPAN), :])

    g = g_ref[0]
    outs = []
    for h in range(H):
        rows = slice(h * TQ, (h + 1) * TQ)
        outs.append(g[:, 3 * h:3 * h + 1] * o_c[rows] + g[:, 3 * h + 1:3 * h + 2] * o_s[rows]
                    + g[:, 3 * h + 2:3 * h + 3] * o_w[rows])
    o_ref[0] = jnp.concatenate(outs, axis=1)


def nsa_attention(q, kv, gates, cmp_kv, overlap):
    b, s, w = q.shape
    ns, nc = overlap.shape
    assert ns <= HEAD_DIM
    TQ = NSA_TQ
    return pl.pallas_call(
        _nsa_kernel,
        grid=(b, s // TQ),
        in_specs=[pl.BlockSpec((1, TQ, w), lambda bi, i: (bi, i, 0)),
                  pl.BlockSpec((1, 2, nc, HEAD_DIM), lambda bi, i: (bi, 0, 0, 0)),
                  pl.BlockSpec((1, s, LANES), lambda bi, i: (bi, 0, 1)),
                  pl.BlockSpec((1, s, LANES), lambda bi, i: (bi, 0, 2)),
                  pl.BlockSpec((1, TQ, LANES), lambda bi, i: (bi, i, 0)),
                  pl.BlockSpec((ns, nc), lambda bi, i: (0, 0))],
        out_specs=pl.BlockSpec((1, TQ, w), lambda bi, i: (bi, i, 0)),
        out_shape=jax.ShapeDtypeStruct((b, s, w), F32),
        scratch_shapes=[pltpu.VMEM((s, LANES), BF16)] + [pltpu.VMEM((s, HEAD_DIM), BF16)] * 3,
        compiler_params=_cparams(("arbitrary", "arbitrary")),
        name="nsa_attention",
    )(q, cmp_kv, kv, kv, gates, overlap)


def _nsa_overlap(s):
    nc = s // NSA_CMP_STRIDE
    ns = s // NSA_SLC_BLOCK
    cstart = np.arange(nc) * NSA_CMP_STRIDE
    cend = cstart + NSA_CMP_LEN - 1
    sstart = np.arange(ns) * NSA_SLC_BLOCK
    ov = (cstart[:, None] <= sstart[None, :] + NSA_SLC_BLOCK - 1) & (cend[:, None] >= sstart[None, :])
    ov[nc - 1] = False
    return jnp.asarray(ov.T.astype(np.float32))


def _mod_kernel(c_ref, w_ref, b_ref, o_ref):
    c = c_ref[...]
    act = c * jax.nn.sigmoid(c)
    o_ref[0] = jnp.dot(act, w_ref[0], precision=HI, preferred_element_type=F32) + b_ref[0]


def ada_modulation(c, ada_w, ada_b):
    nl, d, wide = ada_w.shape
    b = c.shape[0]
    tn = D_MODEL
    return pl.pallas_call(
        _mod_kernel,
        grid=(nl, wide // tn),
        in_specs=[pl.BlockSpec((b, d), lambda l, j: (0, 0)),
                  pl.BlockSpec((1, d, tn), lambda l, j: (l, 0, j)),
                  pl.BlockSpec((1, 1, tn), lambda l, j: (l, 0, j))],
        out_specs=pl.BlockSpec((1, b, tn), lambda l, j: (l, 0, j)),
        out_shape=jax.ShapeDtypeStruct((nl, b, wide), F32),
        compiler_params=_cparams(("arbitrary", "arbitrary")),
        name="ada_modulation",
    )(c, ada_w, ada_b.reshape(nl, 1, wide))


def _rms(x, g):
    return x * lax.rsqrt(jnp.mean(x * x, axis=-1, keepdims=True) + RMS_EPS) * g


_GATE_PAD = LANES
_COLS = {}
_off = 0
for _name, _w in (("qa", MOBA_W), ("ka", MOBA_W), ("va", MOBA_W), ("qb", NSA_W), ("kvb", NSA_W),
                  ("gb", _GATE_PAD), ("qc", SB_W), ("kc", SB_W), ("vc", SB_W)):
    _COLS[_name] = (_off, _w)
    _off += _w
IN_W_PACKED = _off
PROJ_TM = 256


def _rope_block(p, cs, sm, sp):
    return p * cs + pltpu.roll(p, LANES - ROPE_DIM // 2, 1) * sm + pltpu.roll(p, ROPE_DIM // 2, 1) * sp


def _in_proj_kernel(x_ref, sc_ref, sh_ref, g_ref, w_ref, cqk_ref, mqk_ref, pqk_ref,
                    ckv_ref, mkv_ref, pkv_ref,
                    qa_ref, ka_ref, va_ref, qb_ref, kvb_ref, gb_ref, qc_ref, kc_ref, vc_ref):
    hm = _rms(x_ref[0], g_ref[...]) * (1.0 + sc_ref[0]) + sh_ref[0]
    p = _dot(hm.astype(BF16), w_ref[...])
    qk = (cqk_ref[...], mqk_ref[...], pqk_ref[...])
    kv = (ckv_ref[...], mkv_ref[...], pkv_ref[...])

    def emit(ref, name, tabs):
        off, w = _COLS[name]
        for j in range(w // LANES):
            blk = p[:, off + j * LANES: off + (j + 1) * LANES]
            if tabs is not None:
                blk = _rope_block(blk, *tabs)
            ref[0, :, j * LANES:(j + 1) * LANES] = blk

    emit(qa_ref, "qa", qk)
    emit(ka_ref, "ka", qk)
    emit(va_ref, "va", None)
    emit(qb_ref, "qb", qk)
    emit(kvb_ref, "kvb", kv)
    emit(qc_ref, "qc", None)
    emit(kc_ref, "kc", None)
    emit(vc_ref, "vc", None)
    off, w = _COLS["gb"]
    gb_ref[0] = jax.nn.sigmoid(p[:, off:off + w])


def in_projection(x, sc, sh, g, w_packed, tabs):
    b, s, d = x.shape
    tm = PROJ_TM
    row = lambda w: pl.BlockSpec((1, tm, w), lambda bi, i: (bi, i, 0))
    vec = pl.BlockSpec((1, 1, d), lambda bi, i: (bi, 0, 0))
    tab = pl.BlockSpec((tm, LANES), lambda bi, i: (i, 0))
    names = ("qa", "ka", "va", "qb", "kvb", "gb", "qc", "kc", "vc")
    return pl.pallas_call(
        _in_proj_kernel,
        grid=(b, s // tm),
        in_specs=[row(d), vec, vec, pl.BlockSpec((1, d), lambda bi, i: (0, 0)),
                  pl.BlockSpec((d, IN_W_PACKED), lambda bi, i: (0, 0))] + [tab] * 6,
        out_specs=[row(_COLS[n][1]) for n in names],
        out_shape=[jax.ShapeDtypeStruct((b, s, _COLS[n][1]), F32) for n in names],
        compiler_params=_cparams(("arbitrary", "arbitrary")),
        name="in_projection",
    )(x, sc, sh, g, w_packed, *tabs)


def _pack_w_in(w_in):
    widths = (MOBA_W, MOBA_W, MOBA_W, NSA_W, NSA_W, 3 * NSA_HEADS, SB_W, SB_W, SB_W)
    offs = np.cumsum((0,) + widths)
    parts = []
    for i, w in enumerate(widths):
        blk = w_in[:, offs[i]:offs[i + 1]]
        if w == 3 * NSA_HEADS:
            blk = jnp.pad(blk, ((0, 0), (0, _GATE_PAD - w)))
        parts.append(blk)
    return jnp.concatenate(parts, axis=1).astype(BF16)


def _rope_tables(s):
    half = ROPE_DIM // 2
    inv_freq = ROPE_THETA ** (-jnp.arange(0, ROPE_DIM, 2, dtype=F32) / ROPE_DIM)
    ang = jnp.arange(s, dtype=F32)[:, None] * inv_freq[None, :]
    cos, sin = jnp.cos(ang), jnp.sin(ang)
    zeros = jnp.zeros((s, HEAD_DIM - ROPE_DIM), F32)
    z8 = jnp.zeros((s, half), F32)
    cs_h = jnp.concatenate([cos, cos, zeros + 1.0], axis=1)
    sm_h = jnp.concatenate([-sin, z8, zeros], axis=1)
    sp_h = jnp.concatenate([z8, sin, zeros], axis=1)
    ident = (jnp.ones((s, HEAD_DIM), F32), jnp.zeros((s, HEAD_DIM), F32), jnp.zeros((s, HEAD_DIM), F32))
    qk = tuple(jnp.concatenate([t, t], axis=1) for t in (cs_h, sm_h, sp_h))
    kv = tuple(jnp.concatenate([t, i], axis=1) for t, i in zip((cs_h, sm_h, sp_h), ident))
    return qk + kv


OUT_TM = 256


def _out_proj_kernel(oa_ref, ob_ref, oc_ref, x_ref, gg_ref, w_ref, g1_ref, n1_ref, n2_ref,
                     sc_ref, sh_ref, rw_ref, rb_ref, x1_ref, hf_ref, lg_ref):
    gg = gg_ref[...]
    y = jnp.concatenate([_rms(oa_ref[0], gg[:, :MOBA_W]),
                         _rms(ob_ref[0], gg[:, MOBA_W:MOBA_W + NSA_W]),
                         _rms(oc_ref[0], gg[:, MOBA_W + NSA_W:])], axis=1)
    y = _dot(y.astype(BF16), w_ref[...])
    x1 = x_ref[0] + g1_ref[0] * _rms(y, n1_ref[...])
    x1_ref[0] = x1
    hf = _rms(x1, n2_ref[...]) * (1.0 + sc_ref[0]) + sh_ref[0]
    hf_ref[0] = hf.astype(BF16)
    lg_ref[...] = lax.dot_general(rw_ref[...], hf, (((1,), (1,)), ((), ())),
                                  precision=HI, preferred_element_type=F32) + rb_ref[...]


def out_projection(oa, ob, oc, x, grp_g, w_out, g1, n1, n2, sc2, sh2, rw_t, rb):
    b, s, d = x.shape
    ne = rw_t.shape[0]
    tm = OUT_TM
    steps = s // tm
    row = lambda w: pl.BlockSpec((1, tm, w), lambda bi, i: (bi, i, 0))
    vec = pl.BlockSpec((1, 1, d), lambda bi, i: (bi, 0, 0))
    cst = lambda r, w: pl.BlockSpec((r, w), lambda bi, i: (0, 0))
    return pl.pallas_call(
        _out_proj_kernel,
        grid=(b, steps),
        in_specs=[row(MOBA_W), row(NSA_W), row(SB_W), row(d), cst(1, d), cst(d, d), vec,
                  cst(1, d), cst(1, d), vec, vec, cst(ne, d), cst(ne, 1)],
        out_specs=[row(d), row(d), pl.BlockSpec((ne, tm), lambda bi, i: (0, bi * steps + i))],
        out_shape=[jax.ShapeDtypeStruct((b, s, d), F32), jax.ShapeDtypeStruct((b, s, d), BF16),
                   jax.ShapeDtypeStruct((ne, b * s), F32)],
        compiler_params=_cparams(("arbitrary", "arbitrary")),
        name="out_projection",
    )(oa, ob, oc, x, grp_g, w_out, g1, n1, n2, sc2, sh2, rw_t, rb)


ROUTE_TT = 512


def _router_kernel(lg_ref, idx_ref, gate_ref, rank_ref, cnt_ref):
    ne, tt = lg_ref.shape

    @pl.when(pl.program_id(0) == 0)
    def _():
        cnt_ref[...] = jnp.zeros(cnt_ref.shape, cnt_ref.dtype)

    v = lg_ref[...]
    erow = lax.broadcasted_iota(jnp.int32, (ne, tt), 0)
    vals, hots, firsts = [], [], []
    for _ in range(TOP_K):
        m = jnp.max(v, axis=0, keepdims=True)
        first = jnp.min(jnp.where(v == m, erow, ne), axis=0, keepdims=True)
        hot = erow == first
        v = jnp.where(hot, -jnp.inf, v)
        vals.append(m)
        hots.append(hot)
        firsts.append(first)
    exps = [jnp.exp(val - vals[0]) for val in vals]
    den = exps[0] + exps[1] + exps[2] + exps[3]
    gate_ref[...] = jnp.concatenate([e / den for e in exps], axis=0)
    idx_ref[...] = jnp.concatenate(firsts, axis=0)

    cnt = jnp.zeros((ne, tt), F32)
    for hot in hots:
        cnt = cnt + jnp.where(hot, 1.0, 0.0)
    before = (lax.broadcasted_iota(jnp.int32, (tt, tt), 0)
              < lax.broadcasted_iota(jnp.int32, (tt, tt), 1)).astype(BF16)
    excl = _dot(cnt.astype(BF16), before) + cnt_ref[:, 0:1]
    ranks = [jnp.sum(jnp.where(hot, excl, 0.0), axis=0, keepdims=True) for hot in hots]
    rank_ref[...] = jnp.concatenate(ranks, axis=0).astype(jnp.int32)
    cnt_ref[...] = cnt_ref[...] + jnp.sum(cnt, axis=1, keepdims=True)


def moe_router(logits_t):
    ne, n = logits_t.shape
    tt = ROUTE_TT
    slot = pl.BlockSpec((TOP_K, tt), lambda i: (0, i))
    return pl.pallas_call(
        _router_kernel,
        grid=(n // tt,),
        in_specs=[pl.BlockSpec((ne, tt), lambda i: (0, i))],
        out_specs=[slot, slot, slot, pl.BlockSpec((ne, LANES), lambda i: (0, 0))],
        out_shape=[jax.ShapeDtypeStruct((TOP_K, n), jnp.int32), jax.ShapeDtypeStruct((TOP_K, n), F32),
                   jax.ShapeDtypeStruct((TOP_K, n), jnp.int32), jax.ShapeDtypeStruct((ne, LANES), F32)],
        compiler_params=_cparams(("arbitrary",)),
        name="moe_router",
    )(logits_t)


MOE_TM = 256


def _expert_kernel(be_ref, nu_ref, xs_ref, wgu_ref, bgu_ref, wdn_ref, bdn_ref, y_ref,
                   wgu_bf, wdn_bf):
    i = pl.program_id(0)
    used = i < nu_ref[0]

    @pl.when(used & ((i == 0) | (be_ref[i] != be_ref[jnp.maximum(i - 1, 0)])))
    def _():
        wgu_bf[...] = wgu_ref[0].astype(BF16)
        wdn_bf[...] = wdn_ref[0].astype(BF16)

    @pl.when(used)
    def _():
        hgu = _dot(xs_ref[...], wgu_bf[...]) + bgu_ref[0]
        de = hgu.shape[1] // 2
        glu = jnp.minimum(hgu[:, :de], SWIGLU_LIMIT)
        lin = jnp.clip(hgu[:, de:], -SWIGLU_LIMIT, SWIGLU_LIMIT)
        act = glu * jax.nn.sigmoid(SWIGLU_ALPHA * glu) * (lin + 1.0)
        y_ref[...] = _dot(act.astype(BF16), wdn_bf[...]) + bdn_ref[0]

    @pl.when(jnp.logical_not(used))
    def _():
        y_ref[...] = jnp.zeros(y_ref.shape, y_ref.dtype)


EXPERT_VMEM_LIMIT = 56 * 1024 * 1024


def expert_ffn(blk_exp, n_used, xs, w_gu, b_gu, w_dn, b_dn):
    r, d = xs.shape
    ne, _, wide = w_gu.shape
    tm = MOE_TM
    grid_spec = pltpu.PrefetchScalarGridSpec(
        num_scalar_prefetch=2,
        grid=(r // tm,),
        in_specs=[pl.BlockSpec((tm, d), lambda i, be, nu: (i, 0)),
                  pl.BlockSpec((1, d, wide), lambda i, be, nu: (be[i], 0, 0)),
                  pl.BlockSpec((1, 1, wide), lambda i, be, nu: (be[i], 0, 0)),
                  pl.BlockSpec((1, wide // 2, d), lambda i, be, nu: (be[i], 0, 0)),
                  pl.BlockSpec((1, 1, d), lambda i, be, nu: (be[i], 0, 0))],
        out_specs=pl.BlockSpec((tm, d), lambda i, be, nu: (i, 0)),
        scratch_shapes=[pltpu.VMEM((d, wide), BF16), pltpu.VMEM((wide // 2, d), BF16)],
    )
    return pl.pallas_call(
        _expert_kernel,
        grid_spec=grid_spec,
        out_shape=jax.ShapeDtypeStruct((r, d), F32),
        compiler_params=pltpu.CompilerParams(dimension_semantics=("arbitrary",),
                                             vmem_limit_bytes=EXPERT_VMEM_LIMIT),
        name="expert_ffn",
    )(blk_exp, n_used, xs, w_gu, b_gu.reshape(ne, 1, wide), w_dn, b_dn.reshape(ne, 1, d))


COMB_TM = 256


def _combine_kernel(yg_ref, gate_ref, x_ref, g2_ref, n3_ref, o_ref):
    gate = gate_ref[0]
    y = gate[:, 0:1] * yg_ref[0, 0]
    for k in range(1, TOP_K):
        y = y + gate[:, k:k + 1] * yg_ref[0, k]
    o_ref[0] = x_ref[0] + g2_ref[0] * _rms(y, n3_ref[...])


def moe_combine(yg, gate, x, g2, n3):
    b, s, d = x.shape
    tm = COMB_TM
    return pl.pallas_call(
        _combine_kernel,
        grid=(b, s // tm),
        in_specs=[pl.BlockSpec((1, TOP_K, tm, d), lambda bi, i: (bi, 0, i, 0)),
                  pl.BlockSpec((1, tm, LANES), lambda bi, i: (bi, i, 0)),
                  pl.BlockSpec((1, tm, d), lambda bi, i: (bi, i, 0)),
                  pl.BlockSpec((1, 1, d), lambda bi, i: (bi, 0, 0)),
                  pl.BlockSpec((1, d), lambda bi, i: (0, 0))],
        out_specs=pl.BlockSpec((1, tm, d), lambda bi, i: (bi, i, 0)),
        out_shape=jax.ShapeDtypeStruct((b, s, d), F32),
        compiler_params=_cparams(("arbitrary", "arbitrary")),
        name="moe_combine",
    )(yg, gate, x, g2, n3)


def _dispatch_plan(idx, rank, counts):
    n = idx.shape[1]
    tm = MOE_TM
    padded = (counts + tm - 1) // tm * tm
    pad_end = jnp.cumsum(padded)
    start = pad_end - padded
    experts = jnp.arange(N_EXPERTS, dtype=jnp.int32)
    dest = rank + jnp.sum(jnp.where(idx[..., None] == experts, start, 0), axis=-1)
    n_rows = -(-(n * TOP_K + N_EXPERTS * (tm - 1)) // tm) * tm
    n_blk = n_rows // tm
    blk_start = jnp.arange(n_blk, dtype=jnp.int32) * tm
    blk_exp = jnp.minimum(jnp.sum((pad_end[None, :] <= blk_start[:, None]).astype(jnp.int32), axis=1),
                          N_EXPERTS - 1)
    n_used = (pad_end[-1] // tm).astype(jnp.int32).reshape(1)
    return dest, blk_exp, n_used, n_rows


def _layer(x, mod, norm_g, w_in, cmp_pe, cmp_w1, cmp_w2, grp_g, w_out, router_w, router_b,
           w_gu, b_gu, w_dn, b_dn, tabs, overlap):
    b, s, d = x.shape
    sh1, sc1, g1, sh2, sc2, g2 = (m.reshape(b, 1, d) for m in jnp.split(mod, 6, axis=-1))
    ng = norm_g.reshape(4, 1, d)

    qa, ka, va, qb, kvb, gb, qc, kc, vc = in_projection(x, sc1, sh1, ng[0], _pack_w_in(w_in), tabs)
    oa = moba_attention(qa, ka, va)
    nc = s // NSA_CMP_STRIDE
    kv16 = jnp.stack([kvb[:, :, :HEAD_DIM], kvb[:, :, HEAD_DIM:2 * HEAD_DIM]], axis=1)
    kv16 = kv16.reshape(b, 2, nc, NSA_CMP_STRIDE * HEAD_DIM)
    cmp_kv = nsa_compress(kv16, cmp_pe.reshape(2, 1, NSA_CMP_LEN * HEAD_DIM), cmp_w1, cmp_w2)
    ob = nsa_attention(qb, kvb, gb, cmp_kv, overlap)
    oc = sb_attention(qc, kc, vc)

    x1, hf, logits_t = out_projection(oa, ob, oc, x, grp_g.reshape(1, d), w_out.astype(BF16), g1,
                                      ng[1], ng[2], sc2, sh2, router_w.T, router_b.reshape(-1, 1))

    n = b * s
    idx, gate, rank, counts = moe_router(logits_t)
    dest, blk_exp, n_used, n_rows = _dispatch_plan(idx, rank, counts[:, 0].astype(jnp.int32))
    tok = jnp.tile(jnp.arange(n, dtype=jnp.int32), TOP_K)
    row_tok = jnp.zeros((n_rows,), jnp.int32).at[dest.reshape(-1)].set(tok)
    xs = jnp.take(hf.reshape(n, d), row_tok, axis=0)
    y_rows = expert_ffn(blk_exp, n_used, xs, w_gu, b_gu, w_dn, b_dn)
    yg = jnp.take(y_rows, dest.reshape(TOP_K, b, s).transpose(1, 0, 2), axis=0)
    gate = jnp.pad(gate.T, ((0, 0), (0, LANES - TOP_K))).reshape(b, s, LANES)
    return moe_combine(yg, gate, x1, g2, ng[3])


def kernel(x, c, ada_w, ada_b, norm_g, w_in, nsa_cmp_pe, nsa_cmp_w1, nsa_cmp_w2, mix_out_g,
           w_out, router_w, router_b, exp_w_gu, exp_b_gu, exp_w_dn, exp_b_dn):
    s = x.shape[1]
    tabs = _rope_tables(s)
    overlap = _nsa_overlap(s)
    mod = ada_modulation(c, ada_w, ada_b)
    for l in range(ada_w.shape[0]):
        x = _layer(x, mod[l], norm_g[l], w_in[l], nsa_cmp_pe[l], nsa_cmp_w1[l], nsa_cmp_w2[l],
                   mix_out_g[l], w_out[l], router_w[l], router_b[l], exp_w_gu[l], exp_b_gu[l],
                   exp_w_dn[l], exp_b_dn[l], tabs, overlap)
    return x
```

```python
import functools

import numpy as np
import jax
import jax.numpy as jnp
from jax import lax
from jax.experimental import pallas as pl
from jax.experimental.pallas import tpu as pltpu

F32 = jnp.float32
BF16 = jnp.bfloat16
HI = lax.Precision.HIGHEST

D_MODEL = 1024
N_HEADS = 16
HEAD_DIM = 64
MOBA_HEADS = 4
NSA_HEADS = 6
SB_HEADS = 6
MOBA_W = MOBA_HEADS * HEAD_DIM
NSA_W = NSA_HEADS * HEAD_DIM
SB_W = SB_HEADS * HEAD_DIM
ROPE_DIM = 16
ROPE_THETA = 500000.0
MOBA_BLOCK = 256
MOBA_TOPK = 3
NSA_CMP_LEN = 32
NSA_CMP_STRIDE = 16
NSA_SLC_BLOCK = 64
NSA_SLC_TOPK = 16
NSA_WINDOW = 512
N_EXPERTS = 32
TOP_K = 4
SWIGLU_LIMIT = 7.0
SWIGLU_ALPHA = 1.702
RMS_EPS = 1e-6
NEG_INF = -1e30
SEL_FORCE = 1e4
SCALE = HEAD_DIM ** -0.5
LOG2E = 1.4426950408889634

LANES = 128
VMEM_LIMIT = 48 * 1024 * 1024


def _cparams(sem):
    return pltpu.CompilerParams(dimension_semantics=sem, vmem_limit_bytes=VMEM_LIMIT)


def _dot(a, b):
    return jnp.dot(a, b, preferred_element_type=F32)


def _dot_nt(a, b):
    return lax.dot_general(a, b, (((1,), (1,)), ((), ())), preferred_element_type=F32)


SB_TILE = 256


SB_GROUP_W = SB_W


def _sb_kernel(q_ref, k_ref, v_ref, o_ref, ks_ref, vs_ref):
    T = SB_TILE
    H = ks_ref.shape[0]
    qi = pl.program_id(2)

    @pl.when(qi == 0)
    def _():
        for h in range(H):
            ks_ref[h] = k_ref[0, :, h * HEAD_DIM:(h + 1) * HEAD_DIM].astype(BF16)
            vs_ref[h] = v_ref[0, :, h * HEAD_DIM:(h + 1) * HEAD_DIM].astype(BF16)

    row = lax.broadcasted_iota(jnp.int32, (T, T), 0)
    col = lax.broadcasted_iota(jnp.int32, (T, T), 1)
    incl = (row >= col).astype(BF16)
    sum_rhs = jnp.concatenate([incl, incl], axis=0)
    diag_mask = col < row
    qs = [(q_ref[0, :, h * HEAD_DIM:(h + 1) * HEAD_DIM] * (SCALE * LOG2E)).astype(BF16)
          for h in range(H)]
    sign = jnp.uint32(0x80000000)

    def scores(h, j):
        return _dot_nt(qs[h], ks_ref[h, pl.ds(pl.multiple_of(j * T, T), T), :])

    def neg_log2_keep(z, masked):
        neg_abs = lax.bitcast_convert_type(lax.bitcast_convert_type(z, jnp.uint32) | sign, F32)
        nlk = jnp.maximum(z, 0.0) + jnp.log2(1.0 + jnp.exp2(neg_abs))
        return jnp.where(diag_mask, nlk, 0.0) if masked else nlk

    def suffix_sums(nlk):
        hi = nlk.astype(BF16)
        lo = (nlk - hi.astype(F32)).astype(BF16)
        return _dot(jnp.concatenate([hi, lo], axis=1), sum_rhs)

    def weights(z, sums, run, masked):
        a = jnp.exp2(z - sums - run)
        return (jnp.where(diag_mask, a, 0.0) if masked else a).astype(BF16)

    def pv(h, p, j):
        return _dot(p, vs_ref[h, pl.ds(pl.multiple_of(j * T, T), T), :])

    def step(zs, runs, masked):
        sums = [suffix_sums(neg_log2_keep(zs[h], masked)) for h in range(H)]
        ps = [weights(zs[h], sums[h], runs[h], masked) for h in range(H)]
        return ps, [runs[h] + sums[h][:, 0:1] for h in range(H)]

    zeros = jnp.zeros((T, 1), F32)
    ps, runs = step([scores(h, qi) for h in range(H)], [zeros] * H, True)
    accs = [pv(h, ps[h], qi) for h in range(H)]

    def past(i, carry):
        runs, accs = carry
        j = qi - 1 - i
        ps, runs = step([scores(h, j) for h in range(H)], runs, False)
        return runs, [accs[h] + pv(h, ps[h], j) for h in range(H)]

    runs, accs = lax.fori_loop(0, qi, past, (runs, accs))
    o_ref[0] = jnp.concatenate(accs, axis=1)


def sb_attention(q, k, v):
    b, s, w = q.shape
    T = SB_TILE
    gw = SB_GROUP_W
    nh = gw // HEAD_DIM
    qspec = pl.BlockSpec((1, T, gw), lambda bi, p, i: (bi, i, p))
    kvspec = pl.BlockSpec((1, s, gw), lambda bi, p, i: (bi, 0, p))
    return pl.pallas_call(
        _sb_kernel,
        grid=(b, w // gw, s // T),
        in_specs=[qspec, kvspec, kvspec],
        out_specs=qspec,
        out_shape=jax.ShapeDtypeStruct((b, s, w), F32),
        scratch_shapes=[pltpu.VMEM((nh, s, HEAD_DIM), BF16), pltpu.VMEM((nh, s, HEAD_DIM), BF16)],
        compiler_params=_cparams(("arbitrary", "arbitrary", "arbitrary")),
        name="sb_attention",
    )(q, k, v)


def _rank_before_t(vals, n):
    idx = lax.broadcasted_iota(jnp.int32, vals.shape, 0)
    rank = jnp.zeros(vals.shape, F32)
    for j2 in range(n):
        other = vals[j2:j2 + 1, :]
        ahead = (other > vals) | ((other == vals) & (idx > j2))
        rank = rank + jnp.where(ahead, 1.0, 0.0)
    return rank


MOBA_KT = 2 * MOBA_BLOCK
MOBA_GROUP_W = MOBA_W


def _moba_kernel(q_ref, k_ref, v_ref, o_ref, ks_ref, vs_ref, km_ref):
    T, KT = MOBA_BLOCK, MOBA_KT
    H = ks_ref.shape[0]
    nb = km_ref.shape[1]
    s_len = ks_ref.shape[1]
    qi = pl.program_id(2)

    @pl.when(qi == 0)
    def _():
        key_blk = lax.broadcasted_iota(jnp.int32, (s_len, HEAD_DIM), 0) // T
        onehot = (lax.broadcasted_iota(jnp.int32, (s_len, HEAD_DIM), 1) == key_blk).astype(F32)
        ones_col = (lax.broadcasted_iota(jnp.int32, (s_len, HEAD_DIM), 1) == 0).astype(F32)
        for h in range(H):
            kh = k_ref[0, :, h * HEAD_DIM:(h + 1) * HEAD_DIM]
            ks_ref[h] = jnp.concatenate([kh, onehot], axis=1).astype(BF16)
            vs_ref[h] = jnp.concatenate([v_ref[0, :, h * HEAD_DIM:(h + 1) * HEAD_DIM], ones_col],
                                        axis=1).astype(BF16)
            km_ref[h] = jnp.mean(kh.reshape(nb, T, HEAD_DIM), axis=1)

    row = lax.broadcasted_iota(jnp.int32, (T, T), 0)
    col = lax.broadcasted_iota(jnp.int32, (T, T), 1)
    causal = col <= row
    blk_t = lax.broadcasted_iota(jnp.int32, (nb, T), 0)
    start = pl.multiple_of(qi * T, T)

    qfs = [q_ref[0, :, h * HEAD_DIM:(h + 1) * HEAD_DIM] for h in range(H)]
    qss = [qf * SCALE for qf in qfs]
    s_own = [_dot_nt(qss[h].astype(BF16), ks_ref[h, pl.ds(start, T), 0:HEAD_DIM]) for h in range(H)]
    q_aug, state = [], []
    for h in range(H):
        gate = lax.dot_general(km_ref[h], qfs[h], (((1,), (1,)), ((), ())),
                               precision=HI, preferred_element_type=F32)
        gate = jnp.where(blk_t < qi, gate, NEG_INF)
        sel = (_rank_before_t(gate, nb) < float(MOBA_TOPK)) & (gate > 0.5 * NEG_INF)
        selb = jnp.where(sel, 0.0, NEG_INF)
        selb = jnp.concatenate([selb, jnp.full((LANES - nb, T), NEG_INF, F32)], axis=0).T
        q_aug.append(jnp.concatenate([qss[h], selb[:, :HEAD_DIM]], axis=1).astype(BF16))

        s = jnp.where(causal, s_own[h], NEG_INF)
        m = jnp.max(s, axis=1, keepdims=True)
        p = jnp.exp((s - m).astype(BF16))
        state += [m, _dot(p, vs_ref[h, pl.ds(start, T), :])]

    def past(i, carry):
        st = pl.multiple_of(i * KT, KT)
        ss = [_dot_nt(q_aug[h], ks_ref[h, pl.ds(st, KT), :]) for h in range(H)]
        out = []
        for h in range(H):
            m, acc = carry[2 * h:2 * h + 2]
            m_new = jnp.maximum(m, jnp.max(ss[h], axis=1, keepdims=True))
            alpha = jnp.exp(m - m_new)
            p = jnp.exp((ss[h] - m_new).astype(BF16))
            out += [m_new, alpha * acc + _dot(p, vs_ref[h, pl.ds(st, KT), :])]
        return tuple(out)

    state = lax.fori_loop(0, (qi + 1) // 2, past, tuple(state))
    outs = [state[2 * h + 1][:, :HEAD_DIM] / jnp.maximum(state[2 * h + 1][:, HEAD_DIM:HEAD_DIM + 1], 1e-30)
            for h in range(H)]
    o_ref[0] = jnp.concatenate(outs, axis=1)


def moba_attention(q, k, v):
    b, s, w = q.shape
    T = MOBA_BLOCK
    nb = s // T
    assert nb <= HEAD_DIM and s % MOBA_KT == 0
    gw = MOBA_GROUP_W
    nh = gw // HEAD_DIM
    qspec = pl.BlockSpec((1, T, gw), lambda bi, p, i: (bi, i, p))
    kvspec = pl.BlockSpec((1, s, gw), lambda bi, p, i: (bi, 0, p))
    return pl.pallas_call(
        _moba_kernel,
        grid=(b, w // gw, nb),
        in_specs=[qspec, kvspec, kvspec],
        out_specs=qspec,
        out_shape=jax.ShapeDtypeStruct((b, s, w), F32),
        scratch_shapes=[pltpu.VMEM((nh, s, LANES), BF16), pltpu.VMEM((nh, s, LANES), BF16),
                        pltpu.VMEM((nh, nb, HEAD_DIM), F32)],
        compiler_params=_cparams(("arbitrary", "arbitrary", "arbitrary")),
        name="moba_attention",
    )(q, k, v)


def _nsa_compress_kernel(x_ref, pe_ref, w1_ref, w2_ref, o_ref):
    nc = x_ref.shape[2]
    half = NSA_CMP_STRIDE * HEAD_DIM
    x = x_ref[0, 0]
    w1 = w1_ref[0]
    first = jnp.dot(x, w1[:half], precision=HI, preferred_element_type=F32)
    second = jnp.dot(x, w1[half:], precision=HI, preferred_element_type=F32)
    pe = jnp.broadcast_to(pe_ref[0], (8, 2 * half))
    peb = jnp.dot(pe, w1, precision=HI, preferred_element_type=F32)[0:1]
    pre = first + pltpu.roll(second, nc - 1, 0) + peb
    hid = pre * jax.nn.sigmoid(pre)
    o_ref[0, 0] = jnp.dot(hid, w2_ref[0], precision=HI, preferred_element_type=F32)


def nsa_compress(kv16, pe, w1, w2):
    b, _, nc, wide = kv16.shape
    return pl.pallas_call(
        _nsa_compress_kernel,
        grid=(b, 2),
        in_specs=[pl.BlockSpec((1, 1, nc, wide), lambda bi, i: (bi, i, 0, 0)),
                  pl.BlockSpec((1, 1, 2 * wide), lambda bi, i: (i, 0, 0)),
                  pl.BlockSpec((1, 2 * wide, HEAD_DIM), lambda bi, i: (i, 0, 0)),
                  pl.BlockSpec((1, HEAD_DIM, HEAD_DIM), lambda bi, i: (i, 0, 0))],
        out_specs=pl.BlockSpec((1, 1, nc, HEAD_DIM), lambda bi, i: (bi, i, 0, 0)),
        out_shape=jax.ShapeDtypeStruct((b, 2, nc, HEAD_DIM), F32),
        compiler_params=_cparams(("arbitrary", "arbitrary")),
        name="nsa_compress",
    )(kv16, pe, w1, w2)


NSA_TQ = 128
NSA_KT = 512
NSA_SPAN = NSA_WINDOW + NSA_TQ
NSA_CHAIN_HEADS = 2


def _softmax_rows(s, mask):
    s = jnp.where(mask, s, NEG_INF)
    m = jnp.max(s, axis=-1, keepdims=True)
    e = jnp.where(mask, jnp.exp(s - m), 0.0)
    return e / jnp.maximum(jnp.sum(e, axis=-1, keepdims=True), 1e-30)


def _split_bf16(x):
    hi = x.astype(BF16)
    return hi, (x - hi.astype(F32)).astype(BF16)


def _nsa_kernel(q_ref, cmp_ref, slc_ref, win_ref, g_ref, ov_ref, o_ref,
                ksl_ref, vsl_ref, kw_ref, vw_ref, kc_ref):
    TQ, KT, H = NSA_TQ, NSA_KT, NSA_HEADS
    ns, nc = ov_ref.shape
    s_len = ksl_ref.shape[0]
    qi = pl.program_id(1)
    t0 = qi * TQ

    @pl.when(qi == 0)
    def _():
        lane = lax.broadcasted_iota(jnp.int32, (s_len, HEAD_DIM), 1)
        key_blk = lax.broadcasted_iota(jnp.int32, (s_len, HEAD_DIM), 0) // NSA_SLC_BLOCK
        onehot = (lane == key_blk).astype(F32)
        ones_col = (lane == 0).astype(F32)
        ksl_ref[...] = jnp.concatenate([slc_ref[0, :, :HEAD_DIM], onehot], axis=1).astype(BF16)
        vsl_ref[...] = jnp.concatenate([slc_ref[0, :, HEAD_DIM:], ones_col], axis=1).astype(BF16)
        kw_ref[...] = win_ref[0, :, :HEAD_DIM].astype(BF16)
        vw_ref[...] = jnp.concatenate([win_ref[0, :, HEAD_DIM:], ones_col], axis=1).astype(BF16)
        k_hi, k_lo = _split_bf16(cmp_ref[0, 0])
        kc_ref[...] = jnp.concatenate([k_hi, k_hi, k_lo], axis=1)

    qf = jnp.concatenate([q_ref[0, :, h * HEAD_DIM:(h + 1) * HEAD_DIM] for h in range(H)],
                         axis=0) * SCALE
    q, q_lo = _split_bf16(qf)

    tq_c = t0 + lax.broadcasted_iota(jnp.int32, (TQ, nc), 0)
    n_c = lax.broadcasted_iota(jnp.int32, (TQ, nc), 1)
    mask_c = (n_c * NSA_CMP_STRIDE + (NSA_CMP_LEN - 1) <= tq_c) & (n_c < nc - 1)
    s_c = _dot_nt(jnp.concatenate([q, q_lo, q], axis=1), kc_ref[...])
    p_c = _softmax_rows(s_c.reshape(H, TQ, nc), mask_c[None])
    o_c = _dot(p_c.reshape(H * TQ, nc).astype(BF16), cmp_ref[0, 1].astype(BF16))

    imp = lax.dot_general(ov_ref[...], jnp.sum(p_c, axis=0), (((1,), (1,)), ((), ())),
                          precision=HI, preferred_element_type=F32)
    tq_s = t0 + lax.broadcasted_iota(jnp.int32, (ns, TQ), 1)
    blk = lax.broadcasted_iota(jnp.int32, (ns, TQ), 0)
    own = tq_s // NSA_SLC_BLOCK
    forced = (blk == 0) | (blk == own) | (blk == own - 1)
    imp = jnp.where(forced, SEL_FORCE, imp)
    imp = jnp.where(blk <= own, imp, NEG_INF)
    sel = (_rank_before_t(imp, ns) < float(min(NSA_SLC_TOPK, ns))) & (imp > 0.5 * NEG_INF)
    selb = jnp.where(sel, 0.0, NEG_INF)
    if ns < LANES:
        selb = jnp.concatenate([selb, jnp.full((LANES - ns, TQ), NEG_INF, F32)], axis=0)
    selb = selb.T[:, :HEAD_DIM]
    q_aug = jnp.concatenate([qf, jnp.concatenate([selb] * H, axis=0)], axis=1).astype(BF16)

    jd = t0 // KT
    start = pl.multiple_of(jd * KT, KT)
    kpos = start + lax.broadcasted_iota(jnp.int32, (TQ, KT), 1)
    tq_k = t0 + lax.broadcasted_iota(jnp.int32, (TQ, KT), 0)
    HC = NSA_CHAIN_HEADS
    G, R = H // HC, HC * TQ
    qa = [q_aug[c * R:(c + 1) * R] for c in range(G)]
    causal_k = (kpos <= tq_k)[None]

    def scores(c, st):
        return _dot_nt(qa[c], ksl_ref[pl.ds(st, KT), :]).reshape(HC, TQ, KT)

    def normalized(acc):
        return acc[:, :HEAD_DIM] / jnp.maximum(acc[:, HEAD_DIM:HEAD_DIM + 1], 1e-30)

    s_own = [scores(c, start) for c in range(G)]
    state = []
    for c in range(G):
        s = jnp.where(causal_k, s_own[c], NEG_INF)
        m = jnp.max(s, axis=-1, keepdims=True)
        p = jnp.exp((s - m).astype(BF16))
        state += [m, _dot(p.reshape(R, KT), vsl_ref[pl.ds(start, KT), :])]

    def past(j, carry):
        st = pl.multiple_of(j * KT, KT)
        ss = [scores(c, st) for c in range(G)]
        out = []
        for c in range(G):
            m, acc = carry[2 * c:2 * c + 2]
            m_new = jnp.maximum(m, jnp.max(ss[c], axis=-1, keepdims=True))
            alpha = jnp.exp(m - m_new)
            p = jnp.exp((ss[c] - m_new).astype(BF16))
            pv = _dot(p.reshape(R, KT), vsl_ref[pl.ds(st, KT), :])
            out += [m_new, alpha.reshape(R, 1) * acc + pv]
        return tuple(out)

    state = lax.fori_loop(0, jd, past, tuple(state))
    o_s = jnp.concatenate([normalized(state[2 * c + 1]) for c in range(G)], axis=0)

    w0 = pl.multiple_of(jnp.maximum(t0 - NSA_WINDOW, 0), TQ)
    wpos = w0 + lax.broadcasted_iota(jnp.int32, (TQ, NSA_SPAN), 1)
    tq_w = t0 + lax.broadcasted_iota(jnp.int32, (TQ, NSA_SPAN), 0)
    mask_w = ((wpos <= tq_w) & (wpos > tq_w - NSA_WINDOW))[None]
    s_w = jnp.where(mask_w, _dot_nt(q, kw_ref[pl.ds(w0, NSA_SPAN), :]).reshape(H, TQ, NSA_SPAN),
                    NEG_INF)
    p_w = jnp.exp((s_w - jnp.max(s_w, axis=-1, keepdims=True)).astype(BF16))
    o_w = normalized(_dot(p_w.reshape(H * TQ, NSA_SPAN), vw_ref[pl.ds(w0, NSA_SPAN), :]))

    g = g_ref[0]
    outs = []
    for h in range(H):
        rows = slice(h * TQ, (h + 1) * TQ)
        outs.append(g[:, 3 * h:3 * h + 1] * o_c[rows] + g[:, 3 * h + 1:3 * h + 2] * o_s[rows]
                    + g[:, 3 * h + 2:3 * h + 3] * o_w[rows])
    o_ref[0] = jnp.concatenate(outs, axis=1)


def nsa_attention(q, kv, gates, cmp_kv, overlap):
    b, s, w = q.shape
    ns, nc = overlap.shape
    assert ns <= HEAD_DIM
    TQ = NSA_TQ
    return pl.pallas_call(
        _nsa_kernel,
        grid=(b, s // TQ),
        in_specs=[pl.BlockSpec((1, TQ, w), lambda bi, i: (bi, i, 0)),
                  pl.BlockSpec((1, 2, nc, HEAD_DIM), lambda bi, i: (bi, 0, 0, 0)),
                  pl.BlockSpec((1, s, LANES), lambda bi, i: (bi, 0, 1)),
                  pl.BlockSpec((1, s, LANES), lambda bi, i: (bi, 0, 2)),
                  pl.BlockSpec((1, TQ, LANES), lambda bi, i: (bi, i, 0)),
                  pl.BlockSpec((ns, nc), lambda bi, i: (0, 0))],
        out_specs=pl.BlockSpec((1, TQ, w), lambda bi, i: (bi, i, 0)),
        out_shape=jax.ShapeDtypeStruct((b, s, w), F32),
        scratch_shapes=[pltpu.VMEM((s, LANES), BF16), pltpu.VMEM((s, LANES), BF16),
                        pltpu.VMEM((s, HEAD_DIM), BF16), pltpu.VMEM((s, LANES), BF16),
                        pltpu.VMEM((nc, 3 * HEAD_DIM), BF16)],
        compiler_params=_cparams(("arbitrary", "arbitrary")),
        name="nsa_attention",
    )(q, cmp_kv, kv, kv, gates, overlap)


def _nsa_overlap(s):
    nc = s // NSA_CMP_STRIDE
    ns = s // NSA_SLC_BLOCK
    cstart = np.arange(nc) * NSA_CMP_STRIDE
    cend = cstart + NSA_CMP_LEN - 1
    sstart = np.arange(ns) * NSA_SLC_BLOCK
    ov = (cstart[:, None] <= sstart[None, :] + NSA_SLC_BLOCK - 1) & (cend[:, None] >= sstart[None, :])
    ov[nc - 1] = False
    return jnp.asarray(ov.T.astype(np.float32))


def _mod_kernel(c_ref, w_ref, b_ref, o_ref):
    c = c_ref[...]
    act = c * jax.nn.sigmoid(c)
    o_ref[0] = jnp.dot(act, w_ref[0], precision=HI, preferred_element_type=F32) + b_ref[0]


def ada_modulation(c, ada_w, ada_b):
    nl, d, wide = ada_w.shape
    b = c.shape[0]
    tn = D_MODEL
    return pl.pallas_call(
        _mod_kernel,
        grid=(nl, wide // tn),
        in_specs=[pl.BlockSpec((b, d), lambda l, j: (0, 0)),
                  pl.BlockSpec((1, d, tn), lambda l, j: (l, 0, j)),
                  pl.BlockSpec((1, 1, tn), lambda l, j: (l, 0, j))],
        out_specs=pl.BlockSpec((1, b, tn), lambda l, j: (l, 0, j)),
        out_shape=jax.ShapeDtypeStruct((nl, b, wide), F32),
        compiler_params=_cparams(("arbitrary", "arbitrary")),
        name="ada_modulation",
    )(c, ada_w, ada_b.reshape(nl, 1, wide))


def _rms(x, g):
    return x * lax.rsqrt(jnp.mean(x * x, axis=-1, keepdims=True) + RMS_EPS) * g


_GATE_PAD = LANES
_COLS = {}
_off = 0
for _name, _w in (("qa", MOBA_W), ("ka", MOBA_W), ("va", MOBA_W), ("qb", NSA_W), ("kvb", NSA_W),
                  ("gb", _GATE_PAD), ("qc", SB_W), ("kc", SB_W), ("vc", SB_W)):
    _COLS[_name] = (_off, _w)
    _off += _w
IN_W_PACKED = _off
PROJ_TM = 256


def _rope_block(p, cs, sm, sp):
    return p * cs + pltpu.roll(p, LANES - ROPE_DIM // 2, 1) * sm + pltpu.roll(p, ROPE_DIM // 2, 1) * sp


def _in_proj_kernel(x_ref, sc_ref, sh_ref, g_ref, w_ref, cqk_ref, mqk_ref, pqk_ref,
                    ckv_ref, mkv_ref, pkv_ref,
                    qa_ref, ka_ref, va_ref, qb_ref, kvb_ref, gb_ref, qc_ref, kc_ref, vc_ref):
    hm = _rms(x_ref[0], g_ref[...]) * (1.0 + sc_ref[0]) + sh_ref[0]
    p = _dot(hm.astype(BF16), w_ref[...])
    qk = (cqk_ref[...], mqk_ref[...], pqk_ref[...])
    kv = (ckv_ref[...], mkv_ref[...], pkv_ref[...])

    def emit(ref, name, tabs):
        off, w = _COLS[name]
        for j in range(w // LANES):
            blk = p[:, off + j * LANES: off + (j + 1) * LANES]
            if tabs is not None:
                blk = _rope_block(blk, *tabs)
            ref[0, :, j * LANES:(j + 1) * LANES] = blk

    emit(qa_ref, "qa", qk)
    emit(ka_ref, "ka", qk)
    emit(va_ref, "va", None)
    emit(qb_ref, "qb", qk)
    emit(kvb_ref, "kvb", kv)
    emit(qc_ref, "qc", None)
    emit(kc_ref, "kc", None)
    emit(vc_ref, "vc", None)
    off, w = _COLS["gb"]
    gb_ref[0] = jax.nn.sigmoid(p[:, off:off + w])


def in_projection(x, sc, sh, g, w_packed, tabs):
    b, s, d = x.shape
    tm = PROJ_TM
    row = lambda w: pl.BlockSpec((1, tm, w), lambda bi, i: (bi, i, 0))
    vec = pl.BlockSpec((1, 1, d), lambda bi, i: (bi, 0, 0))
    tab = pl.BlockSpec((tm, LANES), lambda bi, i: (i, 0))
    names = ("qa", "ka", "va", "qb", "kvb", "gb", "qc", "kc", "vc")
    return pl.pallas_call(
        _in_proj_kernel,
        grid=(b, s // tm),
        in_specs=[row(d), vec, vec, pl.BlockSpec((1, d), lambda bi, i: (0, 0)),
                  pl.BlockSpec((d, IN_W_PACKED), lambda bi, i: (0, 0))] + [tab] * 6,
        out_specs=[row(_COLS[n][1]) for n in names],
        out_shape=[jax.ShapeDtypeStruct((b, s, _COLS[n][1]), F32) for n in names],
        compiler_params=_cparams(("arbitrary", "arbitrary")),
        name="in_projection",
    )(x, sc, sh, g, w_packed, *tabs)


def _pack_w_in(w_in):
    widths = (MOBA_W, MOBA_W, MOBA_W, NSA_W, NSA_W, 3 * NSA_HEADS, SB_W, SB_W, SB_W)
    offs = np.cumsum((0,) + widths)
    parts = []
    for i, w in enumerate(widths):
        blk = w_in[:, offs[i]:offs[i + 1]]
        if w == 3 * NSA_HEADS:
            blk = jnp.pad(blk, ((0, 0), (0, _GATE_PAD - w)))
        parts.append(blk)
    return jnp.concatenate(parts, axis=1).astype(BF16)


def _rope_tables(s):
    half = ROPE_DIM // 2
    inv_freq = ROPE_THETA ** (-jnp.arange(0, ROPE_DIM, 2, dtype=F32) / ROPE_DIM)
    ang = jnp.arange(s, dtype=F32)[:, None] * inv_freq[None, :]
    cos, sin = jnp.cos(ang), jnp.sin(ang)
    zeros = jnp.zeros((s, HEAD_DIM - ROPE_DIM), F32)
    z8 = jnp.zeros((s, half), F32)
    cs_h = jnp.concatenate([cos, cos, zeros + 1.0], axis=1)
    sm_h = jnp.concatenate([-sin, z8, zeros], axis=1)
    sp_h = jnp.concatenate([z8, sin, zeros], axis=1)
    ident = (jnp.ones((s, HEAD_DIM), F32), jnp.zeros((s, HEAD_DIM), F32), jnp.zeros((s, HEAD_DIM), F32))
    qk = tuple(jnp.concatenate([t, t], axis=1) for t in (cs_h, sm_h, sp_h))
    kv = tuple(jnp.concatenate([t, i], axis=1) for t, i in zip((cs_h, sm_h, sp_h), ident))
    return qk + kv


OUT_TM = 256


def _out_proj_kernel(oa_ref, ob_ref, oc_ref, x_ref, gg_ref, w_ref, g1_ref, n1_ref, n2_ref,
                     sc_ref, sh_ref, rw_ref, rb_ref, x1_ref, hf_ref, lg_ref):
    gg = gg_ref[...]
    y = jnp.concatenate([_rms(oa_ref[0], gg[:, :MOBA_W]),
                         _rms(ob_ref[0], gg[:, MOBA_W:MOBA_W + NSA_W]),
                         _rms(oc_ref[0], gg[:, MOBA_W + NSA_W:])], axis=1)
    y = _dot(y.astype(BF16), w_ref[...])
    x1 = x_ref[0] + g1_ref[0] * _rms(y, n1_ref[...])
    x1_ref[0] = x1
    hf = _rms(x1, n2_ref[...]) * (1.0 + sc_ref[0]) + sh_ref[0]
    hf_ref[0] = hf.astype(BF16)
    lg_ref[...] = lax.dot_general(rw_ref[...], hf, (((1,), (1,)), ((), ())),
                                  precision=HI, preferred_element_type=F32) + rb_ref[...]


def out_projection(oa, ob, oc, x, grp_g, w_out, g1, n1, n2, sc2, sh2, rw_t, rb):
    b, s, d = x.shape
    ne = rw_t.shape[0]
    tm = OUT_TM
    steps = s // tm
    row = lambda w: pl.BlockSpec((1, tm, w), lambda bi, i: (bi, i, 0))
    vec = pl.BlockSpec((1, 1, d), lambda bi, i: (bi, 0, 0))
    cst = lambda r, w: pl.BlockSpec((r, w), lambda bi, i: (0, 0))
    return pl.pallas_call(
        _out_proj_kernel,
        grid=(b, steps),
        in_specs=[row(MOBA_W), row(NSA_W), row(SB_W), row(d), cst(1, d), cst(d, d), vec,
                  cst(1, d), cst(1, d), vec, vec, cst(ne, d), cst(ne, 1)],
        out_specs=[row(d), row(d), pl.BlockSpec((ne, tm), lambda bi, i: (0, bi * steps + i))],
        out_shape=[jax.ShapeDtypeStruct((b, s, d), F32), jax.ShapeDtypeStruct((b, s, d), BF16),
                   jax.ShapeDtypeStruct((ne, b * s), F32)],
        compiler_params=_cparams(("arbitrary", "arbitrary")),
        name="out_projection",
    )(oa, ob, oc, x, grp_g, w_out, g1, n1, n2, sc2, sh2, rw_t, rb)


ROUTE_TT = 512


def _router_kernel(lg_ref, idx_ref, gate_ref, rank_ref, cnt_ref):
    ne, tt = lg_ref.shape

    @pl.when(pl.program_id(0) == 0)
    def _():
        cnt_ref[...] = jnp.zeros(cnt_ref.shape, cnt_ref.dtype)

    v = lg_ref[...]
    erow = lax.broadcasted_iota(jnp.int32, (ne, tt), 0)
    vals, hots, firsts = [], [], []
    for _ in range(TOP_K):
        m = jnp.max(v, axis=0, keepdims=True)
        first = jnp.min(jnp.where(v == m, erow, ne), axis=0, keepdims=True)
        hot = erow == first
        v = jnp.where(hot, -jnp.inf, v)
        vals.append(m)
        hots.append(hot)
        firsts.append(first)
    exps = [jnp.exp(val - vals[0]) for val in vals]
    den = exps[0] + exps[1] + exps[2] + exps[3]
    gate_ref[...] = jnp.concatenate([e / den for e in exps], axis=0)
    idx_ref[...] = jnp.concatenate(firsts, axis=0)

    cnt = jnp.zeros((ne, tt), F32)
    for hot in hots:
        cnt = cnt + jnp.where(hot, 1.0, 0.0)
    before = (lax.broadcasted_iota(jnp.int32, (tt, tt), 0)
              < lax.broadcasted_iota(jnp.int32, (tt, tt), 1)).astype(BF16)
    excl = _dot(cnt.astype(BF16), before) + cnt_ref[:, 0:1]
    ranks = [jnp.sum(jnp.where(hot, excl, 0.0), axis=0, keepdims=True) for hot in hots]
    rank_ref[...] = jnp.concatenate(ranks, axis=0).astype(jnp.int32)
    cnt_ref[...] = cnt_ref[...] + jnp.sum(cnt, axis=1, keepdims=True)


def moe_router(logits_t):
    ne, n = logits_t.shape
    tt = ROUTE_TT
    slot = pl.BlockSpec((TOP_K, tt), lambda i: (0, i))
    return pl.pallas_call(
        _router_kernel,
        grid=(n // tt,),
        in_specs=[pl.BlockSpec((ne, tt), lambda i: (0, i))],
        out_specs=[slot, slot, slot, pl.BlockSpec((ne, LANES), lambda i: (0, 0))],
        out_shape=[jax.ShapeDtypeStruct((TOP_K, n), jnp.int32), jax.ShapeDtypeStruct((TOP_K, n), F32),
                   jax.ShapeDtypeStruct((TOP_K, n), jnp.int32), jax.ShapeDtypeStruct((ne, LANES), F32)],
        compiler_params=_cparams(("arbitrary",)),
        name="moe_router",
    )(logits_t)


MOE_TM = 256


def _expert_kernel(be_ref, nu_ref, xs_ref, wgu_ref, bgu_ref, wdn_ref, bdn_ref, y_ref,
                   wgu_bf, wdn_bf):
    i = pl.program_id(0)
    used = i < nu_ref[0]

    @pl.when(used & ((i == 0) | (be_ref[i] != be_ref[jnp.maximum(i - 1, 0)])))
    def _():
        wgu_bf[...] = wgu_ref[0].astype(BF16)
        wdn_bf[...] = wdn_ref[0].astype(BF16)

    @pl.when(used)
    def _():
        hgu = _dot(xs_ref[...], wgu_bf[...]) + bgu_ref[0]
        de = hgu.shape[1] // 2
        glu = jnp.minimum(hgu[:, :de], SWIGLU_LIMIT)
        lin = jnp.clip(hgu[:, de:], -SWIGLU_LIMIT, SWIGLU_LIMIT)
        act = glu * jax.nn.sigmoid(SWIGLU_ALPHA * glu) * (lin + 1.0)
        y_ref[...] = _dot(act.astype(BF16), wdn_bf[...]) + bdn_ref[0]

    @pl.when(jnp.logical_not(used))
    def _():
        y_ref[...] = jnp.zeros(y_ref.shape, y_ref.dtype)


EXPERT_VMEM_LIMIT = 56 * 1024 * 1024


def expert_ffn(layer, blk_exp, n_used, xs, w_gu, b_gu, w_dn, b_dn):
    r, d = xs.shape
    nl, ne, _, wide = w_gu.shape
    w_gu = w_gu.reshape(nl * ne, d, wide)
    w_dn = w_dn.reshape(nl * ne, wide // 2, d)
    b_gu = b_gu.reshape(nl * ne, wide)
    b_dn = b_dn.reshape(nl * ne, d)
    ne, base = nl * ne, layer * ne
    tm = MOE_TM
    grid_spec = pltpu.PrefetchScalarGridSpec(
        num_scalar_prefetch=2,
        grid=(r // tm,),
        in_specs=[pl.BlockSpec((tm, d), lambda i, be, nu: (i, 0)),
                  pl.BlockSpec((1, d, wide), lambda i, be, nu: (base + be[i], 0, 0)),
                  pl.BlockSpec((1, 1, wide), lambda i, be, nu: (base + be[i], 0, 0)),
                  pl.BlockSpec((1, wide // 2, d), lambda i, be, nu: (base + be[i], 0, 0)),
                  pl.BlockSpec((1, 1, d), lambda i, be, nu: (base + be[i], 0, 0))],
        out_specs=pl.BlockSpec((tm, d), lambda i, be, nu: (i, 0)),
        scratch_shapes=[pltpu.VMEM((d, wide), BF16), pltpu.VMEM((wide // 2, d), BF16)],
    )
    return pl.pallas_call(
        _expert_kernel,
        grid_spec=grid_spec,
        out_shape=jax.ShapeDtypeStruct((r, d), F32),
        compiler_params=pltpu.CompilerParams(dimension_semantics=("arbitrary",),
                                             vmem_limit_bytes=EXPERT_VMEM_LIMIT),
        name="expert_ffn",
    )(blk_exp, n_used, xs, w_gu, b_gu.reshape(ne, 1, wide), w_dn, b_dn.reshape(ne, 1, d))


COMB_TM = 256


def _combine_kernel(yg_ref, gate_ref, x_ref, g2_ref, n3_ref, o_ref):
    gate = gate_ref[0]
    y = gate[:, 0:1] * yg_ref[0, 0]
    for k in range(1, TOP_K):
        y = y + gate[:, k:k + 1] * yg_ref[0, k]
    o_ref[0] = x_ref[0] + g2_ref[0] * _rms(y, n3_ref[...])


def moe_combine(yg, gate, x, g2, n3):
    b, s, d = x.shape
    tm = COMB_TM
    return pl.pallas_call(
        _combine_kernel,
        grid=(b, s // tm),
        in_specs=[pl.BlockSpec((1, TOP_K, tm, d), lambda bi, i: (bi, 0, i, 0)),
                  pl.BlockSpec((1, tm, LANES), lambda bi, i: (bi, i, 0)),
                  pl.BlockSpec((1, tm, d), lambda bi, i: (bi, i, 0)),
                  pl.BlockSpec((1, 1, d), lambda bi, i: (bi, 0, 0)),
                  pl.BlockSpec((1, d), lambda bi, i: (0, 0))],
        out_specs=pl.BlockSpec((1, tm, d), lambda bi, i: (bi, i, 0)),
        out_shape=jax.ShapeDtypeStruct((b, s, d), F32),
        compiler_params=_cparams(("arbitrary", "arbitrary")),
        name="moe_combine",
    )(yg, gate, x, g2, n3)


def _dispatch_plan(idx, rank, counts):
    n = idx.shape[1]
    tm = MOE_TM
    padded = (counts + tm - 1) // tm * tm
    pad_end = jnp.cumsum(padded)
    start = pad_end - padded
    experts = jnp.arange(N_EXPERTS, dtype=jnp.int32)
    dest = rank + jnp.sum(jnp.where(idx[..., None] == experts, start, 0), axis=-1)
    n_rows = -(-(n * TOP_K + N_EXPERTS * (tm - 1)) // tm) * tm
    n_blk = n_rows // tm
    blk_start = jnp.arange(n_blk, dtype=jnp.int32) * tm
    blk_exp = jnp.minimum(jnp.sum((pad_end[None, :] <= blk_start[:, None]).astype(jnp.int32), axis=1),
                          N_EXPERTS - 1)
    n_used = (pad_end[-1] // tm).astype(jnp.int32).reshape(1)
    return dest, blk_exp, n_used, n_rows


def _layer(layer, x, mod, norm_g, w_in, cmp_pe, cmp_w1, cmp_w2, grp_g, w_out, router_w, router_b,
           w_gu, b_gu, w_dn, b_dn, tabs, overlap):
    b, s, d = x.shape
    sh1, sc1, g1, sh2, sc2, g2 = (m.reshape(b, 1, d) for m in jnp.split(mod, 6, axis=-1))
    ng = norm_g.reshape(4, 1, d)

    qa, ka, va, qb, kvb, gb, qc, kc, vc = in_projection(x, sc1, sh1, ng[0], _pack_w_in(w_in), tabs)
    oa = moba_attention(qa, ka, va)
    nc = s // NSA_CMP_STRIDE
    kv16 = jnp.stack([kvb[:, :, :HEAD_DIM], kvb[:, :, HEAD_DIM:2 * HEAD_DIM]], axis=1)
    kv16 = kv16.reshape(b, 2, nc, NSA_CMP_STRIDE * HEAD_DIM)
    cmp_kv = nsa_compress(kv16, cmp_pe.reshape(2, 1, NSA_CMP_LEN * HEAD_DIM), cmp_w1, cmp_w2)
    ob = nsa_attention(qb, kvb, gb, cmp_kv, overlap)
    oc = sb_attention(qc, kc, vc)

    x1, hf, logits_t = out_projection(oa, ob, oc, x, grp_g.reshape(1, d), w_out.astype(BF16), g1,
                                      ng[1], ng[2], sc2, sh2, router_w.T, router_b.reshape(-1, 1))

    n = b * s
    idx, gate, rank, counts = moe_router(logits_t)
    dest, blk_exp, n_used, n_rows = _dispatch_plan(idx, rank, counts[:, 0].astype(jnp.int32))
    tok = jnp.tile(jnp.arange(n, dtype=jnp.int32), TOP_K)
    row_tok = jnp.zeros((n_rows,), jnp.int32).at[dest.reshape(-1)].set(tok)
    xs = jnp.take(hf.reshape(n, d), row_tok, axis=0, mode="clip")
    y_rows = expert_ffn(layer, blk_exp, n_used, xs, w_gu, b_gu, w_dn, b_dn)
    yg = jnp.take(y_rows, dest.reshape(TOP_K, b, s).transpose(1, 0, 2), axis=0, mode="clip")
    gate = jnp.pad(gate.T, ((0, 0), (0, LANES - TOP_K))).reshape(b, s, LANES)
    return moe_combine(yg, gate, x1, g2, ng[3])


def kernel(x, c, ada_w, ada_b, norm_g, w_in, nsa_cmp_pe, nsa_cmp_w1, nsa_cmp_w2, mix_out_g,
           w_out, router_w, router_b, exp_w_gu, exp_b_gu, exp_w_dn, exp_b_dn):
    s = x.shape[1]
    tabs = _rope_tables(s)
    overlap = _nsa_overlap(s)
    mod = ada_modulation(c, ada_w, ada_b)
    for l in range(ada_w.shape[0]):
        x = _layer(l, x, mod[l], norm_g[l], w_in[l], nsa_cmp_pe[l], nsa_cmp_w1[l], nsa_cmp_w2[l],
                   mix_out_g[l], w_out[l], router_w[l], router_b[l], exp_w_gu, exp_b_gu,
                   exp_w_dn, exp_b_dn, tabs, overlap)
    return x
```

```python
import functools

import numpy as np
import jax
import jax.numpy as jnp
from jax import lax
from jax.experimental import pallas as pl
from jax.experimental.pallas import tpu as pltpu

F32 = jnp.float32
BF16 = jnp.bfloat16
HI = lax.Precision.HIGHEST

D_MODEL = 1024
N_HEADS = 16
HEAD_DIM = 64
MOBA_HEADS = 4
NSA_HEADS = 6
SB_HEADS = 6
MOBA_W = MOBA_HEADS * HEAD_DIM
NSA_W = NSA_HEADS * HEAD_DIM
SB_W = SB_HEADS * HEAD_DIM
ROPE_DIM = 16
ROPE_THETA = 500000.0
MOBA_BLOCK = 256
MOBA_TOPK = 3
NSA_CMP_LEN = 32
NSA_CMP_STRIDE = 16
NSA_SLC_BLOCK = 64
NSA_SLC_TOPK = 16
NSA_WINDOW = 512
N_EXPERTS = 32
TOP_K = 4
SWIGLU_LIMIT = 7.0
SWIGLU_ALPHA = 1.702
RMS_EPS = 1e-6
NEG_INF = -1e30
SEL_FORCE = 1e4
SCALE = HEAD_DIM ** -0.5
LOG2E = 1.4426950408889634

LANES = 128
VMEM_LIMIT = 48 * 1024 * 1024


def _cparams(sem):
    return pltpu.CompilerParams(dimension_semantics=sem, vmem_limit_bytes=VMEM_LIMIT)


def _dot(a, b):
    return jnp.dot(a, b, preferred_element_type=F32)


def _dot_nt(a, b):
    return lax.dot_general(a, b, (((1,), (1,)), ((), ())), preferred_element_type=F32)


SB_TILE = 256


SB_GROUP_W = SB_W


def _sb_kernel(q_ref, k_ref, v_ref, o_ref, ks_ref, vs_ref):
    T = SB_TILE
    H = ks_ref.shape[0]
    qi = pl.program_id(2)

    @pl.when(qi == 0)
    def _():
        for h in range(H):
            ks_ref[h] = k_ref[0, :, h * HEAD_DIM:(h + 1) * HEAD_DIM].astype(BF16)
            vs_ref[h] = v_ref[0, :, h * HEAD_DIM:(h + 1) * HEAD_DIM].astype(BF16)

    row = lax.broadcasted_iota(jnp.int32, (T, T), 0)
    col = lax.broadcasted_iota(jnp.int32, (T, T), 1)
    incl = (row >= col).astype(BF16)
    sum_rhs = jnp.concatenate([incl, incl], axis=0)
    diag_mask = col < row
    qs = [(q_ref[0, :, h * HEAD_DIM:(h + 1) * HEAD_DIM] * (SCALE * LOG2E)).astype(BF16)
          for h in range(H)]
    sign = jnp.uint32(0x80000000)

    def scores(h, j):
        return _dot_nt(qs[h], ks_ref[h, pl.ds(pl.multiple_of(j * T, T), T), :])

    def neg_log2_keep(z, masked):
        neg_abs = lax.bitcast_convert_type(lax.bitcast_convert_type(z, jnp.uint32) | sign, F32)
        nlk = jnp.maximum(z, 0.0) + jnp.log2(1.0 + jnp.exp2(neg_abs))
        return jnp.where(diag_mask, nlk, 0.0) if masked else nlk

    def suffix_sums(nlk):
        hi = nlk.astype(BF16)
        lo = (nlk - hi.astype(F32)).astype(BF16)
        return _dot(jnp.concatenate([hi, lo], axis=1), sum_rhs)

    def weights(z, sums, run, masked):
        x = z - sums - run
        if masked:
            x = jnp.where(diag_mask, x, NEG_INF)
        return jnp.exp2(x.astype(BF16))

    def pv(h, p, j):
        return _dot(p, vs_ref[h, pl.ds(pl.multiple_of(j * T, T), T), :])

    def step(zs, runs, masked):
        sums = [suffix_sums(neg_log2_keep(zs[h], masked)) for h in range(H)]
        ps = [weights(zs[h], sums[h], runs[h], masked) for h in range(H)]
        return ps, [runs[h] + sums[h][:, 0:1] for h in range(H)]

    zeros = jnp.zeros((T, 1), F32)
    ps, runs = step([scores(h, qi) for h in range(H)], [zeros] * H, True)
    accs = [pv(h, ps[h], qi) for h in range(H)]

    def past(i, carry):
        runs, accs = carry
        j = qi - 1 - i
        ps, runs = step([scores(h, j) for h in range(H)], runs, False)
        return runs, [accs[h] + pv(h, ps[h], j) for h in range(H)]

    runs, accs = lax.fori_loop(0, qi, past, (runs, accs))
    o_ref[0] = jnp.concatenate(accs, axis=1)


def sb_attention(q, k, v):
    b, s, w = q.shape
    T = SB_TILE
    gw = SB_GROUP_W
    nh = gw // HEAD_DIM
    qspec = pl.BlockSpec((1, T, gw), lambda bi, p, i: (bi, i, p))
    kvspec = pl.BlockSpec((1, s, gw), lambda bi, p, i: (bi, 0, p))
    return pl.pallas_call(
        _sb_kernel,
        grid=(b, w // gw, s // T),
        in_specs=[qspec, kvspec, kvspec],
        out_specs=qspec,
        out_shape=jax.ShapeDtypeStruct((b, s, w), F32),
        scratch_shapes=[pltpu.VMEM((nh, s, HEAD_DIM), BF16), pltpu.VMEM((nh, s, HEAD_DIM), BF16)],
        compiler_params=_cparams(("arbitrary", "arbitrary", "arbitrary")),
        name="sb_attention",
    )(q, k, v)


def _rank_before_t(vals, n):
    idx = lax.broadcasted_iota(jnp.int32, vals.shape, 0)
    rank = jnp.zeros(vals.shape, F32)
    for j2 in range(n):
        other = vals[j2:j2 + 1, :]
        ahead = (other > vals) | ((other == vals) & (idx > j2))
        rank = rank + jnp.where(ahead, 1.0, 0.0)
    return rank


MOBA_KT = 2 * MOBA_BLOCK
MOBA_GROUP_W = MOBA_W


def _moba_kernel(q_ref, k_ref, v_ref, o_ref, ks_ref, vs_ref, km_ref):
    T, KT = MOBA_BLOCK, MOBA_KT
    H = ks_ref.shape[0]
    nb = km_ref.shape[1]
    s_len = ks_ref.shape[1]
    qi = pl.program_id(2)

    @pl.when(qi == 0)
    def _():
        key_blk = lax.broadcasted_iota(jnp.int32, (s_len, HEAD_DIM), 0) // T
        onehot = (lax.broadcasted_iota(jnp.int32, (s_len, HEAD_DIM), 1) == key_blk).astype(F32)
        ones_col = (lax.broadcasted_iota(jnp.int32, (s_len, HEAD_DIM), 1) == 0).astype(F32)
        for h in range(H):
            kh = k_ref[0, :, h * HEAD_DIM:(h + 1) * HEAD_DIM]
            ks_ref[h] = jnp.concatenate([kh, onehot], axis=1).astype(BF16)
            vs_ref[h] = jnp.concatenate([v_ref[0, :, h * HEAD_DIM:(h + 1) * HEAD_DIM], ones_col],
                                        axis=1).astype(BF16)
            km_ref[h] = jnp.mean(kh.reshape(nb, T, HEAD_DIM), axis=1)

    row = lax.broadcasted_iota(jnp.int32, (T, T), 0)
    col = lax.broadcasted_iota(jnp.int32, (T, T), 1)
    causal = col <= row
    blk_t = lax.broadcasted_iota(jnp.int32, (nb, T), 0)
    start = pl.multiple_of(qi * T, T)

    qfs = [q_ref[0, :, h * HEAD_DIM:(h + 1) * HEAD_DIM] for h in range(H)]
    qss = [qf * SCALE for qf in qfs]
    s_own = [_dot_nt(qss[h].astype(BF16), ks_ref[h, pl.ds(start, T), 0:HEAD_DIM]) for h in range(H)]
    q_aug, state = [], []
    for h in range(H):
        gate = lax.dot_general(km_ref[h], qfs[h], (((1,), (1,)), ((), ())),
                               precision=HI, preferred_element_type=F32)
        gate = jnp.where(blk_t < qi, gate, NEG_INF)
        sel = (_rank_before_t(gate, nb) < float(MOBA_TOPK)) & (gate > 0.5 * NEG_INF)
        selb = jnp.where(sel, 0.0, NEG_INF)
        selb = jnp.concatenate([selb, jnp.full((LANES - nb, T), NEG_INF, F32)], axis=0).T
        q_aug.append(jnp.concatenate([qss[h], selb[:, :HEAD_DIM]], axis=1).astype(BF16))

        s = jnp.where(causal, s_own[h], NEG_INF)
        m = jnp.max(s, axis=1, keepdims=True)
        p = jnp.exp((s - m).astype(BF16))
        state += [m, _dot(p, vs_ref[h, pl.ds(start, T), :])]

    def past(i, carry):
        st = pl.multiple_of(i * KT, KT)
        ss = [_dot_nt(q_aug[h], ks_ref[h, pl.ds(st, KT), :]) for h in range(H)]
        out = []
        for h in range(H):
            m, acc = carry[2 * h:2 * h + 2]
            m_new = jnp.maximum(m, jnp.max(ss[h], axis=1, keepdims=True))
            alpha = jnp.exp(m - m_new)
            p = jnp.exp((ss[h] - m_new).astype(BF16))
            out += [m_new, alpha * acc + _dot(p, vs_ref[h, pl.ds(st, KT), :])]
        return tuple(out)

    state = lax.fori_loop(0, (qi + 1) // 2, past, tuple(state))
    outs = [state[2 * h + 1][:, :HEAD_DIM] / jnp.maximum(state[2 * h + 1][:, HEAD_DIM:HEAD_DIM + 1], 1e-30)
            for h in range(H)]
    o_ref[0] = jnp.concatenate(outs, axis=1)


def moba_attention(q, k, v):
    b, s, w = q.shape
    T = MOBA_BLOCK
    nb = s // T
    assert nb <= HEAD_DIM and s % MOBA_KT == 0
    gw = MOBA_GROUP_W
    nh = gw // HEAD_DIM
    qspec = pl.BlockSpec((1, T, gw), lambda bi, p, i: (bi, i, p))
    kvspec = pl.BlockSpec((1, s, gw), lambda bi, p, i: (bi, 0, p))
    return pl.pallas_call(
        _moba_kernel,
        grid=(b, w // gw, nb),
        in_specs=[qspec, kvspec, kvspec],
        out_specs=qspec,
        out_shape=jax.ShapeDtypeStruct((b, s, w), F32),
        scratch_shapes=[pltpu.VMEM((nh, s, LANES), BF16), pltpu.VMEM((nh, s, LANES), BF16),
                        pltpu.VMEM((nh, nb, HEAD_DIM), F32)],
        compiler_params=_cparams(("arbitrary", "arbitrary", "arbitrary")),
        name="moba_attention",
    )(q, k, v)


def _nsa_compress_kernel(x_ref, pe_ref, w1_ref, w2_ref, o_ref):
    nc = x_ref.shape[2]
    half = NSA_CMP_STRIDE * HEAD_DIM
    x = x_ref[0, 0]
    w1 = w1_ref[0]
    first = jnp.dot(x, w1[:half], precision=HI, preferred_element_type=F32)
    second = jnp.dot(x, w1[half:], precision=HI, preferred_element_type=F32)
    pe = jnp.broadcast_to(pe_ref[0], (8, 2 * half))
    peb = jnp.dot(pe, w1, precision=HI, preferred_element_type=F32)[0:1]
    pre = first + pltpu.roll(second, nc - 1, 0) + peb
    hid = pre * jax.nn.sigmoid(pre)
    o_ref[0, 0] = jnp.dot(hid, w2_ref[0], precision=HI, preferred_element_type=F32)


def nsa_compress(kv16, pe, w1, w2):
    b, _, nc, wide = kv16.shape
    return pl.pallas_call(
        _nsa_compress_kernel,
        grid=(b, 2),
        in_specs=[pl.BlockSpec((1, 1, nc, wide), lambda bi, i: (bi, i, 0, 0)),
                  pl.BlockSpec((1, 1, 2 * wide), lambda bi, i: (i, 0, 0)),
                  pl.BlockSpec((1, 2 * wide, HEAD_DIM), lambda bi, i: (i, 0, 0)),
                  pl.BlockSpec((1, HEAD_DIM, HEAD_DIM), lambda bi, i: (i, 0, 0))],
        out_specs=pl.BlockSpec((1, 1, nc, HEAD_DIM), lambda bi, i: (bi, i, 0, 0)),
        out_shape=jax.ShapeDtypeStruct((b, 2, nc, HEAD_DIM), F32),
        compiler_params=_cparams(("arbitrary", "arbitrary")),
        name="nsa_compress",
    )(kv16, pe, w1, w2)


NSA_TQ = 128
NSA_KT = 512
NSA_SPAN = NSA_WINDOW + NSA_TQ
NSA_CHAIN_HEADS = 2


def _softmax_rows(s, mask):
    s = jnp.where(mask, s, NEG_INF)
    m = jnp.max(s, axis=-1, keepdims=True)
    e = jnp.where(mask, jnp.exp(s - m), 0.0)
    return e / jnp.maximum(jnp.sum(e, axis=-1, keepdims=True), 1e-30)


def _split_bf16(x):
    hi = x.astype(BF16)
    return hi, (x - hi.astype(F32)).astype(BF16)


def _nsa_kernel(q_ref, cmp_ref, slc_ref, win_ref, g_ref, ov_ref, o_ref,
                ksl_ref, vsl_ref, kw_ref, vw_ref, kc_ref):
    TQ, KT, H = NSA_TQ, NSA_KT, NSA_HEADS
    ns, nc = ov_ref.shape
    s_len = ksl_ref.shape[0]
    qi = pl.program_id(1)
    t0 = qi * TQ

    @pl.when(qi == 0)
    def _():
        lane = lax.broadcasted_iota(jnp.int32, (s_len, HEAD_DIM), 1)
        key_blk = lax.broadcasted_iota(jnp.int32, (s_len, HEAD_DIM), 0) // NSA_SLC_BLOCK
        onehot = (lane == key_blk).astype(F32)
        ones_col = (lane == 0).astype(F32)
        ksl_ref[...] = jnp.concatenate([slc_ref[0, :, :HEAD_DIM], onehot], axis=1).astype(BF16)
        vsl_ref[...] = jnp.concatenate([slc_ref[0, :, HEAD_DIM:], ones_col], axis=1).astype(BF16)
        kw_ref[...] = win_ref[0, :, :HEAD_DIM].astype(BF16)
        vw_ref[...] = jnp.concatenate([win_ref[0, :, HEAD_DIM:], ones_col], axis=1).astype(BF16)
        k_hi, k_lo = _split_bf16(cmp_ref[0, 0])
        kc_ref[...] = jnp.concatenate([k_hi, k_hi, k_lo], axis=1)

    qf = jnp.concatenate([q_ref[0, :, h * HEAD_DIM:(h + 1) * HEAD_DIM] for h in range(H)],
                         axis=0) * SCALE
    q, q_lo = _split_bf16(qf)

    tq_c = t0 + lax.broadcasted_iota(jnp.int32, (TQ, nc), 0)
    n_c = lax.broadcasted_iota(jnp.int32, (TQ, nc), 1)
    mask_c = (n_c * NSA_CMP_STRIDE + (NSA_CMP_LEN - 1) <= tq_c) & (n_c < nc - 1)
    s_c = _dot_nt(jnp.concatenate([q, q_lo, q], axis=1), kc_ref[...])
    p_c = _softmax_rows(s_c.reshape(H, TQ, nc), mask_c[None])
    o_c = _dot(p_c.reshape(H * TQ, nc).astype(BF16), cmp_ref[0, 1].astype(BF16))

    imp = lax.dot_general(ov_ref[...], jnp.sum(p_c, axis=0), (((1,), (1,)), ((), ())),
                          precision=HI, preferred_element_type=F32)
    tq_s = t0 + lax.broadcasted_iota(jnp.int32, (ns, TQ), 1)
    blk = lax.broadcasted_iota(jnp.int32, (ns, TQ), 0)
    own = tq_s // NSA_SLC_BLOCK
    forced = (blk == 0) | (blk == own) | (blk == own - 1)
    imp = jnp.where(forced, SEL_FORCE, imp)
    imp = jnp.where(blk <= own, imp, NEG_INF)
    sel = (_rank_before_t(imp, ns) < float(min(NSA_SLC_TOPK, ns))) & (imp > 0.5 * NEG_INF)
    selb = jnp.where(sel, 0.0, NEG_INF)
    if ns < LANES:
        selb = jnp.concatenate([selb, jnp.full((LANES - ns, TQ), NEG_INF, F32)], axis=0)
    selb = selb.T[:, :HEAD_DIM]
    q_aug = jnp.concatenate([qf, jnp.concatenate([selb] * H, axis=0)], axis=1).astype(BF16)

    jd = t0 // KT
    start = pl.multiple_of(jd * KT, KT)
    kpos = start + lax.broadcasted_iota(jnp.int32, (TQ, KT), 1)
    tq_k = t0 + lax.broadcasted_iota(jnp.int32, (TQ, KT), 0)
    HC = NSA_CHAIN_HEADS
    G, R = H // HC, HC * TQ
    qa = [q_aug[c * R:(c + 1) * R] for c in range(G)]
    causal_k = (kpos <= tq_k)[None]

    def scores(c, st):
        return _dot_nt(qa[c], ksl_ref[pl.ds(st, KT), :]).reshape(HC, TQ, KT)

    def normalized(acc):
        return acc[:, :HEAD_DIM] / jnp.maximum(acc[:, HEAD_DIM:HEAD_DIM + 1], 1e-30)

    s_own = [scores(c, start) for c in range(G)]
    state = []
    for c in range(G):
        s = jnp.where(causal_k, s_own[c], NEG_INF)
        m = jnp.max(s, axis=-1, keepdims=True)
        p = jnp.exp((s - m).astype(BF16))
        state += [m, _dot(p.reshape(R, KT), vsl_ref[pl.ds(start, KT), :])]

    def past(j, carry):
        st = pl.multiple_of(j * KT, KT)
        ss = [scores(c, st) for c in range(G)]
        out = []
        for c in range(G):
            m, acc = carry[2 * c:2 * c + 2]
            m_new = jnp.maximum(m, jnp.max(ss[c], axis=-1, keepdims=True))
            alpha = jnp.exp(m - m_new)
            p = jnp.exp((ss[c] - m_new).astype(BF16))
            pv = _dot(p.reshape(R, KT), vsl_ref[pl.ds(st, KT), :])
            out += [m_new, alpha.reshape(R, 1) * acc + pv]
        return tuple(out)

    state = lax.fori_loop(0, jd, past, tuple(state))
    o_s = jnp.concatenate([normalized(state[2 * c + 1]) for c in range(G)], axis=0)

    w0 = pl.multiple_of(jnp.maximum(t0 - NSA_WINDOW, 0), TQ)
    wpos = w0 + lax.broadcasted_iota(jnp.int32, (TQ, NSA_SPAN), 1)
    tq_w = t0 + lax.broadcasted_iota(jnp.int32, (TQ, NSA_SPAN), 0)
    mask_w = ((wpos <= tq_w) & (wpos > tq_w - NSA_WINDOW))[None]
    s_w = jnp.where(mask_w, _dot_nt(q, kw_ref[pl.ds(w0, NSA_SPAN), :]).reshape(H, TQ, NSA_SPAN),
                    NEG_INF)
    p_w = jnp.exp((s_w - jnp.max(s_w, axis=-1, keepdims=True)).astype(BF16))
    o_w = normalized(_dot(p_w.reshape(H * TQ, NSA_SPAN), vw_ref[pl.ds(w0, NSA_SPAN), :]))

    g = g_ref[0]
    outs = []
    for h in range(H):
        rows = slice(h * TQ, (h + 1) * TQ)
        outs.append(g[:, 3 * h:3 * h + 1] * o_c[rows] + g[:, 3 * h + 1:3 * h + 2] * o_s[rows]
                    + g[:, 3 * h + 2:3 * h + 3] * o_w[rows])
    o_ref[0] = jnp.concatenate(outs, axis=1)


def nsa_attention(q, kv, gates, cmp_kv, overlap):
    b, s, w = q.shape
    ns, nc = overlap.shape
    assert ns <= HEAD_DIM
    TQ = NSA_TQ
    return pl.pallas_call(
        _nsa_kernel,
        grid=(b, s // TQ),
        in_specs=[pl.BlockSpec((1, TQ, w), lambda bi, i: (bi, i, 0)),
                  pl.BlockSpec((1, 2, nc, HEAD_DIM), lambda bi, i: (bi, 0, 0, 0)),
                  pl.BlockSpec((1, s, LANES), lambda bi, i: (bi, 0, 1)),
                  pl.BlockSpec((1, s, LANES), lambda bi, i: (bi, 0, 2)),
                  pl.BlockSpec((1, TQ, LANES), lambda bi, i: (bi, i, 0)),
                  pl.BlockSpec((ns, nc), lambda bi, i: (0, 0))],
        out_specs=pl.BlockSpec((1, TQ, w), lambda bi, i: (bi, i, 0)),
        out_shape=jax.ShapeDtypeStruct((b, s, w), F32),
        scratch_shapes=[pltpu.VMEM((s, LANES), BF16), pltpu.VMEM((s, LANES), BF16),
                        pltpu.VMEM((s, HEAD_DIM), BF16), pltpu.VMEM((s, LANES), BF16),
                        pltpu.VMEM((nc, 3 * HEAD_DIM), BF16)],
        compiler_params=_cparams(("arbitrary", "arbitrary")),
        name="nsa_attention",
    )(q, cmp_kv, kv, kv, gates, overlap)


def _nsa_overlap(s):
    nc = s // NSA_CMP_STRIDE
    ns = s // NSA_SLC_BLOCK
    cstart = np.arange(nc) * NSA_CMP_STRIDE
    cend = cstart + NSA_CMP_LEN - 1
    sstart = np.arange(ns) * NSA_SLC_BLOCK
    ov = (cstart[:, None] <= sstart[None, :] + NSA_SLC_BLOCK - 1) & (cend[:, None] >= sstart[None, :])
    ov[nc - 1] = False
    return jnp.asarray(ov.T.astype(np.float32))


def _mod_kernel(c_ref, w_ref, b_ref, o_ref):
    c = c_ref[...]
    act = c * jax.nn.sigmoid(c)
    o_ref[0] = jnp.dot(act, w_ref[0], precision=HI, preferred_element_type=F32) + b_ref[0]


def ada_modulation(c, ada_w, ada_b):
    nl, d, wide = ada_w.shape
    b = c.shape[0]
    tn = D_MODEL
    return pl.pallas_call(
        _mod_kernel,
        grid=(nl, wide // tn),
        in_specs=[pl.BlockSpec((b, d), lambda l, j: (0, 0)),
                  pl.BlockSpec((1, d, tn), lambda l, j: (l, 0, j)),
                  pl.BlockSpec((1, 1, tn), lambda l, j: (l, 0, j))],
        out_specs=pl.BlockSpec((1, b, tn), lambda l, j: (l, 0, j)),
        out_shape=jax.ShapeDtypeStruct((nl, b, wide), F32),
        compiler_params=_cparams(("arbitrary", "arbitrary")),
        name="ada_modulation",
    )(c, ada_w, ada_b.reshape(nl, 1, wide))


def _rms(x, g):
    return x * lax.rsqrt(jnp.mean(x * x, axis=-1, keepdims=True) + RMS_EPS) * g


_GATE_PAD = LANES
_COLS = {}
_off = 0
for _name, _w in (("qa", MOBA_W), ("ka", MOBA_W), ("va", MOBA_W), ("qb", NSA_W), ("kvb", NSA_W),
                  ("gb", _GATE_PAD), ("qc", SB_W), ("kc", SB_W), ("vc", SB_W)):
    _COLS[_name] = (_off, _w)
    _off += _w
IN_W_PACKED = _off
PROJ_TM = 256


def _rope_block(p, cs, sm, sp):
    return p * cs + pltpu.roll(p, LANES - ROPE_DIM // 2, 1) * sm + pltpu.roll(p, ROPE_DIM // 2, 1) * sp


def _in_proj_kernel(x_ref, sc_ref, sh_ref, g_ref, w_ref, cqk_ref, mqk_ref, pqk_ref,
                    ckv_ref, mkv_ref, pkv_ref,
                    qa_ref, ka_ref, va_ref, qb_ref, kvb_ref, gb_ref, qc_ref, kc_ref, vc_ref):
    hm = _rms(x_ref[0], g_ref[...]) * (1.0 + sc_ref[0]) + sh_ref[0]
    p = _dot(hm.astype(BF16), w_ref[...])
    qk = (cqk_ref[...], mqk_ref[...], pqk_ref[...])
    kv = (ckv_ref[...], mkv_ref[...], pkv_ref[...])

    def emit(ref, name, tabs):
        off, w = _COLS[name]
        for j in range(w // LANES):
            blk = p[:, off + j * LANES: off + (j + 1) * LANES]
            if tabs is not None:
                blk = _rope_block(blk, *tabs)
            ref[0, :, j * LANES:(j + 1) * LANES] = blk

    emit(qa_ref, "qa", qk)
    emit(ka_ref, "ka", qk)
    emit(va_ref, "va", None)
    emit(qb_ref, "qb", qk)
    emit(kvb_ref, "kvb", kv)
    emit(qc_ref, "qc", None)
    emit(kc_ref, "kc", None)
    emit(vc_ref, "vc", None)
    off, w = _COLS["gb"]
    gb_ref[0] = jax.nn.sigmoid(p[:, off:off + w])


def in_projection(x, sc, sh, g, w_packed, tabs):
    b, s, d = x.shape
    tm = PROJ_TM
    row = lambda w: pl.BlockSpec((1, tm, w), lambda bi, i: (bi, i, 0))
    vec = pl.BlockSpec((1, 1, d), lambda bi, i: (bi, 0, 0))
    tab = pl.BlockSpec((tm, LANES), lambda bi, i: (i, 0))
    names = ("qa", "ka", "va", "qb", "kvb", "gb", "qc", "kc", "vc")
    return pl.pallas_call(
        _in_proj_kernel,
        grid=(b, s // tm),
        in_specs=[row(d), vec, vec, pl.BlockSpec((1, d), lambda bi, i: (0, 0)),
                  pl.BlockSpec((d, IN_W_PACKED), lambda bi, i: (0, 0))] + [tab] * 6,
        out_specs=[row(_COLS[n][1]) for n in names],
        out_shape=[jax.ShapeDtypeStruct((b, s, _COLS[n][1]), F32) for n in names],
        compiler_params=_cparams(("arbitrary", "arbitrary")),
        name="in_projection",
    )(x, sc, sh, g, w_packed, *tabs)


def _pack_w_in(w_in):
    widths = (MOBA_W, MOBA_W, MOBA_W, NSA_W, NSA_W, 3 * NSA_HEADS, SB_W, SB_W, SB_W)
    offs = np.cumsum((0,) + widths)
    parts = []
    for i, w in enumerate(widths):
        blk = w_in[:, offs[i]:offs[i + 1]]
        if w == 3 * NSA_HEADS:
            blk = jnp.pad(blk, ((0, 0), (0, _GATE_PAD - w)))
        parts.append(blk)
    return jnp.concatenate(parts, axis=1).astype(BF16)


def _rope_tables(s):
    half = ROPE_DIM // 2
    inv_freq = ROPE_THETA ** (-jnp.arange(0, ROPE_DIM, 2, dtype=F32) / ROPE_DIM)
    ang = jnp.arange(s, dtype=F32)[:, None] * inv_freq[None, :]
    cos, sin = jnp.cos(ang), jnp.sin(ang)
    zeros = jnp.zeros((s, HEAD_DIM - ROPE_DIM), F32)
    z8 = jnp.zeros((s, half), F32)
    cs_h = jnp.concatenate([cos, cos, zeros + 1.0], axis=1)
    sm_h = jnp.concatenate([-sin, z8, zeros], axis=1)
    sp_h = jnp.concatenate([z8, sin, zeros], axis=1)
    ident = (jnp.ones((s, HEAD_DIM), F32), jnp.zeros((s, HEAD_DIM), F32), jnp.zeros((s, HEAD_DIM), F32))
    qk = tuple(jnp.concatenate([t, t], axis=1) for t in (cs_h, sm_h, sp_h))
    kv = tuple(jnp.concatenate([t, i], axis=1) for t, i in zip((cs_h, sm_h, sp_h), ident))
    return qk + kv


OUT_TM = 256


def _out_proj_kernel(oa_ref, ob_ref, oc_ref, x_ref, gg_ref, w_ref, g1_ref, n1_ref, n2_ref,
                     sc_ref, sh_ref, rw_ref, rb_ref, x1_ref, hf_ref, lg_ref):
    gg = gg_ref[...]
    y = jnp.concatenate([_rms(oa_ref[0], gg[:, :MOBA_W]),
                         _rms(ob_ref[0], gg[:, MOBA_W:MOBA_W + NSA_W]),
                         _rms(oc_ref[0], gg[:, MOBA_W + NSA_W:])], axis=1)
    y = _dot(y.astype(BF16), w_ref[...])
    x1 = x_ref[0] + g1_ref[0] * _rms(y, n1_ref[...])
    x1_ref[0] = x1
    hf = _rms(x1, n2_ref[...]) * (1.0 + sc_ref[0]) + sh_ref[0]
    hf_ref[0] = hf
    lg_ref[...] = lax.dot_general(rw_ref[...], hf, (((1,), (1,)), ((), ())),
                                  precision=HI, preferred_element_type=F32) + rb_ref[...]


def out_projection(oa, ob, oc, x, grp_g, w_out, g1, n1, n2, sc2, sh2, rw_t, rb):
    b, s, d = x.shape
    ne = rw_t.shape[0]
    tm = OUT_TM
    steps = s // tm
    row = lambda w: pl.BlockSpec((1, tm, w), lambda bi, i: (bi, i, 0))
    vec = pl.BlockSpec((1, 1, d), lambda bi, i: (bi, 0, 0))
    cst = lambda r, w: pl.BlockSpec((r, w), lambda bi, i: (0, 0))
    return pl.pallas_call(
        _out_proj_kernel,
        grid=(b, steps),
        in_specs=[row(MOBA_W), row(NSA_W), row(SB_W), row(d), cst(1, d), cst(d, d), vec,
                  cst(1, d), cst(1, d), vec, vec, cst(ne, d), cst(ne, 1)],
        out_specs=[row(d), row(d), pl.BlockSpec((ne, tm), lambda bi, i: (0, bi * steps + i))],
        out_shape=[jax.ShapeDtypeStruct((b, s, d), F32), jax.ShapeDtypeStruct((b, s, d), F32),
                   jax.ShapeDtypeStruct((ne, b * s), F32)],
        compiler_params=_cparams(("arbitrary", "arbitrary")),
        name="out_projection",
    )(oa, ob, oc, x, grp_g, w_out, g1, n1, n2, sc2, sh2, rw_t, rb)


ROUTE_TT = 512


def _router_kernel(lg_ref, idx_ref, gate_ref, rank_ref, cnt_ref):
    ne, tt = lg_ref.shape

    @pl.when(pl.program_id(0) == 0)
    def _():
        cnt_ref[...] = jnp.zeros(cnt_ref.shape, cnt_ref.dtype)

    v = lg_ref[...]
    erow = lax.broadcasted_iota(jnp.int32, (ne, tt), 0)
    vals, hots, firsts = [], [], []
    for _ in range(TOP_K):
        m = jnp.max(v, axis=0, keepdims=True)
        first = jnp.min(jnp.where(v == m, erow, ne), axis=0, keepdims=True)
        hot = erow == first
        v = jnp.where(hot, -jnp.inf, v)
        vals.append(m)
        hots.append(hot)
        firsts.append(first)
    exps = [jnp.exp(val - vals[0]) for val in vals]
    den = exps[0] + exps[1] + exps[2] + exps[3]
    gate_ref[...] = jnp.concatenate([e / den for e in exps], axis=0)
    idx_ref[...] = jnp.concatenate(firsts, axis=0)

    cnt = jnp.zeros((ne, tt), F32)
    for hot in hots:
        cnt = cnt + jnp.where(hot, 1.0, 0.0)
    before = (lax.broadcasted_iota(jnp.int32, (tt, tt), 0)
              < lax.broadcasted_iota(jnp.int32, (tt, tt), 1)).astype(BF16)
    excl = _dot(cnt.astype(BF16), before) + cnt_ref[:, 0:1]
    ranks = [jnp.sum(jnp.where(hot, excl, 0.0), axis=0, keepdims=True) for hot in hots]
    rank_ref[...] = jnp.concatenate(ranks, axis=0).astype(jnp.int32)
    cnt_ref[...] = cnt_ref[...] + jnp.sum(cnt, axis=1, keepdims=True)


def moe_router(logits_t):
    ne, n = logits_t.shape
    tt = ROUTE_TT
    slot = pl.BlockSpec((TOP_K, tt), lambda i: (0, i))
    return pl.pallas_call(
        _router_kernel,
        grid=(n // tt,),
        in_specs=[pl.BlockSpec((ne, tt), lambda i: (0, i))],
        out_specs=[slot, slot, slot, pl.BlockSpec((ne, LANES), lambda i: (0, 0))],
        out_shape=[jax.ShapeDtypeStruct((TOP_K, n), jnp.int32), jax.ShapeDtypeStruct((TOP_K, n), F32),
                   jax.ShapeDtypeStruct((TOP_K, n), jnp.int32), jax.ShapeDtypeStruct((ne, LANES), F32)],
        compiler_params=_cparams(("arbitrary",)),
        name="moe_router",
    )(logits_t)


MOE_TM = 256


def _dispatch_kernel(dest_ref, zrow_ref, hf_ref, xs_ref, zbuf, sem, zsem):
    tt = hf_ref.shape[0]
    n = dest_ref.shape[0] // TOP_K
    base = pl.program_id(0) * tt

    @pl.when(pl.program_id(0) == 0)
    def _():
        zbuf[...] = jnp.zeros(zbuf.shape, zbuf.dtype)

        def fill(j):
            row = pl.multiple_of(jnp.maximum(zrow_ref[j], 0), MOE_TM)
            return pltpu.make_async_copy(zbuf, xs_ref.at[pl.ds(row, MOE_TM), :], zsem)

        for j in range(zrow_ref.shape[0]):
            pl.when(zrow_ref[j] >= 0)(fill(j).start)
        for j in range(zrow_ref.shape[0]):
            pl.when(zrow_ref[j] >= 0)(fill(j).wait)

    def issue(t, carry):
        for k in range(TOP_K):
            row = dest_ref[k * n + base + t]
            pltpu.make_async_copy(hf_ref.at[pl.ds(t, 1), :], xs_ref.at[pl.ds(row, 1), :], sem).start()
        return carry

    lax.fori_loop(0, tt, issue, 0, unroll=8)
    for _ in range(TOP_K):
        pltpu.make_async_copy(hf_ref, xs_ref.at[pl.ds(0, tt), :], sem).wait()


DISPATCH_TT = 256


def moe_dispatch(dest_flat, zero_rows, hf, n_rows):
    n, d = hf.shape
    tt = DISPATCH_TT
    grid_spec = pltpu.PrefetchScalarGridSpec(
        num_scalar_prefetch=2,
        grid=(n // tt,),
        in_specs=[pl.BlockSpec((tt, d), lambda i, dest, zrow: (i, 0))],
        out_specs=pl.BlockSpec(memory_space=pl.ANY),
        scratch_shapes=[pltpu.VMEM((MOE_TM, d), F32), pltpu.SemaphoreType.DMA, pltpu.SemaphoreType.DMA],
    )
    return pl.pallas_call(
        _dispatch_kernel,
        grid_spec=grid_spec,
        out_shape=jax.ShapeDtypeStruct((n_rows, d), F32),
        compiler_params=_cparams(("arbitrary",)),
        name="moe_dispatch",
    )(dest_flat, zero_rows, hf)


def _expert_kernel(be_ref, nu_ref, xs_ref, wgu_ref, bgu_ref, wdn_ref, bdn_ref, y_ref,
                   wgu_bf, wdn_bf):
    i = pl.program_id(0)
    used = i < nu_ref[0]

    @pl.when(used & ((i == 0) | (be_ref[i] != be_ref[jnp.maximum(i - 1, 0)])))
    def _():
        wgu_bf[...] = wgu_ref[0].astype(BF16)
        wdn_bf[...] = wdn_ref[0].astype(BF16)

    @pl.when(used)
    def _():
        hgu = _dot(xs_ref[...].astype(BF16), wgu_bf[...]) + bgu_ref[0]
        de = hgu.shape[1] // 2
        glu = jnp.minimum(hgu[:, :de], SWIGLU_LIMIT)
        lin = jnp.clip(hgu[:, de:], -SWIGLU_LIMIT, SWIGLU_LIMIT)
        act = glu * jax.nn.sigmoid(SWIGLU_ALPHA * glu) * (lin + 1.0)
        y_ref[...] = _dot(act.astype(BF16), wdn_bf[...]) + bdn_ref[0]

    @pl.when(jnp.logical_not(used))
    def _():
        y_ref[...] = jnp.zeros(y_ref.shape, y_ref.dtype)


EXPERT_VMEM_LIMIT = 56 * 1024 * 1024


def expert_ffn(layer, blk_exp, n_used, xs, w_gu, b_gu, w_dn, b_dn):
    r, d = xs.shape
    nl, ne, _, wide = w_gu.shape
    w_gu = w_gu.reshape(nl * ne, d, wide)
    w_dn = w_dn.reshape(nl * ne, wide // 2, d)
    b_gu = b_gu.reshape(nl * ne, wide)
    b_dn = b_dn.reshape(nl * ne, d)
    ne, base = nl * ne, layer * ne
    tm = MOE_TM
    grid_spec = pltpu.PrefetchScalarGridSpec(
        num_scalar_prefetch=2,
        grid=(r // tm,),
        in_specs=[pl.BlockSpec((tm, d), lambda i, be, nu: (i, 0)),
                  pl.BlockSpec((1, d, wide), lambda i, be, nu: (base + be[i], 0, 0)),
                  pl.BlockSpec((1, 1, wide), lambda i, be, nu: (base + be[i], 0, 0)),
                  pl.BlockSpec((1, wide // 2, d), lambda i, be, nu: (base + be[i], 0, 0)),
                  pl.BlockSpec((1, 1, d), lambda i, be, nu: (base + be[i], 0, 0))],
        out_specs=pl.BlockSpec((tm, d), lambda i, be, nu: (i, 0)),
        scratch_shapes=[pltpu.VMEM((d, wide), BF16), pltpu.VMEM((wide // 2, d), BF16)],
    )
    return pl.pallas_call(
        _expert_kernel,
        grid_spec=grid_spec,
        out_shape=jax.ShapeDtypeStruct((r, d), F32),
        compiler_params=pltpu.CompilerParams(dimension_semantics=("arbitrary",),
                                             vmem_limit_bytes=EXPERT_VMEM_LIMIT),
        name="expert_ffn",
    )(blk_exp, n_used, xs, w_gu, b_gu.reshape(ne, 1, wide), w_dn, b_dn.reshape(ne, 1, d))


COMBINE_TT = 128


def _combine_kernel(dest_ref, gate_ref, x_ref, g2_ref, n3_ref, y_ref, o_ref, ybuf, sems):
    tt = x_ref.shape[0]
    n = dest_ref.shape[0] // TOP_K
    i = pl.program_id(0)
    steps = pl.num_programs(0)

    def gather(step, slot):
        base = step * tt

        def issue(t, carry):
            for k in range(TOP_K):
                row = dest_ref[k * n + base + t]
                pltpu.make_async_copy(y_ref.at[pl.ds(row, 1), :], ybuf.at[slot, k, pl.ds(t, 1), :],
                                      sems.at[slot]).start()
            return carry

        lax.fori_loop(0, tt, issue, 0, unroll=8)

    @pl.when(i == 0)
    def _():
        gather(0, 0)

    @pl.when(i + 1 < steps)
    def _():
        gather(i + 1, (i + 1) % 2)

    slot = i % 2
    for k in range(TOP_K):
        pltpu.make_async_copy(y_ref.at[pl.ds(0, tt), :], ybuf.at[slot, k], sems.at[slot]).wait()
    gate = gate_ref[...]
    y = gate[:, 0:1] * ybuf[slot, 0]
    for k in range(1, TOP_K):
        y = y + gate[:, k:k + 1] * ybuf[slot, k]
    o_ref[...] = x_ref[...] + g2_ref[0] * _rms(y, n3_ref[...])


def moe_combine(dest_flat, gate, x, g2, n3, y_rows):
    b, s, d = x.shape
    n = b * s
    tt = COMBINE_TT
    per_batch = s // tt
    grid_spec = pltpu.PrefetchScalarGridSpec(
        num_scalar_prefetch=1,
        grid=(n // tt,),
        in_specs=[pl.BlockSpec((tt, LANES), lambda i, dest: (i, 0)),
                  pl.BlockSpec((tt, d), lambda i, dest: (i, 0)),
                  pl.BlockSpec((1, 1, d), lambda i, dest: (i // per_batch, 0, 0)),
                  pl.BlockSpec((1, d), lambda i, dest: (0, 0)),
                  pl.BlockSpec(memory_space=pl.ANY)],
        out_specs=pl.BlockSpec((tt, d), lambda i, dest: (i, 0)),
        scratch_shapes=[pltpu.VMEM((2, TOP_K, tt, d), F32), pltpu.SemaphoreType.DMA((2,))],
    )
    out = pl.pallas_call(
        _combine_kernel,
        grid_spec=grid_spec,
        out_shape=jax.ShapeDtypeStruct((n, d), F32),
        compiler_params=_cparams(("arbitrary",)),
        name="moe_combine",
    )(dest_flat, gate, x.reshape(n, d), g2, n3, y_rows)
    return out.reshape(b, s, d)


def _dispatch_plan(idx, rank, counts):
    n = idx.shape[1]
    tm = MOE_TM
    padded = (counts + tm - 1) // tm * tm
    pad_end = jnp.cumsum(padded)
    start = pad_end - padded
    experts = jnp.arange(N_EXPERTS, dtype=jnp.int32)
    dest = rank + jnp.sum(jnp.where(idx[..., None] == experts, start, 0), axis=-1)
    n_rows = -(-(n * TOP_K + N_EXPERTS * (tm - 1)) // tm) * tm
    n_blk = n_rows // tm
    blk_start = jnp.arange(n_blk, dtype=jnp.int32) * tm
    blk_exp = jnp.minimum(jnp.sum((pad_end[None, :] <= blk_start[:, None]).astype(jnp.int32), axis=1),
                          N_EXPERTS - 1)
    n_used = (pad_end[-1] // tm).astype(jnp.int32).reshape(1)
    tail = pad_end[-1] + experts * tm
    zero_rows = jnp.concatenate([jnp.where(padded > 0, pad_end - tm, -1),
                                 jnp.where(tail < n_rows, tail, -1)]).astype(jnp.int32)
    return dest.reshape(-1), blk_exp, n_used, zero_rows, n_rows


def _layer(layer, x, mod, norm_g, w_in, cmp_pe, cmp_w1, cmp_w2, grp_g, w_out, router_w, router_b,
           w_gu, b_gu, w_dn, b_dn, tabs, overlap):
    b, s, d = x.shape
    sh1, sc1, g1, sh2, sc2, g2 = (m.reshape(b, 1, d) for m in jnp.split(mod, 6, axis=-1))
    ng = norm_g.reshape(4, 1, d)

    qa, ka, va, qb, kvb, gb, qc, kc, vc = in_projection(x, sc1, sh1, ng[0], _pack_w_in(w_in), tabs)
    oa = moba_attention(qa, ka, va)
    nc = s // NSA_CMP_STRIDE
    kv16 = jnp.stack([kvb[:, :, :HEAD_DIM], kvb[:, :, HEAD_DIM:2 * HEAD_DIM]], axis=1)
    kv16 = kv16.reshape(b, 2, nc, NSA_CMP_STRIDE * HEAD_DIM)
    cmp_kv = nsa_compress(kv16, cmp_pe.reshape(2, 1, NSA_CMP_LEN * HEAD_DIM), cmp_w1, cmp_w2)
    ob = nsa_attention(qb, kvb, gb, cmp_kv, overlap)
    oc = sb_attention(qc, kc, vc)

    x1, hf, logits_t = out_projection(oa, ob, oc, x, grp_g.reshape(1, d), w_out.astype(BF16), g1,
                                      ng[1], ng[2], sc2, sh2, router_w.T, router_b.reshape(-1, 1))

    n = b * s
    idx, gate, rank, counts = moe_router(logits_t)
    dest, blk_exp, n_used, zero_rows, n_rows = _dispatch_plan(idx, rank, counts[:, 0].astype(jnp.int32))
    xs = moe_dispatch(dest, zero_rows, hf.reshape(n, d), n_rows)
    y_rows = expert_ffn(layer, blk_exp, n_used, xs, w_gu, b_gu, w_dn, b_dn)
    gate = jnp.pad(gate.T, ((0, 0), (0, LANES - TOP_K)))
    return moe_combine(dest, gate, x1, g2, ng[3], y_rows)


def kernel(x, c, ada_w, ada_b, norm_g, w_in, nsa_cmp_pe, nsa_cmp_w1, nsa_cmp_w2, mix_out_g,
           w_out, router_w, router_b, exp_w_gu, exp_b_gu, exp_w_dn, exp_b_dn):
    s = x.shape[1]
    tabs = _rope_tables(s)
    overlap = _nsa_overlap(s)
    mod = ada_modulation(c, ada_w, ada_b)
    for l in range(ada_w.shape[0]):
        x = _layer(l, x, mod[l], norm_g[l], w_in[l], nsa_cmp_pe[l], nsa_cmp_w1[l], nsa_cmp_w2[l],
                   mix_out_g[l], w_out[l], router_w[l], router_b[l], exp_w_gu, exp_b_gu,
                   exp_w_dn, exp_b_dn, tabs, overlap)
    return x
```

```python
import functools

import numpy as np
import jax
import jax.numpy as jnp
from jax import lax
from jax.experimental import pallas as pl
from jax.experimental.pallas import tpu as pltpu

F32 = jnp.float32
BF16 = jnp.bfloat16
HI = lax.Precision.HIGHEST

D_MODEL = 1024
N_HEADS = 16
HEAD_DIM = 64
MOBA_HEADS = 4
NSA_HEADS = 6
SB_HEADS = 6
MOBA_W = MOBA_HEADS * HEAD_DIM
NSA_W = NSA_HEADS * HEAD_DIM
SB_W = SB_HEADS * HEAD_DIM
ROPE_DIM = 16
ROPE_THETA = 500000.0
MOBA_BLOCK = 256
MOBA_TOPK = 3
NSA_CMP_LEN = 32
NSA_CMP_STRIDE = 16
NSA_SLC_BLOCK = 64
NSA_SLC_TOPK = 16
NSA_WINDOW = 512
N_EXPERTS = 32
TOP_K = 4
SWIGLU_LIMIT = 7.0
SWIGLU_ALPHA = 1.702
RMS_EPS = 1e-6
NEG_INF = -1e30
SEL_FORCE = 1e4
SCALE = HEAD_DIM ** -0.5
LOG2E = 1.4426950408889634

LANES = 128
VMEM_LIMIT = 48 * 1024 * 1024


def _cparams(sem):
    return pltpu.CompilerParams(dimension_semantics=sem, vmem_limit_bytes=VMEM_LIMIT)


def _dot(a, b):
    return jnp.dot(a, b, preferred_element_type=F32)


def _dot_nt(a, b):
    return lax.dot_general(a, b, (((1,), (1,)), ((), ())), preferred_element_type=F32)


SB_TILE = 256


SB_GROUP_W = SB_W


def _sb_kernel(q_ref, k_ref, v_ref, o_ref, ks_ref, vs_ref):
    T = SB_TILE
    H = ks_ref.shape[0]
    qi = pl.program_id(2)

    @pl.when(qi == 0)
    def _():
        for h in range(H):
            ks_ref[h] = k_ref[0, :, h * HEAD_DIM:(h + 1) * HEAD_DIM].astype(BF16)
            vs_ref[h] = v_ref[0, :, h * HEAD_DIM:(h + 1) * HEAD_DIM].astype(BF16)

    row = lax.broadcasted_iota(jnp.int32, (T, T), 0)
    col = lax.broadcasted_iota(jnp.int32, (T, T), 1)
    incl = (row >= col).astype(BF16)
    sum_rhs = jnp.concatenate([incl, incl], axis=0)
    diag_mask = col < row
    qs = [(q_ref[0, :, h * HEAD_DIM:(h + 1) * HEAD_DIM] * (SCALE * LOG2E)).astype(BF16)
          for h in range(H)]
    sign = jnp.uint32(0x80000000)

    def scores(h, j):
        return _dot_nt(qs[h], ks_ref[h, pl.ds(pl.multiple_of(j * T, T), T), :])

    def neg_log2_keep(z, masked):
        neg_abs = lax.bitcast_convert_type(lax.bitcast_convert_type(z, jnp.uint32) | sign, F32)
        nlk = jnp.maximum(z, 0.0) + jnp.log2(1.0 + jnp.exp2(neg_abs))
        return jnp.where(diag_mask, nlk, 0.0) if masked else nlk

    def suffix_sums(nlk):
        hi = nlk.astype(BF16)
        lo = (nlk - hi.astype(F32)).astype(BF16)
        return _dot(jnp.concatenate([hi, lo], axis=1), sum_rhs)

    def weights(z, sums, run, masked):
        x = z - sums - run
        if masked:
            x = jnp.where(diag_mask, x, NEG_INF)
        return jnp.exp2(x.astype(BF16))

    def pv(h, p, j):
        return _dot(p, vs_ref[h, pl.ds(pl.multiple_of(j * T, T), T), :])

    def step(zs, runs, masked):
        sums = [suffix_sums(neg_log2_keep(zs[h], masked)) for h in range(H)]
        ps = [weights(zs[h], sums[h], runs[h], masked) for h in range(H)]
        return ps, [runs[h] + sums[h][:, 0:1] for h in range(H)]

    zeros = jnp.zeros((T, 1), F32)
    ps, runs = step([scores(h, qi) for h in range(H)], [zeros] * H, True)
    accs = [pv(h, ps[h], qi) for h in range(H)]

    def past(i, carry):
        runs, accs = carry
        j = qi - 1 - i
        ps, runs = step([scores(h, j) for h in range(H)], runs, False)
        return runs, [accs[h] + pv(h, ps[h], j) for h in range(H)]

    runs, accs = lax.fori_loop(0, qi, past, (runs, accs))
    o_ref[0] = jnp.concatenate(accs, axis=1)


def sb_attention(q, k, v):
    b, s, w = q.shape
    T = SB_TILE
    gw = SB_GROUP_W
    nh = gw // HEAD_DIM
    qspec = pl.BlockSpec((1, T, gw), lambda bi, p, i: (bi, i, p))
    kvspec = pl.BlockSpec((1, s, gw), lambda bi, p, i: (bi, 0, p))
    return pl.pallas_call(
        _sb_kernel,
        grid=(b, w // gw, s // T),
        in_specs=[qspec, kvspec, kvspec],
        out_specs=qspec,
        out_shape=jax.ShapeDtypeStruct((b, s, w), F32),
        scratch_shapes=[pltpu.VMEM((nh, s, HEAD_DIM), BF16), pltpu.VMEM((nh, s, HEAD_DIM), BF16)],
        compiler_params=_cparams(("arbitrary", "arbitrary", "arbitrary")),
        name="sb_attention",
    )(q, k, v)


def _rank_before_t(vals, n):
    idx = lax.broadcasted_iota(jnp.int32, vals.shape, 0)
    rank = jnp.zeros(vals.shape, F32)
    for j2 in range(n):
        other = vals[j2:j2 + 1, :]
        ahead = (other > vals) | ((other == vals) & (idx > j2))
        rank = rank + jnp.where(ahead, 1.0, 0.0)
    return rank


MOBA_KT = 2 * MOBA_BLOCK
MOBA_GROUP_W = MOBA_W


def _moba_kernel(q_ref, k_ref, v_ref, o_ref, ks_ref, vs_ref, km_ref):
    T, KT = MOBA_BLOCK, MOBA_KT
    H = ks_ref.shape[0]
    nb = km_ref.shape[1]
    s_len = ks_ref.shape[1]
    qi = pl.program_id(2)

    @pl.when(qi == 0)
    def _():
        key_blk = lax.broadcasted_iota(jnp.int32, (s_len, HEAD_DIM), 0) // T
        onehot = (lax.broadcasted_iota(jnp.int32, (s_len, HEAD_DIM), 1) == key_blk).astype(F32)
        ones_col = (lax.broadcasted_iota(jnp.int32, (s_len, HEAD_DIM), 1) == 0).astype(F32)
        for h in range(H):
            kh = k_ref[0, :, h * HEAD_DIM:(h + 1) * HEAD_DIM]
            ks_ref[h] = jnp.concatenate([kh, onehot], axis=1).astype(BF16)
            vs_ref[h] = jnp.concatenate([v_ref[0, :, h * HEAD_DIM:(h + 1) * HEAD_DIM], ones_col],
                                        axis=1).astype(BF16)
            km_ref[h] = jnp.mean(kh.reshape(nb, T, HEAD_DIM), axis=1)

    row = lax.broadcasted_iota(jnp.int32, (T, T), 0)
    col = lax.broadcasted_iota(jnp.int32, (T, T), 1)
    causal = col <= row
    blk_t = lax.broadcasted_iota(jnp.int32, (nb, T), 0)
    start = pl.multiple_of(qi * T, T)

    qfs = [q_ref[0, :, h * HEAD_DIM:(h + 1) * HEAD_DIM] for h in range(H)]
    qss = [qf * SCALE for qf in qfs]
    s_own = [_dot_nt(qss[h].astype(BF16), ks_ref[h, pl.ds(start, T), 0:HEAD_DIM]) for h in range(H)]
    q_aug, state = [], []
    for h in range(H):
        gate = lax.dot_general(km_ref[h], qfs[h], (((1,), (1,)), ((), ())),
                               precision=HI, preferred_element_type=F32)
        gate = jnp.where(blk_t < qi, gate, NEG_INF)
        sel = (_rank_before_t(gate, nb) < float(MOBA_TOPK)) & (gate > 0.5 * NEG_INF)
        selb = jnp.where(sel, 0.0, NEG_INF)
        selb = jnp.concatenate([selb, jnp.full((LANES - nb, T), NEG_INF, F32)], axis=0).T
        q_aug.append(jnp.concatenate([qss[h], selb[:, :HEAD_DIM]], axis=1).astype(BF16))

        s = jnp.where(causal, s_own[h], NEG_INF)
        m = jnp.max(s, axis=1, keepdims=True)
        p = jnp.exp((s - m).astype(BF16))
        state += [m, _dot(p, vs_ref[h, pl.ds(start, T), :])]

    def past(i, carry):
        st = pl.multiple_of(i * KT, KT)
        ss = [_dot_nt(q_aug[h], ks_ref[h, pl.ds(st, KT), :]) for h in range(H)]
        out = []
        for h in range(H):
            m, acc = carry[2 * h:2 * h + 2]
            m_new = jnp.maximum(m, jnp.max(ss[h], axis=1, keepdims=True))
            alpha = jnp.exp(m - m_new)
            p = jnp.exp((ss[h] - m_new).astype(BF16))
            out += [m_new, alpha * acc + _dot(p, vs_ref[h, pl.ds(st, KT), :])]
        return tuple(out)

    state = lax.fori_loop(0, (qi + 1) // 2, past, tuple(state))
    outs = [state[2 * h + 1][:, :HEAD_DIM] / jnp.maximum(state[2 * h + 1][:, HEAD_DIM:HEAD_DIM + 1], 1e-30)
            for h in range(H)]
    o_ref[0] = jnp.concatenate(outs, axis=1)


def moba_attention(q, k, v):
    b, s, w = q.shape
    T = MOBA_BLOCK
    nb = s // T
    assert nb <= HEAD_DIM and s % MOBA_KT == 0
    gw = MOBA_GROUP_W
    nh = gw // HEAD_DIM
    qspec = pl.BlockSpec((1, T, gw), lambda bi, p, i: (bi, i, p))
    kvspec = pl.BlockSpec((1, s, gw), lambda bi, p, i: (bi, 0, p))
    return pl.pallas_call(
        _moba_kernel,
        grid=(b, w // gw, nb),
        in_specs=[qspec, kvspec, kvspec],
        out_specs=qspec,
        out_shape=jax.ShapeDtypeStruct((b, s, w), F32),
        scratch_shapes=[pltpu.VMEM((nh, s, LANES), BF16), pltpu.VMEM((nh, s, LANES), BF16),
                        pltpu.VMEM((nh, nb, HEAD_DIM), F32)],
        compiler_params=_cparams(("arbitrary", "arbitrary", "arbitrary")),
        name="moba_attention",
    )(q, k, v)


def _nsa_compress_kernel(x_ref, pe_ref, w1_ref, w2_ref, o_ref):
    nc = x_ref.shape[2]
    half = NSA_CMP_STRIDE * HEAD_DIM
    x = x_ref[0, 0]
    w1 = w1_ref[0]
    first = jnp.dot(x, w1[:half], precision=HI, preferred_element_type=F32)
    second = jnp.dot(x, w1[half:], precision=HI, preferred_element_type=F32)
    pe = jnp.broadcast_to(pe_ref[0], (8, 2 * half))
    peb = jnp.dot(pe, w1, precision=HI, preferred_element_type=F32)[0:1]
    pre = first + pltpu.roll(second, nc - 1, 0) + peb
    hid = pre * jax.nn.sigmoid(pre)
    o_ref[0, 0] = jnp.dot(hid, w2_ref[0], precision=HI, preferred_element_type=F32)


def nsa_compress(kv16, pe, w1, w2):
    b, _, nc, wide = kv16.shape
    return pl.pallas_call(
        _nsa_compress_kernel,
        grid=(b, 2),
        in_specs=[pl.BlockSpec((1, 1, nc, wide), lambda bi, i: (bi, i, 0, 0)),
                  pl.BlockSpec((1, 1, 2 * wide), lambda bi, i: (i, 0, 0)),
                  pl.BlockSpec((1, 2 * wide, HEAD_DIM), lambda bi, i: (i, 0, 0)),
                  pl.BlockSpec((1, HEAD_DIM, HEAD_DIM), lambda bi, i: (i, 0, 0))],
        out_specs=pl.BlockSpec((1, 1, nc, HEAD_DIM), lambda bi, i: (bi, i, 0, 0)),
        out_shape=jax.ShapeDtypeStruct((b, 2, nc, HEAD_DIM), F32),
        compiler_params=_cparams(("arbitrary", "arbitrary")),
        name="nsa_compress",
    )(kv16, pe, w1, w2)


NSA_TQ = 256
NSA_KT = 512
NSA_SPAN = NSA_WINDOW + NSA_TQ
NSA_CHAIN_HEADS = 2


def _softmax_rows(s, mask):
    s = jnp.where(mask, s, NEG_INF)
    m = jnp.max(s, axis=-1, keepdims=True)
    e = jnp.where(mask, jnp.exp(s - m), 0.0)
    return e / jnp.maximum(jnp.sum(e, axis=-1, keepdims=True), 1e-30)


def _split_bf16(x):
    hi = x.astype(BF16)
    return hi, (x - hi.astype(F32)).astype(BF16)


def _nsa_kernel(q_ref, cmp_ref, slc_ref, win_ref, g_ref, ov_ref, o_ref,
                ksl_ref, vsl_ref, kw_ref, vw_ref, kc_ref):
    TQ, KT, H = NSA_TQ, NSA_KT, NSA_HEADS
    ns, nc = ov_ref.shape
    s_len = ksl_ref.shape[0]
    qi = pl.program_id(1)
    t0 = qi * TQ

    @pl.when(qi == 0)
    def _():
        lane = lax.broadcasted_iota(jnp.int32, (s_len, HEAD_DIM), 1)
        key_blk = lax.broadcasted_iota(jnp.int32, (s_len, HEAD_DIM), 0) // NSA_SLC_BLOCK
        onehot = (lane == key_blk).astype(F32)
        ones_col = (lane == 0).astype(F32)
        ksl_ref[...] = jnp.concatenate([slc_ref[0, :, :HEAD_DIM], onehot], axis=1).astype(BF16)
        vsl_ref[...] = jnp.concatenate([slc_ref[0, :, HEAD_DIM:], ones_col], axis=1).astype(BF16)
        kw_ref[...] = win_ref[0, :, :HEAD_DIM].astype(BF16)
        vw_ref[...] = jnp.concatenate([win_ref[0, :, HEAD_DIM:], ones_col], axis=1).astype(BF16)
        k_hi, k_lo = _split_bf16(cmp_ref[0, 0])
        kc_ref[...] = jnp.concatenate([k_hi, k_hi, k_lo], axis=1)

    qf = jnp.concatenate([q_ref[0, :, h * HEAD_DIM:(h + 1) * HEAD_DIM] for h in range(H)],
                         axis=0) * SCALE
    q, q_lo = _split_bf16(qf)

    tq_c = t0 + lax.broadcasted_iota(jnp.int32, (TQ, nc), 0)
    n_c = lax.broadcasted_iota(jnp.int32, (TQ, nc), 1)
    mask_c = (n_c * NSA_CMP_STRIDE + (NSA_CMP_LEN - 1) <= tq_c) & (n_c < nc - 1)
    s_c = _dot_nt(jnp.concatenate([q, q_lo, q], axis=1), kc_ref[...])
    p_c = _softmax_rows(s_c.reshape(H, TQ, nc), mask_c[None])
    o_c = _dot(p_c.reshape(H * TQ, nc).astype(BF16), cmp_ref[0, 1].astype(BF16))

    imp = lax.dot_general(ov_ref[...], jnp.sum(p_c, axis=0), (((1,), (1,)), ((), ())),
                          precision=HI, preferred_element_type=F32)
    tq_s = t0 + lax.broadcasted_iota(jnp.int32, (ns, TQ), 1)
    blk = lax.broadcasted_iota(jnp.int32, (ns, TQ), 0)
    own = tq_s // NSA_SLC_BLOCK
    forced = (blk == 0) | (blk == own) | (blk == own - 1)
    imp = jnp.where(forced, SEL_FORCE, imp)
    imp = jnp.where(blk <= own, imp, NEG_INF)
    sel = (_rank_before_t(imp, ns) < float(min(NSA_SLC_TOPK, ns))) & (imp > 0.5 * NEG_INF)
    selb = jnp.where(sel, 0.0, NEG_INF)
    if ns < LANES:
        selb = jnp.concatenate([selb, jnp.full((LANES - ns, TQ), NEG_INF, F32)], axis=0)
    selb = selb.T[:, :HEAD_DIM]
    q_aug = jnp.concatenate([qf, jnp.concatenate([selb] * H, axis=0)], axis=1).astype(BF16)

    jd = t0 // KT
    start = pl.multiple_of(jd * KT, KT)
    kpos = start + lax.broadcasted_iota(jnp.int32, (TQ, KT), 1)
    tq_k = t0 + lax.broadcasted_iota(jnp.int32, (TQ, KT), 0)
    HC = NSA_CHAIN_HEADS
    G, R = H // HC, HC * TQ
    qa = [q_aug[c * R:(c + 1) * R] for c in range(G)]
    causal_k = (kpos <= tq_k)[None]

    def scores(c, st):
        return _dot_nt(qa[c], ksl_ref[pl.ds(st, KT), :]).reshape(HC, TQ, KT)

    def normalized(acc):
        return acc[:, :HEAD_DIM] / jnp.maximum(acc[:, HEAD_DIM:HEAD_DIM + 1], 1e-30)

    s_own = [scores(c, start) for c in range(G)]
    state = []
    for c in range(G):
        s = jnp.where(causal_k, s_own[c], NEG_INF)
        m = jnp.max(s, axis=-1, keepdims=True)
        p = jnp.exp((s - m).astype(BF16))
        state += [m, _dot(p.reshape(R, KT), vsl_ref[pl.ds(start, KT), :])]

    def past(j, carry):
        st = pl.multiple_of(j * KT, KT)
        ss = [scores(c, st) for c in range(G)]
        out = []
        for c in range(G):
            m, acc = carry[2 * c:2 * c + 2]
            m_new = jnp.maximum(m, jnp.max(ss[c], axis=-1, keepdims=True))
            alpha = jnp.exp(m - m_new)
            p = jnp.exp((ss[c] - m_new).astype(BF16))
            pv = _dot(p.reshape(R, KT), vsl_ref[pl.ds(st, KT), :])
            out += [m_new, alpha.reshape(R, 1) * acc + pv]
        return tuple(out)

    state = lax.fori_loop(0, jd, past, tuple(state))
    o_s = jnp.concatenate([normalized(state[2 * c + 1]) for c in range(G)], axis=0)

    w0 = pl.multiple_of(jnp.maximum(t0 - NSA_WINDOW, 0), TQ)
    wpos = w0 + lax.broadcasted_iota(jnp.int32, (TQ, NSA_SPAN), 1)
    tq_w = t0 + lax.broadcasted_iota(jnp.int32, (TQ, NSA_SPAN), 0)
    mask_w = ((wpos <= tq_w) & (wpos > tq_w - NSA_WINDOW))[None]
    s_w = jnp.where(mask_w, _dot_nt(q, kw_ref[pl.ds(w0, NSA_SPAN), :]).reshape(H, TQ, NSA_SPAN),
                    NEG_INF)
    p_w = jnp.exp((s_w - jnp.max(s_w, axis=-1, keepdims=True)).astype(BF16))
    o_w = normalized(_dot(p_w.reshape(H * TQ, NSA_SPAN), vw_ref[pl.ds(w0, NSA_SPAN), :]))

    g = g_ref[0]
    outs = []
    for h in range(H):
        rows = slice(h * TQ, (h + 1) * TQ)
        outs.append(g[:, 3 * h:3 * h + 1] * o_c[rows] + g[:, 3 * h + 1:3 * h + 2] * o_s[rows]
                    + g[:, 3 * h + 2:3 * h + 3] * o_w[rows])
    o_ref[0] = jnp.concatenate(outs, axis=1)


def nsa_attention(q, kv, gates, cmp_kv, overlap):
    b, s, w = q.shape
    ns, nc = overlap.shape
    assert ns <= HEAD_DIM
    TQ = NSA_TQ
    return pl.pallas_call(
        _nsa_kernel,
        grid=(b, s // TQ),
        in_specs=[pl.BlockSpec((1, TQ, w), lambda bi, i: (bi, i, 0)),
                  pl.BlockSpec((1, 2, nc, HEAD_DIM), lambda bi, i: (bi, 0, 0, 0)),
                  pl.BlockSpec((1, s, LANES), lambda bi, i: (bi, 0, 1)),
                  pl.BlockSpec((1, s, LANES), lambda bi, i: (bi, 0, 2)),
                  pl.BlockSpec((1, TQ, LANES), lambda bi, i: (bi, i, 0)),
                  pl.BlockSpec((ns, nc), lambda bi, i: (0, 0))],
        out_specs=pl.BlockSpec((1, TQ, w), lambda bi, i: (bi, i, 0)),
        out_shape=jax.ShapeDtypeStruct((b, s, w), F32),
        scratch_shapes=[pltpu.VMEM((s, LANES), BF16), pltpu.VMEM((s, LANES), BF16),
                        pltpu.VMEM((s, HEAD_DIM), BF16), pltpu.VMEM((s, LANES), BF16),
                        pltpu.VMEM((nc, 3 * HEAD_DIM), BF16)],
        compiler_params=_cparams(("arbitrary", "arbitrary")),
        name="nsa_attention",
    )(q, cmp_kv, kv, kv, gates, overlap)


def _nsa_overlap(s):
    nc = s // NSA_CMP_STRIDE
    ns = s // NSA_SLC_BLOCK
    cstart = np.arange(nc) * NSA_CMP_STRIDE
    cend = cstart + NSA_CMP_LEN - 1
    sstart = np.arange(ns) * NSA_SLC_BLOCK
    ov = (cstart[:, None] <= sstart[None, :] + NSA_SLC_BLOCK - 1) & (cend[:, None] >= sstart[None, :])
    ov[nc - 1] = False
    return jnp.asarray(ov.T.astype(np.float32))


def _mod_kernel(c_ref, w_ref, b_ref, o_ref):
    c = c_ref[...]
    act = c * jax.nn.sigmoid(c)
    o_ref[0] = jnp.dot(act, w_ref[0], precision=HI, preferred_element_type=F32) + b_ref[0]


def ada_modulation(c, ada_w, ada_b):
    nl, d, wide = ada_w.shape
    b = c.shape[0]
    tn = D_MODEL
    return pl.pallas_call(
        _mod_kernel,
        grid=(nl, wide // tn),
        in_specs=[pl.BlockSpec((b, d), lambda l, j: (0, 0)),
                  pl.BlockSpec((1, d, tn), lambda l, j: (l, 0, j)),
                  pl.BlockSpec((1, 1, tn), lambda l, j: (l, 0, j))],
        out_specs=pl.BlockSpec((1, b, tn), lambda l, j: (l, 0, j)),
        out_shape=jax.ShapeDtypeStruct((nl, b, wide), F32),
        compiler_params=_cparams(("arbitrary", "arbitrary")),
        name="ada_modulation",
    )(c, ada_w, ada_b.reshape(nl, 1, wide))


def _rms(x, g):
    return x * lax.rsqrt(jnp.mean(x * x, axis=-1, keepdims=True) + RMS_EPS) * g


_GATE_PAD = LANES
_COLS = {}
_off = 0
for _name, _w in (("qa", MOBA_W), ("ka", MOBA_W), ("va", MOBA_W), ("qb", NSA_W), ("kvb", NSA_W),
                  ("gb", _GATE_PAD), ("qc", SB_W), ("kc", SB_W), ("vc", SB_W)):
    _COLS[_name] = (_off, _w)
    _off += _w
IN_W_PACKED = _off
PROJ_TM = 256


def _rope_block(p, cs, sm, sp):
    return p * cs + pltpu.roll(p, LANES - ROPE_DIM // 2, 1) * sm + pltpu.roll(p, ROPE_DIM // 2, 1) * sp


def _in_proj_kernel(x_ref, sc_ref, sh_ref, g_ref, w_ref, cqk_ref, mqk_ref, pqk_ref,
                    ckv_ref, mkv_ref, pkv_ref,
                    qa_ref, ka_ref, va_ref, qb_ref, kvb_ref, gb_ref, qc_ref, kc_ref, vc_ref):
    hm = _rms(x_ref[0], g_ref[...]) * (1.0 + sc_ref[0]) + sh_ref[0]
    p = _dot(hm.astype(BF16), w_ref[...])
    qk = (cqk_ref[...], mqk_ref[...], pqk_ref[...])
    kv = (ckv_ref[...], mkv_ref[...], pkv_ref[...])

    def emit(ref, name, tabs):
        off, w = _COLS[name]
        for j in range(w // LANES):
            blk = p[:, off + j * LANES: off + (j + 1) * LANES]
            if tabs is not None:
                blk = _rope_block(blk, *tabs)
            ref[0, :, j * LANES:(j + 1) * LANES] = blk

    emit(qa_ref, "qa", qk)
    emit(ka_ref, "ka", qk)
    emit(va_ref, "va", None)
    emit(qb_ref, "qb", qk)
    emit(kvb_ref, "kvb", kv)
    emit(qc_ref, "qc", None)
    emit(kc_ref, "kc", None)
    emit(vc_ref, "vc", None)
    off, w = _COLS["gb"]
    gb_ref[0] = jax.nn.sigmoid(p[:, off:off + w])


def in_projection(x, sc, sh, g, w_packed, tabs):
    b, s, d = x.shape
    tm = PROJ_TM
    row = lambda w: pl.BlockSpec((1, tm, w), lambda bi, i: (bi, i, 0))
    vec = pl.BlockSpec((1, 1, d), lambda bi, i: (bi, 0, 0))
    tab = pl.BlockSpec((tm, LANES), lambda bi, i: (i, 0))
    names = ("qa", "ka", "va", "qb", "kvb", "gb", "qc", "kc", "vc")
    return pl.pallas_call(
        _in_proj_kernel,
        grid=(b, s // tm),
        in_specs=[row(d), vec, vec, pl.BlockSpec((1, d), lambda bi, i: (0, 0)),
                  pl.BlockSpec((d, IN_W_PACKED), lambda bi, i: (0, 0))] + [tab] * 6,
        out_specs=[row(_COLS[n][1]) for n in names],
        out_shape=[jax.ShapeDtypeStruct((b, s, _COLS[n][1]), F32) for n in names],
        compiler_params=_cparams(("arbitrary", "arbitrary")),
        name="in_projection",
    )(x, sc, sh, g, w_packed, *tabs)


def _pack_w_in(w_in):
    widths = (MOBA_W, MOBA_W, MOBA_W, NSA_W, NSA_W, 3 * NSA_HEADS, SB_W, SB_W, SB_W)
    offs = np.cumsum((0,) + widths)
    parts = []
    for i, w in enumerate(widths):
        blk = w_in[:, offs[i]:offs[i + 1]]
        if w == 3 * NSA_HEADS:
            blk = jnp.pad(blk, ((0, 0), (0, _GATE_PAD - w)))
        parts.append(blk)
    return jnp.concatenate(parts, axis=1).astype(BF16)


def _rope_tables(s):
    half = ROPE_DIM // 2
    inv_freq = ROPE_THETA ** (-jnp.arange(0, ROPE_DIM, 2, dtype=F32) / ROPE_DIM)
    ang = jnp.arange(s, dtype=F32)[:, None] * inv_freq[None, :]
    cos, sin = jnp.cos(ang), jnp.sin(ang)
    zeros = jnp.zeros((s, HEAD_DIM - ROPE_DIM), F32)
    z8 = jnp.zeros((s, half), F32)
    cs_h = jnp.concatenate([cos, cos, zeros + 1.0], axis=1)
    sm_h = jnp.concatenate([-sin, z8, zeros], axis=1)
    sp_h = jnp.concatenate([z8, sin, zeros], axis=1)
    ident = (jnp.ones((s, HEAD_DIM), F32), jnp.zeros((s, HEAD_DIM), F32), jnp.zeros((s, HEAD_DIM), F32))
    qk = tuple(jnp.concatenate([t, t], axis=1) for t in (cs_h, sm_h, sp_h))
    kv = tuple(jnp.concatenate([t, i], axis=1) for t, i in zip((cs_h, sm_h, sp_h), ident))
    return qk + kv


OUT_TM = 256
SUBLANES = 8


def _store_token_tiles(ref, val):
    rows = val.shape[0]
    for j in range(SUBLANES):
        ref[pl.ds(j, rows, stride=SUBLANES), :] = val[:, j * LANES:(j + 1) * LANES]


def _load_token_tiles(ref):
    rows = ref.shape[0] // SUBLANES
    return jnp.concatenate([ref[pl.ds(j, rows, stride=SUBLANES), :] for j in range(SUBLANES)], axis=1)


def _out_proj_kernel(oa_ref, ob_ref, oc_ref, x_ref, gg_ref, w_ref, g1_ref, n1_ref, n2_ref,
                     sc_ref, sh_ref, rw_ref, rb_ref, x1_ref, hf_ref, lg_ref):
    gg = gg_ref[...]
    y = jnp.concatenate([_rms(oa_ref[0], gg[:, :MOBA_W]),
                         _rms(ob_ref[0], gg[:, MOBA_W:MOBA_W + NSA_W]),
                         _rms(oc_ref[0], gg[:, MOBA_W + NSA_W:])], axis=1)
    y = _dot(y.astype(BF16), w_ref[...])
    x1 = x_ref[0] + g1_ref[0] * _rms(y, n1_ref[...])
    x1_ref[0] = x1
    hf = _rms(x1, n2_ref[...]) * (1.0 + sc_ref[0]) + sh_ref[0]
    _store_token_tiles(hf_ref.at[0], hf)
    lg_ref[...] = lax.dot_general(rw_ref[...], hf, (((1,), (1,)), ((), ())),
                                  precision=HI, preferred_element_type=F32) + rb_ref[...]


def out_projection(oa, ob, oc, x, grp_g, w_out, g1, n1, n2, sc2, sh2, rw_t, rb):
    b, s, d = x.shape
    assert d == SUBLANES * LANES
    ne = rw_t.shape[0]
    tm = OUT_TM
    steps = s // tm
    row = lambda w: pl.BlockSpec((1, tm, w), lambda bi, i: (bi, i, 0))
    vec = pl.BlockSpec((1, 1, d), lambda bi, i: (bi, 0, 0))
    cst = lambda r, w: pl.BlockSpec((r, w), lambda bi, i: (0, 0))
    return pl.pallas_call(
        _out_proj_kernel,
        grid=(b, steps),
        in_specs=[row(MOBA_W), row(NSA_W), row(SB_W), row(d), cst(1, d), cst(d, d), vec,
                  cst(1, d), cst(1, d), vec, vec, cst(ne, d), cst(ne, 1)],
        out_specs=[row(d), pl.BlockSpec((1, tm * SUBLANES, LANES), lambda bi, i: (bi, i, 0)),
                   pl.BlockSpec((ne, tm), lambda bi, i: (0, bi * steps + i))],
        out_shape=[jax.ShapeDtypeStruct((b, s, d), F32),
                   jax.ShapeDtypeStruct((b, s * SUBLANES, LANES), F32),
                   jax.ShapeDtypeStruct((ne, b * s), F32)],
        compiler_params=_cparams(("arbitrary", "arbitrary")),
        name="out_projection",
    )(oa, ob, oc, x, grp_g, w_out, g1, n1, n2, sc2, sh2, rw_t, rb)


ROUTE_TT = 512


def _router_kernel(lg_ref, idx_ref, gate_ref, rank_ref, cnt_ref):
    ne, tt = lg_ref.shape

    @pl.when(pl.program_id(0) == 0)
    def _():
        cnt_ref[...] = jnp.zeros(cnt_ref.shape, cnt_ref.dtype)

    v = lg_ref[...]
    erow = lax.broadcasted_iota(jnp.int32, (ne, tt), 0)
    vals, hots, firsts = [], [], []
    for _ in range(TOP_K):
        m = jnp.max(v, axis=0, keepdims=True)
        first = jnp.min(jnp.where(v == m, erow, ne), axis=0, keepdims=True)
        hot = erow == first
        v = jnp.where(hot, -jnp.inf, v)
        vals.append(m)
        hots.append(hot)
        firsts.append(first)
    exps = [jnp.exp(val - vals[0]) for val in vals]
    den = exps[0] + exps[1] + exps[2] + exps[3]
    gate_ref[...] = jnp.concatenate([e / den for e in exps], axis=0)
    idx_ref[...] = jnp.concatenate(firsts, axis=0)

    cnt = jnp.zeros((ne, tt), F32)
    for hot in hots:
        cnt = cnt + jnp.where(hot, 1.0, 0.0)
    before = (lax.broadcasted_iota(jnp.int32, (tt, tt), 0)
              < lax.broadcasted_iota(jnp.int32, (tt, tt), 1)).astype(BF16)
    excl = _dot(cnt.astype(BF16), before) + cnt_ref[:, 0:1]
    ranks = [jnp.sum(jnp.where(hot, excl, 0.0), axis=0, keepdims=True) for hot in hots]
    rank_ref[...] = jnp.concatenate(ranks, axis=0).astype(jnp.int32)
    cnt_ref[...] = cnt_ref[...] + jnp.sum(cnt, axis=1, keepdims=True)


def moe_router(logits_t):
    ne, n = logits_t.shape
    tt = ROUTE_TT
    slot = pl.BlockSpec((TOP_K, tt), lambda i: (0, i))
    return pl.pallas_call(
        _router_kernel,
        grid=(n // tt,),
        in_specs=[pl.BlockSpec((ne, tt), lambda i: (0, i))],
        out_specs=[slot, slot, slot, pl.BlockSpec((ne, LANES), lambda i: (0, 0))],
        out_shape=[jax.ShapeDtypeStruct((TOP_K, n), jnp.int32), jax.ShapeDtypeStruct((TOP_K, n), F32),
                   jax.ShapeDtypeStruct((TOP_K, n), jnp.int32), jax.ShapeDtypeStruct((ne, LANES), F32)],
        compiler_params=_cparams(("arbitrary",)),
        name="moe_router",
    )(logits_t)


MOE_TM = 256


def _dispatch_kernel(dest_ref, zrow_ref, hf_ref, xs_ref, zbuf, sem, zsem):
    tt = hf_ref.shape[0] // SUBLANES
    n = dest_ref.shape[0] // TOP_K
    base = pl.program_id(0) * tt

    def tile(ref, row):
        return ref.at[pl.ds(pl.multiple_of(row * SUBLANES, SUBLANES), SUBLANES), :]

    @pl.when(pl.program_id(0) == 0)
    def _():
        zbuf[...] = jnp.zeros(zbuf.shape, zbuf.dtype)

        def fill(j):
            first = pl.multiple_of(jnp.maximum(zrow_ref[j], 0) * SUBLANES, MOE_TM * SUBLANES)
            return pltpu.make_async_copy(zbuf, xs_ref.at[pl.ds(first, MOE_TM * SUBLANES), :], zsem)

        for j in range(zrow_ref.shape[0]):
            pl.when(zrow_ref[j] >= 0)(fill(j).start)
        for j in range(zrow_ref.shape[0]):
            pl.when(zrow_ref[j] >= 0)(fill(j).wait)

    def issue(t, carry):
        for k in range(TOP_K):
            pltpu.make_async_copy(tile(hf_ref, t), tile(xs_ref, dest_ref[k * n + base + t]), sem).start()
        return carry

    lax.fori_loop(0, tt, issue, 0, unroll=8)
    for _ in range(TOP_K):
        pltpu.make_async_copy(hf_ref, xs_ref.at[pl.ds(0, tt * SUBLANES), :], sem).wait()


DISPATCH_TT = 256


def moe_dispatch(dest_flat, zero_rows, hf, n_rows):
    n = hf.shape[0] // SUBLANES
    tt = DISPATCH_TT
    grid_spec = pltpu.PrefetchScalarGridSpec(
        num_scalar_prefetch=2,
        grid=(n // tt,),
        in_specs=[pl.BlockSpec((tt * SUBLANES, LANES), lambda i, dest, zrow: (i, 0))],
        out_specs=pl.BlockSpec(memory_space=pl.ANY),
        scratch_shapes=[pltpu.VMEM((MOE_TM * SUBLANES, LANES), F32), pltpu.SemaphoreType.DMA,
                        pltpu.SemaphoreType.DMA],
    )
    return pl.pallas_call(
        _dispatch_kernel,
        grid_spec=grid_spec,
        out_shape=jax.ShapeDtypeStruct((n_rows * SUBLANES, LANES), F32),
        compiler_params=_cparams(("arbitrary",)),
        name="moe_dispatch",
    )(dest_flat, zero_rows, hf)


def _expert_kernel(be_ref, nu_ref, xs_ref, wgu_ref, bgu_ref, wdn_ref, bdn_ref, y_ref,
                   wgu_bf, wdn_bf):
    i = pl.program_id(0)
    used = i < nu_ref[0]

    @pl.when(used & ((i == 0) | (be_ref[i] != be_ref[jnp.maximum(i - 1, 0)])))
    def _():
        wgu_bf[...] = wgu_ref[0].astype(BF16)
        wdn_bf[...] = wdn_ref[0].astype(BF16)

    @pl.when(used)
    def _():
        hgu = _dot(_load_token_tiles(xs_ref).astype(BF16), wgu_bf[...]) + bgu_ref[0]
        de = hgu.shape[1] // 2
        glu = jnp.minimum(hgu[:, :de], SWIGLU_LIMIT)
        lin = jnp.clip(hgu[:, de:], -SWIGLU_LIMIT, SWIGLU_LIMIT)
        act = glu * jax.nn.sigmoid(SWIGLU_ALPHA * glu) * (lin + 1.0)
        _store_token_tiles(y_ref, _dot(act.astype(BF16), wdn_bf[...]) + bdn_ref[0])

    @pl.when(jnp.logical_not(used))
    def _():
        y_ref[...] = jnp.zeros(y_ref.shape, y_ref.dtype)


EXPERT_VMEM_LIMIT = 56 * 1024 * 1024


def expert_ffn(layer, blk_exp, n_used, xs, w_gu, b_gu, w_dn, b_dn):
    r = xs.shape[0] // SUBLANES
    d = SUBLANES * LANES
    nl, ne, _, wide = w_gu.shape
    w_gu = w_gu.reshape(nl * ne, d, wide)
    w_dn = w_dn.reshape(nl * ne, wide // 2, d)
    b_gu = b_gu.reshape(nl * ne, wide)
    b_dn = b_dn.reshape(nl * ne, d)
    ne, base = nl * ne, layer * ne
    tm = MOE_TM
    grid_spec = pltpu.PrefetchScalarGridSpec(
        num_scalar_prefetch=2,
        grid=(r // tm,),
        in_specs=[pl.BlockSpec((tm * SUBLANES, LANES), lambda i, be, nu: (i, 0)),
                  pl.BlockSpec((1, d, wide), lambda i, be, nu: (base + be[i], 0, 0)),
                  pl.BlockSpec((1, 1, wide), lambda i, be, nu: (base + be[i], 0, 0)),
                  pl.BlockSpec((1, wide // 2, d), lambda i, be, nu: (base + be[i], 0, 0)),
                  pl.BlockSpec((1, 1, d), lambda i, be, nu: (base + be[i], 0, 0))],
        out_specs=pl.BlockSpec((tm * SUBLANES, LANES), lambda i, be, nu: (i, 0)),
        scratch_shapes=[pltpu.VMEM((d, wide), BF16), pltpu.VMEM((wide // 2, d), BF16)],
    )
    return pl.pallas_call(
        _expert_kernel,
        grid_spec=grid_spec,
        out_shape=jax.ShapeDtypeStruct((r * SUBLANES, LANES), F32),
        compiler_params=pltpu.CompilerParams(dimension_semantics=("arbitrary",),
                                             vmem_limit_bytes=EXPERT_VMEM_LIMIT),
        name="expert_ffn",
    )(blk_exp, n_used, xs, w_gu, b_gu.reshape(ne, 1, wide), w_dn, b_dn.reshape(ne, 1, d))


COMBINE_TT = 128


def _combine_kernel(dest_ref, gate_ref, x_ref, g2_ref, n3_ref, y_ref, o_ref, ybuf, sems):
    tt = x_ref.shape[0]
    n = dest_ref.shape[0] // TOP_K
    i = pl.program_id(0)
    steps = pl.num_programs(0)

    def gather(step, slot):
        base = step * tt

        def issue(t, carry):
            dst = pl.ds(pl.multiple_of(t * SUBLANES, SUBLANES), SUBLANES)
            for k in range(TOP_K):
                row = dest_ref[k * n + base + t]
                src = pl.ds(pl.multiple_of(row * SUBLANES, SUBLANES), SUBLANES)
                pltpu.make_async_copy(y_ref.at[src, :], ybuf.at[slot, k, dst, :], sems.at[slot]).start()
            return carry

        lax.fori_loop(0, tt, issue, 0, unroll=8)

    @pl.when(i == 0)
    def _():
        gather(0, 0)

    @pl.when(i + 1 < steps)
    def _():
        gather(i + 1, (i + 1) % 2)

    slot = i % 2
    for k in range(TOP_K):
        pltpu.make_async_copy(y_ref.at[pl.ds(0, tt * SUBLANES), :], ybuf.at[slot, k], sems.at[slot]).wait()
    gate = gate_ref[...]
    y = gate[:, 0:1] * _load_token_tiles(ybuf.at[slot, 0])
    for k in range(1, TOP_K):
        y = y + gate[:, k:k + 1] * _load_token_tiles(ybuf.at[slot, k])
    o_ref[...] = x_ref[...] + g2_ref[0] * _rms(y, n3_ref[...])


def moe_combine(dest_flat, gate, x, g2, n3, y_rows):
    b, s, d = x.shape
    n = b * s
    tt = COMBINE_TT
    per_batch = s // tt
    grid_spec = pltpu.PrefetchScalarGridSpec(
        num_scalar_prefetch=1,
        grid=(n // tt,),
        in_specs=[pl.BlockSpec((tt, LANES), lambda i, dest: (i, 0)),
                  pl.BlockSpec((tt, d), lambda i, dest: (i, 0)),
                  pl.BlockSpec((1, 1, d), lambda i, dest: (i // per_batch, 0, 0)),
                  pl.BlockSpec((1, d), lambda i, dest: (0, 0)),
                  pl.BlockSpec(memory_space=pl.ANY)],
        out_specs=pl.BlockSpec((tt, d), lambda i, dest: (i, 0)),
        scratch_shapes=[pltpu.VMEM((2, TOP_K, tt * SUBLANES, LANES), F32),
                        pltpu.SemaphoreType.DMA((2,))],
    )
    out = pl.pallas_call(
        _combine_kernel,
        grid_spec=grid_spec,
        out_shape=jax.ShapeDtypeStruct((n, d), F32),
        compiler_params=_cparams(("arbitrary",)),
        name="moe_combine",
    )(dest_flat, gate, x.reshape(n, d), g2, n3, y_rows)
    return out.reshape(b, s, d)


def _dispatch_plan(idx, rank, counts):
    n = idx.shape[1]
    tm = MOE_TM
    padded = (counts + tm - 1) // tm * tm
    pad_end = jnp.cumsum(padded)
    start = pad_end - padded
    experts = jnp.arange(N_EXPERTS, dtype=jnp.int32)
    dest = rank + jnp.sum(jnp.where(idx[..., None] == experts, start, 0), axis=-1)
    n_rows = -(-(n * TOP_K + N_EXPERTS * (tm - 1)) // tm) * tm
    n_blk = n_rows // tm
    blk_start = jnp.arange(n_blk, dtype=jnp.int32) * tm
    blk_exp = jnp.minimum(jnp.sum((pad_end[None, :] <= blk_start[:, None]).astype(jnp.int32), axis=1),
                          N_EXPERTS - 1)
    n_used = (pad_end[-1] // tm).astype(jnp.int32).reshape(1)
    tail = pad_end[-1] + experts * tm
    zero_rows = jnp.concatenate([jnp.where(padded > 0, pad_end - tm, -1),
                                 jnp.where(tail < n_rows, tail, -1)]).astype(jnp.int32)
    return dest.reshape(-1), blk_exp, n_used, zero_rows, n_rows


def _layer(layer, x, mod, norm_g, w_in, cmp_pe, cmp_w1, cmp_w2, grp_g, w_out, router_w, router_b,
           w_gu, b_gu, w_dn, b_dn, tabs, overlap):
    b, s, d = x.shape
    sh1, sc1, g1, sh2, sc2, g2 = (m.reshape(b, 1, d) for m in jnp.split(mod, 6, axis=-1))
    ng = norm_g.reshape(4, 1, d)

    qa, ka, va, qb, kvb, gb, qc, kc, vc = in_projection(x, sc1, sh1, ng[0], _pack_w_in(w_in), tabs)
    oa = moba_attention(qa, ka, va)
    nc = s // NSA_CMP_STRIDE
    kv16 = jnp.stack([kvb[:, :, :HEAD_DIM], kvb[:, :, HEAD_DIM:2 * HEAD_DIM]], axis=1)
    kv16 = kv16.reshape(b, 2, nc, NSA_CMP_STRIDE * HEAD_DIM)
    cmp_kv = nsa_compress(kv16, cmp_pe.reshape(2, 1, NSA_CMP_LEN * HEAD_DIM), cmp_w1, cmp_w2)
    ob = nsa_attention(qb, kvb, gb, cmp_kv, overlap)
    oc = sb_attention(qc, kc, vc)

    x1, hf, logits_t = out_projection(oa, ob, oc, x, grp_g.reshape(1, d), w_out.astype(BF16), g1,
                                      ng[1], ng[2], sc2, sh2, router_w.T, router_b.reshape(-1, 1))

    n = b * s
    idx, gate, rank, counts = moe_router(logits_t)
    dest, blk_exp, n_used, zero_rows, n_rows = _dispatch_plan(idx, rank, counts[:, 0].astype(jnp.int32))
    xs = moe_dispatch(dest, zero_rows, hf.reshape(n * SUBLANES, LANES), n_rows)
    y_rows = expert_ffn(layer, blk_exp, n_used, xs, w_gu, b_gu, w_dn, b_dn)
    gate = jnp.pad(gate.T, ((0, 0), (0, LANES - TOP_K)))
    return moe_combine(dest, gate, x1, g2, ng[3], y_rows)


def kernel(x, c, ada_w, ada_b, norm_g, w_in, nsa_cmp_pe, nsa_cmp_w1, nsa_cmp_w2, mix_out_g,
           w_out, router_w, router_b, exp_w_gu, exp_b_gu, exp_w_dn, exp_b_dn):
    s = x.shape[1]
    tabs = _rope_tables(s)
    overlap = _nsa_overlap(s)
    mod = ada_modulation(c, ada_w, ada_b)
    for l in range(ada_w.shape[0]):
        x = _layer(l, x, mod[l], norm_g[l], w_in[l], nsa_cmp_pe[l], nsa_cmp_w1[l], nsa_cmp_w2[l],
                   mix_out_g[l], w_out[l], router_w[l], router_b[l], exp_w_gu, exp_b_gu,
                   exp_w_dn, exp_b_dn, tabs, overlap)
    return x
```

```python
import functools

import numpy as np
import jax
import jax.numpy as jnp
from jax import lax
from jax.experimental import pallas as pl
from jax.experimental.pallas import tpu as pltpu

F32 = jnp.float32
BF16 = jnp.bfloat16
HI = lax.Precision.HIGHEST

D_MODEL = 1024
N_HEADS = 16
HEAD_DIM = 64
MOBA_HEADS = 4
NSA_HEADS = 6
SB_HEADS = 6
MOBA_W = MOBA_HEADS * HEAD_DIM
NSA_W = NSA_HEADS * HEAD_DIM
SB_W = SB_HEADS * HEAD_DIM
ROPE_DIM = 16
ROPE_THETA = 500000.0
MOBA_BLOCK = 256
MOBA_TOPK = 3
NSA_CMP_LEN = 32
NSA_CMP_STRIDE = 16
NSA_SLC_BLOCK = 64
NSA_SLC_TOPK = 16
NSA_WINDOW = 512
N_EXPERTS = 32
TOP_K = 4
SWIGLU_LIMIT = 7.0
SWIGLU_ALPHA = 1.702
RMS_EPS = 1e-6
NEG_INF = -1e30
SEL_FORCE = 1e4
SCALE = HEAD_DIM ** -0.5
LOG2E = 1.4426950408889634

LANES = 128
VMEM_LIMIT = 48 * 1024 * 1024


def _cparams(sem):
    return pltpu.CompilerParams(dimension_semantics=sem, vmem_limit_bytes=VMEM_LIMIT)


def _dot(a, b):
    return jnp.dot(a, b, preferred_element_type=F32)


def _dot_nt(a, b):
    return lax.dot_general(a, b, (((1,), (1,)), ((), ())), preferred_element_type=F32)


SB_TILE = 256


SB_GROUP_W = SB_W


def _sb_kernel(q_ref, k_ref, v_ref, o_ref, ks_ref, vs_ref):
    T = SB_TILE
    H = ks_ref.shape[0]
    qi = pl.program_id(2)

    @pl.when(qi == 0)
    def _():
        for h in range(H):
            ks_ref[h] = k_ref[0, :, h * HEAD_DIM:(h + 1) * HEAD_DIM].astype(BF16)
            vs_ref[h] = v_ref[0, :, h * HEAD_DIM:(h + 1) * HEAD_DIM].astype(BF16)

    row = lax.broadcasted_iota(jnp.int32, (T, T), 0)
    col = lax.broadcasted_iota(jnp.int32, (T, T), 1)
    incl = (row >= col).astype(BF16)
    sum_rhs = jnp.concatenate([incl, incl], axis=0)
    diag_mask = col < row
    qs = [(q_ref[0, :, h * HEAD_DIM:(h + 1) * HEAD_DIM] * (SCALE * LOG2E)).astype(BF16)
          for h in range(H)]
    sign = jnp.uint32(0x80000000)

    def scores(h, j):
        return _dot_nt(qs[h], ks_ref[h, pl.ds(pl.multiple_of(j * T, T), T), :])

    def neg_log2_keep(z, masked):
        neg_abs = lax.bitcast_convert_type(lax.bitcast_convert_type(z, jnp.uint32) | sign, F32)
        nlk = jnp.maximum(z, 0.0) + jnp.log2(1.0 + jnp.exp2(neg_abs))
        return jnp.where(diag_mask, nlk, 0.0) if masked else nlk

    def suffix_sums(nlk):
        hi = nlk.astype(BF16)
        lo = (nlk - hi.astype(F32)).astype(BF16)
        return _dot(jnp.concatenate([hi, lo], axis=1), sum_rhs)

    def weights(z, sums, run, masked):
        x = z - sums - run
        if masked:
            x = jnp.where(diag_mask, x, NEG_INF)
        return jnp.exp2(x.astype(BF16))

    def pv(h, p, j):
        return _dot(p, vs_ref[h, pl.ds(pl.multiple_of(j * T, T), T), :])

    def step(zs, runs, masked):
        sums = [suffix_sums(neg_log2_keep(zs[h], masked)) for h in range(H)]
        ps = [weights(zs[h], sums[h], runs[h], masked) for h in range(H)]
        return ps, [runs[h] + sums[h][:, 0:1] for h in range(H)]

    zeros = jnp.zeros((T, 1), F32)
    ps, runs = step([scores(h, qi) for h in range(H)], [zeros] * H, True)
    accs = [pv(h, ps[h], qi) for h in range(H)]

    def past(i, carry):
        runs, accs = carry
        j = qi - 1 - i
        ps, runs = step([scores(h, j) for h in range(H)], runs, False)
        return runs, [accs[h] + pv(h, ps[h], j) for h in range(H)]

    runs, accs = lax.fori_loop(0, qi, past, (runs, accs))
    o_ref[0] = jnp.concatenate(accs, axis=1)


def sb_attention(q, k, v):
    b, s, w = q.shape
    T = SB_TILE
    gw = SB_GROUP_W
    nh = gw // HEAD_DIM
    qspec = pl.BlockSpec((1, T, gw), lambda bi, p, i: (bi, i, p))
    kvspec = pl.BlockSpec((1, s, gw), lambda bi, p, i: (bi, 0, p))
    return pl.pallas_call(
        _sb_kernel,
        grid=(b, w // gw, s // T),
        in_specs=[qspec, kvspec, kvspec],
        out_specs=qspec,
        out_shape=jax.ShapeDtypeStruct((b, s, w), F32),
        scratch_shapes=[pltpu.VMEM((nh, s, HEAD_DIM), BF16), pltpu.VMEM((nh, s, HEAD_DIM), BF16)],
        compiler_params=_cparams(("arbitrary", "arbitrary", "arbitrary")),
        name="sb_attention",
    )(q, k, v)


def _rank_before_t(vals, n):
    idx = lax.broadcasted_iota(jnp.int32, vals.shape, 0)
    rank = jnp.zeros(vals.shape, F32)
    for j2 in range(n):
        other = vals[j2:j2 + 1, :]
        ahead = (other > vals) | ((other == vals) & (idx > j2))
        rank = rank + jnp.where(ahead, 1.0, 0.0)
    return rank


MOBA_KT = 2 * MOBA_BLOCK
MOBA_GROUP_W = MOBA_W


def _moba_kernel(q_ref, k_ref, v_ref, o_ref, ks_ref, vs_ref, km_ref):
    T, KT = MOBA_BLOCK, MOBA_KT
    H = ks_ref.shape[0]
    nb = km_ref.shape[1]
    s_len = ks_ref.shape[1]
    qi = pl.program_id(2)

    @pl.when(qi == 0)
    def _():
        key_blk = lax.broadcasted_iota(jnp.int32, (s_len, HEAD_DIM), 0) // T
        onehot = (lax.broadcasted_iota(jnp.int32, (s_len, HEAD_DIM), 1) == key_blk).astype(F32)
        ones_col = (lax.broadcasted_iota(jnp.int32, (s_len, HEAD_DIM), 1) == 0).astype(F32)
        for h in range(H):
            kh = k_ref[0, :, h * HEAD_DIM:(h + 1) * HEAD_DIM]
            ks_ref[h] = jnp.concatenate([kh, onehot], axis=1).astype(BF16)
            vs_ref[h] = jnp.concatenate([v_ref[0, :, h * HEAD_DIM:(h + 1) * HEAD_DIM], ones_col],
                                        axis=1).astype(BF16)
            km_ref[h] = jnp.mean(kh.reshape(nb, T, HEAD_DIM), axis=1)

    row = lax.broadcasted_iota(jnp.int32, (T, T), 0)
    col = lax.broadcasted_iota(jnp.int32, (T, T), 1)
    causal = col <= row
    blk_t = lax.broadcasted_iota(jnp.int32, (nb, T), 0)
    start = pl.multiple_of(qi * T, T)

    qfs = [q_ref[0, :, h * HEAD_DIM:(h + 1) * HEAD_DIM] for h in range(H)]
    qss = [qf * SCALE for qf in qfs]
    s_own = [_dot_nt(qss[h].astype(BF16), ks_ref[h, pl.ds(start, T), 0:HEAD_DIM]) for h in range(H)]
    q_aug, state = [], []
    for h in range(H):
        gate = lax.dot_general(km_ref[h], qfs[h], (((1,), (1,)), ((), ())),
                               precision=HI, preferred_element_type=F32)
        gate = jnp.where(blk_t < qi, gate, NEG_INF)
        sel = (_rank_before_t(gate, nb) < float(MOBA_TOPK)) & (gate > 0.5 * NEG_INF)
        selb = jnp.where(sel, 0.0, NEG_INF)
        selb = jnp.concatenate([selb, jnp.full((LANES - nb, T), NEG_INF, F32)], axis=0).T
        q_aug.append(jnp.concatenate([qss[h], selb[:, :HEAD_DIM]], axis=1).astype(BF16))

        s = jnp.where(causal, s_own[h], NEG_INF)
        m = jnp.max(s, axis=1, keepdims=True)
        p = jnp.exp((s - m).astype(BF16))
        state += [m, _dot(p, vs_ref[h, pl.ds(start, T), :])]

    def past(i, carry):
        st = pl.multiple_of(i * KT, KT)
        ss = [_dot_nt(q_aug[h], ks_ref[h, pl.ds(st, KT), :]) for h in range(H)]
        out = []
        for h in range(H):
            m, acc = carry[2 * h:2 * h + 2]
            m_new = jnp.maximum(m, jnp.max(ss[h], axis=1, keepdims=True))
            alpha = jnp.exp(m - m_new)
            p = jnp.exp((ss[h] - m_new).astype(BF16))
            out += [m_new, alpha * acc + _dot(p, vs_ref[h, pl.ds(st, KT), :])]
        return tuple(out)

    state = lax.fori_loop(0, (qi + 1) // 2, past, tuple(state))
    outs = [state[2 * h + 1][:, :HEAD_DIM] / jnp.maximum(state[2 * h + 1][:, HEAD_DIM:HEAD_DIM + 1], 1e-30)
            for h in range(H)]
    o_ref[0] = jnp.concatenate(outs, axis=1)


def moba_attention(q, k, v):
    b, s, w = q.shape
    T = MOBA_BLOCK
    nb = s // T
    assert nb <= HEAD_DIM and s % MOBA_KT == 0
    gw = MOBA_GROUP_W
    nh = gw // HEAD_DIM
    qspec = pl.BlockSpec((1, T, gw), lambda bi, p, i: (bi, i, p))
    kvspec = pl.BlockSpec((1, s, gw), lambda bi, p, i: (bi, 0, p))
    return pl.pallas_call(
        _moba_kernel,
        grid=(b, w // gw, nb),
        in_specs=[qspec, kvspec, kvspec],
        out_specs=qspec,
        out_shape=jax.ShapeDtypeStruct((b, s, w), F32),
        scratch_shapes=[pltpu.VMEM((nh, s, LANES), BF16), pltpu.VMEM((nh, s, LANES), BF16),
                        pltpu.VMEM((nh, nb, HEAD_DIM), F32)],
        compiler_params=_cparams(("arbitrary", "arbitrary", "arbitrary")),
        name="moba_attention",
    )(q, k, v)


def _nsa_compress_kernel(x_ref, pe_ref, w1_ref, w2_ref, o_ref):
    nc = x_ref.shape[2]
    half = NSA_CMP_STRIDE * HEAD_DIM
    x = x_ref[0, 0]
    w1 = w1_ref[0]
    first = jnp.dot(x, w1[:half], precision=HI, preferred_element_type=F32)
    second = jnp.dot(x, w1[half:], precision=HI, preferred_element_type=F32)
    pe = jnp.broadcast_to(pe_ref[0], (8, 2 * half))
    peb = jnp.dot(pe, w1, precision=HI, preferred_element_type=F32)[0:1]
    pre = first + pltpu.roll(second, nc - 1, 0) + peb
    hid = pre * jax.nn.sigmoid(pre)
    o_ref[0, 0] = jnp.dot(hid, w2_ref[0], precision=HI, preferred_element_type=F32)


def nsa_compress(kv16, pe, w1, w2):
    b, _, nc, wide = kv16.shape
    return pl.pallas_call(
        _nsa_compress_kernel,
        grid=(b, 2),
        in_specs=[pl.BlockSpec((1, 1, nc, wide), lambda bi, i: (bi, i, 0, 0)),
                  pl.BlockSpec((1, 1, 2 * wide), lambda bi, i: (i, 0, 0)),
                  pl.BlockSpec((1, 2 * wide, HEAD_DIM), lambda bi, i: (i, 0, 0)),
                  pl.BlockSpec((1, HEAD_DIM, HEAD_DIM), lambda bi, i: (i, 0, 0))],
        out_specs=pl.BlockSpec((1, 1, nc, HEAD_DIM), lambda bi, i: (bi, i, 0, 0)),
        out_shape=jax.ShapeDtypeStruct((b, 2, nc, HEAD_DIM), F32),
        compiler_params=_cparams(("arbitrary", "arbitrary")),
        name="nsa_compress",
    )(kv16, pe, w1, w2)


NSA_TQ = 256
NSA_KT = 512
NSA_SPAN = NSA_WINDOW + NSA_TQ
NSA_CHAIN_HEADS = 2


def _softmax_rows(s, mask):
    s = jnp.where(mask, s, NEG_INF)
    m = jnp.max(s, axis=-1, keepdims=True)
    e = jnp.where(mask, jnp.exp(s - m), 0.0)
    return e / jnp.maximum(jnp.sum(e, axis=-1, keepdims=True), 1e-30)


def _split_bf16(x):
    hi = x.astype(BF16)
    return hi, (x - hi.astype(F32)).astype(BF16)


def _nsa_kernel(q_ref, cmp_ref, slc_ref, win_ref, g_ref, ov_ref, o_ref,
                ksl_ref, vsl_ref, kw_ref, vw_ref, kc_ref):
    TQ, KT, H = NSA_TQ, NSA_KT, NSA_HEADS
    ns, nc = ov_ref.shape
    s_len = ksl_ref.shape[0]
    qi = pl.program_id(1)
    t0 = qi * TQ

    @pl.when(qi == 0)
    def _():
        lane = lax.broadcasted_iota(jnp.int32, (s_len, HEAD_DIM), 1)
        key_blk = lax.broadcasted_iota(jnp.int32, (s_len, HEAD_DIM), 0) // NSA_SLC_BLOCK
        onehot = (lane == key_blk).astype(F32)
        ones_col = (lane == 0).astype(F32)
        ksl_ref[...] = jnp.concatenate([slc_ref[0, :, :HEAD_DIM], onehot], axis=1).astype(BF16)
        vsl_ref[...] = jnp.concatenate([slc_ref[0, :, HEAD_DIM:], ones_col], axis=1).astype(BF16)
        kw_ref[...] = win_ref[0, :, :HEAD_DIM].astype(BF16)
        vw_ref[...] = jnp.concatenate([win_ref[0, :, HEAD_DIM:], ones_col], axis=1).astype(BF16)
        k_hi, k_lo = _split_bf16(cmp_ref[0, 0])
        kc_ref[...] = jnp.concatenate([k_hi, k_hi, k_lo], axis=1)

    qf = jnp.concatenate([q_ref[0, :, h * HEAD_DIM:(h + 1) * HEAD_DIM] for h in range(H)],
                         axis=0) * SCALE
    q, q_lo = _split_bf16(qf)

    tq_c = t0 + lax.broadcasted_iota(jnp.int32, (TQ, nc), 0)
    n_c = lax.broadcasted_iota(jnp.int32, (TQ, nc), 1)
    mask_c = (n_c * NSA_CMP_STRIDE + (NSA_CMP_LEN - 1) <= tq_c) & (n_c < nc - 1)
    s_c = _dot_nt(jnp.concatenate([q, q_lo, q], axis=1), kc_ref[...])
    p_c = _softmax_rows(s_c.reshape(H, TQ, nc), mask_c[None])
    o_c = _dot(p_c.reshape(H * TQ, nc).astype(BF16), cmp_ref[0, 1].astype(BF16))

    imp = lax.dot_general(ov_ref[...], jnp.sum(p_c, axis=0), (((1,), (1,)), ((), ())),
                          precision=HI, preferred_element_type=F32)
    tq_s = t0 + lax.broadcasted_iota(jnp.int32, (ns, TQ), 1)
    blk = lax.broadcasted_iota(jnp.int32, (ns, TQ), 0)
    own = tq_s // NSA_SLC_BLOCK
    forced = (blk == 0) | (blk == own) | (blk == own - 1)
    imp = jnp.where(forced, SEL_FORCE, imp)
    imp = jnp.where(blk <= own, imp, NEG_INF)
    sel = (_rank_before_t(imp, ns) < float(min(NSA_SLC_TOPK, ns))) & (imp > 0.5 * NEG_INF)
    selb = jnp.where(sel, 0.0, NEG_INF)
    if ns < LANES:
        selb = jnp.concatenate([selb, jnp.full((LANES - ns, TQ), NEG_INF, F32)], axis=0)
    selb = selb.T[:, :HEAD_DIM]
    q_aug = jnp.concatenate([qf, jnp.concatenate([selb] * H, axis=0)], axis=1).astype(BF16)

    jd = t0 // KT
    start = pl.multiple_of(jd * KT, KT)
    kpos = start + lax.broadcasted_iota(jnp.int32, (TQ, KT), 1)
    tq_k = t0 + lax.broadcasted_iota(jnp.int32, (TQ, KT), 0)
    HC = NSA_CHAIN_HEADS
    G, R = H // HC, HC * TQ
    qa = [q_aug[c * R:(c + 1) * R] for c in range(G)]
    causal_k = (kpos <= tq_k)[None]

    def scores(c, st):
        return _dot_nt(qa[c], ksl_ref[pl.ds(st, KT), :]).reshape(HC, TQ, KT)

    def normalized(acc):
        return acc[:, :HEAD_DIM] / jnp.maximum(acc[:, HEAD_DIM:HEAD_DIM + 1], 1e-30)

    s_own = [scores(c, start) for c in range(G)]
    state = []
    for c in range(G):
        s = jnp.where(causal_k, s_own[c], NEG_INF)
        m = jnp.max(s, axis=-1, keepdims=True)
        p = jnp.exp((s - m).astype(BF16))
        state += [m, _dot(p.reshape(R, KT), vsl_ref[pl.ds(start, KT), :])]

    def past(j, carry):
        st = pl.multiple_of(j * KT, KT)
        ss = [scores(c, st) for c in range(G)]
        out = []
        for c in range(G):
            m, acc = carry[2 * c:2 * c + 2]
            m_new = jnp.maximum(m, jnp.max(ss[c], axis=-1, keepdims=True))
            alpha = jnp.exp(m - m_new)
            p = jnp.exp((ss[c] - m_new).astype(BF16))
            pv = _dot(p.reshape(R, KT), vsl_ref[pl.ds(st, KT), :])
            out += [m_new, alpha.reshape(R, 1) * acc + pv]
        return tuple(out)

    state = lax.fori_loop(0, jd, past, tuple(state))
    o_s = jnp.concatenate([normalized(state[2 * c + 1]) for c in range(G)], axis=0)

    w0 = pl.multiple_of(jnp.maximum(t0 - NSA_WINDOW, 0), TQ)
    wpos = w0 + lax.broadcasted_iota(jnp.int32, (TQ, NSA_SPAN), 1)
    tq_w = t0 + lax.broadcasted_iota(jnp.int32, (TQ, NSA_SPAN), 0)
    mask_w = ((wpos <= tq_w) & (wpos > tq_w - NSA_WINDOW))[None]
    s_w = jnp.where(mask_w, _dot_nt(q, kw_ref[pl.ds(w0, NSA_SPAN), :]).reshape(H, TQ, NSA_SPAN),
                    NEG_INF)
    p_w = jnp.exp((s_w - jnp.max(s_w, axis=-1, keepdims=True)).astype(BF16))
    o_w = normalized(_dot(p_w.reshape(H * TQ, NSA_SPAN), vw_ref[pl.ds(w0, NSA_SPAN), :]))

    g = g_ref[0]
    outs = []
    for h in range(H):
        rows = slice(h * TQ, (h + 1) * TQ)
        outs.append(g[:, 3 * h:3 * h + 1] * o_c[rows] + g[:, 3 * h + 1:3 * h + 2] * o_s[rows]
                    + g[:, 3 * h + 2:3 * h + 3] * o_w[rows])
    o_ref[0] = jnp.concatenate(outs, axis=1)


def nsa_attention(q, kv, gates, cmp_kv, overlap):
    b, s, w = q.shape
    ns, nc = overlap.shape
    assert ns <= HEAD_DIM
    TQ = NSA_TQ
    return pl.pallas_call(
        _nsa_kernel,
        grid=(b, s // TQ),
        in_specs=[pl.BlockSpec((1, TQ, w), lambda bi, i: (bi, i, 0)),
                  pl.BlockSpec((1, 2, nc, HEAD_DIM), lambda bi, i: (bi, 0, 0, 0)),
                  pl.BlockSpec((1, s, LANES), lambda bi, i: (bi, 0, 1)),
                  pl.BlockSpec((1, s, LANES), lambda bi, i: (bi, 0, 2)),
                  pl.BlockSpec((1, TQ, LANES), lambda bi, i: (bi, i, 0)),
                  pl.BlockSpec((ns, nc), lambda bi, i: (0, 0))],
        out_specs=pl.BlockSpec((1, TQ, w), lambda bi, i: (bi, i, 0)),
        out_shape=jax.ShapeDtypeStruct((b, s, w), F32),
        scratch_shapes=[pltpu.VMEM((s, LANES), BF16), pltpu.VMEM((s, LANES), BF16),
                        pltpu.VMEM((s, HEAD_DIM), BF16), pltpu.VMEM((s, LANES), BF16),
                        pltpu.VMEM((nc, 3 * HEAD_DIM), BF16)],
        compiler_params=_cparams(("arbitrary", "arbitrary")),
        name="nsa_attention",
    )(q, cmp_kv, kv, kv, gates, overlap)


def _nsa_overlap(s):
    nc = s // NSA_CMP_STRIDE
    ns = s // NSA_SLC_BLOCK
    cstart = np.arange(nc) * NSA_CMP_STRIDE
    cend = cstart + NSA_CMP_LEN - 1
    sstart = np.arange(ns) * NSA_SLC_BLOCK
    ov = (cstart[:, None] <= sstart[None, :] + NSA_SLC_BLOCK - 1) & (cend[:, None] >= sstart[None, :])
    ov[nc - 1] = False
    return jnp.asarray(ov.T.astype(np.float32))


def _mod_kernel(c_ref, w_ref, b_ref, o_ref):
    c = c_ref[...]
    act = c * jax.nn.sigmoid(c)
    o_ref[0] = jnp.dot(act, w_ref[0], precision=HI, preferred_element_type=F32) + b_ref[0]


def ada_modulation(c, ada_w, ada_b):
    nl, d, wide = ada_w.shape
    b = c.shape[0]
    tn = D_MODEL
    return pl.pallas_call(
        _mod_kernel,
        grid=(nl, wide // tn),
        in_specs=[pl.BlockSpec((b, d), lambda l, j: (0, 0)),
                  pl.BlockSpec((1, d, tn), lambda l, j: (l, 0, j)),
                  pl.BlockSpec((1, 1, tn), lambda l, j: (l, 0, j))],
        out_specs=pl.BlockSpec((1, b, tn), lambda l, j: (l, 0, j)),
        out_shape=jax.ShapeDtypeStruct((nl, b, wide), F32),
        compiler_params=_cparams(("arbitrary", "arbitrary")),
        name="ada_modulation",
    )(c, ada_w, ada_b.reshape(nl, 1, wide))


def _rms(x, g):
    return x * lax.rsqrt(jnp.mean(x * x, axis=-1, keepdims=True) + RMS_EPS) * g


_GATE_PAD = LANES
_COLS = {}
_off = 0
for _name, _w in (("qa", MOBA_W), ("ka", MOBA_W), ("va", MOBA_W), ("qb", NSA_W), ("kvb", NSA_W),
                  ("gb", _GATE_PAD), ("qc", SB_W), ("kc", SB_W), ("vc", SB_W)):
    _COLS[_name] = (_off, _w)
    _off += _w
IN_W_PACKED = _off
PROJ_TM = 256


def _rope_block(p, cs, sm, sp):
    return p * cs + pltpu.roll(p, LANES - ROPE_DIM // 2, 1) * sm + pltpu.roll(p, ROPE_DIM // 2, 1) * sp


def _in_proj_kernel(x_ref, sc_ref, sh_ref, g_ref, w_ref, cqk_ref, mqk_ref, pqk_ref,
                    ckv_ref, mkv_ref, pkv_ref,
                    qa_ref, ka_ref, va_ref, qb_ref, kvb_ref, gb_ref, qc_ref, kc_ref, vc_ref):
    hm = _rms(x_ref[0], g_ref[...]) * (1.0 + sc_ref[0]) + sh_ref[0]
    p = _dot(hm.astype(BF16), w_ref[...])
    qk = (cqk_ref[...], mqk_ref[...], pqk_ref[...])
    kv = (ckv_ref[...], mkv_ref[...], pkv_ref[...])

    def emit(ref, name, tabs):
        off, w = _COLS[name]
        for j in range(w // LANES):
            blk = p[:, off + j * LANES: off + (j + 1) * LANES]
            if tabs is not None:
                blk = _rope_block(blk, *tabs)
            ref[0, :, j * LANES:(j + 1) * LANES] = blk

    emit(qa_ref, "qa", qk)
    emit(ka_ref, "ka", qk)
    emit(va_ref, "va", None)
    emit(qb_ref, "qb", qk)
    emit(kvb_ref, "kvb", kv)
    emit(qc_ref, "qc", None)
    emit(kc_ref, "kc", None)
    emit(vc_ref, "vc", None)
    off, w = _COLS["gb"]
    gb_ref[0] = jax.nn.sigmoid(p[:, off:off + w])


def in_projection(x, sc, sh, g, w_packed, tabs):
    b, s, d = x.shape
    tm = PROJ_TM
    row = lambda w: pl.BlockSpec((1, tm, w), lambda bi, i: (bi, i, 0))
    vec = pl.BlockSpec((1, 1, d), lambda bi, i: (bi, 0, 0))
    tab = pl.BlockSpec((tm, LANES), lambda bi, i: (i, 0))
    names = ("qa", "ka", "va", "qb", "kvb", "gb", "qc", "kc", "vc")
    return pl.pallas_call(
        _in_proj_kernel,
        grid=(b, s // tm),
        in_specs=[row(d), vec, vec, pl.BlockSpec((1, d), lambda bi, i: (0, 0)),
                  pl.BlockSpec((d, IN_W_PACKED), lambda bi, i: (0, 0))] + [tab] * 6,
        out_specs=[row(_COLS[n][1]) for n in names],
        out_shape=[jax.ShapeDtypeStruct((b, s, _COLS[n][1]), F32) for n in names],
        compiler_params=_cparams(("arbitrary", "arbitrary")),
        name="in_projection",
    )(x, sc, sh, g, w_packed, *tabs)


def _pack_w_in(w_in):
    widths = (MOBA_W, MOBA_W, MOBA_W, NSA_W, NSA_W, 3 * NSA_HEADS, SB_W, SB_W, SB_W)
    offs = np.cumsum((0,) + widths)
    parts = []
    for i, w in enumerate(widths):
        blk = w_in[:, offs[i]:offs[i + 1]]
        if w == 3 * NSA_HEADS:
            blk = jnp.pad(blk, ((0, 0), (0, _GATE_PAD - w)))
        parts.append(blk)
    return jnp.concatenate(parts, axis=1).astype(BF16)


def _rope_tables(s):
    half = ROPE_DIM // 2
    inv_freq = ROPE_THETA ** (-jnp.arange(0, ROPE_DIM, 2, dtype=F32) / ROPE_DIM)
    ang = jnp.arange(s, dtype=F32)[:, None] * inv_freq[None, :]
    cos, sin = jnp.cos(ang), jnp.sin(ang)
    zeros = jnp.zeros((s, HEAD_DIM - ROPE_DIM), F32)
    z8 = jnp.zeros((s, half), F32)
    cs_h = jnp.concatenate([cos, cos, zeros + 1.0], axis=1)
    sm_h = jnp.concatenate([-sin, z8, zeros], axis=1)
    sp_h = jnp.concatenate([z8, sin, zeros], axis=1)
    ident = (jnp.ones((s, HEAD_DIM), F32), jnp.zeros((s, HEAD_DIM), F32), jnp.zeros((s, HEAD_DIM), F32))
    qk = tuple(jnp.concatenate([t, t], axis=1) for t in (cs_h, sm_h, sp_h))
    kv = tuple(jnp.concatenate([t, i], axis=1) for t, i in zip((cs_h, sm_h, sp_h), ident))
    return qk + kv


OUT_TM = 256
SUBLANES = 8


def _store_token_tiles(ref, val):
    rows = val.shape[0]
    for j in range(SUBLANES):
        ref[pl.ds(j, rows, stride=SUBLANES), :] = val[:, j * LANES:(j + 1) * LANES]


def _load_token_tiles(ref):
    rows = ref.shape[0] // SUBLANES
    return jnp.concatenate([ref[pl.ds(j, rows, stride=SUBLANES), :] for j in range(SUBLANES)], axis=1)


def _out_proj_kernel(oa_ref, ob_ref, oc_ref, x_ref, gg_ref, w_ref, g1_ref, n1_ref, n2_ref,
                     sc_ref, sh_ref, rw_ref, rb_ref, x1_ref, hf_ref, lg_ref):
    gg = gg_ref[...]
    y = jnp.concatenate([_rms(oa_ref[0], gg[:, :MOBA_W]),
                         _rms(ob_ref[0], gg[:, MOBA_W:MOBA_W + NSA_W]),
                         _rms(oc_ref[0], gg[:, MOBA_W + NSA_W:])], axis=1)
    y = _dot(y.astype(BF16), w_ref[...])
    x1 = x_ref[0] + g1_ref[0] * _rms(y, n1_ref[...])
    x1_ref[0] = x1
    hf = _rms(x1, n2_ref[...]) * (1.0 + sc_ref[0]) + sh_ref[0]
    _store_token_tiles(hf_ref.at[0], hf)
    lg_ref[...] = lax.dot_general(rw_ref[...], hf, (((1,), (1,)), ((), ())),
                                  precision=HI, preferred_element_type=F32) + rb_ref[...]


def out_projection(oa, ob, oc, x, grp_g, w_out, g1, n1, n2, sc2, sh2, rw_t, rb):
    b, s, d = x.shape
    assert d == SUBLANES * LANES
    ne = rw_t.shape[0]
    tm = OUT_TM
    steps = s // tm
    row = lambda w: pl.BlockSpec((1, tm, w), lambda bi, i: (bi, i, 0))
    vec = pl.BlockSpec((1, 1, d), lambda bi, i: (bi, 0, 0))
    cst = lambda r, w: pl.BlockSpec((r, w), lambda bi, i: (0, 0))
    return pl.pallas_call(
        _out_proj_kernel,
        grid=(b, steps),
        in_specs=[row(MOBA_W), row(NSA_W), row(SB_W), row(d), cst(1, d), cst(d, d), vec,
                  cst(1, d), cst(1, d), vec, vec, cst(ne, d), cst(ne, 1)],
        out_specs=[row(d), pl.BlockSpec((1, tm * SUBLANES, LANES), lambda bi, i: (bi, i, 0)),
                   pl.BlockSpec((ne, tm), lambda bi, i: (0, bi * steps + i))],
        out_shape=[jax.ShapeDtypeStruct((b, s, d), F32),
                   jax.ShapeDtypeStruct((b, s * SUBLANES, LANES), F32),
                   jax.ShapeDtypeStruct((ne, b * s), F32)],
        compiler_params=_cparams(("arbitrary", "arbitrary")),
        name="out_projection",
    )(oa, ob, oc, x, grp_g, w_out, g1, n1, n2, sc2, sh2, rw_t, rb)


ROUTE_TT = 512


def _router_kernel(lg_ref, idx_ref, gate_ref, rank_ref, cnt_ref):
    ne, tt = lg_ref.shape

    @pl.when(pl.program_id(0) == 0)
    def _():
        cnt_ref[...] = jnp.zeros(cnt_ref.shape, cnt_ref.dtype)

    v = lg_ref[...]
    erow = lax.broadcasted_iota(jnp.int32, (ne, tt), 0)
    vals, hots, firsts = [], [], []
    for _ in range(TOP_K):
        m = jnp.max(v, axis=0, keepdims=True)
        first = jnp.min(jnp.where(v == m, erow, ne), axis=0, keepdims=True)
        hot = erow == first
        v = jnp.where(hot, -jnp.inf, v)
        vals.append(m)
        hots.append(hot)
        firsts.append(first)
    exps = [jnp.exp(val - vals[0]) for val in vals]
    den = exps[0] + exps[1] + exps[2] + exps[3]
    gate_ref[...] = jnp.concatenate([e / den for e in exps], axis=0)
    idx_ref[...] = jnp.concatenate(firsts, axis=0)

    cnt = jnp.zeros((ne, tt), F32)
    for hot in hots:
        cnt = cnt + jnp.where(hot, 1.0, 0.0)
    before = (lax.broadcasted_iota(jnp.int32, (tt, tt), 0)
              < lax.broadcasted_iota(jnp.int32, (tt, tt), 1)).astype(BF16)
    excl = _dot(cnt.astype(BF16), before) + cnt_ref[:, 0:1]
    ranks = [jnp.sum(jnp.where(hot, excl, 0.0), axis=0, keepdims=True) for hot in hots]
    rank_ref[...] = jnp.concatenate(ranks, axis=0).astype(jnp.int32)
    cnt_ref[...] = cnt_ref[...] + jnp.sum(cnt, axis=1, keepdims=True)


def moe_router(logits_t):
    ne, n = logits_t.shape
    tt = ROUTE_TT
    slot = pl.BlockSpec((TOP_K, tt), lambda i: (0, i))
    return pl.pallas_call(
        _router_kernel,
        grid=(n // tt,),
        in_specs=[pl.BlockSpec((ne, tt), lambda i: (0, i))],
        out_specs=[slot, slot, slot, pl.BlockSpec((ne, LANES), lambda i: (0, 0))],
        out_shape=[jax.ShapeDtypeStruct((TOP_K, n), jnp.int32), jax.ShapeDtypeStruct((TOP_K, n), F32),
                   jax.ShapeDtypeStruct((TOP_K, n), jnp.int32), jax.ShapeDtypeStruct((ne, LANES), F32)],
        compiler_params=_cparams(("arbitrary",)),
        name="moe_router",
    )(logits_t)


MOE_TM = 512


def _dispatch_kernel(dest_ref, zrow_ref, hf_ref, xs_ref, zbuf, sem, zsem):
    tt = hf_ref.shape[0] // SUBLANES
    n = dest_ref.shape[0] // TOP_K
    base = pl.program_id(0) * tt

    def tile(ref, row):
        return ref.at[pl.ds(pl.multiple_of(row * SUBLANES, SUBLANES), SUBLANES), :]

    @pl.when(pl.program_id(0) == 0)
    def _():
        zbuf[...] = jnp.zeros(zbuf.shape, zbuf.dtype)

        def fill(j):
            first = pl.multiple_of(jnp.maximum(zrow_ref[j], 0) * SUBLANES, MOE_TM * SUBLANES)
            return pltpu.make_async_copy(zbuf, xs_ref.at[pl.ds(first, MOE_TM * SUBLANES), :], zsem)

        for j in range(zrow_ref.shape[0]):
            pl.when(zrow_ref[j] >= 0)(fill(j).start)
        for j in range(zrow_ref.shape[0]):
            pl.when(zrow_ref[j] >= 0)(fill(j).wait)

    def issue(t, carry):
        for k in range(TOP_K):
            pltpu.make_async_copy(tile(hf_ref, t), tile(xs_ref, dest_ref[k * n + base + t]), sem).start()
        return carry

    lax.fori_loop(0, tt, issue, 0, unroll=8)
    for _ in range(TOP_K):
        pltpu.make_async_copy(hf_ref, xs_ref.at[pl.ds(0, tt * SUBLANES), :], sem).wait()


DISPATCH_TT = 256


def moe_dispatch(dest_flat, zero_rows, hf, n_rows):
    n = hf.shape[0] // SUBLANES
    tt = DISPATCH_TT
    grid_spec = pltpu.PrefetchScalarGridSpec(
        num_scalar_prefetch=2,
        grid=(n // tt,),
        in_specs=[pl.BlockSpec((tt * SUBLANES, LANES), lambda i, dest, zrow: (i, 0))],
        out_specs=pl.BlockSpec(memory_space=pl.ANY),
        scratch_shapes=[pltpu.VMEM((MOE_TM * SUBLANES, LANES), F32), pltpu.SemaphoreType.DMA,
                        pltpu.SemaphoreType.DMA],
    )
    return pl.pallas_call(
        _dispatch_kernel,
        grid_spec=grid_spec,
        out_shape=jax.ShapeDtypeStruct((n_rows * SUBLANES, LANES), F32),
        compiler_params=_cparams(("arbitrary",)),
        name="moe_dispatch",
    )(dest_flat, zero_rows, hf)


def _expert_kernel(be_ref, nu_ref, xs_ref, wgu_ref, bgu_ref, wdn_ref, bdn_ref, y_ref,
                   wgu_bf, wdn_bf):
    i = pl.program_id(0)
    used = i < nu_ref[0]

    @pl.when(used & ((i == 0) | (be_ref[i] != be_ref[jnp.maximum(i - 1, 0)])))
    def _():
        wgu_bf[...] = wgu_ref[0].astype(BF16)
        wdn_bf[...] = wdn_ref[0].astype(BF16)

    @pl.when(used)
    def _():
        hgu = _dot(_load_token_tiles(xs_ref).astype(BF16), wgu_bf[...]) + bgu_ref[0]
        de = hgu.shape[1] // 2
        glu = jnp.minimum(hgu[:, :de], SWIGLU_LIMIT)
        lin = jnp.clip(hgu[:, de:], -SWIGLU_LIMIT, SWIGLU_LIMIT)
        act = glu * jax.nn.sigmoid(SWIGLU_ALPHA * glu) * (lin + 1.0)
        _store_token_tiles(y_ref, _dot(act.astype(BF16), wdn_bf[...]) + bdn_ref[0])

    @pl.when(jnp.logical_not(used))
    def _():
        y_ref[...] = jnp.zeros(y_ref.shape, y_ref.dtype)


EXPERT_VMEM_LIMIT = 56 * 1024 * 1024


def expert_ffn(layer, blk_exp, n_used, xs, w_gu, b_gu, w_dn, b_dn):
    r = xs.shape[0] // SUBLANES
    d = SUBLANES * LANES
    nl, ne, _, wide = w_gu.shape
    w_gu = w_gu.reshape(nl * ne, d, wide)
    w_dn = w_dn.reshape(nl * ne, wide // 2, d)
    b_gu = b_gu.reshape(nl * ne, wide)
    b_dn = b_dn.reshape(nl * ne, d)
    ne, base = nl * ne, layer * ne
    tm = MOE_TM
    grid_spec = pltpu.PrefetchScalarGridSpec(
        num_scalar_prefetch=2,
        grid=(r // tm,),
        in_specs=[pl.BlockSpec((tm * SUBLANES, LANES), lambda i, be, nu: (i, 0)),
                  pl.BlockSpec((1, d, wide), lambda i, be, nu: (base + be[i], 0, 0)),
                  pl.BlockSpec((1, 1, wide), lambda i, be, nu: (base + be[i], 0, 0)),
                  pl.BlockSpec((1, wide // 2, d), lambda i, be, nu: (base + be[i], 0, 0)),
                  pl.BlockSpec((1, 1, d), lambda i, be, nu: (base + be[i], 0, 0))],
        out_specs=pl.BlockSpec((tm * SUBLANES, LANES), lambda i, be, nu: (i, 0)),
        scratch_shapes=[pltpu.VMEM((d, wide), BF16), pltpu.VMEM((wide // 2, d), BF16)],
    )
    return pl.pallas_call(
        _expert_kernel,
        grid_spec=grid_spec,
        out_shape=jax.ShapeDtypeStruct((r * SUBLANES, LANES), F32),
        compiler_params=pltpu.CompilerParams(dimension_semantics=("arbitrary",),
                                             vmem_limit_bytes=EXPERT_VMEM_LIMIT),
        name="expert_ffn",
    )(blk_exp, n_used, xs, w_gu, b_gu.reshape(ne, 1, wide), w_dn, b_dn.reshape(ne, 1, d))


COMBINE_TT = 128


def _combine_kernel(dest_ref, gate_ref, x_ref, g2_ref, n3_ref, y_ref, o_ref, ybuf, sems):
    tt = x_ref.shape[0]
    n = dest_ref.shape[0] // TOP_K
    i = pl.program_id(0)
    steps = pl.num_programs(0)

    def gather(step, slot):
        base = step * tt

        def issue(t, carry):
            dst = pl.ds(pl.multiple_of(t * SUBLANES, SUBLANES), SUBLANES)
            for k in range(TOP_K):
                row = dest_ref[k * n + base + t]
                src = pl.ds(pl.multiple_of(row * SUBLANES, SUBLANES), SUBLANES)
                pltpu.make_async_copy(y_ref.at[src, :], ybuf.at[slot, k, dst, :], sems.at[slot]).start()
            return carry

        lax.fori_loop(0, tt, issue, 0, unroll=8)

    @pl.when(i == 0)
    def _():
        gather(0, 0)

    @pl.when(i + 1 < steps)
    def _():
        gather(i + 1, (i + 1) % 2)

    slot = i % 2
    for k in range(TOP_K):
        pltpu.make_async_copy(y_ref.at[pl.ds(0, tt * SUBLANES), :], ybuf.at[slot, k], sems.at[slot]).wait()
    gate = gate_ref[...]
    y = gate[:, 0:1] * _load_token_tiles(ybuf.at[slot, 0])
    for k in range(1, TOP_K):
        y = y + gate[:, k:k + 1] * _load_token_tiles(ybuf.at[slot, k])
    o_ref[...] = x_ref[...] + g2_ref[0] * _rms(y, n3_ref[...])


def moe_combine(dest_flat, gate, x, g2, n3, y_rows):
    b, s, d = x.shape
    n = b * s
    tt = COMBINE_TT
    per_batch = s // tt
    grid_spec = pltpu.PrefetchScalarGridSpec(
        num_scalar_prefetch=1,
        grid=(n // tt,),
        in_specs=[pl.BlockSpec((tt, LANES), lambda i, dest: (i, 0)),
                  pl.BlockSpec((tt, d), lambda i, dest: (i, 0)),
                  pl.BlockSpec((1, 1, d), lambda i, dest: (i // per_batch, 0, 0)),
                  pl.BlockSpec((1, d), lambda i, dest: (0, 0)),
                  pl.BlockSpec(memory_space=pl.ANY)],
        out_specs=pl.BlockSpec((tt, d), lambda i, dest: (i, 0)),
        scratch_shapes=[pltpu.VMEM((2, TOP_K, tt * SUBLANES, LANES), F32),
                        pltpu.SemaphoreType.DMA((2,))],
    )
    out = pl.pallas_call(
        _combine_kernel,
        grid_spec=grid_spec,
        out_shape=jax.ShapeDtypeStruct((n, d), F32),
        compiler_params=_cparams(("arbitrary",)),
        name="moe_combine",
    )(dest_flat, gate, x.reshape(n, d), g2, n3, y_rows)
    return out.reshape(b, s, d)


def _dispatch_plan(idx, rank, counts):
    n = idx.shape[1]
    tm = MOE_TM
    padded = (counts + tm - 1) // tm * tm
    pad_end = jnp.cumsum(padded)
    start = pad_end - padded
    experts = jnp.arange(N_EXPERTS, dtype=jnp.int32)
    dest = rank + jnp.sum(jnp.where(idx[..., None] == experts, start, 0), axis=-1)
    n_rows = -(-(n * TOP_K + N_EXPERTS * (tm - 1)) // tm) * tm
    n_blk = n_rows // tm
    blk_start = jnp.arange(n_blk, dtype=jnp.int32) * tm
    blk_exp = jnp.minimum(jnp.sum((pad_end[None, :] <= blk_start[:, None]).astype(jnp.int32), axis=1),
                          N_EXPERTS - 1)
    n_used = (pad_end[-1] // tm).astype(jnp.int32).reshape(1)
    tail = pad_end[-1] + experts * tm
    zero_rows = jnp.concatenate([jnp.where(padded > 0, pad_end - tm, -1),
                                 jnp.where(tail < n_rows, tail, -1)]).astype(jnp.int32)
    return dest.reshape(-1), blk_exp, n_used, zero_rows, n_rows


def _layer(layer, x, mod, norm_g, w_in, cmp_pe, cmp_w1, cmp_w2, grp_g, w_out, router_w, router_b,
           w_gu, b_gu, w_dn, b_dn, tabs, overlap):
    b, s, d = x.shape
    sh1, sc1, g1, sh2, sc2, g2 = (m.reshape(b, 1, d) for m in jnp.split(mod, 6, axis=-1))
    ng = norm_g.reshape(4, 1, d)

    qa, ka, va, qb, kvb, gb, qc, kc, vc = in_projection(x, sc1, sh1, ng[0], _pack_w_in(w_in), tabs)
    oa = moba_attention(qa, ka, va)
    nc = s // NSA_CMP_STRIDE
    kv16 = jnp.stack([kvb[:, :, :HEAD_DIM], kvb[:, :, HEAD_DIM:2 * HEAD_DIM]], axis=1)
    kv16 = kv16.reshape(b, 2, nc, NSA_CMP_STRIDE * HEAD_DIM)
    cmp_kv = nsa_compress(kv16, cmp_pe.reshape(2, 1, NSA_CMP_LEN * HEAD_DIM), cmp_w1, cmp_w2)
    ob = nsa_attention(qb, kvb, gb, cmp_kv, overlap)
    oc = sb_attention(qc, kc, vc)

    x1, hf, logits_t = out_projection(oa, ob, oc, x, grp_g.reshape(1, d), w_out.astype(BF16), g1,
                                      ng[1], ng[2], sc2, sh2, router_w.T, router_b.reshape(-1, 1))

    n = b * s
    idx, gate, rank, counts = moe_router(logits_t)
    dest, blk_exp, n_used, zero_rows, n_rows = _dispatch_plan(idx, rank, counts[:, 0].astype(jnp.int32))
    xs = moe_dispatch(dest, zero_rows, hf.reshape(n * SUBLANES, LANES), n_rows)
    y_rows = expert_ffn(layer, blk_exp, n_used, xs, w_gu, b_gu, w_dn, b_dn)
    gate = jnp.pad(gate.T, ((0, 0), (0, LANES - TOP_K)))
    return moe_combine(dest, gate, x1, g2, ng[3], y_rows)


def kernel(x, c, ada_w, ada_b, norm_g, w_in, nsa_cmp_pe, nsa_cmp_w1, nsa_cmp_w2, mix_out_g,
           w_out, router_w, router_b, exp_w_gu, exp_b_gu, exp_w_dn, exp_b_dn):
    s = x.shape[1]
    tabs = _rope_tables(s)
    overlap = _nsa_overlap(s)
    mod = ada_modulation(c, ada_w, ada_b)
    for l in range(ada_w.shape[0]):
        x = _layer(l, x, mod[l], norm_g[l], w_in[l], nsa_cmp_pe[l], nsa_cmp_w1[l], nsa_cmp_w2[l],
                   mix_out_g[l], w_out[l], router_w[l], router_b[l], exp_w_gu, exp_b_gu,
                   exp_w_dn, exp_b_dn, tabs, overlap)
    return x
```

```python
import functools

import numpy as np
import jax
import jax.numpy as jnp
from jax import lax
from jax.experimental import pallas as pl
from jax.experimental.pallas import tpu as pltpu

F32 = jnp.float32
BF16 = jnp.bfloat16
HI = lax.Precision.HIGHEST

D_MODEL = 1024
N_HEADS = 16
HEAD_DIM = 64
MOBA_HEADS = 4
NSA_HEADS = 6
SB_HEADS = 6
MOBA_W = MOBA_HEADS * HEAD_DIM
NSA_W = NSA_HEADS * HEAD_DIM
SB_W = SB_HEADS * HEAD_DIM
ROPE_DIM = 16
ROPE_THETA = 500000.0
MOBA_BLOCK = 256
MOBA_TOPK = 3
NSA_CMP_LEN = 32
NSA_CMP_STRIDE = 16
NSA_SLC_BLOCK = 64
NSA_SLC_TOPK = 16
NSA_WINDOW = 512
N_EXPERTS = 32
TOP_K = 4
SWIGLU_LIMIT = 7.0
SWIGLU_ALPHA = 1.702
RMS_EPS = 1e-6
NEG_INF = -1e30
SEL_FORCE = 1e4
SCALE = HEAD_DIM ** -0.5
LOG2E = 1.4426950408889634

LANES = 128
VMEM_LIMIT = 48 * 1024 * 1024


def _cparams(sem):
    return pltpu.CompilerParams(dimension_semantics=sem, vmem_limit_bytes=VMEM_LIMIT)


def _dot(a, b):
    return jnp.dot(a, b, preferred_element_type=F32)


def _dot_nt(a, b):
    return lax.dot_general(a, b, (((1,), (1,)), ((), ())), preferred_element_type=F32)


SB_TILE = 256


SB_GROUP_W = SB_W
SB_LAG = 2


def _sb_kernel(q_ref, k_ref, v_ref, o_ref, ks_ref, vs_ref, run_ref, acc_ref):
    T = SB_TILE
    H = ks_ref.shape[0]
    qi = pl.program_id(2)

    @pl.when(qi == 0)
    def _():
        zeros = jnp.zeros((ks_ref.shape[1], HEAD_DIM), F32)
        for h in range(H):
            ks_ref[h] = k_ref[0, :, h * HEAD_DIM:(h + 1) * HEAD_DIM].astype(BF16)
            vh = v_ref[0, :, h * HEAD_DIM:(h + 1) * HEAD_DIM]
            vs_ref[h] = jnp.concatenate([vh, zeros] if h % 2 == 0 else [zeros, vh], axis=1).astype(BF16)

    run_ref[...] = jnp.zeros(run_ref.shape, F32)
    acc_ref[...] = jnp.zeros(acc_ref.shape, F32)

    row = lax.broadcasted_iota(jnp.int32, (T, T), 0)
    col = lax.broadcasted_iota(jnp.int32, (T, T), 1)
    incl = (row >= col).astype(BF16)
    sum_rhs = jnp.concatenate([incl, incl], axis=0)
    diag_mask = col < row
    qs = [(q_ref[0, :, h * HEAD_DIM:(h + 1) * HEAD_DIM] * (SCALE * LOG2E)).astype(BF16)
          for h in range(H)]
    sign = jnp.uint32(0x80000000)

    def scores(h, j):
        return _dot_nt(qs[h], ks_ref[h, pl.ds(pl.multiple_of(j * T, T), T), :])

    def neg_log2_keep(z, masked):
        neg_abs = lax.bitcast_convert_type(lax.bitcast_convert_type(z, jnp.uint32) | sign, F32)
        nlk = jnp.maximum(z, 0.0) + jnp.log2(1.0 + jnp.exp2(neg_abs))
        return jnp.where(diag_mask, nlk, 0.0) if masked else nlk

    def suffix_sums(nlk):
        hi = nlk.astype(BF16)
        lo = (nlk - hi.astype(F32)).astype(BF16)
        return _dot(jnp.concatenate([hi, lo], axis=1), sum_rhs)

    def weights(z, sums, run, masked):
        x = z - sums - run
        if masked:
            x = jnp.where(diag_mask, x, NEG_INF)
        return jnp.exp2(x.astype(BF16))

    def pv(h, p, j):
        return _dot(p, vs_ref[h, pl.ds(pl.multiple_of(j * T, T), T), :])

    def key_tile(j, masked):
        zs, sums = {0: scores(0, j)}, {}
        for t in range(H + SB_LAG):
            if t + 1 < H:
                zs[t + 1] = scores(t + 1, j)
            if t < H:
                sums[t] = suffix_sums(neg_log2_keep(zs[t], masked))
            h = t - SB_LAG
            if h >= 0:
                run = run_ref[h]
                p = weights(zs.pop(h), sums[h], run, masked)
                run_ref[h] = run + sums.pop(h)[:, 0:1]
                acc_ref[h // 2] = acc_ref[h // 2] + pv(h, p, j)

    key_tile(qi, True)

    def past(i, carry):
        key_tile(qi - 1 - i, False)
        return carry

    lax.fori_loop(0, qi, past, 0)
    for g in range(H // 2):
        o_ref[0, :, g * LANES:(g + 1) * LANES] = acc_ref[g]


def sb_attention(q, k, v):
    b, s, w = q.shape
    T = SB_TILE
    gw = SB_GROUP_W
    nh = gw // HEAD_DIM
    qspec = pl.BlockSpec((1, T, gw), lambda bi, p, i: (bi, i, p))
    kvspec = pl.BlockSpec((1, s, gw), lambda bi, p, i: (bi, 0, p))
    return pl.pallas_call(
        _sb_kernel,
        grid=(b, w // gw, s // T),
        in_specs=[qspec, kvspec, kvspec],
        out_specs=qspec,
        out_shape=jax.ShapeDtypeStruct((b, s, w), F32),
        scratch_shapes=[pltpu.VMEM((nh, s, HEAD_DIM), BF16), pltpu.VMEM((nh, s, LANES), BF16),
                        pltpu.VMEM((nh, T, 1), F32), pltpu.VMEM((nh // 2, T, LANES), F32)],
        compiler_params=_cparams(("arbitrary", "arbitrary", "arbitrary")),
        name="sb_attention",
    )(q, k, v)


def _rank_before_t(vals, n):
    idx = lax.broadcasted_iota(jnp.int32, vals.shape, 0)
    rank = jnp.zeros(vals.shape, F32)
    for j2 in range(n):
        other = vals[j2:j2 + 1, :]
        ahead = (other > vals) | ((other == vals) & (idx > j2))
        rank = rank + jnp.where(ahead, 1.0, 0.0)
    return rank


MOBA_KT = 2 * MOBA_BLOCK
MOBA_GROUP_W = MOBA_W


def _moba_kernel(q_ref, k_ref, v_ref, o_ref, ks_ref, vs_ref, km_ref):
    T, KT = MOBA_BLOCK, MOBA_KT
    H = ks_ref.shape[0]
    nb = km_ref.shape[1]
    s_len = ks_ref.shape[1]
    qi = pl.program_id(2)

    @pl.when(qi == 0)
    def _():
        key_blk = lax.broadcasted_iota(jnp.int32, (s_len, HEAD_DIM), 0) // T
        onehot = (lax.broadcasted_iota(jnp.int32, (s_len, HEAD_DIM), 1) == key_blk).astype(F32)
        ones_col = (lax.broadcasted_iota(jnp.int32, (s_len, HEAD_DIM), 1) == 0).astype(F32)
        for h in range(H):
            kh = k_ref[0, :, h * HEAD_DIM:(h + 1) * HEAD_DIM]
            ks_ref[h] = jnp.concatenate([kh, onehot], axis=1).astype(BF16)
            vs_ref[h] = jnp.concatenate([v_ref[0, :, h * HEAD_DIM:(h + 1) * HEAD_DIM], ones_col],
                                        axis=1).astype(BF16)
            km_ref[h] = jnp.mean(kh.reshape(nb, T, HEAD_DIM), axis=1)

    row = lax.broadcasted_iota(jnp.int32, (T, T), 0)
    col = lax.broadcasted_iota(jnp.int32, (T, T), 1)
    causal = col <= row
    blk_t = lax.broadcasted_iota(jnp.int32, (nb, T), 0)
    start = pl.multiple_of(qi * T, T)

    qfs = [q_ref[0, :, h * HEAD_DIM:(h + 1) * HEAD_DIM] for h in range(H)]
    qss = [qf * SCALE for qf in qfs]
    s_own = [_dot_nt(qss[h].astype(BF16), ks_ref[h, pl.ds(start, T), 0:HEAD_DIM]) for h in range(H)]
    q_aug, state = [], []
    for h in range(H):
        gate = lax.dot_general(km_ref[h], qfs[h], (((1,), (1,)), ((), ())),
                               precision=HI, preferred_element_type=F32)
        gate = jnp.where(blk_t < qi, gate, NEG_INF)
        sel = (_rank_before_t(gate, nb) < float(MOBA_TOPK)) & (gate > 0.5 * NEG_INF)
        selb = jnp.where(sel, 0.0, NEG_INF)
        selb = jnp.concatenate([selb, jnp.full((LANES - nb, T), NEG_INF, F32)], axis=0).T
        q_aug.append(jnp.concatenate([qss[h], selb[:, :HEAD_DIM]], axis=1).astype(BF16))

        s = jnp.where(causal, s_own[h], NEG_INF)
        m = jnp.max(s, axis=1, keepdims=True)
        p = jnp.exp((s - m).astype(BF16))
        state += [m, _dot(p, vs_ref[h, pl.ds(start, T), :])]

    def past(i, carry):
        st = pl.multiple_of(i * KT, KT)
        ss = [_dot_nt(q_aug[h], ks_ref[h, pl.ds(st, KT), :]) for h in range(H)]
        out = []
        for h in range(H):
            m, acc = carry[2 * h:2 * h + 2]
            m_new = jnp.maximum(m, jnp.max(ss[h], axis=1, keepdims=True))
            alpha = jnp.exp(m - m_new)
            p = jnp.exp((ss[h] - m_new).astype(BF16))
            out += [m_new, alpha * acc + _dot(p, vs_ref[h, pl.ds(st, KT), :])]
        return tuple(out)

    state = lax.fori_loop(0, (qi + 1) // 2, past, tuple(state))
    outs = [state[2 * h + 1][:, :HEAD_DIM] / jnp.maximum(state[2 * h + 1][:, HEAD_DIM:HEAD_DIM + 1], 1e-30)
            for h in range(H)]
    o_ref[0] = jnp.concatenate(outs, axis=1)


def moba_attention(q, k, v):
    b, s, w = q.shape
    T = MOBA_BLOCK
    nb = s // T
    assert nb <= HEAD_DIM and s % MOBA_KT == 0
    gw = MOBA_GROUP_W
    nh = gw // HEAD_DIM
    qspec = pl.BlockSpec((1, T, gw), lambda bi, p, i: (bi, i, p))
    kvspec = pl.BlockSpec((1, s, gw), lambda bi, p, i: (bi, 0, p))
    return pl.pallas_call(
        _moba_kernel,
        grid=(b, w // gw, nb),
        in_specs=[qspec, kvspec, kvspec],
        out_specs=qspec,
        out_shape=jax.ShapeDtypeStruct((b, s, w), F32),
        scratch_shapes=[pltpu.VMEM((nh, s, LANES), BF16), pltpu.VMEM((nh, s, LANES), BF16),
                        pltpu.VMEM((nh, nb, HEAD_DIM), F32)],
        compiler_params=_cparams(("arbitrary", "arbitrary", "arbitrary")),
        name="moba_attention",
    )(q, k, v)


def _nsa_compress_kernel(x_ref, pe_ref, w1_ref, w2_ref, o_ref):
    nc = x_ref.shape[2]
    half = NSA_CMP_STRIDE * HEAD_DIM
    x = x_ref[0, 0]
    w1 = w1_ref[0]
    first = jnp.dot(x, w1[:half], precision=HI, preferred_element_type=F32)
    second = jnp.dot(x, w1[half:], precision=HI, preferred_element_type=F32)
    pe = jnp.broadcast_to(pe_ref[0], (8, 2 * half))
    peb = jnp.dot(pe, w1, precision=HI, preferred_element_type=F32)[0:1]
    pre = first + pltpu.roll(second, nc - 1, 0) + peb
    hid = pre * jax.nn.sigmoid(pre)
    o_ref[0, 0] = jnp.dot(hid, w2_ref[0], precision=HI, preferred_element_type=F32)


def nsa_compress(kv16, pe, w1, w2):
    b, _, nc, wide = kv16.shape
    return pl.pallas_call(
        _nsa_compress_kernel,
        grid=(b, 2),
        in_specs=[pl.BlockSpec((1, 1, nc, wide), lambda bi, i: (bi, i, 0, 0)),
                  pl.BlockSpec((1, 1, 2 * wide), lambda bi, i: (i, 0, 0)),
                  pl.BlockSpec((1, 2 * wide, HEAD_DIM), lambda bi, i: (i, 0, 0)),
                  pl.BlockSpec((1, HEAD_DIM, HEAD_DIM), lambda bi, i: (i, 0, 0))],
        out_specs=pl.BlockSpec((1, 1, nc, HEAD_DIM), lambda bi, i: (bi, i, 0, 0)),
        out_shape=jax.ShapeDtypeStruct((b, 2, nc, HEAD_DIM), F32),
        compiler_params=_cparams(("arbitrary", "arbitrary")),
        name="nsa_compress",
    )(kv16, pe, w1, w2)


NSA_TQ = 256
NSA_KT = 512
NSA_SPAN = NSA_WINDOW + NSA_TQ
NSA_CHAIN_HEADS = 2


def _softmax_rows(s, mask):
    s = jnp.where(mask, s, NEG_INF)
    m = jnp.max(s, axis=-1, keepdims=True)
    e = jnp.where(mask, jnp.exp(s - m), 0.0)
    return e / jnp.maximum(jnp.sum(e, axis=-1, keepdims=True), 1e-30)


def _split_bf16(x):
    hi = x.astype(BF16)
    return hi, (x - hi.astype(F32)).astype(BF16)


def _nsa_kernel(q_ref, cmp_ref, slc_ref, win_ref, g_ref, ov_ref, o_ref,
                ksl_ref, vsl_ref, kw_ref, vw_ref, kc_ref):
    TQ, KT, H = NSA_TQ, NSA_KT, NSA_HEADS
    ns, nc = ov_ref.shape
    s_len = ksl_ref.shape[0]
    qi = pl.program_id(1)
    t0 = qi * TQ

    @pl.when(qi == 0)
    def _():
        lane = lax.broadcasted_iota(jnp.int32, (s_len, HEAD_DIM), 1)
        key_blk = lax.broadcasted_iota(jnp.int32, (s_len, HEAD_DIM), 0) // NSA_SLC_BLOCK
        onehot = (lane == key_blk).astype(F32)
        ones_col = (lane == 0).astype(F32)
        ksl_ref[...] = jnp.concatenate([slc_ref[0, :, :HEAD_DIM], onehot], axis=1).astype(BF16)
        vsl_ref[...] = jnp.concatenate([slc_ref[0, :, HEAD_DIM:], ones_col], axis=1).astype(BF16)
        kw_ref[...] = win_ref[0, :, :HEAD_DIM].astype(BF16)
        vw_ref[...] = jnp.concatenate([win_ref[0, :, HEAD_DIM:], ones_col], axis=1).astype(BF16)
        k_hi, k_lo = _split_bf16(cmp_ref[0, 0])
        kc_ref[...] = jnp.concatenate([k_hi, k_hi, k_lo], axis=1)

    qf = jnp.concatenate([q_ref[0, :, h * HEAD_DIM:(h + 1) * HEAD_DIM] for h in range(H)],
                         axis=0) * SCALE
    q, q_lo = _split_bf16(qf)

    tq_c = t0 + lax.broadcasted_iota(jnp.int32, (TQ, nc), 0)
    n_c = lax.broadcasted_iota(jnp.int32, (TQ, nc), 1)
    mask_c = (n_c * NSA_CMP_STRIDE + (NSA_CMP_LEN - 1) <= tq_c) & (n_c < nc - 1)
    s_c = _dot_nt(jnp.concatenate([q, q_lo, q], axis=1), kc_ref[...])
    p_c = _softmax_rows(s_c.reshape(H, TQ, nc), mask_c[None])
    o_c = _dot(p_c.reshape(H * TQ, nc).astype(BF16), cmp_ref[0, 1].astype(BF16))

    imp = lax.dot_general(ov_ref[...], jnp.sum(p_c, axis=0), (((1,), (1,)), ((), ())),
                          precision=HI, preferred_element_type=F32)
    tq_s = t0 + lax.broadcasted_iota(jnp.int32, (ns, TQ), 1)
    blk = lax.broadcasted_iota(jnp.int32, (ns, TQ), 0)
    own = tq_s // NSA_SLC_BLOCK
    forced = (blk == 0) | (blk == own) | (blk == own - 1)
    imp = jnp.where(forced, SEL_FORCE, imp)
    imp = jnp.where(blk <= own, imp, NEG_INF)
    sel = (_rank_before_t(imp, ns) < float(min(NSA_SLC_TOPK, ns))) & (imp > 0.5 * NEG_INF)
    selb = jnp.where(sel, 0.0, NEG_INF)
    if ns < LANES:
        selb = jnp.concatenate([selb, jnp.full((LANES - ns, TQ), NEG_INF, F32)], axis=0)
    selb = selb.T[:, :HEAD_DIM]
    q_aug = jnp.concatenate([qf, jnp.concatenate([selb] * H, axis=0)], axis=1).astype(BF16)

    jd = t0 // KT
    start = pl.multiple_of(jd * KT, KT)
    kpos = start + lax.broadcasted_iota(jnp.int32, (TQ, KT), 1)
    tq_k = t0 + lax.broadcasted_iota(jnp.int32, (TQ, KT), 0)
    HC = NSA_CHAIN_HEADS
    G, R = H // HC, HC * TQ
    qa = [q_aug[c * R:(c + 1) * R] for c in range(G)]
    causal_k = (kpos <= tq_k)[None]

    def scores(c, st):
        return _dot_nt(qa[c], ksl_ref[pl.ds(st, KT), :]).reshape(HC, TQ, KT)

    def normalized(acc):
        return acc[:, :HEAD_DIM] / jnp.maximum(acc[:, HEAD_DIM:HEAD_DIM + 1], 1e-30)

    s_own = [scores(c, start) for c in range(G)]
    state = []
    for c in range(G):
        s = jnp.where(causal_k, s_own[c], NEG_INF)
        m = jnp.max(s, axis=-1, keepdims=True)
        p = jnp.exp((s - m).astype(BF16))
        state += [m, _dot(p.reshape(R, KT), vsl_ref[pl.ds(start, KT), :])]

    def past(j, carry):
        st = pl.multiple_of(j * KT, KT)
        ss = [scores(c, st) for c in range(G)]
        out = []
        for c in range(G):
            m, acc = carry[2 * c:2 * c + 2]
            m_new = jnp.maximum(m, jnp.max(ss[c], axis=-1, keepdims=True))
            alpha = jnp.exp(m - m_new)
            p = jnp.exp((ss[c] - m_new).astype(BF16))
            pv = _dot(p.reshape(R, KT), vsl_ref[pl.ds(st, KT), :])
            out += [m_new, alpha.reshape(R, 1) * acc + pv]
        return tuple(out)

    state = lax.fori_loop(0, jd, past, tuple(state))
    o_s = jnp.concatenate([normalized(state[2 * c + 1]) for c in range(G)], axis=0)

    w0 = pl.multiple_of(jnp.maximum(t0 - NSA_WINDOW, 0), TQ)
    wpos = w0 + lax.broadcasted_iota(jnp.int32, (TQ, NSA_SPAN), 1)
    tq_w = t0 + lax.broadcasted_iota(jnp.int32, (TQ, NSA_SPAN), 0)
    mask_w = ((wpos <= tq_w) & (wpos > tq_w - NSA_WINDOW))[None]
    s_w = jnp.where(mask_w, _dot_nt(q, kw_ref[pl.ds(w0, NSA_SPAN), :]).reshape(H, TQ, NSA_SPAN),
                    NEG_INF)
    p_w = jnp.exp((s_w - jnp.max(s_w, axis=-1, keepdims=True)).astype(BF16))
    o_w = normalized(_dot(p_w.reshape(H * TQ, NSA_SPAN), vw_ref[pl.ds(w0, NSA_SPAN), :]))

    g = g_ref[0]
    outs = []
    for h in range(H):
        rows = slice(h * TQ, (h + 1) * TQ)
        outs.append(g[:, 3 * h:3 * h + 1] * o_c[rows] + g[:, 3 * h + 1:3 * h + 2] * o_s[rows]
                    + g[:, 3 * h + 2:3 * h + 3] * o_w[rows])
    o_ref[0] = jnp.concatenate(outs, axis=1)


def nsa_attention(q, kv, gates, cmp_kv, overlap):
    b, s, w = q.shape
    ns, nc = overlap.shape
    assert ns <= HEAD_DIM
    TQ = NSA_TQ
    return pl.pallas_call(
        _nsa_kernel,
        grid=(b, s // TQ),
        in_specs=[pl.BlockSpec((1, TQ, w), lambda bi, i: (bi, i, 0)),
                  pl.BlockSpec((1, 2, nc, HEAD_DIM), lambda bi, i: (bi, 0, 0, 0)),
                  pl.BlockSpec((1, s, LANES), lambda bi, i: (bi, 0, 1)),
                  pl.BlockSpec((1, s, LANES), lambda bi, i: (bi, 0, 2)),
                  pl.BlockSpec((1, TQ, LANES), lambda bi, i: (bi, i, 0)),
                  pl.BlockSpec((ns, nc), lambda bi, i: (0, 0))],
        out_specs=pl.BlockSpec((1, TQ, w), lambda bi, i: (bi, i, 0)),
        out_shape=jax.ShapeDtypeStruct((b, s, w), F32),
        scratch_shapes=[pltpu.VMEM((s, LANES), BF16), pltpu.VMEM((s, LANES), BF16),
                        pltpu.VMEM((s, HEAD_DIM), BF16), pltpu.VMEM((s, LANES), BF16),
                        pltpu.VMEM((nc, 3 * HEAD_DIM), BF16)],
        compiler_params=_cparams(("arbitrary", "arbitrary")),
        name="nsa_attention",
    )(q, cmp_kv, kv, kv, gates, overlap)


def _nsa_overlap(s):
    nc = s // NSA_CMP_STRIDE
    ns = s // NSA_SLC_BLOCK
    cstart = np.arange(nc) * NSA_CMP_STRIDE
    cend = cstart + NSA_CMP_LEN - 1
    sstart = np.arange(ns) * NSA_SLC_BLOCK
    ov = (cstart[:, None] <= sstart[None, :] + NSA_SLC_BLOCK - 1) & (cend[:, None] >= sstart[None, :])
    ov[nc - 1] = False
    return jnp.asarray(ov.T.astype(np.float32))


def _mod_kernel(c_ref, w_ref, b_ref, o_ref):
    c = c_ref[...]
    act = c * jax.nn.sigmoid(c)
    o_ref[0] = jnp.dot(act, w_ref[0], precision=HI, preferred_element_type=F32) + b_ref[0]


def ada_modulation(c, ada_w, ada_b):
    nl, d, wide = ada_w.shape
    b = c.shape[0]
    tn = D_MODEL
    return pl.pallas_call(
        _mod_kernel,
        grid=(nl, wide // tn),
        in_specs=[pl.BlockSpec((b, d), lambda l, j: (0, 0)),
                  pl.BlockSpec((1, d, tn), lambda l, j: (l, 0, j)),
                  pl.BlockSpec((1, 1, tn), lambda l, j: (l, 0, j))],
        out_specs=pl.BlockSpec((1, b, tn), lambda l, j: (l, 0, j)),
        out_shape=jax.ShapeDtypeStruct((nl, b, wide), F32),
        compiler_params=_cparams(("arbitrary", "arbitrary")),
        name="ada_modulation",
    )(c, ada_w, ada_b.reshape(nl, 1, wide))


def _rms(x, g):
    return x * lax.rsqrt(jnp.mean(x * x, axis=-1, keepdims=True) + RMS_EPS) * g


_GATE_PAD = LANES
_COLS = {}
_off = 0
for _name, _w in (("qa", MOBA_W), ("ka", MOBA_W), ("va", MOBA_W), ("qb", NSA_W), ("kvb", NSA_W),
                  ("gb", _GATE_PAD), ("qc", SB_W), ("kc", SB_W), ("vc", SB_W)):
    _COLS[_name] = (_off, _w)
    _off += _w
IN_W_PACKED = _off
PROJ_TM = 256


def _rope_block(p, cs, sm, sp):
    return p * cs + pltpu.roll(p, LANES - ROPE_DIM // 2, 1) * sm + pltpu.roll(p, ROPE_DIM // 2, 1) * sp


def _in_proj_kernel(x_ref, sc_ref, sh_ref, g_ref, w_ref, cqk_ref, mqk_ref, pqk_ref,
                    ckv_ref, mkv_ref, pkv_ref,
                    qa_ref, ka_ref, va_ref, qb_ref, kvb_ref, gb_ref, qc_ref, kc_ref, vc_ref):
    hm = _rms(x_ref[0], g_ref[...]) * (1.0 + sc_ref[0]) + sh_ref[0]
    p = _dot(hm.astype(BF16), w_ref[...])
    qk = (cqk_ref[...], mqk_ref[...], pqk_ref[...])
    kv = (ckv_ref[...], mkv_ref[...], pkv_ref[...])

    def emit(ref, name, tabs):
        off, w = _COLS[name]
        for j in range(w // LANES):
            blk = p[:, off + j * LANES: off + (j + 1) * LANES]
            if tabs is not None:
                blk = _rope_block(blk, *tabs)
            ref[0, :, j * LANES:(j + 1) * LANES] = blk

    emit(qa_ref, "qa", qk)
    emit(ka_ref, "ka", qk)
    emit(va_ref, "va", None)
    emit(qb_ref, "qb", qk)
    emit(kvb_ref, "kvb", kv)
    emit(qc_ref, "qc", None)
    emit(kc_ref, "kc", None)
    emit(vc_ref, "vc", None)
    off, w = _COLS["gb"]
    gb_ref[0] = jax.nn.sigmoid(p[:, off:off + w])


def in_projection(x, sc, sh, g, w_packed, tabs):
    b, s, d = x.shape
    tm = PROJ_TM
    row = lambda w: pl.BlockSpec((1, tm, w), lambda bi, i: (bi, i, 0))
    vec = pl.BlockSpec((1, 1, d), lambda bi, i: (bi, 0, 0))
    tab = pl.BlockSpec((tm, LANES), lambda bi, i: (i, 0))
    names = ("qa", "ka", "va", "qb", "kvb", "gb", "qc", "kc", "vc")
    return pl.pallas_call(
        _in_proj_kernel,
        grid=(b, s // tm),
        in_specs=[row(d), vec, vec, pl.BlockSpec((1, d), lambda bi, i: (0, 0)),
                  pl.BlockSpec((d, IN_W_PACKED), lambda bi, i: (0, 0))] + [tab] * 6,
        out_specs=[row(_COLS[n][1]) for n in names],
        out_shape=[jax.ShapeDtypeStruct((b, s, _COLS[n][1]), F32) for n in names],
        compiler_params=_cparams(("arbitrary", "arbitrary")),
        name="in_projection",
    )(x, sc, sh, g, w_packed, *tabs)


def _pack_w_in(w_in):
    widths = (MOBA_W, MOBA_W, MOBA_W, NSA_W, NSA_W, 3 * NSA_HEADS, SB_W, SB_W, SB_W)
    offs = np.cumsum((0,) + widths)
    parts = []
    for i, w in enumerate(widths):
        blk = w_in[:, offs[i]:offs[i + 1]]
        if w == 3 * NSA_HEADS:
            blk = jnp.pad(blk, ((0, 0), (0, _GATE_PAD - w)))
        parts.append(blk)
    return jnp.concatenate(parts, axis=1).astype(BF16)


def _rope_tables(s):
    half = ROPE_DIM // 2
    inv_freq = ROPE_THETA ** (-jnp.arange(0, ROPE_DIM, 2, dtype=F32) / ROPE_DIM)
    ang = jnp.arange(s, dtype=F32)[:, None] * inv_freq[None, :]
    cos, sin = jnp.cos(ang), jnp.sin(ang)
    zeros = jnp.zeros((s, HEAD_DIM - ROPE_DIM), F32)
    z8 = jnp.zeros((s, half), F32)
    cs_h = jnp.concatenate([cos, cos, zeros + 1.0], axis=1)
    sm_h = jnp.concatenate([-sin, z8, zeros], axis=1)
    sp_h = jnp.concatenate([z8, sin, zeros], axis=1)
    ident = (jnp.ones((s, HEAD_DIM), F32), jnp.zeros((s, HEAD_DIM), F32), jnp.zeros((s, HEAD_DIM), F32))
    qk = tuple(jnp.concatenate([t, t], axis=1) for t in (cs_h, sm_h, sp_h))
    kv = tuple(jnp.concatenate([t, i], axis=1) for t, i in zip((cs_h, sm_h, sp_h), ident))
    return qk + kv


OUT_TM = 256
SUBLANES = 8


def _store_token_tiles(ref, val):
    rows = val.shape[0]
    for j in range(SUBLANES):
        ref[pl.ds(j, rows, stride=SUBLANES), :] = val[:, j * LANES:(j + 1) * LANES]


def _load_token_tiles(ref):
    rows = ref.shape[0] // SUBLANES
    return jnp.concatenate([ref[pl.ds(j, rows, stride=SUBLANES), :] for j in range(SUBLANES)], axis=1)


def _out_proj_kernel(oa_ref, ob_ref, oc_ref, x_ref, gg_ref, w_ref, g1_ref, n1_ref, n2_ref,
                     sc_ref, sh_ref, rw_ref, rb_ref, x1_ref, hf_ref, lg_ref):
    gg = gg_ref[...]
    y = jnp.concatenate([_rms(oa_ref[0], gg[:, :MOBA_W]),
                         _rms(ob_ref[0], gg[:, MOBA_W:MOBA_W + NSA_W]),
                         _rms(oc_ref[0], gg[:, MOBA_W + NSA_W:])], axis=1)
    y = _dot(y.astype(BF16), w_ref[...])
    x1 = x_ref[0] + g1_ref[0] * _rms(y, n1_ref[...])
    x1_ref[0] = x1
    hf = _rms(x1, n2_ref[...]) * (1.0 + sc_ref[0]) + sh_ref[0]
    _store_token_tiles(hf_ref.at[0], hf)
    lg_ref[...] = lax.dot_general(rw_ref[...], hf, (((1,), (1,)), ((), ())),
                                  precision=HI, preferred_element_type=F32) + rb_ref[...]


def out_projection(oa, ob, oc, x, grp_g, w_out, g1, n1, n2, sc2, sh2, rw_t, rb):
    b, s, d = x.shape
    assert d == SUBLANES * LANES
    ne = rw_t.shape[0]
    tm = OUT_TM
    steps = s // tm
    row = lambda w: pl.BlockSpec((1, tm, w), lambda bi, i: (bi, i, 0))
    vec = pl.BlockSpec((1, 1, d), lambda bi, i: (bi, 0, 0))
    cst = lambda r, w: pl.BlockSpec((r, w), lambda bi, i: (0, 0))
    return pl.pallas_call(
        _out_proj_kernel,
        grid=(b, steps),
        in_specs=[row(MOBA_W), row(NSA_W), row(SB_W), row(d), cst(1, d), cst(d, d), vec,
                  cst(1, d), cst(1, d), vec, vec, cst(ne, d), cst(ne, 1)],
        out_specs=[row(d), pl.BlockSpec((1, tm * SUBLANES, LANES), lambda bi, i: (bi, i, 0)),
                   pl.BlockSpec((ne, tm), lambda bi, i: (0, bi * steps + i))],
        out_shape=[jax.ShapeDtypeStruct((b, s, d), F32),
                   jax.ShapeDtypeStruct((b, s * SUBLANES, LANES), F32),
                   jax.ShapeDtypeStruct((ne, b * s), F32)],
        compiler_params=_cparams(("arbitrary", "arbitrary")),
        name="out_projection",
    )(oa, ob, oc, x, grp_g, w_out, g1, n1, n2, sc2, sh2, rw_t, rb)


ROUTE_TT = 512


def _router_kernel(lg_ref, idx_ref, gate_ref, rank_ref, cnt_ref):
    ne, tt = lg_ref.shape

    @pl.when(pl.program_id(0) == 0)
    def _():
        cnt_ref[...] = jnp.zeros(cnt_ref.shape, cnt_ref.dtype)

    v = lg_ref[...]
    erow = lax.broadcasted_iota(jnp.int32, (ne, tt), 0)
    vals, hots, firsts = [], [], []
    for _ in range(TOP_K):
        m = jnp.max(v, axis=0, keepdims=True)
        first = jnp.min(jnp.where(v == m, erow, ne), axis=0, keepdims=True)
        hot = erow == first
        v = jnp.where(hot, -jnp.inf, v)
        vals.append(m)
        hots.append(hot)
        firsts.append(first)
    exps = [jnp.exp(val - vals[0]) for val in vals]
    den = exps[0] + exps[1] + exps[2] + exps[3]
    gate_ref[...] = jnp.concatenate([e / den for e in exps], axis=0)
    idx_ref[...] = jnp.concatenate(firsts, axis=0)

    cnt = jnp.zeros((ne, tt), F32)
    for hot in hots:
        cnt = cnt + jnp.where(hot, 1.0, 0.0)
    before = (lax.broadcasted_iota(jnp.int32, (tt, tt), 0)
              < lax.broadcasted_iota(jnp.int32, (tt, tt), 1)).astype(BF16)
    excl = _dot(cnt.astype(BF16), before) + cnt_ref[:, 0:1]
    ranks = [jnp.sum(jnp.where(hot, excl, 0.0), axis=0, keepdims=True) for hot in hots]
    rank_ref[...] = jnp.concatenate(ranks, axis=0).astype(jnp.int32)
    cnt_ref[...] = cnt_ref[...] + jnp.sum(cnt, axis=1, keepdims=True)


def moe_router(logits_t):
    ne, n = logits_t.shape
    tt = ROUTE_TT
    slot = pl.BlockSpec((TOP_K, tt), lambda i: (0, i))
    return pl.pallas_call(
        _router_kernel,
        grid=(n // tt,),
        in_specs=[pl.BlockSpec((ne, tt), lambda i: (0, i))],
        out_specs=[slot, slot, slot, pl.BlockSpec((ne, LANES), lambda i: (0, 0))],
        out_shape=[jax.ShapeDtypeStruct((TOP_K, n), jnp.int32), jax.ShapeDtypeStruct((TOP_K, n), F32),
                   jax.ShapeDtypeStruct((TOP_K, n), jnp.int32), jax.ShapeDtypeStruct((ne, LANES), F32)],
        compiler_params=_cparams(("arbitrary",)),
        name="moe_router",
    )(logits_t)


MOE_TM = 512


def _dispatch_kernel(dest_ref, zrow_ref, hf_ref, xs_ref, zbuf, sem, zsem):
    tt = hf_ref.shape[0] // SUBLANES
    n = dest_ref.shape[0] // TOP_K
    base = pl.program_id(0) * tt

    def tile(ref, row):
        return ref.at[pl.ds(pl.multiple_of(row * SUBLANES, SUBLANES), SUBLANES), :]

    @pl.when(pl.program_id(0) == 0)
    def _():
        zbuf[...] = jnp.zeros(zbuf.shape, zbuf.dtype)

        def fill(j):
            first = pl.multiple_of(jnp.maximum(zrow_ref[j], 0) * SUBLANES, MOE_TM * SUBLANES)
            return pltpu.make_async_copy(zbuf, xs_ref.at[pl.ds(first, MOE_TM * SUBLANES), :], zsem)

        for j in range(zrow_ref.shape[0]):
            pl.when(zrow_ref[j] >= 0)(fill(j).start)
        for j in range(zrow_ref.shape[0]):
            pl.when(zrow_ref[j] >= 0)(fill(j).wait)

    def issue(t, carry):
        for k in range(TOP_K):
            pltpu.make_async_copy(tile(hf_ref, t), tile(xs_ref, dest_ref[k * n + base + t]), sem).start()
        return carry

    lax.fori_loop(0, tt, issue, 0, unroll=8)
    for _ in range(TOP_K):
        pltpu.make_async_copy(hf_ref, xs_ref.at[pl.ds(0, tt * SUBLANES), :], sem).wait()


DISPATCH_TT = 256


def moe_dispatch(dest_flat, zero_rows, hf, n_rows):
    n = hf.shape[0] // SUBLANES
    tt = DISPATCH_TT
    grid_spec = pltpu.PrefetchScalarGridSpec(
        num_scalar_prefetch=2,
        grid=(n // tt,),
        in_specs=[pl.BlockSpec((tt * SUBLANES, LANES), lambda i, dest, zrow: (i, 0))],
        out_specs=pl.BlockSpec(memory_space=pl.ANY),
        scratch_shapes=[pltpu.VMEM((MOE_TM * SUBLANES, LANES), F32), pltpu.SemaphoreType.DMA,
                        pltpu.SemaphoreType.DMA],
    )
    return pl.pallas_call(
        _dispatch_kernel,
        grid_spec=grid_spec,
        out_shape=jax.ShapeDtypeStruct((n_rows * SUBLANES, LANES), F32),
        compiler_params=_cparams(("arbitrary",)),
        name="moe_dispatch",
    )(dest_flat, zero_rows, hf)


def _expert_kernel(be_ref, nu_ref, xs_ref, wgu_ref, bgu_ref, wdn_ref, bdn_ref, y_ref,
                   wgu_bf, wdn_bf):
    i = pl.program_id(0)
    used = i < nu_ref[0]

    @pl.when(used & ((i == 0) | (be_ref[i] != be_ref[jnp.maximum(i - 1, 0)])))
    def _():
        wgu_bf[...] = wgu_ref[0].astype(BF16)
        wdn_bf[...] = wdn_ref[0].astype(BF16)

    @pl.when(used)
    def _():
        hgu = _dot(_load_token_tiles(xs_ref).astype(BF16), wgu_bf[...]) + bgu_ref[0]
        de = hgu.shape[1] // 2
        glu = jnp.minimum(hgu[:, :de], SWIGLU_LIMIT)
        lin = jnp.clip(hgu[:, de:], -SWIGLU_LIMIT, SWIGLU_LIMIT)
        act = glu * jax.nn.sigmoid(SWIGLU_ALPHA * glu) * (lin + 1.0)
        _store_token_tiles(y_ref, _dot(act.astype(BF16), wdn_bf[...]) + bdn_ref[0])

    @pl.when(jnp.logical_not(used))
    def _():
        y_ref[...] = jnp.zeros(y_ref.shape, y_ref.dtype)


EXPERT_VMEM_LIMIT = 56 * 1024 * 1024


def expert_ffn(layer, blk_exp, n_used, xs, w_gu, b_gu, w_dn, b_dn):
    r = xs.shape[0] // SUBLANES
    d = SUBLANES * LANES
    nl, ne, _, wide = w_gu.shape
    w_gu = w_gu.reshape(nl * ne, d, wide)
    w_dn = w_dn.reshape(nl * ne, wide // 2, d)
    b_gu = b_gu.reshape(nl * ne, wide)
    b_dn = b_dn.reshape(nl * ne, d)
    ne, base = nl * ne, layer * ne
    tm = MOE_TM
    grid_spec = pltpu.PrefetchScalarGridSpec(
        num_scalar_prefetch=2,
        grid=(r // tm,),
        in_specs=[pl.BlockSpec((tm * SUBLANES, LANES), lambda i, be, nu: (i, 0)),
                  pl.BlockSpec((1, d, wide), lambda i, be, nu: (base + be[i], 0, 0)),
                  pl.BlockSpec((1, 1, wide), lambda i, be, nu: (base + be[i], 0, 0)),
                  pl.BlockSpec((1, wide // 2, d), lambda i, be, nu: (base + be[i], 0, 0)),
                  pl.BlockSpec((1, 1, d), lambda i, be, nu: (base + be[i], 0, 0))],
        out_specs=pl.BlockSpec((tm * SUBLANES, LANES), lambda i, be, nu: (i, 0)),
        scratch_shapes=[pltpu.VMEM((d, wide), BF16), pltpu.VMEM((wide // 2, d), BF16)],
    )
    return pl.pallas_call(
        _expert_kernel,
        grid_spec=grid_spec,
        out_shape=jax.ShapeDtypeStruct((r * SUBLANES, LANES), F32),
        compiler_params=pltpu.CompilerParams(dimension_semantics=("arbitrary",),
                                             vmem_limit_bytes=EXPERT_VMEM_LIMIT),
        name="expert_ffn",
    )(blk_exp, n_used, xs, w_gu, b_gu.reshape(ne, 1, wide), w_dn, b_dn.reshape(ne, 1, d))


COMBINE_TT = 128


def _combine_kernel(dest_ref, gate_ref, x_ref, g2_ref, n3_ref, y_ref, o_ref, ybuf, sems):
    tt = x_ref.shape[0]
    n = dest_ref.shape[0] // TOP_K
    i = pl.program_id(0)
    steps = pl.num_programs(0)

    def gather(step, slot):
        base = step * tt

        def issue(t, carry):
            dst = pl.ds(pl.multiple_of(t * SUBLANES, SUBLANES), SUBLANES)
            for k in range(TOP_K):
                row = dest_ref[k * n + base + t]
                src = pl.ds(pl.multiple_of(row * SUBLANES, SUBLANES), SUBLANES)
                pltpu.make_async_copy(y_ref.at[src, :], ybuf.at[slot, k, dst, :], sems.at[slot]).start()
            return carry

        lax.fori_loop(0, tt, issue, 0, unroll=8)

    @pl.when(i == 0)
    def _():
        gather(0, 0)

    @pl.when(i + 1 < steps)
    def _():
        gather(i + 1, (i + 1) % 2)

    slot = i % 2
    for k in range(TOP_K):
        pltpu.make_async_copy(y_ref.at[pl.ds(0, tt * SUBLANES), :], ybuf.at[slot, k], sems.at[slot]).wait()
    gate = gate_ref[...]
    y = gate[:, 0:1] * _load_token_tiles(ybuf.at[slot, 0])
    for k in range(1, TOP_K):
        y = y + gate[:, k:k + 1] * _load_token_tiles(ybuf.at[slot, k])
    o_ref[...] = x_ref[...] + g2_ref[0] * _rms(y, n3_ref[...])


def moe_combine(dest_flat, gate, x, g2, n3, y_rows):
    b, s, d = x.shape
    n = b * s
    tt = COMBINE_TT
    per_batch = s // tt
    grid_spec = pltpu.PrefetchScalarGridSpec(
        num_scalar_prefetch=1,
        grid=(n // tt,),
        in_specs=[pl.BlockSpec((tt, LANES), lambda i, dest: (i, 0)),
                  pl.BlockSpec((tt, d), lambda i, dest: (i, 0)),
                  pl.BlockSpec((1, 1, d), lambda i, dest: (i // per_batch, 0, 0)),
                  pl.BlockSpec((1, d), lambda i, dest: (0, 0)),
                  pl.BlockSpec(memory_space=pl.ANY)],
        out_specs=pl.BlockSpec((tt, d), lambda i, dest: (i, 0)),
        scratch_shapes=[pltpu.VMEM((2, TOP_K, tt * SUBLANES, LANES), F32),
                        pltpu.SemaphoreType.DMA((2,))],
    )
    out = pl.pallas_call(
        _combine_kernel,
        grid_spec=grid_spec,
        out_shape=jax.ShapeDtypeStruct((n, d), F32),
        compiler_params=_cparams(("arbitrary",)),
        name="moe_combine",
    )(dest_flat, gate, x.reshape(n, d), g2, n3, y_rows)
    return out.reshape(b, s, d)


def _dispatch_plan(idx, rank, counts):
    n = idx.shape[1]
    tm = MOE_TM
    padded = (counts + tm - 1) // tm * tm
    pad_end = jnp.cumsum(padded)
    start = pad_end - padded
    experts = jnp.arange(N_EXPERTS, dtype=jnp.int32)
    dest = rank + jnp.sum(jnp.where(idx[..., None] == experts, start, 0), axis=-1)
    n_rows = -(-(n * TOP_K + N_EXPERTS * (tm - 1)) // tm) * tm
    n_blk = n_rows // tm
    blk_start = jnp.arange(n_blk, dtype=jnp.int32) * tm
    blk_exp = jnp.minimum(jnp.sum((pad_end[None, :] <= blk_start[:, None]).astype(jnp.int32), axis=1),
                          N_EXPERTS - 1)
    n_used = (pad_end[-1] // tm).astype(jnp.int32).reshape(1)
    tail = pad_end[-1] + experts * tm
    zero_rows = jnp.concatenate([jnp.where(padded > 0, pad_end - tm, -1),
                                 jnp.where(tail < n_rows, tail, -1)]).astype(jnp.int32)
    return dest.reshape(-1), blk_exp, n_used, zero_rows, n_rows


def _layer(layer, x, mod, norm_g, w_in, cmp_pe, cmp_w1, cmp_w2, grp_g, w_out, router_w, router_b,
           w_gu, b_gu, w_dn, b_dn, tabs, overlap):
    b, s, d = x.shape
    sh1, sc1, g1, sh2, sc2, g2 = (m.reshape(b, 1, d) for m in jnp.split(mod, 6, axis=-1))
    ng = norm_g.reshape(4, 1, d)

    qa, ka, va, qb, kvb, gb, qc, kc, vc = in_projection(x, sc1, sh1, ng[0], _pack_w_in(w_in), tabs)
    oa = moba_attention(qa, ka, va)
    nc = s // NSA_CMP_STRIDE
    kv16 = jnp.stack([kvb[:, :, :HEAD_DIM], kvb[:, :, HEAD_DIM:2 * HEAD_DIM]], axis=1)
    kv16 = kv16.reshape(b, 2, nc, NSA_CMP_STRIDE * HEAD_DIM)
    cmp_kv = nsa_compress(kv16, cmp_pe.reshape(2, 1, NSA_CMP_LEN * HEAD_DIM), cmp_w1, cmp_w2)
    ob = nsa_attention(qb, kvb, gb, cmp_kv, overlap)
    oc = sb_attention(qc, kc, vc)

    x1, hf, logits_t = out_projection(oa, ob, oc, x, grp_g.reshape(1, d), w_out.astype(BF16), g1,
                                      ng[1], ng[2], sc2, sh2, router_w.T, router_b.reshape(-1, 1))

    n = b * s
    idx, gate, rank, counts = moe_router(logits_t)
    dest, blk_exp, n_used, zero_rows, n_rows = _dispatch_plan(idx, rank, counts[:, 0].astype(jnp.int32))
    xs = moe_dispatch(dest, zero_rows, hf.reshape(n * SUBLANES, LANES), n_rows)
    y_rows = expert_ffn(layer, blk_exp, n_used, xs, w_gu, b_gu, w_dn, b_dn)
    gate = jnp.pad(gate.T, ((0, 0), (0, LANES - TOP_K)))
    return moe_combine(dest, gate, x1, g2, ng[3], y_rows)


def kernel(x, c, ada_w, ada_b, norm_g, w_in, nsa_cmp_pe, nsa_cmp_w1, nsa_cmp_w2, mix_out_g,
           w_out, router_w, router_b, exp_w_gu, exp_b_gu, exp_w_dn, exp_b_dn):
    s = x.shape[1]
    tabs = _rope_tables(s)
    overlap = _nsa_overlap(s)
    mod = ada_modulation(c, ada_w, ada_b)
    for l in range(ada_w.shape[0]):
        x = _layer(l, x, mod[l], norm_g[l], w_in[l], nsa_cmp_pe[l], nsa_cmp_w1[l], nsa_cmp_w2[l],
                   mix_out_g[l], w_out[l], router_w[l], router_b[l], exp_w_gu, exp_b_gu,
                   exp_w_dn, exp_b_dn, tabs, overlap)
    return x
```

```python
import functools

import numpy as np
import jax
import jax.numpy as jnp
from jax import lax
from jax.experimental import pallas as pl
from jax.experimental.pallas import tpu as pltpu

F32 = jnp.float32
BF16 = jnp.bfloat16
HI = lax.Precision.HIGHEST

D_MODEL = 1024
N_HEADS = 16
HEAD_DIM = 64
MOBA_HEADS = 4
NSA_HEADS = 6
SB_HEADS = 6
MOBA_W = MOBA_HEADS * HEAD_DIM
NSA_W = NSA_HEADS * HEAD_DIM
SB_W = SB_HEADS * HEAD_DIM
ROPE_DIM = 16
ROPE_THETA = 500000.0
MOBA_BLOCK = 256
MOBA_TOPK = 3
NSA_CMP_LEN = 32
NSA_CMP_STRIDE = 16
NSA_SLC_BLOCK = 64
NSA_SLC_TOPK = 16
NSA_WINDOW = 512
N_EXPERTS = 32
TOP_K = 4
SWIGLU_LIMIT = 7.0
SWIGLU_ALPHA = 1.702
RMS_EPS = 1e-6
NEG_INF = -1e30
SEL_FORCE = 1e4
SCALE = HEAD_DIM ** -0.5
LOG2E = 1.4426950408889634

LANES = 128
VMEM_LIMIT = 48 * 1024 * 1024


def _cparams(sem):
    return pltpu.CompilerParams(dimension_semantics=sem, vmem_limit_bytes=VMEM_LIMIT)


def _dot(a, b):
    return jnp.dot(a, b, preferred_element_type=F32)


def _dot_nt(a, b):
    return lax.dot_general(a, b, (((1,), (1,)), ((), ())), preferred_element_type=F32)


SB_TILE = 256


SB_GROUP_W = SB_W
SB_LAG = 2


def _sb_kernel(q_ref, k_ref, v_ref, o_ref, ks_ref, vs_ref, run_ref, acc_ref):
    T = SB_TILE
    H = ks_ref.shape[0]
    qi = pl.program_id(2)

    @pl.when(qi == 0)
    def _():
        zeros = jnp.zeros((ks_ref.shape[1], HEAD_DIM), F32)
        for h in range(H):
            ks_ref[h] = k_ref[0, :, h * HEAD_DIM:(h + 1) * HEAD_DIM].astype(BF16)
            vh = v_ref[0, :, h * HEAD_DIM:(h + 1) * HEAD_DIM]
            vs_ref[h] = jnp.concatenate([vh, zeros] if h % 2 == 0 else [zeros, vh], axis=1).astype(BF16)

    run_ref[...] = jnp.zeros(run_ref.shape, F32)
    acc_ref[...] = jnp.zeros(acc_ref.shape, F32)

    row = lax.broadcasted_iota(jnp.int32, (T, T), 0)
    col = lax.broadcasted_iota(jnp.int32, (T, T), 1)
    incl = (row >= col).astype(BF16)
    sum_rhs = jnp.concatenate([incl, incl], axis=0)
    diag_mask = col < row
    qs = [(q_ref[0, :, h * HEAD_DIM:(h + 1) * HEAD_DIM] * (SCALE * LOG2E)).astype(BF16)
          for h in range(H)]
    sign = jnp.uint32(0x80000000)

    def scores(h, j):
        return _dot_nt(qs[h], ks_ref[h, pl.ds(pl.multiple_of(j * T, T), T), :])

    def neg_log2_keep(z, masked):
        neg_abs = lax.bitcast_convert_type(lax.bitcast_convert_type(z, jnp.uint32) | sign, F32)
        nlk = jnp.maximum(z, 0.0) + jnp.log2(1.0 + jnp.exp2(neg_abs))
        return jnp.where(diag_mask, nlk, 0.0) if masked else nlk

    def suffix_sums(nlk):
        hi = nlk.astype(BF16)
        lo = (nlk - hi.astype(F32)).astype(BF16)
        return _dot(jnp.concatenate([hi, lo], axis=1), sum_rhs)

    def weights(z, sums, run, masked):
        x = z - sums - run
        if masked:
            x = jnp.where(diag_mask, x, NEG_INF)
        return jnp.exp2(x.astype(BF16))

    def pv(h, p, j):
        return _dot(p, vs_ref[h, pl.ds(pl.multiple_of(j * T, T), T), :])

    def key_tile(j, masked):
        zs, sums = {0: scores(0, j)}, {}
        for t in range(H + SB_LAG):
            if t + 1 < H:
                zs[t + 1] = scores(t + 1, j)
            if t < H:
                sums[t] = suffix_sums(neg_log2_keep(zs[t], masked))
            h = t - SB_LAG
            if h >= 0:
                run = run_ref[h]
                p = weights(zs.pop(h), sums[h], run, masked)
                run_ref[h] = run + sums.pop(h)[:, 0:1]
                acc_ref[h // 2] = acc_ref[h // 2] + pv(h, p, j)

    key_tile(qi, True)

    def past(i, carry):
        key_tile(qi - 1 - i, False)
        return carry

    lax.fori_loop(0, qi, past, 0)
    for g in range(H // 2):
        o_ref[0, :, g * LANES:(g + 1) * LANES] = acc_ref[g]


def sb_attention(q, k, v):
    b, s, w = q.shape
    T = SB_TILE
    gw = SB_GROUP_W
    nh = gw // HEAD_DIM
    qspec = pl.BlockSpec((1, T, gw), lambda bi, p, i: (bi, i, p))
    kvspec = pl.BlockSpec((1, s, gw), lambda bi, p, i: (bi, 0, p))
    return pl.pallas_call(
        _sb_kernel,
        grid=(b, w // gw, s // T),
        in_specs=[qspec, kvspec, kvspec],
        out_specs=qspec,
        out_shape=jax.ShapeDtypeStruct((b, s, w), F32),
        scratch_shapes=[pltpu.VMEM((nh, s, HEAD_DIM), BF16), pltpu.VMEM((nh, s, LANES), BF16),
                        pltpu.VMEM((nh, T, 1), F32), pltpu.VMEM((nh // 2, T, LANES), F32)],
        compiler_params=_cparams(("arbitrary", "arbitrary", "arbitrary")),
        name="sb_attention",
    )(q, k, v)


def _rank_before_t(vals, n):
    idx = lax.broadcasted_iota(jnp.int32, vals.shape, 0)
    rank = jnp.zeros(vals.shape, F32)
    for j2 in range(n):
        other = vals[j2:j2 + 1, :]
        ahead = (other > vals) | ((other == vals) & (idx > j2))
        rank = rank + jnp.where(ahead, 1.0, 0.0)
    return rank


MOBA_KT = 2 * MOBA_BLOCK
MOBA_GROUP_W = MOBA_W


def _moba_kernel(q_ref, k_ref, v_ref, o_ref, ks_ref, vs_ref, km_ref):
    T, KT = MOBA_BLOCK, MOBA_KT
    H = ks_ref.shape[0]
    nb = km_ref.shape[1]
    s_len = ks_ref.shape[1]
    qi = pl.program_id(2)

    @pl.when(qi == 0)
    def _():
        key_blk = lax.broadcasted_iota(jnp.int32, (s_len, HEAD_DIM), 0) // T
        onehot = (lax.broadcasted_iota(jnp.int32, (s_len, HEAD_DIM), 1) == key_blk).astype(F32)
        ones_col = (lax.broadcasted_iota(jnp.int32, (s_len, HEAD_DIM), 1) == 0).astype(F32)
        for h in range(H):
            kh = k_ref[0, :, h * HEAD_DIM:(h + 1) * HEAD_DIM]
            ks_ref[h] = jnp.concatenate([kh, onehot], axis=1).astype(BF16)
            vs_ref[h] = jnp.concatenate([v_ref[0, :, h * HEAD_DIM:(h + 1) * HEAD_DIM], ones_col],
                                        axis=1).astype(BF16)
            km_ref[h] = jnp.mean(kh.reshape(nb, T, HEAD_DIM), axis=1)

    row = lax.broadcasted_iota(jnp.int32, (T, T), 0)
    col = lax.broadcasted_iota(jnp.int32, (T, T), 1)
    causal = col <= row
    blk_t = lax.broadcasted_iota(jnp.int32, (nb, T), 0)
    start = pl.multiple_of(qi * T, T)

    qfs = [q_ref[0, :, h * HEAD_DIM:(h + 1) * HEAD_DIM] for h in range(H)]
    qss = [qf * SCALE for qf in qfs]
    s_own = [_dot_nt(qss[h].astype(BF16), ks_ref[h, pl.ds(start, T), 0:HEAD_DIM]) for h in range(H)]
    q_aug, state = [], []
    for h in range(H):
        gate = lax.dot_general(km_ref[h], qfs[h], (((1,), (1,)), ((), ())),
                               precision=HI, preferred_element_type=F32)
        gate = jnp.where(blk_t < qi, gate, NEG_INF)
        sel = (_rank_before_t(gate, nb) < float(MOBA_TOPK)) & (gate > 0.5 * NEG_INF)
        selb = jnp.where(sel, 0.0, NEG_INF)
        selb = jnp.concatenate([selb, jnp.full((LANES - nb, T), NEG_INF, F32)], axis=0).T
        q_aug.append(jnp.concatenate([qss[h], selb[:, :HEAD_DIM]], axis=1).astype(BF16))

        s = jnp.where(causal, s_own[h], NEG_INF)
        m = jnp.max(s, axis=1, keepdims=True)
        p = jnp.exp((s - m).astype(BF16))
        state += [m, _dot(p, vs_ref[h, pl.ds(start, T), :])]

    def past(i, carry):
        st = pl.multiple_of(i * KT, KT)
        ss = [_dot_nt(q_aug[h], ks_ref[h, pl.ds(st, KT), :]) for h in range(H)]
        out = []
        for h in range(H):
            m, acc = carry[2 * h:2 * h + 2]
            m_new = jnp.maximum(m, jnp.max(ss[h], axis=1, keepdims=True))
            alpha = jnp.exp(m - m_new)
            p = jnp.exp((ss[h] - m_new).astype(BF16))
            out += [m_new, alpha * acc + _dot(p, vs_ref[h, pl.ds(st, KT), :])]
        return tuple(out)

    state = lax.fori_loop(0, (qi + 1) // 2, past, tuple(state))
    outs = [state[2 * h + 1][:, :HEAD_DIM] / jnp.maximum(state[2 * h + 1][:, HEAD_DIM:HEAD_DIM + 1], 1e-30)
            for h in range(H)]
    o_ref[0] = jnp.concatenate(outs, axis=1)


def moba_attention(q, k, v):
    b, s, w = q.shape
    T = MOBA_BLOCK
    nb = s // T
    assert nb <= HEAD_DIM and s % MOBA_KT == 0
    gw = MOBA_GROUP_W
    nh = gw // HEAD_DIM
    qspec = pl.BlockSpec((1, T, gw), lambda bi, p, i: (bi, i, p))
    kvspec = pl.BlockSpec((1, s, gw), lambda bi, p, i: (bi, 0, p))
    return pl.pallas_call(
        _moba_kernel,
        grid=(b, w // gw, nb),
        in_specs=[qspec, kvspec, kvspec],
        out_specs=qspec,
        out_shape=jax.ShapeDtypeStruct((b, s, w), F32),
        scratch_shapes=[pltpu.VMEM((nh, s, LANES), BF16), pltpu.VMEM((nh, s, LANES), BF16),
                        pltpu.VMEM((nh, nb, HEAD_DIM), F32)],
        compiler_params=_cparams(("arbitrary", "arbitrary", "arbitrary")),
        name="moba_attention",
    )(q, k, v)


def _nsa_compress_kernel(x_ref, pe_ref, w1_ref, w2_ref, o_ref):
    nc = x_ref.shape[2]
    half = NSA_CMP_STRIDE * HEAD_DIM
    x = x_ref[0, 0]
    w1 = w1_ref[0]
    first = jnp.dot(x, w1[:half], precision=HI, preferred_element_type=F32)
    second = jnp.dot(x, w1[half:], precision=HI, preferred_element_type=F32)
    pe = jnp.broadcast_to(pe_ref[0], (8, 2 * half))
    peb = jnp.dot(pe, w1, precision=HI, preferred_element_type=F32)[0:1]
    pre = first + pltpu.roll(second, nc - 1, 0) + peb
    hid = pre * jax.nn.sigmoid(pre)
    o_ref[0, 0] = jnp.dot(hid, w2_ref[0], precision=HI, preferred_element_type=F32)


def nsa_compress(kv16, pe, w1, w2):
    b, _, nc, wide = kv16.shape
    return pl.pallas_call(
        _nsa_compress_kernel,
        grid=(b, 2),
        in_specs=[pl.BlockSpec((1, 1, nc, wide), lambda bi, i: (bi, i, 0, 0)),
                  pl.BlockSpec((1, 1, 2 * wide), lambda bi, i: (i, 0, 0)),
                  pl.BlockSpec((1, 2 * wide, HEAD_DIM), lambda bi, i: (i, 0, 0)),
                  pl.BlockSpec((1, HEAD_DIM, HEAD_DIM), lambda bi, i: (i, 0, 0))],
        out_specs=pl.BlockSpec((1, 1, nc, HEAD_DIM), lambda bi, i: (bi, i, 0, 0)),
        out_shape=jax.ShapeDtypeStruct((b, 2, nc, HEAD_DIM), F32),
        compiler_params=_cparams(("arbitrary", "arbitrary")),
        name="nsa_compress",
    )(kv16, pe, w1, w2)


NSA_TQ = 256
NSA_KT = 512
NSA_SPAN = NSA_WINDOW + NSA_TQ
NSA_CHAIN_HEADS = 2


def _softmax_rows(s, mask):
    s = jnp.where(mask, s, NEG_INF)
    m = jnp.max(s, axis=-1, keepdims=True)
    e = jnp.where(mask, jnp.exp(s - m), 0.0)
    return e / jnp.maximum(jnp.sum(e, axis=-1, keepdims=True), 1e-30)


def _split_bf16(x):
    hi = x.astype(BF16)
    return hi, (x - hi.astype(F32)).astype(BF16)


def _nsa_kernel(q_ref, cmp_ref, slc_ref, win_ref, g_ref, ov_ref, o_ref,
                ksl_ref, vsl_ref, kw_ref, vw_ref, kc_ref):
    TQ, KT, H = NSA_TQ, NSA_KT, NSA_HEADS
    ns, nc = ov_ref.shape
    s_len = ksl_ref.shape[0]
    qi = pl.program_id(1)
    t0 = qi * TQ

    @pl.when(qi == 0)
    def _():
        lane = lax.broadcasted_iota(jnp.int32, (s_len, HEAD_DIM), 1)
        key_blk = lax.broadcasted_iota(jnp.int32, (s_len, HEAD_DIM), 0) // NSA_SLC_BLOCK
        onehot = (lane == key_blk).astype(F32)
        ones_col = (lane == 0).astype(F32)
        ksl_ref[...] = jnp.concatenate([slc_ref[0, :, :HEAD_DIM], onehot], axis=1).astype(BF16)
        vsl_ref[...] = jnp.concatenate([slc_ref[0, :, HEAD_DIM:], ones_col], axis=1).astype(BF16)
        kw_ref[...] = win_ref[0, :, :HEAD_DIM].astype(BF16)
        vw_ref[...] = jnp.concatenate([win_ref[0, :, HEAD_DIM:], ones_col], axis=1).astype(BF16)
        k_hi, k_lo = _split_bf16(cmp_ref[0, 0])
        kc_ref[...] = jnp.concatenate([k_hi, k_hi, k_lo], axis=1)

    qf = jnp.concatenate([q_ref[0, :, h * HEAD_DIM:(h + 1) * HEAD_DIM] for h in range(H)],
                         axis=0) * SCALE
    q, q_lo = _split_bf16(qf)

    def normalized(acc):
        return acc[:, :HEAD_DIM] / jnp.maximum(acc[:, HEAD_DIM:HEAD_DIM + 1], 1e-30)

    w0 = pl.multiple_of(jnp.maximum(t0 - NSA_WINDOW, 0), TQ)
    wpos = w0 + lax.broadcasted_iota(jnp.int32, (TQ, NSA_SPAN), 1)
    tq_w = t0 + lax.broadcasted_iota(jnp.int32, (TQ, NSA_SPAN), 0)
    mask_w = ((wpos <= tq_w) & (wpos > tq_w - NSA_WINDOW))[None]
    o_w = []
    for c in range(H // NSA_CHAIN_HEADS):
        rows = slice(c * NSA_CHAIN_HEADS * TQ, (c + 1) * NSA_CHAIN_HEADS * TQ)
        s_w = _dot_nt(q[rows], kw_ref[pl.ds(w0, NSA_SPAN), :]).reshape(NSA_CHAIN_HEADS, TQ, NSA_SPAN)
        s_w = jnp.where(mask_w, s_w, NEG_INF)
        p_w = jnp.exp((s_w - jnp.max(s_w, axis=-1, keepdims=True)).astype(BF16))
        o_w.append(normalized(_dot(p_w.reshape(NSA_CHAIN_HEADS * TQ, NSA_SPAN),
                                   vw_ref[pl.ds(w0, NSA_SPAN), :])))
    o_w = jnp.concatenate(o_w, axis=0)

    tq_c = t0 + lax.broadcasted_iota(jnp.int32, (TQ, nc), 0)
    n_c = lax.broadcasted_iota(jnp.int32, (TQ, nc), 1)
    mask_c = (n_c * NSA_CMP_STRIDE + (NSA_CMP_LEN - 1) <= tq_c) & (n_c < nc - 1)
    s_c = _dot_nt(jnp.concatenate([q, q_lo, q], axis=1), kc_ref[...])
    p_c = _softmax_rows(s_c.reshape(H, TQ, nc), mask_c[None])
    o_c = _dot(p_c.reshape(H * TQ, nc).astype(BF16), cmp_ref[0, 1].astype(BF16))

    imp = lax.dot_general(ov_ref[...], jnp.sum(p_c, axis=0), (((1,), (1,)), ((), ())),
                          precision=HI, preferred_element_type=F32)
    tq_s = t0 + lax.broadcasted_iota(jnp.int32, (ns, TQ), 1)
    blk = lax.broadcasted_iota(jnp.int32, (ns, TQ), 0)
    own = tq_s // NSA_SLC_BLOCK
    forced = (blk == 0) | (blk == own) | (blk == own - 1)
    imp = jnp.where(forced, SEL_FORCE, imp)
    imp = jnp.where(blk <= own, imp, NEG_INF)
    sel = (_rank_before_t(imp, ns) < float(min(NSA_SLC_TOPK, ns))) & (imp > 0.5 * NEG_INF)
    selb = jnp.where(sel, 0.0, NEG_INF)
    if ns < LANES:
        selb = jnp.concatenate([selb, jnp.full((LANES - ns, TQ), NEG_INF, F32)], axis=0)
    selb = selb.T[:, :HEAD_DIM]
    q_aug = jnp.concatenate([qf, jnp.concatenate([selb] * H, axis=0)], axis=1).astype(BF16)

    jd = t0 // KT
    start = pl.multiple_of(jd * KT, KT)
    kpos = start + lax.broadcasted_iota(jnp.int32, (TQ, KT), 1)
    tq_k = t0 + lax.broadcasted_iota(jnp.int32, (TQ, KT), 0)
    HC = NSA_CHAIN_HEADS
    G, R = H // HC, HC * TQ
    qa = [q_aug[c * R:(c + 1) * R] for c in range(G)]
    causal_k = (kpos <= tq_k)[None]

    def scores(c, st):
        return _dot_nt(qa[c], ksl_ref[pl.ds(st, KT), :]).reshape(HC, TQ, KT)

    s_own = [scores(c, start) for c in range(G)]
    state = []
    for c in range(G):
        s = jnp.where(causal_k, s_own[c], NEG_INF)
        m = jnp.max(s, axis=-1, keepdims=True)
        p = jnp.exp((s - m).astype(BF16))
        state += [m, _dot(p.reshape(R, KT), vsl_ref[pl.ds(start, KT), :])]

    def past(j, carry):
        st = pl.multiple_of(j * KT, KT)
        ss = [scores(c, st) for c in range(G)]
        out = []
        for c in range(G):
            m, acc = carry[2 * c:2 * c + 2]
            m_new = jnp.maximum(m, jnp.max(ss[c], axis=-1, keepdims=True))
            alpha = jnp.exp(m - m_new)
            p = jnp.exp((ss[c] - m_new).astype(BF16))
            pv = _dot(p.reshape(R, KT), vsl_ref[pl.ds(st, KT), :])
            out += [m_new, alpha.reshape(R, 1) * acc + pv]
        return tuple(out)

    state = lax.fori_loop(0, jd, past, tuple(state))
    o_s = jnp.concatenate([normalized(state[2 * c + 1]) for c in range(G)], axis=0)

    g = g_ref[0]
    outs = []
    for h in range(H):
        rows = slice(h * TQ, (h + 1) * TQ)
        outs.append(g[:, 3 * h:3 * h + 1] * o_c[rows] + g[:, 3 * h + 1:3 * h + 2] * o_s[rows]
                    + g[:, 3 * h + 2:3 * h + 3] * o_w[rows])
    o_ref[0] = jnp.concatenate(outs, axis=1)


def nsa_attention(q, kv, gates, cmp_kv, overlap):
    b, s, w = q.shape
    ns, nc = overlap.shape
    assert ns <= HEAD_DIM
    TQ = NSA_TQ
    return pl.pallas_call(
        _nsa_kernel,
        grid=(b, s // TQ),
        in_specs=[pl.BlockSpec((1, TQ, w), lambda bi, i: (bi, i, 0)),
                  pl.BlockSpec((1, 2, nc, HEAD_DIM), lambda bi, i: (bi, 0, 0, 0)),
                  pl.BlockSpec((1, s, LANES), lambda bi, i: (bi, 0, 1)),
                  pl.BlockSpec((1, s, LANES), lambda bi, i: (bi, 0, 2)),
                  pl.BlockSpec((1, TQ, LANES), lambda bi, i: (bi, i, 0)),
                  pl.BlockSpec((ns, nc), lambda bi, i: (0, 0))],
        out_specs=pl.BlockSpec((1, TQ, w), lambda bi, i: (bi, i, 0)),
        out_shape=jax.ShapeDtypeStruct((b, s, w), F32),
        scratch_shapes=[pltpu.VMEM((s, LANES), BF16), pltpu.VMEM((s, LANES), BF16),
                        pltpu.VMEM((s, HEAD_DIM), BF16), pltpu.VMEM((s, LANES), BF16),
                        pltpu.VMEM((nc, 3 * HEAD_DIM), BF16)],
        compiler_params=_cparams(("arbitrary", "arbitrary")),
        name="nsa_attention",
    )(q, cmp_kv, kv, kv, gates, overlap)


def _nsa_overlap(s):
    nc = s // NSA_CMP_STRIDE
    ns = s // NSA_SLC_BLOCK
    cstart = np.arange(nc) * NSA_CMP_STRIDE
    cend = cstart + NSA_CMP_LEN - 1
    sstart = np.arange(ns) * NSA_SLC_BLOCK
    ov = (cstart[:, None] <= sstart[None, :] + NSA_SLC_BLOCK - 1) & (cend[:, None] >= sstart[None, :])
    ov[nc - 1] = False
    return jnp.asarray(ov.T.astype(np.float32))


def _mod_kernel(c_ref, w_ref, b_ref, o_ref):
    c = c_ref[...]
    act = c * jax.nn.sigmoid(c)
    o_ref[0] = jnp.dot(act, w_ref[0], precision=HI, preferred_element_type=F32) + b_ref[0]


def ada_modulation(c, ada_w, ada_b):
    nl, d, wide = ada_w.shape
    b = c.shape[0]
    tn = D_MODEL
    return pl.pallas_call(
        _mod_kernel,
        grid=(nl, wide // tn),
        in_specs=[pl.BlockSpec((b, d), lambda l, j: (0, 0)),
                  pl.BlockSpec((1, d, tn), lambda l, j: (l, 0, j)),
                  pl.BlockSpec((1, 1, tn), lambda l, j: (l, 0, j))],
        out_specs=pl.BlockSpec((1, b, tn), lambda l, j: (l, 0, j)),
        out_shape=jax.ShapeDtypeStruct((nl, b, wide), F32),
        compiler_params=_cparams(("arbitrary", "arbitrary")),
        name="ada_modulation",
    )(c, ada_w, ada_b.reshape(nl, 1, wide))


def _rms(x, g):
    return x * lax.rsqrt(jnp.mean(x * x, axis=-1, keepdims=True) + RMS_EPS) * g


_GATE_PAD = LANES
_COLS = {}
_off = 0
for _name, _w in (("qa", MOBA_W), ("ka", MOBA_W), ("va", MOBA_W), ("qb", NSA_W), ("kvb", NSA_W),
                  ("gb", _GATE_PAD), ("qc", SB_W), ("kc", SB_W), ("vc", SB_W)):
    _COLS[_name] = (_off, _w)
    _off += _w
IN_W_PACKED = _off
PROJ_TM = 256


def _rope_block(p, cs, sm, sp):
    return p * cs + pltpu.roll(p, LANES - ROPE_DIM // 2, 1) * sm + pltpu.roll(p, ROPE_DIM // 2, 1) * sp


def _in_proj_kernel(x_ref, sc_ref, sh_ref, g_ref, w_ref, cqk_ref, mqk_ref, pqk_ref,
                    ckv_ref, mkv_ref, pkv_ref,
                    qa_ref, ka_ref, va_ref, qb_ref, kvb_ref, gb_ref, qc_ref, kc_ref, vc_ref):
    hm = _rms(x_ref[0], g_ref[...]) * (1.0 + sc_ref[0]) + sh_ref[0]
    p = _dot(hm.astype(BF16), w_ref[...])
    qk = (cqk_ref[...], mqk_ref[...], pqk_ref[...])
    kv = (ckv_ref[...], mkv_ref[...], pkv_ref[...])

    def emit(ref, name, tabs):
        off, w = _COLS[name]
        for j in range(w // LANES):
            blk = p[:, off + j * LANES: off + (j + 1) * LANES]
            if tabs is not None:
                blk = _rope_block(blk, *tabs)
            ref[0, :, j * LANES:(j + 1) * LANES] = blk

    emit(qa_ref, "qa", qk)
    emit(ka_ref, "ka", qk)
    emit(va_ref, "va", None)
    emit(qb_ref, "qb", qk)
    emit(kvb_ref, "kvb", kv)
    emit(qc_ref, "qc", None)
    emit(kc_ref, "kc", None)
    emit(vc_ref, "vc", None)
    off, w = _COLS["gb"]
    gb_ref[0] = jax.nn.sigmoid(p[:, off:off + w])


def in_projection(x, sc, sh, g, w_packed, tabs):
    b, s, d = x.shape
    tm = PROJ_TM
    row = lambda w: pl.BlockSpec((1, tm, w), lambda bi, i: (bi, i, 0))
    vec = pl.BlockSpec((1, 1, d), lambda bi, i: (bi, 0, 0))
    tab = pl.BlockSpec((tm, LANES), lambda bi, i: (i, 0))
    names = ("qa", "ka", "va", "qb", "kvb", "gb", "qc", "kc", "vc")
    return pl.pallas_call(
        _in_proj_kernel,
        grid=(b, s // tm),
        in_specs=[row(d), vec, vec, pl.BlockSpec((1, d), lambda bi, i: (0, 0)),
                  pl.BlockSpec((d, IN_W_PACKED), lambda bi, i: (0, 0))] + [tab] * 6,
        out_specs=[row(_COLS[n][1]) for n in names],
        out_shape=[jax.ShapeDtypeStruct((b, s, _COLS[n][1]), F32) for n in names],
        compiler_params=_cparams(("arbitrary", "arbitrary")),
        name="in_projection",
    )(x, sc, sh, g, w_packed, *tabs)


def _pack_w_in(w_in):
    widths = (MOBA_W, MOBA_W, MOBA_W, NSA_W, NSA_W, 3 * NSA_HEADS, SB_W, SB_W, SB_W)
    offs = np.cumsum((0,) + widths)
    parts = []
    for i, w in enumerate(widths):
        blk = w_in[:, offs[i]:offs[i + 1]]
        if w == 3 * NSA_HEADS:
            blk = jnp.pad(blk, ((0, 0), (0, _GATE_PAD - w)))
        parts.append(blk)
    return jnp.concatenate(parts, axis=1).astype(BF16)


def _rope_tables(s):
    half = ROPE_DIM // 2
    inv_freq = ROPE_THETA ** (-jnp.arange(0, ROPE_DIM, 2, dtype=F32) / ROPE_DIM)
    ang = jnp.arange(s, dtype=F32)[:, None] * inv_freq[None, :]
    cos, sin = jnp.cos(ang), jnp.sin(ang)
    zeros = jnp.zeros((s, HEAD_DIM - ROPE_DIM), F32)
    z8 = jnp.zeros((s, half), F32)
    cs_h = jnp.concatenate([cos, cos, zeros + 1.0], axis=1)
    sm_h = jnp.concatenate([-sin, z8, zeros], axis=1)
    sp_h = jnp.concatenate([z8, sin, zeros], axis=1)
    ident = (jnp.ones((s, HEAD_DIM), F32), jnp.zeros((s, HEAD_DIM), F32), jnp.zeros((s, HEAD_DIM), F32))
    qk = tuple(jnp.concatenate([t, t], axis=1) for t in (cs_h, sm_h, sp_h))
    kv = tuple(jnp.concatenate([t, i], axis=1) for t, i in zip((cs_h, sm_h, sp_h), ident))
    return qk + kv


OUT_TM = 256
SUBLANES = 8


def _store_token_tiles(ref, val):
    rows = val.shape[0]
    for j in range(SUBLANES):
        ref[pl.ds(j, rows, stride=SUBLANES), :] = val[:, j * LANES:(j + 1) * LANES]


def _load_token_tiles(ref):
    rows = ref.shape[0] // SUBLANES
    return jnp.concatenate([ref[pl.ds(j, rows, stride=SUBLANES), :] for j in range(SUBLANES)], axis=1)


def _out_proj_kernel(oa_ref, ob_ref, oc_ref, x_ref, gg_ref, w_ref, g1_ref, n1_ref, n2_ref,
                     sc_ref, sh_ref, rw_ref, rb_ref, x1_ref, hf_ref, lg_ref):
    gg = gg_ref[...]
    y = jnp.concatenate([_rms(oa_ref[0], gg[:, :MOBA_W]),
                         _rms(ob_ref[0], gg[:, MOBA_W:MOBA_W + NSA_W]),
                         _rms(oc_ref[0], gg[:, MOBA_W + NSA_W:])], axis=1)
    y = _dot(y.astype(BF16), w_ref[...])
    x1 = x_ref[0] + g1_ref[0] * _rms(y, n1_ref[...])
    x1_ref[0] = x1
    hf = _rms(x1, n2_ref[...]) * (1.0 + sc_ref[0]) + sh_ref[0]
    _store_token_tiles(hf_ref.at[0], hf)
    lg_ref[...] = lax.dot_general(rw_ref[...], hf, (((1,), (1,)), ((), ())),
                                  precision=HI, preferred_element_type=F32) + rb_ref[...]


def out_projection(oa, ob, oc, x, grp_g, w_out, g1, n1, n2, sc2, sh2, rw_t, rb):
    b, s, d = x.shape
    assert d == SUBLANES * LANES
    ne = rw_t.shape[0]
    tm = OUT_TM
    steps = s // tm
    row = lambda w: pl.BlockSpec((1, tm, w), lambda bi, i: (bi, i, 0))
    vec = pl.BlockSpec((1, 1, d), lambda bi, i: (bi, 0, 0))
    cst = lambda r, w: pl.BlockSpec((r, w), lambda bi, i: (0, 0))
    return pl.pallas_call(
        _out_proj_kernel,
        grid=(b, steps),
        in_specs=[row(MOBA_W), row(NSA_W), row(SB_W), row(d), cst(1, d), cst(d, d), vec,
                  cst(1, d), cst(1, d), vec, vec, cst(ne, d), cst(ne, 1)],
        out_specs=[row(d), pl.BlockSpec((1, tm * SUBLANES, LANES), lambda bi, i: (bi, i, 0)),
                   pl.BlockSpec((ne, tm), lambda bi, i: (0, bi * steps + i))],
        out_shape=[jax.ShapeDtypeStruct((b, s, d), F32),
                   jax.ShapeDtypeStruct((b, s * SUBLANES, LANES), F32),
                   jax.ShapeDtypeStruct((ne, b * s), F32)],
        compiler_params=_cparams(("arbitrary", "arbitrary")),
        name="out_projection",
    )(oa, ob, oc, x, grp_g, w_out, g1, n1, n2, sc2, sh2, rw_t, rb)


ROUTE_TT = 512


def _router_kernel(lg_ref, idx_ref, gate_ref, rank_ref, cnt_ref):
    ne, tt = lg_ref.shape

    @pl.when(pl.program_id(0) == 0)
    def _():
        cnt_ref[...] = jnp.zeros(cnt_ref.shape, cnt_ref.dtype)

    v = lg_ref[...]
    erow = lax.broadcasted_iota(jnp.int32, (ne, tt), 0)
    vals, hots, firsts = [], [], []
    for _ in range(TOP_K):
        m = jnp.max(v, axis=0, keepdims=True)
        first = jnp.min(jnp.where(v == m, erow, ne), axis=0, keepdims=True)
        hot = erow == first
        v = jnp.where(hot, -jnp.inf, v)
        vals.append(m)
        hots.append(hot)
        firsts.append(first)
    exps = [jnp.exp(val - vals[0]) for val in vals]
    den = exps[0] + exps[1] + exps[2] + exps[3]
    gate_ref[...] = jnp.concatenate([e / den for e in exps], axis=0)
    idx_ref[...] = jnp.concatenate(firsts, axis=0)

    cnt = jnp.zeros((ne, tt), F32)
    for hot in hots:
        cnt = cnt + jnp.where(hot, 1.0, 0.0)
    before = (lax.broadcasted_iota(jnp.int32, (tt, tt), 0)
              < lax.broadcasted_iota(jnp.int32, (tt, tt), 1)).astype(BF16)
    excl = _dot(cnt.astype(BF16), before) + cnt_ref[:, 0:1]
    ranks = [jnp.sum(jnp.where(hot, excl, 0.0), axis=0, keepdims=True) for hot in hots]
    rank_ref[...] = jnp.concatenate(ranks, axis=0).astype(jnp.int32)
    cnt_ref[...] = cnt_ref[...] + jnp.sum(cnt, axis=1, keepdims=True)


def moe_router(logits_t):
    ne, n = logits_t.shape
    tt = ROUTE_TT
    slot = pl.BlockSpec((TOP_K, tt), lambda i: (0, i))
    return pl.pallas_call(
        _router_kernel,
        grid=(n // tt,),
        in_specs=[pl.BlockSpec((ne, tt), lambda i: (0, i))],
        out_specs=[slot, slot, slot, pl.BlockSpec((ne, LANES), lambda i: (0, 0))],
        out_shape=[jax.ShapeDtypeStruct((TOP_K, n), jnp.int32), jax.ShapeDtypeStruct((TOP_K, n), F32),
                   jax.ShapeDtypeStruct((TOP_K, n), jnp.int32), jax.ShapeDtypeStruct((ne, LANES), F32)],
        compiler_params=_cparams(("arbitrary",)),
        name="moe_router",
    )(logits_t)


MOE_TM = 512


def _dispatch_kernel(dest_ref, zrow_ref, hf_ref, xs_ref, zbuf, sem, zsem):
    tt = hf_ref.shape[0] // SUBLANES
    n = dest_ref.shape[0] // TOP_K
    base = pl.program_id(0) * tt

    def tile(ref, row):
        return ref.at[pl.ds(pl.multiple_of(row * SUBLANES, SUBLANES), SUBLANES), :]

    @pl.when(pl.program_id(0) == 0)
    def _():
        zbuf[...] = jnp.zeros(zbuf.shape, zbuf.dtype)

        def fill(j):
            first = pl.multiple_of(jnp.maximum(zrow_ref[j], 0) * SUBLANES, MOE_TM * SUBLANES)
            return pltpu.make_async_copy(zbuf, xs_ref.at[pl.ds(first, MOE_TM * SUBLANES), :], zsem)

        for j in range(zrow_ref.shape[0]):
            pl.when(zrow_ref[j] >= 0)(fill(j).start)
        for j in range(zrow_ref.shape[0]):
            pl.when(zrow_ref[j] >= 0)(fill(j).wait)

    def issue(t, carry):
        for k in range(TOP_K):
            pltpu.make_async_copy(tile(hf_ref, t), tile(xs_ref, dest_ref[k * n + base + t]), sem).start()
        return carry

    lax.fori_loop(0, tt, issue, 0, unroll=8)
    for _ in range(TOP_K):
        pltpu.make_async_copy(hf_ref, xs_ref.at[pl.ds(0, tt * SUBLANES), :], sem).wait()


DISPATCH_TT = 256


def moe_dispatch(dest_flat, zero_rows, hf, n_rows):
    n = hf.shape[0] // SUBLANES
    tt = DISPATCH_TT
    grid_spec = pltpu.PrefetchScalarGridSpec(
        num_scalar_prefetch=2,
        grid=(n // tt,),
        in_specs=[pl.BlockSpec((tt * SUBLANES, LANES), lambda i, dest, zrow: (i, 0))],
        out_specs=pl.BlockSpec(memory_space=pl.ANY),
        scratch_shapes=[pltpu.VMEM((MOE_TM * SUBLANES, LANES), F32), pltpu.SemaphoreType.DMA,
                        pltpu.SemaphoreType.DMA],
    )
    return pl.pallas_call(
        _dispatch_kernel,
        grid_spec=grid_spec,
        out_shape=jax.ShapeDtypeStruct((n_rows * SUBLANES, LANES), F32),
        compiler_params=_cparams(("arbitrary",)),
        name="moe_dispatch",
    )(dest_flat, zero_rows, hf)


def _expert_kernel(be_ref, nu_ref, xs_ref, wgu_ref, bgu_ref, wdn_ref, bdn_ref, y_ref,
                   wgu_bf, wdn_bf):
    i = pl.program_id(0)
    used = i < nu_ref[0]

    @pl.when(used & ((i == 0) | (be_ref[i] != be_ref[jnp.maximum(i - 1, 0)])))
    def _():
        wgu_bf[...] = wgu_ref[0].astype(BF16)
        wdn_bf[...] = wdn_ref[0].astype(BF16)

    @pl.when(used)
    def _():
        hgu = _dot(_load_token_tiles(xs_ref).astype(BF16), wgu_bf[...]) + bgu_ref[0]
        de = hgu.shape[1] // 2
        glu = jnp.minimum(hgu[:, :de], SWIGLU_LIMIT)
        lin = jnp.clip(hgu[:, de:], -SWIGLU_LIMIT, SWIGLU_LIMIT)
        act = glu * jax.nn.sigmoid(SWIGLU_ALPHA * glu) * (lin + 1.0)
        _store_token_tiles(y_ref, _dot(act.astype(BF16), wdn_bf[...]) + bdn_ref[0])

    @pl.when(jnp.logical_not(used))
    def _():
        y_ref[...] = jnp.zeros(y_ref.shape, y_ref.dtype)


EXPERT_VMEM_LIMIT = 56 * 1024 * 1024


def expert_ffn(layer, blk_exp, n_used, xs, w_gu, b_gu, w_dn, b_dn):
    r = xs.shape[0] // SUBLANES
    d = SUBLANES * LANES
    nl, ne, _, wide = w_gu.shape
    w_gu = w_gu.reshape(nl * ne, d, wide)
    w_dn = w_dn.reshape(nl * ne, wide // 2, d)
    b_gu = b_gu.reshape(nl * ne, wide)
    b_dn = b_dn.reshape(nl * ne, d)
    ne, base = nl * ne, layer * ne
    tm = MOE_TM
    grid_spec = pltpu.PrefetchScalarGridSpec(
        num_scalar_prefetch=2,
        grid=(r // tm,),
        in_specs=[pl.BlockSpec((tm * SUBLANES, LANES), lambda i, be, nu: (i, 0)),
                  pl.BlockSpec((1, d, wide), lambda i, be, nu: (base + be[i], 0, 0)),
                  pl.BlockSpec((1, 1, wide), lambda i, be, nu: (base + be[i], 0, 0)),
                  pl.BlockSpec((1, wide // 2, d), lambda i, be, nu: (base + be[i], 0, 0)),
                  pl.BlockSpec((1, 1, d), lambda i, be, nu: (base + be[i], 0, 0))],
        out_specs=pl.BlockSpec((tm * SUBLANES, LANES), lambda i, be, nu: (i, 0)),
        scratch_shapes=[pltpu.VMEM((d, wide), BF16), pltpu.VMEM((wide // 2, d), BF16)],
    )
    return pl.pallas_call(
        _expert_kernel,
        grid_spec=grid_spec,
        out_shape=jax.ShapeDtypeStruct((r * SUBLANES, LANES), F32),
        compiler_params=pltpu.CompilerParams(dimension_semantics=("arbitrary",),
                                             vmem_limit_bytes=EXPERT_VMEM_LIMIT),
        name="expert_ffn",
    )(blk_exp, n_used, xs, w_gu, b_gu.reshape(ne, 1, wide), w_dn, b_dn.reshape(ne, 1, d))


COMBINE_TT = 128


def _combine_kernel(dest_ref, gate_ref, x_ref, g2_ref, n3_ref, y_ref, o_ref, ybuf, sems):
    tt = x_ref.shape[0]
    n = dest_ref.shape[0] // TOP_K
    i = pl.program_id(0)
    steps = pl.num_programs(0)

    def gather(step, slot):
        base = step * tt

        def issue(t, carry):
            dst = pl.ds(pl.multiple_of(t * SUBLANES, SUBLANES), SUBLANES)
            for k in range(TOP_K):
                row = dest_ref[k * n + base + t]
                src = pl.ds(pl.multiple_of(row * SUBLANES, SUBLANES), SUBLANES)
                pltpu.make_async_copy(y_ref.at[src, :], ybuf.at[slot, k, dst, :], sems.at[slot]).start()
            return carry

        lax.fori_loop(0, tt, issue, 0, unroll=8)

    @pl.when(i == 0)
    def _():
        gather(0, 0)

    @pl.when(i + 1 < steps)
    def _():
        gather(i + 1, (i + 1) % 2)

    slot = i % 2
    for k in range(TOP_K):
        pltpu.make_async_copy(y_ref.at[pl.ds(0, tt * SUBLANES), :], ybuf.at[slot, k], sems.at[slot]).wait()
    gate = gate_ref[...]
    y = gate[:, 0:1] * _load_token_tiles(ybuf.at[slot, 0])
    for k in range(1, TOP_K):
        y = y + gate[:, k:k + 1] * _load_token_tiles(ybuf.at[slot, k])
    o_ref[...] = x_ref[...] + g2_ref[0] * _rms(y, n3_ref[...])


def moe_combine(dest_flat, gate, x, g2, n3, y_rows):
    b, s, d = x.shape
    n = b * s
    tt = COMBINE_TT
    per_batch = s // tt
    grid_spec = pltpu.PrefetchScalarGridSpec(
        num_scalar_prefetch=1,
        grid=(n // tt,),
        in_specs=[pl.BlockSpec((tt, LANES), lambda i, dest: (i, 0)),
                  pl.BlockSpec((tt, d), lambda i, dest: (i, 0)),
                  pl.BlockSpec((1, 1, d), lambda i, dest: (i // per_batch, 0, 0)),
                  pl.BlockSpec((1, d), lambda i, dest: (0, 0)),
                  pl.BlockSpec(memory_space=pl.ANY)],
        out_specs=pl.BlockSpec((tt, d), lambda i, dest: (i, 0)),
        scratch_shapes=[pltpu.VMEM((2, TOP_K, tt * SUBLANES, LANES), F32),
                        pltpu.SemaphoreType.DMA((2,))],
    )
    out = pl.pallas_call(
        _combine_kernel,
        grid_spec=grid_spec,
        out_shape=jax.ShapeDtypeStruct((n, d), F32),
        compiler_params=_cparams(("arbitrary",)),
        name="moe_combine",
    )(dest_flat, gate, x.reshape(n, d), g2, n3, y_rows)
    return out.reshape(b, s, d)


def _dispatch_plan(idx, rank, counts):
    n = idx.shape[1]
    tm = MOE_TM
    padded = (counts + tm - 1) // tm * tm
    pad_end = jnp.cumsum(padded)
    start = pad_end - padded
    experts = jnp.arange(N_EXPERTS, dtype=jnp.int32)
    dest = rank + jnp.sum(jnp.where(idx[..., None] == experts, start, 0), axis=-1)
    n_rows = -(-(n * TOP_K + N_EXPERTS * (tm - 1)) // tm) * tm
    n_blk = n_rows // tm
    blk_start = jnp.arange(n_blk, dtype=jnp.int32) * tm
    blk_exp = jnp.minimum(jnp.sum((pad_end[None, :] <= blk_start[:, None]).astype(jnp.int32), axis=1),
                          N_EXPERTS - 1)
    n_used = (pad_end[-1] // tm).astype(jnp.int32).reshape(1)
    tail = pad_end[-1] + experts * tm
    zero_rows = jnp.concatenate([jnp.where(padded > 0, pad_end - tm, -1),
                                 jnp.where(tail < n_rows, tail, -1)]).astype(jnp.int32)
    return dest.reshape(-1), blk_exp, n_used, zero_rows, n_rows


def _layer(layer, x, mod, norm_g, w_in, cmp_pe, cmp_w1, cmp_w2, grp_g, w_out, router_w, router_b,
           w_gu, b_gu, w_dn, b_dn, tabs, overlap):
    b, s, d = x.shape
    sh1, sc1, g1, sh2, sc2, g2 = (m.reshape(b, 1, d) for m in jnp.split(mod, 6, axis=-1))
    ng = norm_g.reshape(4, 1, d)

    qa, ka, va, qb, kvb, gb, qc, kc, vc = in_projection(x, sc1, sh1, ng[0], _pack_w_in(w_in), tabs)
    oa = moba_attention(qa, ka, va)
    nc = s // NSA_CMP_STRIDE
    kv16 = jnp.stack([kvb[:, :, :HEAD_DIM], kvb[:, :, HEAD_DIM:2 * HEAD_DIM]], axis=1)
    kv16 = kv16.reshape(b, 2, nc, NSA_CMP_STRIDE * HEAD_DIM)
    cmp_kv = nsa_compress(kv16, cmp_pe.reshape(2, 1, NSA_CMP_LEN * HEAD_DIM), cmp_w1, cmp_w2)
    ob = nsa_attention(qb, kvb, gb, cmp_kv, overlap)
    oc = sb_attention(qc, kc, vc)

    x1, hf, logits_t = out_projection(oa, ob, oc, x, grp_g.reshape(1, d), w_out.astype(BF16), g1,
                                      ng[1], ng[2], sc2, sh2, router_w.T, router_b.reshape(-1, 1))

    n = b * s
    idx, gate, rank, counts = moe_router(logits_t)
    dest, blk_exp, n_used, zero_rows, n_rows = _dispatch_plan(idx, rank, counts[:, 0].astype(jnp.int32))
    xs = moe_dispatch(dest, zero_rows, hf.reshape(n * SUBLANES, LANES), n_rows)
    y_rows = expert_ffn(layer, blk_exp, n_used, xs, w_gu, b_gu, w_dn, b_dn)
    gate = jnp.pad(gate.T, ((0, 0), (0, LANES - TOP_K)))
    return moe_combine(dest, gate, x1, g2, ng[3], y_rows)


def kernel(x, c, ada_w, ada_b, norm_g, w_in, nsa_cmp_pe, nsa_cmp_w1, nsa_cmp_w2, mix_out_g,
           w_out, router_w, router_b, exp_w_gu, exp_b_gu, exp_w_dn, exp_b_dn):
    s = x.shape[1]
    tabs = _rope_tables(s)
    overlap = _nsa_overlap(s)
    mod = ada_modulation(c, ada_w, ada_b)
    for l in range(ada_w.shape[0]):
        x = _layer(l, x, mod[l], norm_g[l], w_in[l], nsa_cmp_pe[l], nsa_cmp_w1[l], nsa_cmp_w2[l],
                   mix_out_g[l], w_out[l], router_w[l], router_b[l], exp_w_gu, exp_b_gu,
                   exp_w_dn, exp_b_dn, tabs, overlap)
    return x
```

```python
import functools

import numpy as np
import jax
import jax.numpy as jnp
from jax import lax
from jax.experimental import pallas as pl
from jax.experimental.pallas import tpu as pltpu

F32 = jnp.float32
BF16 = jnp.bfloat16
HI = lax.Precision.HIGHEST

D_MODEL = 1024
N_HEADS = 16
HEAD_DIM = 64
MOBA_HEADS = 4
NSA_HEADS = 6
SB_HEADS = 6
MOBA_W = MOBA_HEADS * HEAD_DIM
NSA_W = NSA_HEADS * HEAD_DIM
SB_W = SB_HEADS * HEAD_DIM
ROPE_DIM = 16
ROPE_THETA = 500000.0
MOBA_BLOCK = 256
MOBA_TOPK = 3
NSA_CMP_LEN = 32
NSA_CMP_STRIDE = 16
NSA_SLC_BLOCK = 64
NSA_SLC_TOPK = 16
NSA_WINDOW = 512
N_EXPERTS = 32
TOP_K = 4
SWIGLU_LIMIT = 7.0
SWIGLU_ALPHA = 1.702
RMS_EPS = 1e-6
NEG_INF = -1e30
SEL_FORCE = 1e4
SCALE = HEAD_DIM ** -0.5
LOG2E = 1.4426950408889634

LANES = 128
VMEM_LIMIT = 48 * 1024 * 1024


def _cparams(sem):
    return pltpu.CompilerParams(dimension_semantics=sem, vmem_limit_bytes=VMEM_LIMIT)


def _dot(a, b):
    return jnp.dot(a, b, preferred_element_type=F32)


def _dot_nt(a, b):
    return lax.dot_general(a, b, (((1,), (1,)), ((), ())), preferred_element_type=F32)


SB_TILE = 256


SB_GROUP_W = SB_W
SB_LAG = 2


def _sb_kernel(q_ref, k_ref, v_ref, o_ref, ks_ref, vs_ref, run_ref, acc_ref):
    T = SB_TILE
    H = ks_ref.shape[0]
    qi = pl.program_id(2)

    @pl.when(qi == 0)
    def _():
        zeros = jnp.zeros((ks_ref.shape[1], HEAD_DIM), F32)
        for h in range(H):
            ks_ref[h] = k_ref[0, :, h * HEAD_DIM:(h + 1) * HEAD_DIM].astype(BF16)
            vh = v_ref[0, :, h * HEAD_DIM:(h + 1) * HEAD_DIM]
            vs_ref[h] = jnp.concatenate([vh, zeros] if h % 2 == 0 else [zeros, vh], axis=1).astype(BF16)

    run_ref[...] = jnp.zeros(run_ref.shape, F32)
    acc_ref[...] = jnp.zeros(acc_ref.shape, F32)

    row = lax.broadcasted_iota(jnp.int32, (T, T), 0)
    col = lax.broadcasted_iota(jnp.int32, (T, T), 1)
    incl = (row >= col).astype(BF16)
    sum_rhs = jnp.concatenate([incl, incl], axis=0)
    diag_mask = col < row
    qs = [(q_ref[0, :, h * HEAD_DIM:(h + 1) * HEAD_DIM] * (SCALE * LOG2E)).astype(BF16)
          for h in range(H)]
    sign = jnp.uint32(0x80000000)

    def scores(h, j):
        return _dot_nt(qs[h], ks_ref[h, pl.ds(pl.multiple_of(j * T, T), T), :])

    def neg_log2_keep(z, masked):
        neg_abs = lax.bitcast_convert_type(lax.bitcast_convert_type(z, jnp.uint32) | sign, F32)
        nlk = jnp.maximum(z, 0.0) + jnp.log2(1.0 + jnp.exp2(neg_abs))
        return jnp.where(diag_mask, nlk, 0.0) if masked else nlk

    def suffix_sums(nlk):
        hi = nlk.astype(BF16)
        lo = (nlk - hi.astype(F32)).astype(BF16)
        return _dot(jnp.concatenate([hi, lo], axis=1), sum_rhs)

    def weights(z, sums, run, masked):
        x = z - sums - run
        if masked:
            x = jnp.where(diag_mask, x, NEG_INF)
        return jnp.exp2(x.astype(BF16))

    def pv(h, p, j):
        return _dot(p, vs_ref[h, pl.ds(pl.multiple_of(j * T, T), T), :])

    def key_tile(j, masked):
        zs, sums = {0: scores(0, j)}, {}
        for t in range(H + SB_LAG):
            if t + 1 < H:
                zs[t + 1] = scores(t + 1, j)
            if t < H:
                sums[t] = suffix_sums(neg_log2_keep(zs[t], masked))
            h = t - SB_LAG
            if h >= 0:
                run = run_ref[h]
                p = weights(zs.pop(h), sums[h], run, masked)
                run_ref[h] = run + sums.pop(h)[:, 0:1]
                acc_ref[h // 2] = acc_ref[h // 2] + pv(h, p, j)

    key_tile(qi, True)

    def past(i, carry):
        key_tile(qi - 1 - i, False)
        return carry

    lax.fori_loop(0, qi, past, 0)
    for g in range(H // 2):
        o_ref[0, :, g * LANES:(g + 1) * LANES] = acc_ref[g]


def sb_attention(q, k, v):
    b, s, w = q.shape
    T = SB_TILE
    gw = SB_GROUP_W
    nh = gw // HEAD_DIM
    qspec = pl.BlockSpec((1, T, gw), lambda bi, p, i: (bi, i, p))
    kvspec = pl.BlockSpec((1, s, gw), lambda bi, p, i: (bi, 0, p))
    return pl.pallas_call(
        _sb_kernel,
        grid=(b, w // gw, s // T),
        in_specs=[qspec, kvspec, kvspec],
        out_specs=qspec,
        out_shape=jax.ShapeDtypeStruct((b, s, w), F32),
        scratch_shapes=[pltpu.VMEM((nh, s, HEAD_DIM), BF16), pltpu.VMEM((nh, s, LANES), BF16),
                        pltpu.VMEM((nh, T, 1), F32), pltpu.VMEM((nh // 2, T, LANES), F32)],
        compiler_params=_cparams(("arbitrary", "arbitrary", "arbitrary")),
        name="sb_attention",
    )(q, k, v)


def _rank_before_t(vals, n):
    idx = lax.broadcasted_iota(jnp.int32, vals.shape, 0)
    rank = jnp.zeros(vals.shape, F32)
    for j2 in range(n):
        other = vals[j2:j2 + 1, :]
        ahead = (other > vals) | ((other == vals) & (idx > j2))
        rank = rank + jnp.where(ahead, 1.0, 0.0)
    return rank


MOBA_KT = 2 * MOBA_BLOCK
MOBA_GROUP_W = MOBA_W


def _moba_kernel(q_ref, k_ref, v_ref, o_ref, ks_ref, vs_ref, km_ref):
    T, KT = MOBA_BLOCK, MOBA_KT
    H = ks_ref.shape[0]
    nb = km_ref.shape[1]
    s_len = ks_ref.shape[1]
    qi = pl.program_id(2)

    @pl.when(qi == 0)
    def _():
        key_blk = lax.broadcasted_iota(jnp.int32, (s_len, HEAD_DIM), 0) // T
        onehot = (lax.broadcasted_iota(jnp.int32, (s_len, HEAD_DIM), 1) == key_blk).astype(F32)
        ones_col = (lax.broadcasted_iota(jnp.int32, (s_len, HEAD_DIM), 1) == 0).astype(F32)
        for h in range(H):
            kh = k_ref[0, :, h * HEAD_DIM:(h + 1) * HEAD_DIM]
            ks_ref[h] = jnp.concatenate([kh, onehot], axis=1).astype(BF16)
            vs_ref[h] = jnp.concatenate([v_ref[0, :, h * HEAD_DIM:(h + 1) * HEAD_DIM], ones_col],
                                        axis=1).astype(BF16)
            km_ref[h] = jnp.mean(kh.reshape(nb, T, HEAD_DIM), axis=1)

    row = lax.broadcasted_iota(jnp.int32, (T, T), 0)
    col = lax.broadcasted_iota(jnp.int32, (T, T), 1)
    causal = col <= row
    blk_t = lax.broadcasted_iota(jnp.int32, (nb, T), 0)
    start = pl.multiple_of(qi * T, T)

    qfs = [q_ref[0, :, h * HEAD_DIM:(h + 1) * HEAD_DIM] for h in range(H)]
    qss = [qf * SCALE for qf in qfs]
    s_own = [_dot_nt(qss[h].astype(BF16), ks_ref[h, pl.ds(start, T), 0:HEAD_DIM]) for h in range(H)]
    q_aug, state = [], []
    for h in range(H):
        gate = lax.dot_general(km_ref[h], qfs[h], (((1,), (1,)), ((), ())),
                               precision=HI, preferred_element_type=F32)
        gate = jnp.where(blk_t < qi, gate, NEG_INF)
        sel = (_rank_before_t(gate, nb) < float(MOBA_TOPK)) & (gate > 0.5 * NEG_INF)
        selb = jnp.where(sel, 0.0, NEG_INF)
        selb = jnp.concatenate([selb, jnp.full((LANES - nb, T), NEG_INF, F32)], axis=0).T
        q_aug.append(jnp.concatenate([qss[h], selb[:, :HEAD_DIM]], axis=1).astype(BF16))

        s = jnp.where(causal, s_own[h], NEG_INF)
        m = jnp.max(s, axis=1, keepdims=True)
        p = jnp.exp((s - m).astype(BF16))
        state += [m, _dot(p, vs_ref[h, pl.ds(start, T), :])]

    def past(i, carry):
        st = pl.multiple_of(i * KT, KT)
        ss = [_dot_nt(q_aug[h], ks_ref[h, pl.ds(st, KT), :]) for h in range(H)]
        out = []
        for h in range(H):
            m, acc = carry[2 * h:2 * h + 2]
            m_new = jnp.maximum(m, jnp.max(ss[h], axis=1, keepdims=True))
            alpha = jnp.exp(m - m_new)
            p = jnp.exp((ss[h] - m_new).astype(BF16))
            out += [m_new, alpha * acc + _dot(p, vs_ref[h, pl.ds(st, KT), :])]
        return tuple(out)

    per = KT // T
    state = lax.fori_loop(0, (qi + per - 1) // per, past, tuple(state))
    outs = [state[2 * h + 1][:, :HEAD_DIM] / jnp.maximum(state[2 * h + 1][:, HEAD_DIM:HEAD_DIM + 1], 1e-30)
            for h in range(H)]
    o_ref[0] = jnp.concatenate(outs, axis=1)


def moba_attention(q, k, v):
    b, s, w = q.shape
    T = MOBA_BLOCK
    nb = s // T
    assert nb <= HEAD_DIM and s % MOBA_KT == 0
    gw = MOBA_GROUP_W
    nh = gw // HEAD_DIM
    qspec = pl.BlockSpec((1, T, gw), lambda bi, p, i: (bi, i, p))
    kvspec = pl.BlockSpec((1, s, gw), lambda bi, p, i: (bi, 0, p))
    return pl.pallas_call(
        _moba_kernel,
        grid=(b, w // gw, nb),
        in_specs=[qspec, kvspec, kvspec],
        out_specs=qspec,
        out_shape=jax.ShapeDtypeStruct((b, s, w), F32),
        scratch_shapes=[pltpu.VMEM((nh, s, LANES), BF16), pltpu.VMEM((nh, s, LANES), BF16),
                        pltpu.VMEM((nh, nb, HEAD_DIM), F32)],
        compiler_params=_cparams(("arbitrary", "arbitrary", "arbitrary")),
        name="moba_attention",
    )(q, k, v)


def _nsa_compress_kernel(x_ref, pe_ref, w1_ref, w2_ref, o_ref):
    nc = x_ref.shape[2]
    half = NSA_CMP_STRIDE * HEAD_DIM
    x = x_ref[0, 0]
    w1 = w1_ref[0]
    first = jnp.dot(x, w1[:half], precision=HI, preferred_element_type=F32)
    second = jnp.dot(x, w1[half:], precision=HI, preferred_element_type=F32)
    pe = jnp.broadcast_to(pe_ref[0], (8, 2 * half))
    peb = jnp.dot(pe, w1, precision=HI, preferred_element_type=F32)[0:1]
    pre = first + pltpu.roll(second, nc - 1, 0) + peb
    hid = pre * jax.nn.sigmoid(pre)
    o_ref[0, 0] = jnp.dot(hid, w2_ref[0], precision=HI, preferred_element_type=F32)


def nsa_compress(kv16, pe, w1, w2):
    b, _, nc, wide = kv16.shape
    return pl.pallas_call(
        _nsa_compress_kernel,
        grid=(b, 2),
        in_specs=[pl.BlockSpec((1, 1, nc, wide), lambda bi, i: (bi, i, 0, 0)),
                  pl.BlockSpec((1, 1, 2 * wide), lambda bi, i: (i, 0, 0)),
                  pl.BlockSpec((1, 2 * wide, HEAD_DIM), lambda bi, i: (i, 0, 0)),
                  pl.BlockSpec((1, HEAD_DIM, HEAD_DIM), lambda bi, i: (i, 0, 0))],
        out_specs=pl.BlockSpec((1, 1, nc, HEAD_DIM), lambda bi, i: (bi, i, 0, 0)),
        out_shape=jax.ShapeDtypeStruct((b, 2, nc, HEAD_DIM), F32),
        compiler_params=_cparams(("arbitrary", "arbitrary")),
        name="nsa_compress",
    )(kv16, pe, w1, w2)


NSA_TQ = 256
NSA_KT = 512
NSA_SPAN = NSA_WINDOW + NSA_TQ
NSA_CHAIN_HEADS = 2
NSA_WINDOW_HEADS = 2


def _softmax_rows(s, mask):
    s = jnp.where(mask, s, NEG_INF)
    m = jnp.max(s, axis=-1, keepdims=True)
    e = jnp.where(mask, jnp.exp(s - m), 0.0)
    return e / jnp.maximum(jnp.sum(e, axis=-1, keepdims=True), 1e-30)


def _split_bf16(x):
    hi = x.astype(BF16)
    return hi, (x - hi.astype(F32)).astype(BF16)


def _nsa_kernel(q_ref, cmp_ref, slc_ref, win_ref, g_ref, ov_ref, o_ref,
                ksl_ref, vsl_ref, kw_ref, vw_ref, kc_ref):
    TQ, KT, H = NSA_TQ, NSA_KT, NSA_HEADS
    ns, nc = ov_ref.shape
    s_len = ksl_ref.shape[0]
    qi = pl.program_id(1)
    t0 = qi * TQ

    @pl.when(qi == 0)
    def _():
        lane = lax.broadcasted_iota(jnp.int32, (s_len, HEAD_DIM), 1)
        key_blk = lax.broadcasted_iota(jnp.int32, (s_len, HEAD_DIM), 0) // NSA_SLC_BLOCK
        onehot = (lane == key_blk).astype(F32)
        ones_col = (lane == 0).astype(F32)
        ksl_ref[...] = jnp.concatenate([slc_ref[0, :, :HEAD_DIM], onehot], axis=1).astype(BF16)
        vsl_ref[...] = jnp.concatenate([slc_ref[0, :, HEAD_DIM:], ones_col], axis=1).astype(BF16)
        kw_ref[...] = win_ref[0, :, :HEAD_DIM].astype(BF16)
        vw_ref[...] = jnp.concatenate([win_ref[0, :, HEAD_DIM:], ones_col], axis=1).astype(BF16)
        k_hi, k_lo = _split_bf16(cmp_ref[0, 0])
        kc_ref[...] = jnp.concatenate([k_hi, k_hi, k_lo], axis=1)

    qf = jnp.concatenate([q_ref[0, :, h * HEAD_DIM:(h + 1) * HEAD_DIM] for h in range(H)],
                         axis=0) * SCALE
    q, q_lo = _split_bf16(qf)

    def normalized(acc):
        return acc[:, :HEAD_DIM] / jnp.maximum(acc[:, HEAD_DIM:HEAD_DIM + 1], 1e-30)

    w0 = pl.multiple_of(jnp.maximum(t0 - NSA_WINDOW, 0), TQ)
    wpos = w0 + lax.broadcasted_iota(jnp.int32, (TQ, NSA_SPAN), 1)
    tq_w = t0 + lax.broadcasted_iota(jnp.int32, (TQ, NSA_SPAN), 0)
    mask_w = ((wpos <= tq_w) & (wpos > tq_w - NSA_WINDOW))[None]
    o_w = []
    for c in range(H // NSA_WINDOW_HEADS):
        rows = slice(c * NSA_WINDOW_HEADS * TQ, (c + 1) * NSA_WINDOW_HEADS * TQ)
        s_w = _dot_nt(q[rows], kw_ref[pl.ds(w0, NSA_SPAN), :]).reshape(NSA_WINDOW_HEADS, TQ, NSA_SPAN)
        s_w = jnp.where(mask_w, s_w, NEG_INF)
        p_w = jnp.exp((s_w - jnp.max(s_w, axis=-1, keepdims=True)).astype(BF16))
        o_w.append(normalized(_dot(p_w.reshape(NSA_WINDOW_HEADS * TQ, NSA_SPAN),
                                   vw_ref[pl.ds(w0, NSA_SPAN), :])))
    o_w = jnp.concatenate(o_w, axis=0)

    tq_c = t0 + lax.broadcasted_iota(jnp.int32, (TQ, nc), 0)
    n_c = lax.broadcasted_iota(jnp.int32, (TQ, nc), 1)
    mask_c = (n_c * NSA_CMP_STRIDE + (NSA_CMP_LEN - 1) <= tq_c) & (n_c < nc - 1)
    s_c = _dot_nt(jnp.concatenate([q, q_lo, q], axis=1), kc_ref[...])
    p_c = _softmax_rows(s_c.reshape(H, TQ, nc), mask_c[None])
    o_c = _dot(p_c.reshape(H * TQ, nc).astype(BF16), cmp_ref[0, 1].astype(BF16))

    imp = lax.dot_general(ov_ref[...], jnp.sum(p_c, axis=0), (((1,), (1,)), ((), ())),
                          precision=HI, preferred_element_type=F32)
    tq_s = t0 + lax.broadcasted_iota(jnp.int32, (ns, TQ), 1)
    blk = lax.broadcasted_iota(jnp.int32, (ns, TQ), 0)
    own = tq_s // NSA_SLC_BLOCK
    forced = (blk == 0) | (blk == own) | (blk == own - 1)
    imp = jnp.where(forced, SEL_FORCE, imp)
    imp = jnp.where(blk <= own, imp, NEG_INF)
    sel = (_rank_before_t(imp, ns) < float(min(NSA_SLC_TOPK, ns))) & (imp > 0.5 * NEG_INF)
    selb = jnp.where(sel, 0.0, NEG_INF)
    if ns < LANES:
        selb = jnp.concatenate([selb, jnp.full((LANES - ns, TQ), NEG_INF, F32)], axis=0)
    selb = selb.T[:, :HEAD_DIM]
    q_aug = jnp.concatenate([qf, jnp.concatenate([selb] * H, axis=0)], axis=1).astype(BF16)

    jd = t0 // KT
    start = pl.multiple_of(jd * KT, KT)
    kpos = start + lax.broadcasted_iota(jnp.int32, (TQ, KT), 1)
    tq_k = t0 + lax.broadcasted_iota(jnp.int32, (TQ, KT), 0)
    HC = NSA_CHAIN_HEADS
    G, R = H // HC, HC * TQ
    qa = [q_aug[c * R:(c + 1) * R] for c in range(G)]
    causal_k = (kpos <= tq_k)[None]

    def scores(c, st):
        return _dot_nt(qa[c], ksl_ref[pl.ds(st, KT), :]).reshape(HC, TQ, KT)

    s_own = [scores(c, start) for c in range(G)]
    state = []
    for c in range(G):
        s = jnp.where(causal_k, s_own[c], NEG_INF)
        m = jnp.max(s, axis=-1, keepdims=True)
        p = jnp.exp((s - m).astype(BF16))
        state += [m, _dot(p.reshape(R, KT), vsl_ref[pl.ds(start, KT), :])]

    def past(j, carry):
        st = pl.multiple_of(j * KT, KT)
        ss = [scores(c, st) for c in range(G)]
        out = []
        for c in range(G):
            m, acc = carry[2 * c:2 * c + 2]
            m_new = jnp.maximum(m, jnp.max(ss[c], axis=-1, keepdims=True))
            alpha = jnp.exp(m - m_new)
            p = jnp.exp((ss[c] - m_new).astype(BF16))
            pv = _dot(p.reshape(R, KT), vsl_ref[pl.ds(st, KT), :])
            out += [m_new, alpha.reshape(R, 1) * acc + pv]
        return tuple(out)

    state = lax.fori_loop(0, jd, past, tuple(state))
    o_s = jnp.concatenate([normalized(state[2 * c + 1]) for c in range(G)], axis=0)

    g = g_ref[0]
    outs = []
    for h in range(H):
        rows = slice(h * TQ, (h + 1) * TQ)
        outs.append(g[:, 3 * h:3 * h + 1] * o_c[rows] + g[:, 3 * h + 1:3 * h + 2] * o_s[rows]
                    + g[:, 3 * h + 2:3 * h + 3] * o_w[rows])
    o_ref[0] = jnp.concatenate(outs, axis=1)


def nsa_attention(q, kv, gates, cmp_kv, overlap):
    b, s, w = q.shape
    ns, nc = overlap.shape
    assert ns <= HEAD_DIM
    TQ = NSA_TQ
    return pl.pallas_call(
        _nsa_kernel,
        grid=(b, s // TQ),
        in_specs=[pl.BlockSpec((1, TQ, w), lambda bi, i: (bi, i, 0)),
                  pl.BlockSpec((1, 2, nc, HEAD_DIM), lambda bi, i: (bi, 0, 0, 0)),
                  pl.BlockSpec((1, s, LANES), lambda bi, i: (bi, 0, 1)),
                  pl.BlockSpec((1, s, LANES), lambda bi, i: (bi, 0, 2)),
                  pl.BlockSpec((1, TQ, LANES), lambda bi, i: (bi, i, 0)),
                  pl.BlockSpec((ns, nc), lambda bi, i: (0, 0))],
        out_specs=pl.BlockSpec((1, TQ, w), lambda bi, i: (bi, i, 0)),
        out_shape=jax.ShapeDtypeStruct((b, s, w), F32),
        scratch_shapes=[pltpu.VMEM((s, LANES), BF16), pltpu.VMEM((s, LANES), BF16),
                        pltpu.VMEM((s, HEAD_DIM), BF16), pltpu.VMEM((s, LANES), BF16),
                        pltpu.VMEM((nc, 3 * HEAD_DIM), BF16)],
        compiler_params=_cparams(("arbitrary", "arbitrary")),
        name="nsa_attention",
    )(q, cmp_kv, kv, kv, gates, overlap)


def _nsa_overlap(s):
    nc = s // NSA_CMP_STRIDE
    ns = s // NSA_SLC_BLOCK
    cstart = np.arange(nc) * NSA_CMP_STRIDE
    cend = cstart + NSA_CMP_LEN - 1
    sstart = np.arange(ns) * NSA_SLC_BLOCK
    ov = (cstart[:, None] <= sstart[None, :] + NSA_SLC_BLOCK - 1) & (cend[:, None] >= sstart[None, :])
    ov[nc - 1] = False
    return jnp.asarray(ov.T.astype(np.float32))


def _mod_kernel(c_ref, w_ref, b_ref, o_ref):
    c = c_ref[...]
    act = c * jax.nn.sigmoid(c)
    o_ref[0] = jnp.dot(act, w_ref[0], precision=HI, preferred_element_type=F32) + b_ref[0]


def ada_modulation(c, ada_w, ada_b):
    nl, d, wide = ada_w.shape
    b = c.shape[0]
    tn = D_MODEL
    return pl.pallas_call(
        _mod_kernel,
        grid=(nl, wide // tn),
        in_specs=[pl.BlockSpec((b, d), lambda l, j: (0, 0)),
                  pl.BlockSpec((1, d, tn), lambda l, j: (l, 0, j)),
                  pl.BlockSpec((1, 1, tn), lambda l, j: (l, 0, j))],
        out_specs=pl.BlockSpec((1, b, tn), lambda l, j: (l, 0, j)),
        out_shape=jax.ShapeDtypeStruct((nl, b, wide), F32),
        compiler_params=_cparams(("arbitrary", "arbitrary")),
        name="ada_modulation",
    )(c, ada_w, ada_b.reshape(nl, 1, wide))


def _rms(x, g):
    return x * lax.rsqrt(jnp.mean(x * x, axis=-1, keepdims=True) + RMS_EPS) * g


_GATE_PAD = LANES
_COLS = {}
_off = 0
for _name, _w in (("qa", MOBA_W), ("ka", MOBA_W), ("va", MOBA_W), ("qb", NSA_W), ("kvb", NSA_W),
                  ("gb", _GATE_PAD), ("qc", SB_W), ("kc", SB_W), ("vc", SB_W)):
    _COLS[_name] = (_off, _w)
    _off += _w
IN_W_PACKED = _off
PROJ_TM = 512


def _rope_block(p, cs, sm, sp):
    return p * cs + pltpu.roll(p, LANES - ROPE_DIM // 2, 1) * sm + pltpu.roll(p, ROPE_DIM // 2, 1) * sp


def _in_proj_kernel(x_ref, sc_ref, sh_ref, g_ref, w_ref, cqk_ref, mqk_ref, pqk_ref,
                    ckv_ref, mkv_ref, pkv_ref,
                    qa_ref, ka_ref, va_ref, qb_ref, kvb_ref, gb_ref, qc_ref, kc_ref, vc_ref):
    hm = _rms(x_ref[0], g_ref[...]) * (1.0 + sc_ref[0]) + sh_ref[0]
    p = _dot(hm.astype(BF16), w_ref[...])
    qk = (cqk_ref[...], mqk_ref[...], pqk_ref[...])
    kv = (ckv_ref[...], mkv_ref[...], pkv_ref[...])

    def emit(ref, name, tabs):
        off, w = _COLS[name]
        for j in range(w // LANES):
            blk = p[:, off + j * LANES: off + (j + 1) * LANES]
            if tabs is not None:
                blk = _rope_block(blk, *tabs)
            ref[0, :, j * LANES:(j + 1) * LANES] = blk

    emit(qa_ref, "qa", qk)
    emit(ka_ref, "ka", qk)
    emit(va_ref, "va", None)
    emit(qb_ref, "qb", qk)
    emit(kvb_ref, "kvb", kv)
    emit(qc_ref, "qc", None)
    emit(kc_ref, "kc", None)
    emit(vc_ref, "vc", None)
    off, w = _COLS["gb"]
    gb_ref[0] = jax.nn.sigmoid(p[:, off:off + w])


def in_projection(x, sc, sh, g, w_packed, tabs):
    b, s, d = x.shape
    tm = PROJ_TM
    row = lambda w: pl.BlockSpec((1, tm, w), lambda bi, i: (bi, i, 0))
    vec = pl.BlockSpec((1, 1, d), lambda bi, i: (bi, 0, 0))
    tab = pl.BlockSpec((tm, LANES), lambda bi, i: (i, 0))
    names = ("qa", "ka", "va", "qb", "kvb", "gb", "qc", "kc", "vc")
    return pl.pallas_call(
        _in_proj_kernel,
        grid=(b, s // tm),
        in_specs=[row(d), vec, vec, pl.BlockSpec((1, d), lambda bi, i: (0, 0)),
                  pl.BlockSpec((d, IN_W_PACKED), lambda bi, i: (0, 0))] + [tab] * 6,
        out_specs=[row(_COLS[n][1]) for n in names],
        out_shape=[jax.ShapeDtypeStruct((b, s, _COLS[n][1]), F32) for n in names],
        compiler_params=_cparams(("arbitrary", "arbitrary")),
        name="in_projection",
    )(x, sc, sh, g, w_packed, *tabs)


def _pack_w_in(w_in):
    widths = (MOBA_W, MOBA_W, MOBA_W, NSA_W, NSA_W, 3 * NSA_HEADS, SB_W, SB_W, SB_W)
    offs = np.cumsum((0,) + widths)
    parts = []
    for i, w in enumerate(widths):
        blk = w_in[:, offs[i]:offs[i + 1]]
        if w == 3 * NSA_HEADS:
            blk = jnp.pad(blk, ((0, 0), (0, _GATE_PAD - w)))
        parts.append(blk)
    return jnp.concatenate(parts, axis=1).astype(BF16)


def _rope_tables(s):
    half = ROPE_DIM // 2
    inv_freq = ROPE_THETA ** (-jnp.arange(0, ROPE_DIM, 2, dtype=F32) / ROPE_DIM)
    ang = jnp.arange(s, dtype=F32)[:, None] * inv_freq[None, :]
    cos, sin = jnp.cos(ang), jnp.sin(ang)
    zeros = jnp.zeros((s, HEAD_DIM - ROPE_DIM), F32)
    z8 = jnp.zeros((s, half), F32)
    cs_h = jnp.concatenate([cos, cos, zeros + 1.0], axis=1)
    sm_h = jnp.concatenate([-sin, z8, zeros], axis=1)
    sp_h = jnp.concatenate([z8, sin, zeros], axis=1)
    ident = (jnp.ones((s, HEAD_DIM), F32), jnp.zeros((s, HEAD_DIM), F32), jnp.zeros((s, HEAD_DIM), F32))
    qk = tuple(jnp.concatenate([t, t], axis=1) for t in (cs_h, sm_h, sp_h))
    kv = tuple(jnp.concatenate([t, i], axis=1) for t, i in zip((cs_h, sm_h, sp_h), ident))
    return qk + kv


OUT_TM = 512
SUBLANES = 8


def _store_token_tiles(ref, val):
    rows = val.shape[0]
    for j in range(SUBLANES):
        ref[pl.ds(j, rows, stride=SUBLANES), :] = val[:, j * LANES:(j + 1) * LANES]


def _load_token_tiles(ref):
    rows = ref.shape[0] // SUBLANES
    return jnp.concatenate([ref[pl.ds(j, rows, stride=SUBLANES), :] for j in range(SUBLANES)], axis=1)


def _out_proj_kernel(oa_ref, ob_ref, oc_ref, x_ref, gg_ref, w_ref, g1_ref, n1_ref, n2_ref,
                     sc_ref, sh_ref, rw_ref, rb_ref, x1_ref, hf_ref, lg_ref):
    gg = gg_ref[...]
    y = jnp.concatenate([_rms(oa_ref[0], gg[:, :MOBA_W]),
                         _rms(ob_ref[0], gg[:, MOBA_W:MOBA_W + NSA_W]),
                         _rms(oc_ref[0], gg[:, MOBA_W + NSA_W:])], axis=1)
    y = _dot(y.astype(BF16), w_ref[...])
    x1 = x_ref[0] + g1_ref[0] * _rms(y, n1_ref[...])
    x1_ref[0] = x1
    hf = _rms(x1, n2_ref[...]) * (1.0 + sc_ref[0]) + sh_ref[0]
    _store_token_tiles(hf_ref.at[0], hf)
    h_hi, h_lo = _split_bf16(hf)
    lg_ref[...] = _dot_nt(rw_ref[...], jnp.concatenate([h_hi, h_lo, h_hi], axis=1)) + rb_ref[...]


def out_projection(oa, ob, oc, x, grp_g, w_out, g1, n1, n2, sc2, sh2, rw_t, rb):
    b, s, d = x.shape
    assert d == SUBLANES * LANES
    ne = rw_t.shape[0]
    tm = OUT_TM
    steps = s // tm
    row = lambda w: pl.BlockSpec((1, tm, w), lambda bi, i: (bi, i, 0))
    vec = pl.BlockSpec((1, 1, d), lambda bi, i: (bi, 0, 0))
    cst = lambda r, w: pl.BlockSpec((r, w), lambda bi, i: (0, 0))
    return pl.pallas_call(
        _out_proj_kernel,
        grid=(b, steps),
        in_specs=[row(MOBA_W), row(NSA_W), row(SB_W), row(d), cst(1, d), cst(d, d), vec,
                  cst(1, d), cst(1, d), vec, vec, cst(ne, 3 * d), cst(ne, 1)],
        out_specs=[row(d), pl.BlockSpec((1, tm * SUBLANES, LANES), lambda bi, i: (bi, i, 0)),
                   pl.BlockSpec((ne, tm), lambda bi, i: (0, bi * steps + i))],
        out_shape=[jax.ShapeDtypeStruct((b, s, d), F32),
                   jax.ShapeDtypeStruct((b, s * SUBLANES, LANES), F32),
                   jax.ShapeDtypeStruct((ne, b * s), F32)],
        compiler_params=_cparams(("arbitrary", "arbitrary")),
        name="out_projection",
    )(oa, ob, oc, x, grp_g, w_out, g1, n1, n2, sc2, sh2, rw_t, rb)


ROUTE_TT = 512


def _router_kernel(lg_ref, idx_ref, gate_ref, rank_ref, cnt_ref):
    ne, tt = lg_ref.shape

    @pl.when(pl.program_id(0) == 0)
    def _():
        cnt_ref[...] = jnp.zeros(cnt_ref.shape, cnt_ref.dtype)

    v = lg_ref[...]
    erow = lax.broadcasted_iota(jnp.int32, (ne, tt), 0)
    vals, hots, firsts = [], [], []
    for _ in range(TOP_K):
        m = jnp.max(v, axis=0, keepdims=True)
        first = jnp.min(jnp.where(v == m, erow, ne), axis=0, keepdims=True)
        hot = erow == first
        v = jnp.where(hot, -jnp.inf, v)
        vals.append(m)
        hots.append(hot)
        firsts.append(first)
    exps = [jnp.exp(val - vals[0]) for val in vals]
    den = exps[0] + exps[1] + exps[2] + exps[3]
    gate_ref[...] = jnp.concatenate([e / den for e in exps], axis=0)
    idx_ref[...] = jnp.concatenate(firsts, axis=0)

    cnt = jnp.zeros((ne, tt), F32)
    for hot in hots:
        cnt = cnt + jnp.where(hot, 1.0, 0.0)
    before = (lax.broadcasted_iota(jnp.int32, (tt, tt), 0)
              < lax.broadcasted_iota(jnp.int32, (tt, tt), 1)).astype(BF16)
    excl = _dot(cnt.astype(BF16), before) + cnt_ref[:, 0:1]
    ranks = [jnp.sum(jnp.where(hot, excl, 0.0), axis=0, keepdims=True) for hot in hots]
    rank_ref[...] = jnp.concatenate(ranks, axis=0).astype(jnp.int32)
    cnt_ref[...] = cnt_ref[...] + jnp.sum(cnt, axis=1, keepdims=True)


def moe_router(logits_t):
    ne, n = logits_t.shape
    tt = ROUTE_TT
    slot = pl.BlockSpec((TOP_K, tt), lambda i: (0, i))
    return pl.pallas_call(
        _router_kernel,
        grid=(n // tt,),
        in_specs=[pl.BlockSpec((ne, tt), lambda i: (0, i))],
        out_specs=[slot, slot, slot, pl.BlockSpec((ne, LANES), lambda i: (0, 0))],
        out_shape=[jax.ShapeDtypeStruct((TOP_K, n), jnp.int32), jax.ShapeDtypeStruct((TOP_K, n), F32),
                   jax.ShapeDtypeStruct((TOP_K, n), jnp.int32), jax.ShapeDtypeStruct((ne, LANES), F32)],
        compiler_params=_cparams(("arbitrary",)),
        name="moe_router",
    )(logits_t)


MOE_TM = 512


def _dispatch_kernel(dest_ref, zrow_ref, hf_ref, xs_ref, zbuf, sem, zsem):
    tt = hf_ref.shape[0] // SUBLANES
    n = dest_ref.shape[0] // TOP_K
    base = pl.program_id(0) * tt

    def tile(ref, row):
        return ref.at[pl.ds(pl.multiple_of(row * SUBLANES, SUBLANES), SUBLANES), :]

    @pl.when(pl.program_id(0) == 0)
    def _():
        zbuf[...] = jnp.zeros(zbuf.shape, zbuf.dtype)

        def fill(j):
            first = pl.multiple_of(jnp.maximum(zrow_ref[j], 0) * SUBLANES, MOE_TM * SUBLANES)
            return pltpu.make_async_copy(zbuf, xs_ref.at[pl.ds(first, MOE_TM * SUBLANES), :], zsem)

        for j in range(zrow_ref.shape[0]):
            pl.when(zrow_ref[j] >= 0)(fill(j).start)
        for j in range(zrow_ref.shape[0]):
            pl.when(zrow_ref[j] >= 0)(fill(j).wait)

    def issue(t, carry):
        for k in range(TOP_K):
            pltpu.make_async_copy(tile(hf_ref, t), tile(xs_ref, dest_ref[k * n + base + t]), sem).start()
        return carry

    lax.fori_loop(0, tt, issue, 0, unroll=8)
    for _ in range(TOP_K):
        pltpu.make_async_copy(hf_ref, xs_ref.at[pl.ds(0, tt * SUBLANES), :], sem).wait()


DISPATCH_TT = 512


def moe_dispatch(dest_flat, zero_rows, hf, n_rows):
    n = hf.shape[0] // SUBLANES
    tt = DISPATCH_TT
    grid_spec = pltpu.PrefetchScalarGridSpec(
        num_scalar_prefetch=2,
        grid=(n // tt,),
        in_specs=[pl.BlockSpec((tt * SUBLANES, LANES), lambda i, dest, zrow: (i, 0))],
        out_specs=pl.BlockSpec(memory_space=pl.ANY),
        scratch_shapes=[pltpu.VMEM((MOE_TM * SUBLANES, LANES), F32), pltpu.SemaphoreType.DMA,
                        pltpu.SemaphoreType.DMA],
    )
    return pl.pallas_call(
        _dispatch_kernel,
        grid_spec=grid_spec,
        out_shape=jax.ShapeDtypeStruct((n_rows * SUBLANES, LANES), F32),
        compiler_params=_cparams(("arbitrary",)),
        name="moe_dispatch",
    )(dest_flat, zero_rows, hf)


def _expert_kernel(be_ref, nu_ref, xs_ref, wgu_ref, bgu_ref, wdn_ref, bdn_ref, y_ref,
                   wgu_bf, wdn_bf):
    i = pl.program_id(0)
    used = i < nu_ref[0]

    @pl.when(used & ((i == 0) | (be_ref[i] != be_ref[jnp.maximum(i - 1, 0)])))
    def _():
        wgu_bf[...] = wgu_ref[0].astype(BF16)
        wdn_bf[...] = wdn_ref[0].astype(BF16)

    @pl.when(used)
    def _():
        hgu = _dot(_load_token_tiles(xs_ref).astype(BF16), wgu_bf[...]) + bgu_ref[0]
        de = hgu.shape[1] // 2
        glu = jnp.minimum(hgu[:, :de], SWIGLU_LIMIT)
        lin = jnp.clip(hgu[:, de:], -SWIGLU_LIMIT, SWIGLU_LIMIT)
        act = glu * jax.nn.sigmoid(SWIGLU_ALPHA * glu) * (lin + 1.0)
        _store_token_tiles(y_ref, _dot(act.astype(BF16), wdn_bf[...]) + bdn_ref[0])

    @pl.when(jnp.logical_not(used))
    def _():
        y_ref[...] = jnp.zeros(y_ref.shape, y_ref.dtype)


EXPERT_VMEM_LIMIT = 56 * 1024 * 1024


def expert_ffn(layer, blk_exp, n_used, xs, w_gu, b_gu, w_dn, b_dn):
    r = xs.shape[0] // SUBLANES
    d = SUBLANES * LANES
    nl, ne, _, wide = w_gu.shape
    w_gu = w_gu.reshape(nl * ne, d, wide)
    w_dn = w_dn.reshape(nl * ne, wide // 2, d)
    b_gu = b_gu.reshape(nl * ne, wide)
    b_dn = b_dn.reshape(nl * ne, d)
    ne, base = nl * ne, layer * ne
    tm = MOE_TM
    grid_spec = pltpu.PrefetchScalarGridSpec(
        num_scalar_prefetch=2,
        grid=(r // tm,),
        in_specs=[pl.BlockSpec((tm * SUBLANES, LANES), lambda i, be, nu: (i, 0)),
                  pl.BlockSpec((1, d, wide), lambda i, be, nu: (base + be[i], 0, 0)),
                  pl.BlockSpec((1, 1, wide), lambda i, be, nu: (base + be[i], 0, 0)),
                  pl.BlockSpec((1, wide // 2, d), lambda i, be, nu: (base + be[i], 0, 0)),
                  pl.BlockSpec((1, 1, d), lambda i, be, nu: (base + be[i], 0, 0))],
        out_specs=pl.BlockSpec((tm * SUBLANES, LANES), lambda i, be, nu: (i, 0)),
        scratch_shapes=[pltpu.VMEM((d, wide), BF16), pltpu.VMEM((wide // 2, d), BF16)],
    )
    return pl.pallas_call(
        _expert_kernel,
        grid_spec=grid_spec,
        out_shape=jax.ShapeDtypeStruct((r * SUBLANES, LANES), F32),
        compiler_params=pltpu.CompilerParams(dimension_semantics=("arbitrary",),
                                             vmem_limit_bytes=EXPERT_VMEM_LIMIT),
        name="expert_ffn",
    )(blk_exp, n_used, xs, w_gu, b_gu.reshape(ne, 1, wide), w_dn, b_dn.reshape(ne, 1, d))


COMBINE_TT = 256


def _combine_kernel(dest_ref, gate_ref, x_ref, g2_ref, n3_ref, y_ref, o_ref, ybuf, sems):
    tt = x_ref.shape[0]
    n = dest_ref.shape[0] // TOP_K
    i = pl.program_id(0)
    steps = pl.num_programs(0)

    def gather(step, slot):
        base = step * tt

        def issue(t, carry):
            dst = pl.ds(pl.multiple_of(t * SUBLANES, SUBLANES), SUBLANES)
            for k in range(TOP_K):
                row = dest_ref[k * n + base + t]
                src = pl.ds(pl.multiple_of(row * SUBLANES, SUBLANES), SUBLANES)
                pltpu.make_async_copy(y_ref.at[src, :], ybuf.at[slot, k, dst, :], sems.at[slot]).start()
            return carry

        lax.fori_loop(0, tt, issue, 0, unroll=8)

    @pl.when(i == 0)
    def _():
        gather(0, 0)

    @pl.when(i + 1 < steps)
    def _():
        gather(i + 1, (i + 1) % 2)

    slot = i % 2
    for k in range(TOP_K):
        pltpu.make_async_copy(y_ref.at[pl.ds(0, tt * SUBLANES), :], ybuf.at[slot, k], sems.at[slot]).wait()
    gate = gate_ref[...]
    y = gate[:, 0:1] * _load_token_tiles(ybuf.at[slot, 0])
    for k in range(1, TOP_K):
        y = y + gate[:, k:k + 1] * _load_token_tiles(ybuf.at[slot, k])
    o_ref[...] = x_ref[...] + g2_ref[0] * _rms(y, n3_ref[...])


def moe_combine(dest_flat, gate, x, g2, n3, y_rows):
    b, s, d = x.shape
    n = b * s
    tt = COMBINE_TT
    per_batch = s // tt
    grid_spec = pltpu.PrefetchScalarGridSpec(
        num_scalar_prefetch=1,
        grid=(n // tt,),
        in_specs=[pl.BlockSpec((tt, LANES), lambda i, dest: (i, 0)),
                  pl.BlockSpec((tt, d), lambda i, dest: (i, 0)),
                  pl.BlockSpec((1, 1, d), lambda i, dest: (i // per_batch, 0, 0)),
                  pl.BlockSpec((1, d), lambda i, dest: (0, 0)),
                  pl.BlockSpec(memory_space=pl.ANY)],
        out_specs=pl.BlockSpec((tt, d), lambda i, dest: (i, 0)),
        scratch_shapes=[pltpu.VMEM((2, TOP_K, tt * SUBLANES, LANES), F32),
                        pltpu.SemaphoreType.DMA((2,))],
    )
    out = pl.pallas_call(
        _combine_kernel,
        grid_spec=grid_spec,
        out_shape=jax.ShapeDtypeStruct((n, d), F32),
        compiler_params=_cparams(("arbitrary",)),
        name="moe_combine",
    )(dest_flat, gate, x.reshape(n, d), g2, n3, y_rows)
    return out.reshape(b, s, d)


def _dispatch_plan(idx, rank, counts):
    n = idx.shape[1]
    tm = MOE_TM
    padded = (counts + tm - 1) // tm * tm
    pad_end = jnp.cumsum(padded)
    start = pad_end - padded
    experts = jnp.arange(N_EXPERTS, dtype=jnp.int32)
    dest = rank + jnp.sum(jnp.where(idx[..., None] == experts, start, 0), axis=-1)
    n_rows = -(-(n * TOP_K + N_EXPERTS * (tm - 1)) // tm) * tm
    n_blk = n_rows // tm
    blk_start = jnp.arange(n_blk, dtype=jnp.int32) * tm
    blk_exp = jnp.minimum(jnp.sum((pad_end[None, :] <= blk_start[:, None]).astype(jnp.int32), axis=1),
                          N_EXPERTS - 1)
    n_used = (pad_end[-1] // tm).astype(jnp.int32).reshape(1)
    tail = pad_end[-1] + experts * tm
    zero_rows = jnp.concatenate([jnp.where(padded > 0, pad_end - tm, -1),
                                 jnp.where(tail < n_rows, tail, -1)]).astype(jnp.int32)
    return dest.reshape(-1), blk_exp, n_used, zero_rows, n_rows


def _layer(layer, x, mod, norm_g, w_in, cmp_pe, cmp_w1, cmp_w2, grp_g, w_out, router_w, router_b,
           w_gu, b_gu, w_dn, b_dn, tabs, overlap):
    b, s, d = x.shape
    sh1, sc1, g1, sh2, sc2, g2 = (m.reshape(b, 1, d) for m in jnp.split(mod, 6, axis=-1))
    ng = norm_g.reshape(4, 1, d)

    qa, ka, va, qb, kvb, gb, qc, kc, vc = in_projection(x, sc1, sh1, ng[0], _pack_w_in(w_in), tabs)
    oa = moba_attention(qa, ka, va)
    nc = s // NSA_CMP_STRIDE
    kv16 = jnp.stack([kvb[:, :, :HEAD_DIM], kvb[:, :, HEAD_DIM:2 * HEAD_DIM]], axis=1)
    kv16 = kv16.reshape(b, 2, nc, NSA_CMP_STRIDE * HEAD_DIM)
    cmp_kv = nsa_compress(kv16, cmp_pe.reshape(2, 1, NSA_CMP_LEN * HEAD_DIM), cmp_w1, cmp_w2)
    ob = nsa_attention(qb, kvb, gb, cmp_kv, overlap)
    oc = sb_attention(qc, kc, vc)

    rw_hi, rw_lo = _split_bf16(router_w.T)
    rw3 = jnp.concatenate([rw_hi, rw_hi, rw_lo], axis=1)
    x1, hf, logits_t = out_projection(oa, ob, oc, x, grp_g.reshape(1, d), w_out.astype(BF16), g1,
                                      ng[1], ng[2], sc2, sh2, rw3, router_b.reshape(-1, 1))

    n = b * s
    idx, gate, rank, counts = moe_router(logits_t)
    dest, blk_exp, n_used, zero_rows, n_rows = _dispatch_plan(idx, rank, counts[:, 0].astype(jnp.int32))
    xs = moe_dispatch(dest, zero_rows, hf.reshape(n * SUBLANES, LANES), n_rows)
    y_rows = expert_ffn(layer, blk_exp, n_used, xs, w_gu, b_gu, w_dn, b_dn)
    gate = jnp.pad(gate.T, ((0, 0), (0, LANES - TOP_K)))
    return moe_combine(dest, gate, x1, g2, ng[3], y_rows)


def kernel(x, c, ada_w, ada_b, norm_g, w_in, nsa_cmp_pe, nsa_cmp_w1, nsa_cmp_w2, mix_out_g,
           w_out, router_w, router_b, exp_w_gu, exp_b_gu, exp_w_dn, exp_b_dn):
    s = x.shape[1]
    tabs = _rope_tables(s)
    overlap = _nsa_overlap(s)
    mod = ada_modulation(c, ada_w, ada_b)
    for l in range(ada_w.shape[0]):
        x = _layer(l, x, mod[l], norm_g[l], w_in[l], nsa_cmp_pe[l], nsa_cmp_w1[l], nsa_cmp_w2[l],
                   mix_out_g[l], w_out[l], router_w[l], router_b[l], exp_w_gu, exp_b_gu,
                   exp_w_dn, exp_b_dn, tabs, overlap)
    return x
```

```python
import functools

import numpy as np
import jax
import jax.numpy as jnp
from jax import lax
from jax.experimental import pallas as pl
from jax.experimental.pallas import tpu as pltpu

F32 = jnp.float32
BF16 = jnp.bfloat16
HI = lax.Precision.HIGHEST

D_MODEL = 1024
N_HEADS = 16
HEAD_DIM = 64
MOBA_HEADS = 4
NSA_HEADS = 6
SB_HEADS = 6
MOBA_W = MOBA_HEADS * HEAD_DIM
NSA_W = NSA_HEADS * HEAD_DIM
SB_W = SB_HEADS * HEAD_DIM
ROPE_DIM = 16
ROPE_THETA = 500000.0
MOBA_BLOCK = 256
MOBA_TOPK = 3
NSA_CMP_LEN = 32
NSA_CMP_STRIDE = 16
NSA_SLC_BLOCK = 64
NSA_SLC_TOPK = 16
NSA_WINDOW = 512
N_EXPERTS = 32
TOP_K = 4
SWIGLU_LIMIT = 7.0
SWIGLU_ALPHA = 1.702
RMS_EPS = 1e-6
NEG_INF = -1e30
SEL_FORCE = 1e4
SCALE = HEAD_DIM ** -0.5
LOG2E = 1.4426950408889634

LANES = 128
VMEM_LIMIT = 48 * 1024 * 1024


def _cparams(sem):
    return pltpu.CompilerParams(dimension_semantics=sem, vmem_limit_bytes=VMEM_LIMIT)


def _dot(a, b):
    return jnp.dot(a, b, preferred_element_type=F32)


def _dot_nt(a, b):
    return lax.dot_general(a, b, (((1,), (1,)), ((), ())), preferred_element_type=F32)


def _split_bf16(x):
    hi = x.astype(BF16)
    return hi, (x - hi.astype(F32)).astype(BF16)


def _dot_split(a, b):
    (a_hi, a_lo), (b_hi, b_lo) = _split_bf16(a), _split_bf16(b)
    return _dot(jnp.concatenate([a_hi, a_lo, a_hi], axis=1), jnp.concatenate([b_hi, b_hi, b_lo], axis=0))


def _dot_nt_split(a, b):
    (a_hi, a_lo), (b_hi, b_lo) = _split_bf16(a), _split_bf16(b)
    return _dot_nt(jnp.concatenate([a_hi, a_lo, a_hi], axis=1), jnp.concatenate([b_hi, b_hi, b_lo], axis=1))


SB_TILE = 256


SB_GROUP_W = SB_W
SB_LAG = 2


def _sb_kernel(q_ref, k_ref, v_ref, o_ref, ks_ref, vs_ref, run_ref, acc_ref):
    T = SB_TILE
    H = ks_ref.shape[0]
    qi = pl.program_id(2)

    @pl.when(qi == 0)
    def _():
        zeros = jnp.zeros((ks_ref.shape[1], HEAD_DIM), F32)
        for h in range(H):
            ks_ref[h] = k_ref[0, :, h * HEAD_DIM:(h + 1) * HEAD_DIM].astype(BF16)
            vh = v_ref[0, :, h * HEAD_DIM:(h + 1) * HEAD_DIM]
            vs_ref[h] = jnp.concatenate([vh, zeros] if h % 2 == 0 else [zeros, vh], axis=1).astype(BF16)

    run_ref[...] = jnp.zeros(run_ref.shape, F32)
    acc_ref[...] = jnp.zeros(acc_ref.shape, F32)

    row = lax.broadcasted_iota(jnp.int32, (T, T), 0)
    col = lax.broadcasted_iota(jnp.int32, (T, T), 1)
    incl = (row >= col).astype(BF16)
    sum_rhs = jnp.concatenate([incl, incl], axis=0)
    diag_mask = col < row
    qs = [(q_ref[0, :, h * HEAD_DIM:(h + 1) * HEAD_DIM] * (SCALE * LOG2E)).astype(BF16)
          for h in range(H)]
    sign = jnp.uint32(0x80000000)

    def scores(h, j):
        return _dot_nt(qs[h], ks_ref[h, pl.ds(pl.multiple_of(j * T, T), T), :])

    def neg_log2_keep(z, masked):
        neg_abs = lax.bitcast_convert_type(lax.bitcast_convert_type(z, jnp.uint32) | sign, F32)
        nlk = jnp.maximum(z, 0.0) + jnp.log2(1.0 + jnp.exp2(neg_abs))
        return jnp.where(diag_mask, nlk, 0.0) if masked else nlk

    def suffix_sums(nlk):
        hi = nlk.astype(BF16)
        lo = (nlk - hi.astype(F32)).astype(BF16)
        return _dot(jnp.concatenate([hi, lo], axis=1), sum_rhs)

    def weights(z, sums, run, masked):
        x = z - sums - run
        if masked:
            x = jnp.where(diag_mask, x, NEG_INF)
        return jnp.exp2(x.astype(BF16))

    def pv(h, p, j):
        return _dot(p, vs_ref[h, pl.ds(pl.multiple_of(j * T, T), T), :])

    def key_tile(j, masked):
        zs, sums = {0: scores(0, j)}, {}
        for t in range(H + SB_LAG):
            if t + 1 < H:
                zs[t + 1] = scores(t + 1, j)
            if t < H:
                sums[t] = suffix_sums(neg_log2_keep(zs[t], masked))
            h = t - SB_LAG
            if h >= 0:
                run = run_ref[h]
                p = weights(zs.pop(h), sums[h], run, masked)
                run_ref[h] = run + sums.pop(h)[:, 0:1]
                acc_ref[h // 2] = acc_ref[h // 2] + pv(h, p, j)

    key_tile(qi, True)

    def past(i, carry):
        key_tile(qi - 1 - i, False)
        return carry

    lax.fori_loop(0, qi, past, 0)
    for g in range(H // 2):
        o_ref[0, :, g * LANES:(g + 1) * LANES] = acc_ref[g]


def sb_attention(q, k, v):
    b, s, w = q.shape
    T = SB_TILE
    gw = SB_GROUP_W
    nh = gw // HEAD_DIM
    qspec = pl.BlockSpec((1, T, gw), lambda bi, p, i: (bi, i, p))
    kvspec = pl.BlockSpec((1, s, gw), lambda bi, p, i: (bi, 0, p))
    return pl.pallas_call(
        _sb_kernel,
        grid=(b, w // gw, s // T),
        in_specs=[qspec, kvspec, kvspec],
        out_specs=qspec,
        out_shape=jax.ShapeDtypeStruct((b, s, w), F32),
        scratch_shapes=[pltpu.VMEM((nh, s, HEAD_DIM), BF16), pltpu.VMEM((nh, s, LANES), BF16),
                        pltpu.VMEM((nh, T, 1), F32), pltpu.VMEM((nh // 2, T, LANES), F32)],
        compiler_params=_cparams(("arbitrary", "arbitrary", "arbitrary")),
        name="sb_attention",
    )(q, k, v)


def _rank_before_t(vals, n):
    idx = lax.broadcasted_iota(jnp.int32, vals.shape, 0)
    rank = jnp.zeros(vals.shape, F32)
    for j2 in range(n):
        other = vals[j2:j2 + 1, :]
        ahead = (other > vals) | ((other == vals) & (idx > j2))
        rank = rank + jnp.where(ahead, 1.0, 0.0)
    return rank


MOBA_KT = 2 * MOBA_BLOCK
MOBA_GROUP_W = MOBA_W


def _moba_kernel(q_ref, k_ref, v_ref, o_ref, ks_ref, vs_ref, km_ref):
    T, KT = MOBA_BLOCK, MOBA_KT
    H = ks_ref.shape[0]
    nb = km_ref.shape[1]
    s_len = ks_ref.shape[1]
    qi = pl.program_id(2)

    @pl.when(qi == 0)
    def _():
        key_blk = lax.broadcasted_iota(jnp.int32, (s_len, HEAD_DIM), 0) // T
        onehot = (lax.broadcasted_iota(jnp.int32, (s_len, HEAD_DIM), 1) == key_blk).astype(F32)
        ones_col = (lax.broadcasted_iota(jnp.int32, (s_len, HEAD_DIM), 1) == 0).astype(F32)
        for h in range(H):
            kh = k_ref[0, :, h * HEAD_DIM:(h + 1) * HEAD_DIM]
            ks_ref[h] = jnp.concatenate([kh, onehot], axis=1).astype(BF16)
            vs_ref[h] = jnp.concatenate([v_ref[0, :, h * HEAD_DIM:(h + 1) * HEAD_DIM], ones_col],
                                        axis=1).astype(BF16)
            km_ref[h] = jnp.mean(kh.reshape(nb, T, HEAD_DIM), axis=1)

    row = lax.broadcasted_iota(jnp.int32, (T, T), 0)
    col = lax.broadcasted_iota(jnp.int32, (T, T), 1)
    causal = col <= row
    blk_t = lax.broadcasted_iota(jnp.int32, (nb, T), 0)
    start = pl.multiple_of(qi * T, T)

    qfs = [q_ref[0, :, h * HEAD_DIM:(h + 1) * HEAD_DIM] for h in range(H)]
    qss = [qf * SCALE for qf in qfs]
    s_own = [_dot_nt(qss[h].astype(BF16), ks_ref[h, pl.ds(start, T), 0:HEAD_DIM]) for h in range(H)]
    q_aug, state = [], []
    for h in range(H):
        gate = _dot_nt_split(km_ref[h], qfs[h])
        gate = jnp.where(blk_t < qi, gate, NEG_INF)
        sel = (_rank_before_t(gate, nb) < float(MOBA_TOPK)) & (gate > 0.5 * NEG_INF)
        selb = jnp.where(sel, 0.0, NEG_INF)
        selb = jnp.concatenate([selb, jnp.full((LANES - nb, T), NEG_INF, F32)], axis=0).T
        q_aug.append(jnp.concatenate([qss[h], selb[:, :HEAD_DIM]], axis=1).astype(BF16))

        s = jnp.where(causal, s_own[h], NEG_INF)
        m = jnp.max(s, axis=1, keepdims=True)
        p = jnp.exp((s - m).astype(BF16))
        state += [m, _dot(p, vs_ref[h, pl.ds(start, T), :])]

    def past(i, carry):
        st = pl.multiple_of(i * KT, KT)
        ss = [_dot_nt(q_aug[h], ks_ref[h, pl.ds(st, KT), :]) for h in range(H)]
        out = []
        for h in range(H):
            m, acc = carry[2 * h:2 * h + 2]
            m_new = jnp.maximum(m, jnp.max(ss[h], axis=1, keepdims=True))
            alpha = jnp.exp(m - m_new)
            p = jnp.exp((ss[h] - m_new).astype(BF16))
            out += [m_new, alpha * acc + _dot(p, vs_ref[h, pl.ds(st, KT), :])]
        return tuple(out)

    per = KT // T
    state = lax.fori_loop(0, (qi + per - 1) // per, past, tuple(state))
    outs = [state[2 * h + 1][:, :HEAD_DIM] / jnp.maximum(state[2 * h + 1][:, HEAD_DIM:HEAD_DIM + 1], 1e-30)
            for h in range(H)]
    o_ref[0] = jnp.concatenate(outs, axis=1)


def moba_attention(q, k, v):
    b, s, w = q.shape
    T = MOBA_BLOCK
    nb = s // T
    assert nb <= HEAD_DIM and s % MOBA_KT == 0
    gw = MOBA_GROUP_W
    nh = gw // HEAD_DIM
    qspec = pl.BlockSpec((1, T, gw), lambda bi, p, i: (bi, i, p))
    kvspec = pl.BlockSpec((1, s, gw), lambda bi, p, i: (bi, 0, p))
    return pl.pallas_call(
        _moba_kernel,
        grid=(b, w // gw, nb),
        in_specs=[qspec, kvspec, kvspec],
        out_specs=qspec,
        out_shape=jax.ShapeDtypeStruct((b, s, w), F32),
        scratch_shapes=[pltpu.VMEM((nh, s, LANES), BF16), pltpu.VMEM((nh, s, LANES), BF16),
                        pltpu.VMEM((nh, nb, HEAD_DIM), F32)],
        compiler_params=_cparams(("arbitrary", "arbitrary", "arbitrary")),
        name="moba_attention",
    )(q, k, v)


def _nsa_compress_kernel(x_ref, pe_ref, w1_ref, w2_ref, o_ref):
    nc = x_ref.shape[2]
    half = NSA_CMP_STRIDE * HEAD_DIM
    x = x_ref[0, 0]
    w1 = w1_ref[0]
    first = _dot_split(x, w1[:half])
    second = _dot_split(x, w1[half:])
    pe = jnp.broadcast_to(pe_ref[0], (8, 2 * half))
    peb = _dot_split(pe, w1)[0:1]
    pre = first + pltpu.roll(second, nc - 1, 0) + peb
    hid = pre * jax.nn.sigmoid(pre)
    o_ref[0, 0] = _dot_split(hid, w2_ref[0])


def nsa_compress(kv16, pe, w1, w2):
    b, _, nc, wide = kv16.shape
    return pl.pallas_call(
        _nsa_compress_kernel,
        grid=(b, 2),
        in_specs=[pl.BlockSpec((1, 1, nc, wide), lambda bi, i: (bi, i, 0, 0)),
                  pl.BlockSpec((1, 1, 2 * wide), lambda bi, i: (i, 0, 0)),
                  pl.BlockSpec((1, 2 * wide, HEAD_DIM), lambda bi, i: (i, 0, 0)),
                  pl.BlockSpec((1, HEAD_DIM, HEAD_DIM), lambda bi, i: (i, 0, 0))],
        out_specs=pl.BlockSpec((1, 1, nc, HEAD_DIM), lambda bi, i: (bi, i, 0, 0)),
        out_shape=jax.ShapeDtypeStruct((b, 2, nc, HEAD_DIM), F32),
        compiler_params=_cparams(("arbitrary", "arbitrary")),
        name="nsa_compress",
    )(kv16, pe, w1, w2)


NSA_TQ = 256
NSA_KT = 512
NSA_SPAN = NSA_WINDOW + NSA_TQ
NSA_CHAIN_HEADS = 2
NSA_WINDOW_HEADS = 2


def _softmax_rows(s, mask):
    s = jnp.where(mask, s, NEG_INF)
    m = jnp.max(s, axis=-1, keepdims=True)
    e = jnp.where(mask, jnp.exp(s - m), 0.0)
    return e / jnp.maximum(jnp.sum(e, axis=-1, keepdims=True), 1e-30)


def _nsa_kernel(q_ref, cmp_ref, slc_ref, win_ref, g_ref, ov_ref, o_ref,
                ksl_ref, vsl_ref, kw_ref, vw_ref, kc_ref):
    TQ, KT, H = NSA_TQ, NSA_KT, NSA_HEADS
    ns, nc = ov_ref.shape
    s_len = ksl_ref.shape[0]
    qi = pl.program_id(1)
    t0 = qi * TQ

    @pl.when(qi == 0)
    def _():
        lane = lax.broadcasted_iota(jnp.int32, (s_len, HEAD_DIM), 1)
        key_blk = lax.broadcasted_iota(jnp.int32, (s_len, HEAD_DIM), 0) // NSA_SLC_BLOCK
        onehot = (lane == key_blk).astype(F32)
        ones_col = (lane == 0).astype(F32)
        ksl_ref[...] = jnp.concatenate([slc_ref[0, :, :HEAD_DIM], onehot], axis=1).astype(BF16)
        vsl_ref[...] = jnp.concatenate([slc_ref[0, :, HEAD_DIM:], ones_col], axis=1).astype(BF16)
        kw_ref[...] = win_ref[0, :, :HEAD_DIM].astype(BF16)
        vw_ref[...] = jnp.concatenate([win_ref[0, :, HEAD_DIM:], ones_col], axis=1).astype(BF16)
        k_hi, k_lo = _split_bf16(cmp_ref[0, 0])
        kc_ref[...] = jnp.concatenate([k_hi, k_hi, k_lo], axis=1)

    qf = jnp.concatenate([q_ref[0, :, h * HEAD_DIM:(h + 1) * HEAD_DIM] for h in range(H)],
                         axis=0) * SCALE
    q, q_lo = _split_bf16(qf)

    def normalized(acc):
        return acc[:, :HEAD_DIM] / jnp.maximum(acc[:, HEAD_DIM:HEAD_DIM + 1], 1e-30)

    w0 = pl.multiple_of(jnp.maximum(t0 - NSA_WINDOW, 0), TQ)
    wpos = w0 + lax.broadcasted_iota(jnp.int32, (TQ, NSA_SPAN), 1)
    tq_w = t0 + lax.broadcasted_iota(jnp.int32, (TQ, NSA_SPAN), 0)
    mask_w = ((wpos <= tq_w) & (wpos > tq_w - NSA_WINDOW))[None]
    o_w = []
    for c in range(H // NSA_WINDOW_HEADS):
        rows = slice(c * NSA_WINDOW_HEADS * TQ, (c + 1) * NSA_WINDOW_HEADS * TQ)
        s_w = _dot_nt(q[rows], kw_ref[pl.ds(w0, NSA_SPAN), :]).reshape(NSA_WINDOW_HEADS, TQ, NSA_SPAN)
        s_w = jnp.where(mask_w, s_w, NEG_INF)
        p_w = jnp.exp((s_w - jnp.max(s_w, axis=-1, keepdims=True)).astype(BF16))
        o_w.append(normalized(_dot(p_w.reshape(NSA_WINDOW_HEADS * TQ, NSA_SPAN),
                                   vw_ref[pl.ds(w0, NSA_SPAN), :])))
    o_w = jnp.concatenate(o_w, axis=0)

    tq_c = t0 + lax.broadcasted_iota(jnp.int32, (TQ, nc), 0)
    n_c = lax.broadcasted_iota(jnp.int32, (TQ, nc), 1)
    mask_c = (n_c * NSA_CMP_STRIDE + (NSA_CMP_LEN - 1) <= tq_c) & (n_c < nc - 1)
    s_c = _dot_nt(jnp.concatenate([q, q_lo, q], axis=1), kc_ref[...])
    p_c = _softmax_rows(s_c.reshape(H, TQ, nc), mask_c[None])
    o_c = _dot(p_c.reshape(H * TQ, nc).astype(BF16), cmp_ref[0, 1].astype(BF16))

    p_sum = jnp.sum(p_c, axis=0)
    p1, p2 = _split_bf16(p_sum)
    p3 = (p_sum - p1.astype(F32) - p2.astype(F32)).astype(BF16)
    ov = ov_ref[...].astype(BF16)
    imp = _dot_nt(jnp.concatenate([ov, ov, ov], axis=1), jnp.concatenate([p1, p2, p3], axis=1))
    tq_s = t0 + lax.broadcasted_iota(jnp.int32, (ns, TQ), 1)
    blk = lax.broadcasted_iota(jnp.int32, (ns, TQ), 0)
    own = tq_s // NSA_SLC_BLOCK
    forced = (blk == 0) | (blk == own) | (blk == own - 1)
    imp = jnp.where(forced, SEL_FORCE, imp)
    imp = jnp.where(blk <= own, imp, NEG_INF)
    sel = (_rank_before_t(imp, ns) < float(min(NSA_SLC_TOPK, ns))) & (imp > 0.5 * NEG_INF)
    selb = jnp.where(sel, 0.0, NEG_INF)
    if ns < LANES:
        selb = jnp.concatenate([selb, jnp.full((LANES - ns, TQ), NEG_INF, F32)], axis=0)
    selb = selb.T[:, :HEAD_DIM]
    q_aug = jnp.concatenate([qf, jnp.concatenate([selb] * H, axis=0)], axis=1).astype(BF16)

    jd = t0 // KT
    start = pl.multiple_of(jd * KT, KT)
    kpos = start + lax.broadcasted_iota(jnp.int32, (TQ, KT), 1)
    tq_k = t0 + lax.broadcasted_iota(jnp.int32, (TQ, KT), 0)
    HC = NSA_CHAIN_HEADS
    G, R = H // HC, HC * TQ
    qa = [q_aug[c * R:(c + 1) * R] for c in range(G)]
    causal_k = (kpos <= tq_k)[None]

    def scores(c, st):
        return _dot_nt(qa[c], ksl_ref[pl.ds(st, KT), :]).reshape(HC, TQ, KT)

    s_own = [scores(c, start) for c in range(G)]
    state = []
    for c in range(G):
        s = jnp.where(causal_k, s_own[c], NEG_INF)
        m = jnp.max(s, axis=-1, keepdims=True)
        p = jnp.exp((s - m).astype(BF16))
        state += [m, _dot(p.reshape(R, KT), vsl_ref[pl.ds(start, KT), :])]

    def past(j, carry):
        st = pl.multiple_of(j * KT, KT)
        ss = {0: scores(0, st)}
        out = []
        for c in range(G):
            if c + 1 < G:
                ss[c + 1] = scores(c + 1, st)
            m, acc = carry[2 * c:2 * c + 2]
            m_new = jnp.maximum(m, jnp.max(ss[c], axis=-1, keepdims=True))
            alpha = jnp.exp(m - m_new)
            p = jnp.exp((ss[c] - m_new).astype(BF16))
            pv = _dot(p.reshape(R, KT), vsl_ref[pl.ds(st, KT), :])
            out += [m_new, alpha.reshape(R, 1) * acc + pv]
        return tuple(out)

    state = lax.fori_loop(0, jd, past, tuple(state))
    o_s = jnp.concatenate([normalized(state[2 * c + 1]) for c in range(G)], axis=0)

    g = g_ref[0]
    outs = []
    for h in range(H):
        rows = slice(h * TQ, (h + 1) * TQ)
        outs.append(g[:, 3 * h:3 * h + 1] * o_c[rows] + g[:, 3 * h + 1:3 * h + 2] * o_s[rows]
                    + g[:, 3 * h + 2:3 * h + 3] * o_w[rows])
    o_ref[0] = jnp.concatenate(outs, axis=1)


def nsa_attention(q, kv, gates, cmp_kv, overlap):
    b, s, w = q.shape
    ns, nc = overlap.shape
    assert ns <= HEAD_DIM
    TQ = NSA_TQ
    return pl.pallas_call(
        _nsa_kernel,
        grid=(b, s // TQ),
        in_specs=[pl.BlockSpec((1, TQ, w), lambda bi, i: (bi, i, 0)),
                  pl.BlockSpec((1, 2, nc, HEAD_DIM), lambda bi, i: (bi, 0, 0, 0)),
                  pl.BlockSpec((1, s, LANES), lambda bi, i: (bi, 0, 1)),
                  pl.BlockSpec((1, s, LANES), lambda bi, i: (bi, 0, 2)),
                  pl.BlockSpec((1, TQ, LANES), lambda bi, i: (bi, i, 0)),
                  pl.BlockSpec((ns, nc), lambda bi, i: (0, 0))],
        out_specs=pl.BlockSpec((1, TQ, w), lambda bi, i: (bi, i, 0)),
        out_shape=jax.ShapeDtypeStruct((b, s, w), F32),
        scratch_shapes=[pltpu.VMEM((s, LANES), BF16), pltpu.VMEM((s, LANES), BF16),
                        pltpu.VMEM((s, HEAD_DIM), BF16), pltpu.VMEM((s, LANES), BF16),
                        pltpu.VMEM((nc, 3 * HEAD_DIM), BF16)],
        compiler_params=_cparams(("arbitrary", "arbitrary")),
        name="nsa_attention",
    )(q, cmp_kv, kv, kv, gates, overlap)


def _nsa_overlap(s):
    nc = s // NSA_CMP_STRIDE
    ns = s // NSA_SLC_BLOCK
    cstart = np.arange(nc) * NSA_CMP_STRIDE
    cend = cstart + NSA_CMP_LEN - 1
    sstart = np.arange(ns) * NSA_SLC_BLOCK
    ov = (cstart[:, None] <= sstart[None, :] + NSA_SLC_BLOCK - 1) & (cend[:, None] >= sstart[None, :])
    ov[nc - 1] = False
    return jnp.asarray(ov.T.astype(np.float32))


def _mod_kernel(c_ref, w_ref, b_ref, o_ref):
    c = c_ref[...]
    act = c * jax.nn.sigmoid(c)
    o_ref[0] = jnp.dot(act, w_ref[0], precision=HI, preferred_element_type=F32) + b_ref[0]


def ada_modulation(c, ada_w, ada_b):
    nl, d, wide = ada_w.shape
    b = c.shape[0]
    tn = D_MODEL
    return pl.pallas_call(
        _mod_kernel,
        grid=(nl, wide // tn),
        in_specs=[pl.BlockSpec((b, d), lambda l, j: (0, 0)),
                  pl.BlockSpec((1, d, tn), lambda l, j: (l, 0, j)),
                  pl.BlockSpec((1, 1, tn), lambda l, j: (l, 0, j))],
        out_specs=pl.BlockSpec((1, b, tn), lambda l, j: (l, 0, j)),
        out_shape=jax.ShapeDtypeStruct((nl, b, wide), F32),
        compiler_params=_cparams(("arbitrary", "arbitrary")),
        name="ada_modulation",
    )(c, ada_w, ada_b.reshape(nl, 1, wide))


def _rms(x, g):
    return x * lax.rsqrt(jnp.mean(x * x, axis=-1, keepdims=True) + RMS_EPS) * g


_GATE_PAD = LANES
_COLS = {}
_off = 0
for _name, _w in (("qa", MOBA_W), ("ka", MOBA_W), ("va", MOBA_W), ("qb", NSA_W), ("kvb", NSA_W),
                  ("gb", _GATE_PAD), ("qc", SB_W), ("kc", SB_W), ("vc", SB_W)):
    _COLS[_name] = (_off, _w)
    _off += _w
IN_W_PACKED = _off
PROJ_TM = 512


def _rope_block(p, cs, sm, sp):
    return p * cs + pltpu.roll(p, LANES - ROPE_DIM // 2, 1) * sm + pltpu.roll(p, ROPE_DIM // 2, 1) * sp


def _in_proj_kernel(x_ref, sc_ref, sh_ref, g_ref, w_ref, cqk_ref, mqk_ref, pqk_ref,
                    ckv_ref, mkv_ref, pkv_ref,
                    qa_ref, ka_ref, va_ref, qb_ref, kvb_ref, gb_ref, qc_ref, kc_ref, vc_ref):
    hm = _rms(x_ref[0], g_ref[...]) * (1.0 + sc_ref[0]) + sh_ref[0]
    p = _dot(hm.astype(BF16), w_ref[...])
    qk = (cqk_ref[...], mqk_ref[...], pqk_ref[...])
    kv = (ckv_ref[...], mkv_ref[...], pkv_ref[...])

    def emit(ref, name, tabs):
        off, w = _COLS[name]
        for j in range(w // LANES):
            blk = p[:, off + j * LANES: off + (j + 1) * LANES]
            if tabs is not None:
                blk = _rope_block(blk, *tabs)
            ref[0, :, j * LANES:(j + 1) * LANES] = blk

    emit(qa_ref, "qa", qk)
    emit(ka_ref, "ka", qk)
    emit(va_ref, "va", None)
    emit(qb_ref, "qb", qk)
    emit(kvb_ref, "kvb", kv)
    emit(qc_ref, "qc", None)
    emit(kc_ref, "kc", None)
    emit(vc_ref, "vc", None)
    off, w = _COLS["gb"]
    gb_ref[0] = jax.nn.sigmoid(p[:, off:off + w])


def in_projection(x, sc, sh, g, w_packed, tabs):
    b, s, d = x.shape
    tm = PROJ_TM
    row = lambda w: pl.BlockSpec((1, tm, w), lambda bi, i: (bi, i, 0))
    vec = pl.BlockSpec((1, 1, d), lambda bi, i: (bi, 0, 0))
    tab = pl.BlockSpec((tm, LANES), lambda bi, i: (i, 0))
    names = ("qa", "ka", "va", "qb", "kvb", "gb", "qc", "kc", "vc")
    return pl.pallas_call(
        _in_proj_kernel,
        grid=(b, s // tm),
        in_specs=[row(d), vec, vec, pl.BlockSpec((1, d), lambda bi, i: (0, 0)),
                  pl.BlockSpec((d, IN_W_PACKED), lambda bi, i: (0, 0))] + [tab] * 6,
        out_specs=[row(_COLS[n][1]) for n in names],
        out_shape=[jax.ShapeDtypeStruct((b, s, _COLS[n][1]), F32) for n in names],
        compiler_params=_cparams(("arbitrary", "arbitrary")),
        name="in_projection",
    )(x, sc, sh, g, w_packed, *tabs)


def _pack_w_in(w_in):
    widths = (MOBA_W, MOBA_W, MOBA_W, NSA_W, NSA_W, 3 * NSA_HEADS, SB_W, SB_W, SB_W)
    offs = np.cumsum((0,) + widths)
    parts = []
    for i, w in enumerate(widths):
        blk = w_in[:, offs[i]:offs[i + 1]]
        if w == 3 * NSA_HEADS:
            blk = jnp.pad(blk, ((0, 0), (0, _GATE_PAD - w)))
        parts.append(blk)
    return jnp.concatenate(parts, axis=1).astype(BF16)


def _rope_tables(s):
    half = ROPE_DIM // 2
    inv_freq = ROPE_THETA ** (-jnp.arange(0, ROPE_DIM, 2, dtype=F32) / ROPE_DIM)
    ang = jnp.arange(s, dtype=F32)[:, None] * inv_freq[None, :]
    cos, sin = jnp.cos(ang), jnp.sin(ang)
    zeros = jnp.zeros((s, HEAD_DIM - ROPE_DIM), F32)
    z8 = jnp.zeros((s, half), F32)
    cs_h = jnp.concatenate([cos, cos, zeros + 1.0], axis=1)
    sm_h = jnp.concatenate([-sin, z8, zeros], axis=1)
    sp_h = jnp.concatenate([z8, sin, zeros], axis=1)
    ident = (jnp.ones((s, HEAD_DIM), F32), jnp.zeros((s, HEAD_DIM), F32), jnp.zeros((s, HEAD_DIM), F32))
    qk = tuple(jnp.concatenate([t, t], axis=1) for t in (cs_h, sm_h, sp_h))
    kv = tuple(jnp.concatenate([t, i], axis=1) for t, i in zip((cs_h, sm_h, sp_h), ident))
    return qk + kv


OUT_TM = 512
SUBLANES = 8


def _store_token_tiles(ref, val):
    rows = val.shape[0]
    for j in range(SUBLANES):
        ref[pl.ds(j, rows, stride=SUBLANES), :] = val[:, j * LANES:(j + 1) * LANES]


def _load_token_tiles(ref):
    rows = ref.shape[0] // SUBLANES
    return jnp.concatenate([ref[pl.ds(j, rows, stride=SUBLANES), :] for j in range(SUBLANES)], axis=1)


def _out_proj_kernel(oa_ref, ob_ref, oc_ref, x_ref, gg_ref, w_ref, g1_ref, n1_ref, n2_ref,
                     sc_ref, sh_ref, rw_ref, rb_ref, x1_ref, hf_ref, lg_ref):
    gg = gg_ref[...]
    y = jnp.concatenate([_rms(oa_ref[0], gg[:, :MOBA_W]),
                         _rms(ob_ref[0], gg[:, MOBA_W:MOBA_W + NSA_W]),
                         _rms(oc_ref[0], gg[:, MOBA_W + NSA_W:])], axis=1)
    y = _dot(y.astype(BF16), w_ref[...])
    x1 = x_ref[0] + g1_ref[0] * _rms(y, n1_ref[...])
    x1_ref[0] = x1
    hf = _rms(x1, n2_ref[...]) * (1.0 + sc_ref[0]) + sh_ref[0]
    _store_token_tiles(hf_ref.at[0], hf)
    h_hi, h_lo = _split_bf16(hf)
    lg_ref[...] = _dot_nt(rw_ref[...], jnp.concatenate([h_hi, h_lo, h_hi], axis=1)) + rb_ref[...]


def out_projection(oa, ob, oc, x, grp_g, w_out, g1, n1, n2, sc2, sh2, rw_t, rb):
    b, s, d = x.shape
    assert d == SUBLANES * LANES
    ne = rw_t.shape[0]
    tm = OUT_TM
    steps = s // tm
    row = lambda w: pl.BlockSpec((1, tm, w), lambda bi, i: (bi, i, 0))
    vec = pl.BlockSpec((1, 1, d), lambda bi, i: (bi, 0, 0))
    cst = lambda r, w: pl.BlockSpec((r, w), lambda bi, i: (0, 0))
    return pl.pallas_call(
        _out_proj_kernel,
        grid=(b, steps),
        in_specs=[row(MOBA_W), row(NSA_W), row(SB_W), row(d), cst(1, d), cst(d, d), vec,
                  cst(1, d), cst(1, d), vec, vec, cst(ne, 3 * d), cst(ne, 1)],
        out_specs=[row(d), pl.BlockSpec((1, tm * SUBLANES, LANES), lambda bi, i: (bi, i, 0)),
                   pl.BlockSpec((ne, tm), lambda bi, i: (0, bi * steps + i))],
        out_shape=[jax.ShapeDtypeStruct((b, s, d), F32),
                   jax.ShapeDtypeStruct((b, s * SUBLANES, LANES), F32),
                   jax.ShapeDtypeStruct((ne, b * s), F32)],
        compiler_params=_cparams(("arbitrary", "arbitrary")),
        name="out_projection",
    )(oa, ob, oc, x, grp_g, w_out, g1, n1, n2, sc2, sh2, rw_t, rb)


ROUTE_TT = 512


def _router_kernel(lg_ref, idx_ref, gate_ref, rank_ref, cnt_ref):
    ne, tt = lg_ref.shape

    @pl.when(pl.program_id(0) == 0)
    def _():
        cnt_ref[...] = jnp.zeros(cnt_ref.shape, cnt_ref.dtype)

    v = lg_ref[...]
    erow = lax.broadcasted_iota(jnp.int32, (ne, tt), 0)
    vals, hots, firsts = [], [], []
    for _ in range(TOP_K):
        m = jnp.max(v, axis=0, keepdims=True)
        first = jnp.min(jnp.where(v == m, erow, ne), axis=0, keepdims=True)
        hot = erow == first
        v = jnp.where(hot, -jnp.inf, v)
        vals.append(m)
        hots.append(hot)
        firsts.append(first)
    exps = [jnp.exp(val - vals[0]) for val in vals]
    den = exps[0] + exps[1] + exps[2] + exps[3]
    gate_ref[...] = jnp.concatenate([e / den for e in exps], axis=0)
    idx_ref[...] = jnp.concatenate(firsts, axis=0)

    cnt = jnp.zeros((ne, tt), F32)
    for hot in hots:
        cnt = cnt + jnp.where(hot, 1.0, 0.0)
    before = (lax.broadcasted_iota(jnp.int32, (tt, tt), 0)
              < lax.broadcasted_iota(jnp.int32, (tt, tt), 1)).astype(BF16)
    excl = _dot(cnt.astype(BF16), before) + cnt_ref[:, 0:1]
    ranks = [jnp.sum(jnp.where(hot, excl, 0.0), axis=0, keepdims=True) for hot in hots]
    rank_ref[...] = jnp.concatenate(ranks, axis=0).astype(jnp.int32)
    cnt_ref[...] = cnt_ref[...] + jnp.sum(cnt, axis=1, keepdims=True)


def moe_router(logits_t):
    ne, n = logits_t.shape
    tt = ROUTE_TT
    slot = pl.BlockSpec((TOP_K, tt), lambda i: (0, i))
    return pl.pallas_call(
        _router_kernel,
        grid=(n // tt,),
        in_specs=[pl.BlockSpec((ne, tt), lambda i: (0, i))],
        out_specs=[slot, slot, slot, pl.BlockSpec((ne, LANES), lambda i: (0, 0))],
        out_shape=[jax.ShapeDtypeStruct((TOP_K, n), jnp.int32), jax.ShapeDtypeStruct((TOP_K, n), F32),
                   jax.ShapeDtypeStruct((TOP_K, n), jnp.int32), jax.ShapeDtypeStruct((ne, LANES), F32)],
        compiler_params=_cparams(("arbitrary",)),
        name="moe_router",
    )(logits_t)


MOE_TM = 512


def _dispatch_kernel(dest_ref, zrow_ref, hf_ref, xs_ref, zbuf, sem, zsem):
    tt = hf_ref.shape[0] // SUBLANES
    n = dest_ref.shape[0] // TOP_K
    base = pl.program_id(0) * tt

    def tile(ref, row):
        return ref.at[pl.ds(pl.multiple_of(row * SUBLANES, SUBLANES), SUBLANES), :]

    @pl.when(pl.program_id(0) == 0)
    def _():
        zbuf[...] = jnp.zeros(zbuf.shape, zbuf.dtype)

        def fill(j):
            first = pl.multiple_of(jnp.maximum(zrow_ref[j], 0) * SUBLANES, MOE_TM * SUBLANES)
            return pltpu.make_async_copy(zbuf, xs_ref.at[pl.ds(first, MOE_TM * SUBLANES), :], zsem)

        for j in range(zrow_ref.shape[0]):
            pl.when(zrow_ref[j] >= 0)(fill(j).start)
        for j in range(zrow_ref.shape[0]):
            pl.when(zrow_ref[j] >= 0)(fill(j).wait)

    def issue(t, carry):
        for k in range(TOP_K):
            pltpu.make_async_copy(tile(hf_ref, t), tile(xs_ref, dest_ref[k * n + base + t]), sem).start()
        return carry

    lax.fori_loop(0, tt, issue, 0, unroll=8)
    for _ in range(TOP_K):
        pltpu.make_async_copy(hf_ref, xs_ref.at[pl.ds(0, tt * SUBLANES), :], sem).wait()


DISPATCH_TT = 512


def moe_dispatch(dest_flat, zero_rows, hf, n_rows):
    n = hf.shape[0] // SUBLANES
    tt = DISPATCH_TT
    grid_spec = pltpu.PrefetchScalarGridSpec(
        num_scalar_prefetch=2,
        grid=(n // tt,),
        in_specs=[pl.BlockSpec((tt * SUBLANES, LANES), lambda i, dest, zrow: (i, 0))],
        out_specs=pl.BlockSpec(memory_space=pl.ANY),
        scratch_shapes=[pltpu.VMEM((MOE_TM * SUBLANES, LANES), F32), pltpu.SemaphoreType.DMA,
                        pltpu.SemaphoreType.DMA],
    )
    return pl.pallas_call(
        _dispatch_kernel,
        grid_spec=grid_spec,
        out_shape=jax.ShapeDtypeStruct((n_rows * SUBLANES, LANES), F32),
        compiler_params=_cparams(("arbitrary",)),
        name="moe_dispatch",
    )(dest_flat, zero_rows, hf)


def _expert_kernel(be_ref, nu_ref, xs_ref, wgu_ref, bgu_ref, wdn_ref, bdn_ref, y_ref,
                   wgu_bf, wdn_bf):
    i = pl.program_id(0)
    used = i < nu_ref[0]

    @pl.when(used & ((i == 0) | (be_ref[i] != be_ref[jnp.maximum(i - 1, 0)])))
    def _():
        wgu_bf[...] = wgu_ref[0].astype(BF16)
        wdn_bf[...] = wdn_ref[0].astype(BF16)

    @pl.when(used)
    def _():
        hgu = _dot(_load_token_tiles(xs_ref).astype(BF16), wgu_bf[...]) + bgu_ref[0]
        de = hgu.shape[1] // 2
        glu = jnp.minimum(hgu[:, :de], SWIGLU_LIMIT)
        lin = jnp.clip(hgu[:, de:], -SWIGLU_LIMIT, SWIGLU_LIMIT)
        act = glu * jax.nn.sigmoid(SWIGLU_ALPHA * glu) * (lin + 1.0)
        _store_token_tiles(y_ref, _dot(act.astype(BF16), wdn_bf[...]) + bdn_ref[0])

    @pl.when(jnp.logical_not(used))
    def _():
        y_ref[...] = jnp.zeros(y_ref.shape, y_ref.dtype)


EXPERT_VMEM_LIMIT = 56 * 1024 * 1024


def expert_ffn(layer, blk_exp, n_used, xs, w_gu, b_gu, w_dn, b_dn):
    r = xs.shape[0] // SUBLANES
    d = SUBLANES * LANES
    nl, ne, _, wide = w_gu.shape
    w_gu = w_gu.reshape(nl * ne, d, wide)
    w_dn = w_dn.reshape(nl * ne, wide // 2, d)
    b_gu = b_gu.reshape(nl * ne, wide)
    b_dn = b_dn.reshape(nl * ne, d)
    ne, base = nl * ne, layer * ne
    tm = MOE_TM
    grid_spec = pltpu.PrefetchScalarGridSpec(
        num_scalar_prefetch=2,
        grid=(r // tm,),
        in_specs=[pl.BlockSpec((tm * SUBLANES, LANES), lambda i, be, nu: (i, 0)),
                  pl.BlockSpec((1, d, wide), lambda i, be, nu: (base + be[i], 0, 0)),
                  pl.BlockSpec((1, 1, wide), lambda i, be, nu: (base + be[i], 0, 0)),
                  pl.BlockSpec((1, wide // 2, d), lambda i, be, nu: (base + be[i], 0, 0)),
                  pl.BlockSpec((1, 1, d), lambda i, be, nu: (base + be[i], 0, 0))],
        out_specs=pl.BlockSpec((tm * SUBLANES, LANES), lambda i, be, nu: (i, 0)),
        scratch_shapes=[pltpu.VMEM((d, wide), BF16), pltpu.VMEM((wide // 2, d), BF16)],
    )
    return pl.pallas_call(
        _expert_kernel,
        grid_spec=grid_spec,
        out_shape=jax.ShapeDtypeStruct((r * SUBLANES, LANES), F32),
        compiler_params=pltpu.CompilerParams(dimension_semantics=("arbitrary",),
                                             vmem_limit_bytes=EXPERT_VMEM_LIMIT),
        name="expert_ffn",
    )(blk_exp, n_used, xs, w_gu, b_gu.reshape(ne, 1, wide), w_dn, b_dn.reshape(ne, 1, d))


COMBINE_TT = 256


def _combine_kernel(dest_ref, gate_ref, x_ref, g2_ref, n3_ref, y_ref, o_ref, ybuf, sems):
    tt = x_ref.shape[0]
    n = dest_ref.shape[0] // TOP_K
    i = pl.program_id(0)
    steps = pl.num_programs(0)

    def gather(step, slot):
        base = step * tt

        def issue(t, carry):
            dst = pl.ds(pl.multiple_of(t * SUBLANES, SUBLANES), SUBLANES)
            for k in range(TOP_K):
                row = dest_ref[k * n + base + t]
                src = pl.ds(pl.multiple_of(row * SUBLANES, SUBLANES), SUBLANES)
                pltpu.make_async_copy(y_ref.at[src, :], ybuf.at[slot, k, dst, :], sems.at[slot]).start()
            return carry

        lax.fori_loop(0, tt, issue, 0, unroll=8)

    @pl.when(i == 0)
    def _():
        gather(0, 0)

    @pl.when(i + 1 < steps)
    def _():
        gather(i + 1, (i + 1) % 2)

    slot = i % 2
    for k in range(TOP_K):
        pltpu.make_async_copy(y_ref.at[pl.ds(0, tt * SUBLANES), :], ybuf.at[slot, k], sems.at[slot]).wait()
    gate = gate_ref[...]
    y = gate[:, 0:1] * _load_token_tiles(ybuf.at[slot, 0])
    for k in range(1, TOP_K):
        y = y + gate[:, k:k + 1] * _load_token_tiles(ybuf.at[slot, k])
    o_ref[...] = x_ref[...] + g2_ref[0] * _rms(y, n3_ref[...])


def moe_combine(dest_flat, gate, x, g2, n3, y_rows):
    b, s, d = x.shape
    n = b * s
    tt = COMBINE_TT
    per_batch = s // tt
    grid_spec = pltpu.PrefetchScalarGridSpec(
        num_scalar_prefetch=1,
        grid=(n // tt,),
        in_specs=[pl.BlockSpec((tt, LANES), lambda i, dest: (i, 0)),
                  pl.BlockSpec((tt, d), lambda i, dest: (i, 0)),
                  pl.BlockSpec((1, 1, d), lambda i, dest: (i // per_batch, 0, 0)),
                  pl.BlockSpec((1, d), lambda i, dest: (0, 0)),
                  pl.BlockSpec(memory_space=pl.ANY)],
        out_specs=pl.BlockSpec((tt, d), lambda i, dest: (i, 0)),
        scratch_shapes=[pltpu.VMEM((2, TOP_K, tt * SUBLANES, LANES), F32),
                        pltpu.SemaphoreType.DMA((2,))],
    )
    out = pl.pallas_call(
        _combine_kernel,
        grid_spec=grid_spec,
        out_shape=jax.ShapeDtypeStruct((n, d), F32),
        compiler_params=_cparams(("arbitrary",)),
        name="moe_combine",
    )(dest_flat, gate, x.reshape(n, d), g2, n3, y_rows)
    return out.reshape(b, s, d)


def _dispatch_plan(idx, rank, counts):
    n = idx.shape[1]
    tm = MOE_TM
    padded = (counts + tm - 1) // tm * tm
    pad_end = jnp.cumsum(padded)
    start = pad_end - padded
    experts = jnp.arange(N_EXPERTS, dtype=jnp.int32)
    dest = rank + jnp.sum(jnp.where(idx[..., None] == experts, start, 0), axis=-1)
    n_rows = -(-(n * TOP_K + N_EXPERTS * (tm - 1)) // tm) * tm
    n_blk = n_rows // tm
    blk_start = jnp.arange(n_blk, dtype=jnp.int32) * tm
    blk_exp = jnp.minimum(jnp.sum((pad_end[None, :] <= blk_start[:, None]).astype(jnp.int32), axis=1),
                          N_EXPERTS - 1)
    n_used = (pad_end[-1] // tm).astype(jnp.int32).reshape(1)
    tail = pad_end[-1] + experts * tm
    zero_rows = jnp.concatenate([jnp.where(padded > 0, pad_end - tm, -1),
                                 jnp.where(tail < n_rows, tail, -1)]).astype(jnp.int32)
    return dest.reshape(-1), blk_exp, n_used, zero_rows, n_rows


def _layer(layer, x, mod, norm_g, w_in, cmp_pe, cmp_w1, cmp_w2, grp_g, w_out, router_w, router_b,
           w_gu, b_gu, w_dn, b_dn, tabs, overlap):
    b, s, d = x.shape
    sh1, sc1, g1, sh2, sc2, g2 = (m.reshape(b, 1, d) for m in jnp.split(mod, 6, axis=-1))
    ng = norm_g.reshape(4, 1, d)

    qa, ka, va, qb, kvb, gb, qc, kc, vc = in_projection(x, sc1, sh1, ng[0], _pack_w_in(w_in), tabs)
    oa = moba_attention(qa, ka, va)
    nc = s // NSA_CMP_STRIDE
    kv16 = jnp.stack([kvb[:, :, :HEAD_DIM], kvb[:, :, HEAD_DIM:2 * HEAD_DIM]], axis=1)
    kv16 = kv16.reshape(b, 2, nc, NSA_CMP_STRIDE * HEAD_DIM)
    cmp_kv = nsa_compress(kv16, cmp_pe.reshape(2, 1, NSA_CMP_LEN * HEAD_DIM), cmp_w1, cmp_w2)
    ob = nsa_attention(qb, kvb, gb, cmp_kv, overlap)
    oc = sb_attention(qc, kc, vc)

    rw_hi, rw_lo = _split_bf16(router_w.T)
    rw3 = jnp.concatenate([rw_hi, rw_hi, rw_lo], axis=1)
    x1, hf, logits_t = out_projection(oa, ob, oc, x, grp_g.reshape(1, d), w_out.astype(BF16), g1,
                                      ng[1], ng[2], sc2, sh2, rw3, router_b.reshape(-1, 1))

    n = b * s
    idx, gate, rank, counts = moe_router(logits_t)
    dest, blk_exp, n_used, zero_rows, n_rows = _dispatch_plan(idx, rank, counts[:, 0].astype(jnp.int32))
    xs = moe_dispatch(dest, zero_rows, hf.reshape(n * SUBLANES, LANES), n_rows)
    y_rows = expert_ffn(layer, blk_exp, n_used, xs, w_gu, b_gu, w_dn, b_dn)
    gate = jnp.pad(gate.T, ((0, 0), (0, LANES - TOP_K)))
    return moe_combine(dest, gate, x1, g2, ng[3], y_rows)


def kernel(x, c, ada_w, ada_b, norm_g, w_in, nsa_cmp_pe, nsa_cmp_w1, nsa_cmp_w2, mix_out_g,
           w_out, router_w, router_b, exp_w_gu, exp_b_gu, exp_w_dn, exp_b_dn):
    s = x.shape[1]
    tabs = _rope_tables(s)
    overlap = _nsa_overlap(s)
    mod = ada_modulation(c, ada_w, ada_b)
    for l in range(ada_w.shape[0]):
        x = _layer(l, x, mod[l], norm_g[l], w_in[l], nsa_cmp_pe[l], nsa_cmp_w1[l], nsa_cmp_w2[l],
                   mix_out_g[l], w_out[l], router_w[l], router_b[l], exp_w_gu, exp_b_gu,
                   exp_w_dn, exp_b_dn, tabs, overlap)
    return x
```

```python
import functools

import numpy as np
import jax
import jax.numpy as jnp
from jax import lax
from jax.experimental import pallas as pl
from jax.experimental.pallas import tpu as pltpu

F32 = jnp.float32
BF16 = jnp.bfloat16
HI = lax.Precision.HIGHEST

D_MODEL = 1024
N_HEADS = 16
HEAD_DIM = 64
MOBA_HEADS = 4
NSA_HEADS = 6
SB_HEADS = 6
MOBA_W = MOBA_HEADS * HEAD_DIM
NSA_W = NSA_HEADS * HEAD_DIM
SB_W = SB_HEADS * HEAD_DIM
ROPE_DIM = 16
ROPE_THETA = 500000.0
MOBA_BLOCK = 256
MOBA_TOPK = 3
NSA_CMP_LEN = 32
NSA_CMP_STRIDE = 16
NSA_SLC_BLOCK = 64
NSA_SLC_TOPK = 16
NSA_WINDOW = 512
N_EXPERTS = 32
TOP_K = 4
SWIGLU_LIMIT = 7.0
SWIGLU_ALPHA = 1.702
RMS_EPS = 1e-6
NEG_INF = -1e30
SEL_FORCE = 1e4
SCALE = HEAD_DIM ** -0.5
LOG2E = 1.4426950408889634

LANES = 128
VMEM_LIMIT = 48 * 1024 * 1024


def _cparams(sem):
    return pltpu.CompilerParams(dimension_semantics=sem, vmem_limit_bytes=VMEM_LIMIT)


def _dot(a, b):
    return jnp.dot(a, b, preferred_element_type=F32)


def _dot_nt(a, b):
    return lax.dot_general(a, b, (((1,), (1,)), ((), ())), preferred_element_type=F32)


def _split_bf16(x):
    hi = x.astype(BF16)
    return hi, (x - hi.astype(F32)).astype(BF16)


def _dot_split(a, b):
    (a_hi, a_lo), (b_hi, b_lo) = _split_bf16(a), _split_bf16(b)
    return _dot(jnp.concatenate([a_hi, a_lo, a_hi], axis=1), jnp.concatenate([b_hi, b_hi, b_lo], axis=0))


def _dot_nt_split(a, b):
    (a_hi, a_lo), (b_hi, b_lo) = _split_bf16(a), _split_bf16(b)
    return _dot_nt(jnp.concatenate([a_hi, a_lo, a_hi], axis=1), jnp.concatenate([b_hi, b_hi, b_lo], axis=1))


SB_TILE = 256


SB_GROUP_W = SB_W
SB_LAG = 2


def _sb_kernel(q_ref, k_ref, v_ref, o_ref, ks_ref, vs_ref, run_ref, acc_ref):
    T = SB_TILE
    H = ks_ref.shape[0]
    qi = pl.program_id(2)

    @pl.when(qi == 0)
    def _():
        zeros = jnp.zeros((ks_ref.shape[1], HEAD_DIM), F32)
        for h in range(H):
            ks_ref[h] = k_ref[0, :, h * HEAD_DIM:(h + 1) * HEAD_DIM].astype(BF16)
            vh = v_ref[0, :, h * HEAD_DIM:(h + 1) * HEAD_DIM]
            vs_ref[h] = jnp.concatenate([vh, zeros] if h % 2 == 0 else [zeros, vh], axis=1).astype(BF16)

    run_ref[...] = jnp.zeros(run_ref.shape, F32)
    acc_ref[...] = jnp.zeros(acc_ref.shape, F32)

    row = lax.broadcasted_iota(jnp.int32, (T, T), 0)
    col = lax.broadcasted_iota(jnp.int32, (T, T), 1)
    incl = (row >= col).astype(BF16)
    sum_rhs = jnp.concatenate([incl, incl], axis=0)
    diag_mask = col < row
    qs = [(q_ref[0, :, h * HEAD_DIM:(h + 1) * HEAD_DIM] * (SCALE * LOG2E)).astype(BF16)
          for h in range(H)]
    sign = jnp.uint32(0x80000000)

    def scores(h, j):
        return _dot_nt(qs[h], ks_ref[h, pl.ds(pl.multiple_of(j * T, T), T), :])

    def neg_log2_keep(z, masked):
        neg_abs = lax.bitcast_convert_type(lax.bitcast_convert_type(z, jnp.uint32) | sign, F32)
        nlk = jnp.maximum(z, 0.0) + jnp.log2(1.0 + jnp.exp2(neg_abs))
        return jnp.where(diag_mask, nlk, 0.0) if masked else nlk

    def suffix_sums(nlk):
        hi = nlk.astype(BF16)
        lo = (nlk - hi.astype(F32)).astype(BF16)
        return _dot(jnp.concatenate([hi, lo], axis=1), sum_rhs)

    def weights(z, sums, run, masked):
        x = z - sums - run
        if masked:
            x = jnp.where(diag_mask, x, NEG_INF)
        return jnp.exp2(x.astype(BF16))

    def pv(h, p, j):
        return _dot(p, vs_ref[h, pl.ds(pl.multiple_of(j * T, T), T), :])

    def key_tile(j, masked):
        zs, sums = {0: scores(0, j)}, {}
        for t in range(H + SB_LAG):
            if t + 1 < H:
                zs[t + 1] = scores(t + 1, j)
            if t < H:
                sums[t] = suffix_sums(neg_log2_keep(zs[t], masked))
            h = t - SB_LAG
            if h >= 0:
                run = run_ref[h]
                p = weights(zs.pop(h), sums[h], run, masked)
                run_ref[h] = run + sums.pop(h)[:, 0:1]
                acc_ref[h // 2] = acc_ref[h // 2] + pv(h, p, j)

    key_tile(qi, True)

    def past(i, carry):
        key_tile(qi - 1 - i, False)
        return carry

    lax.fori_loop(0, qi, past, 0)
    for g in range(H // 2):
        o_ref[0, :, g * LANES:(g + 1) * LANES] = acc_ref[g]


def sb_attention(q, k, v):
    b, s, w = q.shape
    T = SB_TILE
    gw = SB_GROUP_W
    nh = gw // HEAD_DIM
    qspec = pl.BlockSpec((1, T, gw), lambda bi, p, i: (bi, i, p))
    kvspec = pl.BlockSpec((1, s, gw), lambda bi, p, i: (bi, 0, p))
    return pl.pallas_call(
        _sb_kernel,
        grid=(b, w // gw, s // T),
        in_specs=[qspec, kvspec, kvspec],
        out_specs=qspec,
        out_shape=jax.ShapeDtypeStruct((b, s, w), F32),
        scratch_shapes=[pltpu.VMEM((nh, s, HEAD_DIM), BF16), pltpu.VMEM((nh, s, LANES), BF16),
                        pltpu.VMEM((nh, T, 1), F32), pltpu.VMEM((nh // 2, T, LANES), F32)],
        compiler_params=_cparams(("arbitrary", "arbitrary", "arbitrary")),
        name="sb_attention",
    )(q, k, v)


def _rank_before_t(vals, n):
    idx = lax.broadcasted_iota(jnp.int32, vals.shape, 0)
    rank = jnp.zeros(vals.shape, F32)
    for j2 in range(n):
        other = vals[j2:j2 + 1, :]
        ahead = (other > vals) | ((other == vals) & (idx > j2))
        rank = rank + jnp.where(ahead, 1.0, 0.0)
    return rank


MOBA_KT = 2 * MOBA_BLOCK
MOBA_GROUP_W = MOBA_W


def _moba_kernel(q_ref, k_ref, v_ref, o_ref, ks_ref, vs_ref, km_ref):
    T, KT = MOBA_BLOCK, MOBA_KT
    H = ks_ref.shape[0]
    nb = km_ref.shape[1]
    s_len = ks_ref.shape[1]
    qi = pl.program_id(2)

    @pl.when(qi == 0)
    def _():
        key_blk = lax.broadcasted_iota(jnp.int32, (s_len, HEAD_DIM), 0) // T
        onehot = (lax.broadcasted_iota(jnp.int32, (s_len, HEAD_DIM), 1) == key_blk).astype(F32)
        ones_col = (lax.broadcasted_iota(jnp.int32, (s_len, HEAD_DIM), 1) == 0).astype(F32)
        for h in range(H):
            kh = k_ref[0, :, h * HEAD_DIM:(h + 1) * HEAD_DIM]
            ks_ref[h] = jnp.concatenate([kh, onehot], axis=1).astype(BF16)
            vs_ref[h] = jnp.concatenate([v_ref[0, :, h * HEAD_DIM:(h + 1) * HEAD_DIM], ones_col],
                                        axis=1).astype(BF16)
            km_ref[h] = jnp.mean(kh.reshape(nb, T, HEAD_DIM), axis=1)

    row = lax.broadcasted_iota(jnp.int32, (T, T), 0)
    col = lax.broadcasted_iota(jnp.int32, (T, T), 1)
    causal = col <= row
    blk_t = lax.broadcasted_iota(jnp.int32, (nb, T), 0)
    start = pl.multiple_of(qi * T, T)

    qfs = [q_ref[0, :, h * HEAD_DIM:(h + 1) * HEAD_DIM] for h in range(H)]
    qss = [qf * SCALE for qf in qfs]
    s_own = [_dot_nt(qss[h].astype(BF16), ks_ref[h, pl.ds(start, T), 0:HEAD_DIM]) for h in range(H)]
    q_aug, state = [], []
    for h in range(H):
        gate = _dot_nt_split(km_ref[h], qfs[h])
        gate = jnp.where(blk_t < qi, gate, NEG_INF)
        sel = (_rank_before_t(gate, nb) < float(MOBA_TOPK)) & (gate > 0.5 * NEG_INF)
        selb = jnp.where(sel, 0.0, NEG_INF)
        selb = jnp.concatenate([selb, jnp.full((LANES - nb, T), NEG_INF, F32)], axis=0).T
        q_aug.append(jnp.concatenate([qss[h], selb[:, :HEAD_DIM]], axis=1).astype(BF16))

        s = jnp.where(causal, s_own[h], NEG_INF)
        m = jnp.max(s, axis=1, keepdims=True)
        p = jnp.exp((s - m).astype(BF16))
        state += [m, _dot(p, vs_ref[h, pl.ds(start, T), :])]

    def past(i, carry):
        st = pl.multiple_of(i * KT, KT)
        ss = [_dot_nt(q_aug[h], ks_ref[h, pl.ds(st, KT), :]) for h in range(H)]
        out = []
        for h in range(H):
            m, acc = carry[2 * h:2 * h + 2]
            m_new = jnp.maximum(m, jnp.max(ss[h], axis=1, keepdims=True))
            alpha = jnp.exp(m - m_new)
            p = jnp.exp((ss[h] - m_new).astype(BF16))
            out += [m_new, alpha * acc + _dot(p, vs_ref[h, pl.ds(st, KT), :])]
        return tuple(out)

    per = KT // T
    state = lax.fori_loop(0, (qi + per - 1) // per, past, tuple(state))
    outs = [state[2 * h + 1][:, :HEAD_DIM] / jnp.maximum(state[2 * h + 1][:, HEAD_DIM:HEAD_DIM + 1], 1e-30)
            for h in range(H)]
    o_ref[0] = jnp.concatenate(outs, axis=1)


def moba_attention(q, k, v):
    b, s, w = q.shape
    T = MOBA_BLOCK
    nb = s // T
    assert nb <= HEAD_DIM and s % MOBA_KT == 0
    gw = MOBA_GROUP_W
    nh = gw // HEAD_DIM
    qspec = pl.BlockSpec((1, T, gw), lambda bi, p, i: (bi, i, p))
    kvspec = pl.BlockSpec((1, s, gw), lambda bi, p, i: (bi, 0, p))
    return pl.pallas_call(
        _moba_kernel,
        grid=(b, w // gw, nb),
        in_specs=[qspec, kvspec, kvspec],
        out_specs=qspec,
        out_shape=jax.ShapeDtypeStruct((b, s, w), F32),
        scratch_shapes=[pltpu.VMEM((nh, s, LANES), BF16), pltpu.VMEM((nh, s, LANES), BF16),
                        pltpu.VMEM((nh, nb, HEAD_DIM), F32)],
        compiler_params=_cparams(("arbitrary", "arbitrary", "arbitrary")),
        name="moba_attention",
    )(q, k, v)


def _nsa_compress_kernel(x_ref, pe_ref, w1_ref, w2_ref, o_ref):
    nc = x_ref.shape[2]
    half = NSA_CMP_STRIDE * HEAD_DIM
    x = x_ref[0, 0]
    w1 = w1_ref[0]
    first = _dot_split(x, w1[:half])
    second = _dot_split(x, w1[half:])
    pe = jnp.broadcast_to(pe_ref[0], (8, 2 * half))
    peb = _dot_split(pe, w1)[0:1]
    pre = first + pltpu.roll(second, nc - 1, 0) + peb
    hid = pre * jax.nn.sigmoid(pre)
    o_ref[0, 0] = _dot_split(hid, w2_ref[0])


def nsa_compress(kv16, pe, w1, w2):
    b, _, nc, wide = kv16.shape
    return pl.pallas_call(
        _nsa_compress_kernel,
        grid=(b, 2),
        in_specs=[pl.BlockSpec((1, 1, nc, wide), lambda bi, i: (bi, i, 0, 0)),
                  pl.BlockSpec((1, 1, 2 * wide), lambda bi, i: (i, 0, 0)),
                  pl.BlockSpec((1, 2 * wide, HEAD_DIM), lambda bi, i: (i, 0, 0)),
                  pl.BlockSpec((1, HEAD_DIM, HEAD_DIM), lambda bi, i: (i, 0, 0))],
        out_specs=pl.BlockSpec((1, 1, nc, HEAD_DIM), lambda bi, i: (bi, i, 0, 0)),
        out_shape=jax.ShapeDtypeStruct((b, 2, nc, HEAD_DIM), F32),
        compiler_params=_cparams(("arbitrary", "arbitrary")),
        name="nsa_compress",
    )(kv16, pe, w1, w2)


NSA_TQ = 256
NSA_KT = 512
NSA_SPAN = NSA_WINDOW + NSA_TQ
NSA_CHAIN_HEADS = 2
NSA_WINDOW_HEADS = 2


def _softmax_rows(s, mask):
    s = jnp.where(mask, s, NEG_INF)
    m = jnp.max(s, axis=-1, keepdims=True)
    e = jnp.where(mask, jnp.exp(s - m), 0.0)
    return e / jnp.maximum(jnp.sum(e, axis=-1, keepdims=True), 1e-30)


def _nsa_kernel(q_ref, cmp_ref, slc_ref, win_ref, g_ref, ov_ref, o_ref,
                ksl_ref, vsl_ref, kw_ref, vw_ref, kc_ref):
    TQ, KT, H = NSA_TQ, NSA_KT, NSA_HEADS
    ns, nc = ov_ref.shape
    s_len = ksl_ref.shape[0]
    qi = pl.program_id(1)
    t0 = qi * TQ

    @pl.when(qi == 0)
    def _():
        lane = lax.broadcasted_iota(jnp.int32, (s_len, HEAD_DIM), 1)
        key_blk = lax.broadcasted_iota(jnp.int32, (s_len, HEAD_DIM), 0) // NSA_SLC_BLOCK
        onehot = (lane == key_blk).astype(F32)
        ones_col = (lane == 0).astype(F32)
        ksl_ref[...] = jnp.concatenate([slc_ref[0, :, :HEAD_DIM], onehot], axis=1).astype(BF16)
        vsl_ref[...] = jnp.concatenate([slc_ref[0, :, HEAD_DIM:], ones_col], axis=1).astype(BF16)
        kw_ref[...] = win_ref[0, :, :HEAD_DIM].astype(BF16)
        vw_ref[...] = jnp.concatenate([win_ref[0, :, HEAD_DIM:], ones_col], axis=1).astype(BF16)
        k_hi, k_lo = _split_bf16(cmp_ref[0, 0])
        kc_ref[...] = jnp.concatenate([k_hi, k_hi, k_lo], axis=1)

    qf = jnp.concatenate([q_ref[0, :, h * HEAD_DIM:(h + 1) * HEAD_DIM] for h in range(H)],
                         axis=0) * SCALE
    q, q_lo = _split_bf16(qf)

    def normalized(acc):
        return acc[:, :HEAD_DIM] / jnp.maximum(acc[:, HEAD_DIM:HEAD_DIM + 1], 1e-30)

    w0 = pl.multiple_of(jnp.maximum(t0 - NSA_WINDOW, 0), TQ)
    wpos = w0 + lax.broadcasted_iota(jnp.int32, (TQ, NSA_SPAN), 1)
    tq_w = t0 + lax.broadcasted_iota(jnp.int32, (TQ, NSA_SPAN), 0)
    mask_w = ((wpos <= tq_w) & (wpos > tq_w - NSA_WINDOW))[None]
    o_w = []
    for c in range(H // NSA_WINDOW_HEADS):
        rows = slice(c * NSA_WINDOW_HEADS * TQ, (c + 1) * NSA_WINDOW_HEADS * TQ)
        s_w = _dot_nt(q[rows], kw_ref[pl.ds(w0, NSA_SPAN), :]).reshape(NSA_WINDOW_HEADS, TQ, NSA_SPAN)
        s_w = jnp.where(mask_w, s_w, NEG_INF)
        p_w = jnp.exp((s_w - jnp.max(s_w, axis=-1, keepdims=True)).astype(BF16))
        o_w.append(normalized(_dot(p_w.reshape(NSA_WINDOW_HEADS * TQ, NSA_SPAN),
                                   vw_ref[pl.ds(w0, NSA_SPAN), :])))
    o_w = jnp.concatenate(o_w, axis=0)

    tq_c = t0 + lax.broadcasted_iota(jnp.int32, (TQ, nc), 0)
    n_c = lax.broadcasted_iota(jnp.int32, (TQ, nc), 1)
    mask_c = (n_c * NSA_CMP_STRIDE + (NSA_CMP_LEN - 1) <= tq_c) & (n_c < nc - 1)
    s_c = _dot_nt(jnp.concatenate([q, q_lo, q], axis=1), kc_ref[...])
    p_c = _softmax_rows(s_c.reshape(H, TQ, nc), mask_c[None])
    o_c = _dot(p_c.reshape(H * TQ, nc).astype(BF16), cmp_ref[0, 1].astype(BF16))

    p_sum = jnp.sum(p_c, axis=0)
    p1, p2 = _split_bf16(p_sum)
    p3 = (p_sum - p1.astype(F32) - p2.astype(F32)).astype(BF16)
    ov = ov_ref[...].astype(BF16)
    imp = _dot_nt(jnp.concatenate([ov, ov, ov], axis=1), jnp.concatenate([p1, p2, p3], axis=1))
    tq_s = t0 + lax.broadcasted_iota(jnp.int32, (ns, TQ), 1)
    blk = lax.broadcasted_iota(jnp.int32, (ns, TQ), 0)
    own = tq_s // NSA_SLC_BLOCK
    forced = (blk == 0) | (blk == own) | (blk == own - 1)
    imp = jnp.where(forced, SEL_FORCE, imp)
    imp = jnp.where(blk <= own, imp, NEG_INF)
    sel = (_rank_before_t(imp, ns) < float(min(NSA_SLC_TOPK, ns))) & (imp > 0.5 * NEG_INF)
    selb = jnp.where(sel, 0.0, NEG_INF)
    if ns < LANES:
        selb = jnp.concatenate([selb, jnp.full((LANES - ns, TQ), NEG_INF, F32)], axis=0)
    selb = selb.T[:, :HEAD_DIM]
    q_aug = jnp.concatenate([qf, jnp.concatenate([selb] * H, axis=0)], axis=1).astype(BF16)

    jd = t0 // KT
    start = pl.multiple_of(jd * KT, KT)
    kpos = start + lax.broadcasted_iota(jnp.int32, (TQ, KT), 1)
    tq_k = t0 + lax.broadcasted_iota(jnp.int32, (TQ, KT), 0)
    HC = NSA_CHAIN_HEADS
    G, R = H // HC, HC * TQ
    qa = [q_aug[c * R:(c + 1) * R] for c in range(G)]
    causal_k = (kpos <= tq_k)[None]

    def scores(c, st):
        return _dot_nt(qa[c], ksl_ref[pl.ds(st, KT), :]).reshape(HC, TQ, KT)

    s_own = [scores(c, start) for c in range(G)]
    state = []
    for c in range(G):
        s = jnp.where(causal_k, s_own[c], NEG_INF)
        m = jnp.max(s, axis=-1, keepdims=True)
        p = jnp.exp((s - m).astype(BF16))
        state += [m, _dot(p.reshape(R, KT), vsl_ref[pl.ds(start, KT), :])]

    def past(j, carry):
        st = pl.multiple_of(j * KT, KT)
        ss = {0: scores(0, st)}
        out = []
        for c in range(G):
            if c + 1 < G:
                ss[c + 1] = scores(c + 1, st)
            m, acc = carry[2 * c:2 * c + 2]
            m_new = jnp.maximum(m, jnp.max(ss[c], axis=-1, keepdims=True))
            alpha = jnp.exp(m - m_new)
            p = jnp.exp((ss[c] - m_new).astype(BF16))
            pv = _dot(p.reshape(R, KT), vsl_ref[pl.ds(st, KT), :])
            out += [m_new, alpha.reshape(R, 1) * acc + pv]
        return tuple(out)

    state = lax.fori_loop(0, jd, past, tuple(state))
    o_s = jnp.concatenate([normalized(state[2 * c + 1]) for c in range(G)], axis=0)

    g = g_ref[0]
    outs = []
    for h in range(H):
        rows = slice(h * TQ, (h + 1) * TQ)
        outs.append(g[:, 3 * h:3 * h + 1] * o_c[rows] + g[:, 3 * h + 1:3 * h + 2] * o_s[rows]
                    + g[:, 3 * h + 2:3 * h + 3] * o_w[rows])
    o_ref[0] = jnp.concatenate(outs, axis=1)


def nsa_attention(q, kv, gates, cmp_kv, overlap):
    b, s, w = q.shape
    ns, nc = overlap.shape
    assert ns <= HEAD_DIM
    TQ = NSA_TQ
    return pl.pallas_call(
        _nsa_kernel,
        grid=(b, s // TQ),
        in_specs=[pl.BlockSpec((1, TQ, w), lambda bi, i: (bi, i, 0)),
                  pl.BlockSpec((1, 2, nc, HEAD_DIM), lambda bi, i: (bi, 0, 0, 0)),
                  pl.BlockSpec((1, s, LANES), lambda bi, i: (bi, 0, 1)),
                  pl.BlockSpec((1, s, LANES), lambda bi, i: (bi, 0, 2)),
                  pl.BlockSpec((1, TQ, LANES), lambda bi, i: (bi, i, 0)),
                  pl.BlockSpec((ns, nc), lambda bi, i: (0, 0))],
        out_specs=pl.BlockSpec((1, TQ, w), lambda bi, i: (bi, i, 0)),
        out_shape=jax.ShapeDtypeStruct((b, s, w), F32),
        scratch_shapes=[pltpu.VMEM((s, LANES), BF16), pltpu.VMEM((s, LANES), BF16),
                        pltpu.VMEM((s, HEAD_DIM), BF16), pltpu.VMEM((s, LANES), BF16),
                        pltpu.VMEM((nc, 3 * HEAD_DIM), BF16)],
        compiler_params=_cparams(("arbitrary", "arbitrary")),
        name="nsa_attention",
    )(q, cmp_kv, kv, kv, gates, overlap)


def _nsa_overlap(s):
    nc = s // NSA_CMP_STRIDE
    ns = s // NSA_SLC_BLOCK
    cstart = np.arange(nc) * NSA_CMP_STRIDE
    cend = cstart + NSA_CMP_LEN - 1
    sstart = np.arange(ns) * NSA_SLC_BLOCK
    ov = (cstart[:, None] <= sstart[None, :] + NSA_SLC_BLOCK - 1) & (cend[:, None] >= sstart[None, :])
    ov[nc - 1] = False
    return jnp.asarray(ov.T.astype(np.float32))


def _mod_kernel(c_ref, w_ref, b_ref, o_ref):
    c = c_ref[...]
    act = c * jax.nn.sigmoid(c)
    o_ref[0] = jnp.dot(act, w_ref[0], precision=HI, preferred_element_type=F32) + b_ref[0]


def ada_modulation(c, ada_w, ada_b):
    nl, d, wide = ada_w.shape
    b = c.shape[0]
    tn = D_MODEL
    return pl.pallas_call(
        _mod_kernel,
        grid=(nl, wide // tn),
        in_specs=[pl.BlockSpec((b, d), lambda l, j: (0, 0)),
                  pl.BlockSpec((1, d, tn), lambda l, j: (l, 0, j)),
                  pl.BlockSpec((1, 1, tn), lambda l, j: (l, 0, j))],
        out_specs=pl.BlockSpec((1, b, tn), lambda l, j: (l, 0, j)),
        out_shape=jax.ShapeDtypeStruct((nl, b, wide), F32),
        compiler_params=_cparams(("arbitrary", "arbitrary")),
        name="ada_modulation",
    )(c, ada_w, ada_b.reshape(nl, 1, wide))


def _rms(x, g):
    return x * lax.rsqrt(jnp.mean(x * x, axis=-1, keepdims=True) + RMS_EPS) * g


_GATE_PAD = LANES
_COLS = {}
_off = 0
for _name, _w in (("qa", MOBA_W), ("ka", MOBA_W), ("va", MOBA_W), ("qb", NSA_W), ("kvb", NSA_W),
                  ("gb", _GATE_PAD), ("qc", SB_W), ("kc", SB_W), ("vc", SB_W)):
    _COLS[_name] = (_off, _w)
    _off += _w
IN_W_PACKED = _off
PROJ_TM = 1024


def _rope_block(p, cs, sm, sp):
    return p * cs + pltpu.roll(p, LANES - ROPE_DIM // 2, 1) * sm + pltpu.roll(p, ROPE_DIM // 2, 1) * sp


def _in_proj_kernel(x_ref, sc_ref, sh_ref, g_ref, w_ref, cqk_ref, mqk_ref, pqk_ref,
                    ckv_ref, mkv_ref, pkv_ref,
                    qa_ref, ka_ref, va_ref, qb_ref, kvb_ref, gb_ref, qc_ref, kc_ref, vc_ref):
    hm = _rms(x_ref[0], g_ref[...]) * (1.0 + sc_ref[0]) + sh_ref[0]
    p = _dot(hm.astype(BF16), w_ref[...])
    qk = (cqk_ref[...], mqk_ref[...], pqk_ref[...])
    kv = (ckv_ref[...], mkv_ref[...], pkv_ref[...])

    def emit(ref, name, tabs):
        off, w = _COLS[name]
        for j in range(w // LANES):
            blk = p[:, off + j * LANES: off + (j + 1) * LANES]
            if tabs is not None:
                blk = _rope_block(blk, *tabs)
            ref[0, :, j * LANES:(j + 1) * LANES] = blk

    emit(qa_ref, "qa", qk)
    emit(ka_ref, "ka", qk)
    emit(va_ref, "va", None)
    emit(qb_ref, "qb", qk)
    emit(kvb_ref, "kvb", kv)
    emit(qc_ref, "qc", None)
    emit(kc_ref, "kc", None)
    emit(vc_ref, "vc", None)
    off, w = _COLS["gb"]
    gb_ref[0] = jax.nn.sigmoid(p[:, off:off + w])


def in_projection(x, sc, sh, g, w_packed, tabs):
    b, s, d = x.shape
    tm = PROJ_TM
    row = lambda w: pl.BlockSpec((1, tm, w), lambda bi, i: (bi, i, 0))
    vec = pl.BlockSpec((1, 1, d), lambda bi, i: (bi, 0, 0))
    tab = pl.BlockSpec((tm, LANES), lambda bi, i: (i, 0))
    names = ("qa", "ka", "va", "qb", "kvb", "gb", "qc", "kc", "vc")
    return pl.pallas_call(
        _in_proj_kernel,
        grid=(b, s // tm),
        in_specs=[row(d), vec, vec, pl.BlockSpec((1, d), lambda bi, i: (0, 0)),
                  pl.BlockSpec((d, IN_W_PACKED), lambda bi, i: (0, 0))] + [tab] * 6,
        out_specs=[row(_COLS[n][1]) for n in names],
        out_shape=[jax.ShapeDtypeStruct((b, s, _COLS[n][1]), F32) for n in names],
        compiler_params=_cparams(("arbitrary", "arbitrary")),
        name="in_projection",
    )(x, sc, sh, g, w_packed, *tabs)


def _pack_w_in(w_in):
    widths = (MOBA_W, MOBA_W, MOBA_W, NSA_W, NSA_W, 3 * NSA_HEADS, SB_W, SB_W, SB_W)
    offs = np.cumsum((0,) + widths)
    parts = []
    for i, w in enumerate(widths):
        blk = w_in[:, offs[i]:offs[i + 1]]
        if w == 3 * NSA_HEADS:
            blk = jnp.pad(blk, ((0, 0), (0, _GATE_PAD - w)))
        parts.append(blk)
    return jnp.concatenate(parts, axis=1).astype(BF16)


def _rope_tables(s):
    half = ROPE_DIM // 2
    inv_freq = ROPE_THETA ** (-jnp.arange(0, ROPE_DIM, 2, dtype=F32) / ROPE_DIM)
    ang = jnp.arange(s, dtype=F32)[:, None] * inv_freq[None, :]
    cos, sin = jnp.cos(ang), jnp.sin(ang)
    zeros = jnp.zeros((s, HEAD_DIM - ROPE_DIM), F32)
    z8 = jnp.zeros((s, half), F32)
    cs_h = jnp.concatenate([cos, cos, zeros + 1.0], axis=1)
    sm_h = jnp.concatenate([-sin, z8, zeros], axis=1)
    sp_h = jnp.concatenate([z8, sin, zeros], axis=1)
    ident = (jnp.ones((s, HEAD_DIM), F32), jnp.zeros((s, HEAD_DIM), F32), jnp.zeros((s, HEAD_DIM), F32))
    qk = tuple(jnp.concatenate([t, t], axis=1) for t in (cs_h, sm_h, sp_h))
    kv = tuple(jnp.concatenate([t, i], axis=1) for t, i in zip((cs_h, sm_h, sp_h), ident))
    return qk + kv


OUT_TM = 1024
SUBLANES = 8


def _store_token_tiles(ref, val):
    rows = val.shape[0]
    for j in range(SUBLANES):
        ref[pl.ds(j, rows, stride=SUBLANES), :] = val[:, j * LANES:(j + 1) * LANES]


def _load_token_tiles(ref):
    rows = ref.shape[0] // SUBLANES
    return jnp.concatenate([ref[pl.ds(j, rows, stride=SUBLANES), :] for j in range(SUBLANES)], axis=1)


def _out_proj_kernel(oa_ref, ob_ref, oc_ref, x_ref, gg_ref, w_ref, g1_ref, n1_ref, n2_ref,
                     sc_ref, sh_ref, rw_ref, rb_ref, x1_ref, hf_ref, lg_ref):
    gg = gg_ref[...]
    y = jnp.concatenate([_rms(oa_ref[0], gg[:, :MOBA_W]),
                         _rms(ob_ref[0], gg[:, MOBA_W:MOBA_W + NSA_W]),
                         _rms(oc_ref[0], gg[:, MOBA_W + NSA_W:])], axis=1)
    y = _dot(y.astype(BF16), w_ref[...])
    x1 = x_ref[0] + g1_ref[0] * _rms(y, n1_ref[...])
    x1_ref[0] = x1
    hf = _rms(x1, n2_ref[...]) * (1.0 + sc_ref[0]) + sh_ref[0]
    _store_token_tiles(hf_ref.at[0], hf)
    h_hi, h_lo = _split_bf16(hf)
    lg_ref[...] = _dot_nt(rw_ref[...], jnp.concatenate([h_hi, h_lo, h_hi], axis=1)) + rb_ref[...]


def out_projection(oa, ob, oc, x, grp_g, w_out, g1, n1, n2, sc2, sh2, rw_t, rb):
    b, s, d = x.shape
    assert d == SUBLANES * LANES
    ne = rw_t.shape[0]
    tm = OUT_TM
    steps = s // tm
    row = lambda w: pl.BlockSpec((1, tm, w), lambda bi, i: (bi, i, 0))
    vec = pl.BlockSpec((1, 1, d), lambda bi, i: (bi, 0, 0))
    cst = lambda r, w: pl.BlockSpec((r, w), lambda bi, i: (0, 0))
    return pl.pallas_call(
        _out_proj_kernel,
        grid=(b, steps),
        in_specs=[row(MOBA_W), row(NSA_W), row(SB_W), row(d), cst(1, d), cst(d, d), vec,
                  cst(1, d), cst(1, d), vec, vec, cst(ne, 3 * d), cst(ne, 1)],
        out_specs=[row(d), pl.BlockSpec((1, tm * SUBLANES, LANES), lambda bi, i: (bi, i, 0)),
                   pl.BlockSpec((ne, tm), lambda bi, i: (0, bi * steps + i))],
        out_shape=[jax.ShapeDtypeStruct((b, s, d), F32),
                   jax.ShapeDtypeStruct((b, s * SUBLANES, LANES), F32),
                   jax.ShapeDtypeStruct((ne, b * s), F32)],
        compiler_params=_cparams(("arbitrary", "arbitrary")),
        name="out_projection",
    )(oa, ob, oc, x, grp_g, w_out, g1, n1, n2, sc2, sh2, rw_t, rb)


ROUTE_TT = 512


def _router_kernel(lg_ref, idx_ref, gate_ref, rank_ref, cnt_ref):
    ne, tt = lg_ref.shape

    @pl.when(pl.program_id(0) == 0)
    def _():
        cnt_ref[...] = jnp.zeros(cnt_ref.shape, cnt_ref.dtype)

    v = lg_ref[...]
    erow = lax.broadcasted_iota(jnp.int32, (ne, tt), 0)
    vals, hots, firsts = [], [], []
    for _ in range(TOP_K):
        m = jnp.max(v, axis=0, keepdims=True)
        first = jnp.min(jnp.where(v == m, erow, ne), axis=0, keepdims=True)
        hot = erow == first
        v = jnp.where(hot, -jnp.inf, v)
        vals.append(m)
        hots.append(hot)
        firsts.append(first)
    exps = [jnp.exp(val - vals[0]) for val in vals]
    den = exps[0] + exps[1] + exps[2] + exps[3]
    gate_ref[...] = jnp.concatenate([e / den for e in exps], axis=0)
    idx_ref[...] = jnp.concatenate(firsts, axis=0)

    cnt = jnp.zeros((ne, tt), F32)
    for hot in hots:
        cnt = cnt + jnp.where(hot, 1.0, 0.0)
    before = (lax.broadcasted_iota(jnp.int32, (tt, tt), 0)
              < lax.broadcasted_iota(jnp.int32, (tt, tt), 1)).astype(BF16)
    excl = _dot(cnt.astype(BF16), before) + cnt_ref[:, 0:1]
    ranks = [jnp.sum(jnp.where(hot, excl, 0.0), axis=0, keepdims=True) for hot in hots]
    rank_ref[...] = jnp.concatenate(ranks, axis=0).astype(jnp.int32)
    cnt_ref[...] = cnt_ref[...] + jnp.sum(cnt, axis=1, keepdims=True)


def moe_router(logits_t):
    ne, n = logits_t.shape
    tt = ROUTE_TT
    slot = pl.BlockSpec((TOP_K, tt), lambda i: (0, i))
    return pl.pallas_call(
        _router_kernel,
        grid=(n // tt,),
        in_specs=[pl.BlockSpec((ne, tt), lambda i: (0, i))],
        out_specs=[slot, slot, slot, pl.BlockSpec((ne, LANES), lambda i: (0, 0))],
        out_shape=[jax.ShapeDtypeStruct((TOP_K, n), jnp.int32), jax.ShapeDtypeStruct((TOP_K, n), F32),
                   jax.ShapeDtypeStruct((TOP_K, n), jnp.int32), jax.ShapeDtypeStruct((ne, LANES), F32)],
        compiler_params=_cparams(("arbitrary",)),
        name="moe_router",
    )(logits_t)


MOE_TM = 512


def _dispatch_kernel(dest_ref, zrow_ref, hf_ref, xs_ref, zbuf, sem, zsem):
    tt = hf_ref.shape[0] // SUBLANES
    n = dest_ref.shape[0] // TOP_K
    base = pl.program_id(0) * tt

    def tile(ref, row):
        return ref.at[pl.ds(pl.multiple_of(row * SUBLANES, SUBLANES), SUBLANES), :]

    @pl.when(pl.program_id(0) == 0)
    def _():
        zbuf[...] = jnp.zeros(zbuf.shape, zbuf.dtype)

        def fill(j):
            first = pl.multiple_of(jnp.maximum(zrow_ref[j], 0) * SUBLANES, MOE_TM * SUBLANES)
            return pltpu.make_async_copy(zbuf, xs_ref.at[pl.ds(first, MOE_TM * SUBLANES), :], zsem)

        for j in range(zrow_ref.shape[0]):
            pl.when(zrow_ref[j] >= 0)(fill(j).start)
        for j in range(zrow_ref.shape[0]):
            pl.when(zrow_ref[j] >= 0)(fill(j).wait)

    def issue(t, carry):
        for k in range(TOP_K):
            pltpu.make_async_copy(tile(hf_ref, t), tile(xs_ref, dest_ref[k * n + base + t]), sem).start()
        return carry

    lax.fori_loop(0, tt, issue, 0, unroll=8)
    for _ in range(TOP_K):
        pltpu.make_async_copy(hf_ref, xs_ref.at[pl.ds(0, tt * SUBLANES), :], sem).wait()


DISPATCH_TT = 512


def moe_dispatch(dest_flat, zero_rows, hf, n_rows):
    n = hf.shape[0] // SUBLANES
    tt = DISPATCH_TT
    grid_spec = pltpu.PrefetchScalarGridSpec(
        num_scalar_prefetch=2,
        grid=(n // tt,),
        in_specs=[pl.BlockSpec((tt * SUBLANES, LANES), lambda i, dest, zrow: (i, 0))],
        out_specs=pl.BlockSpec(memory_space=pl.ANY),
        scratch_shapes=[pltpu.VMEM((MOE_TM * SUBLANES, LANES), F32), pltpu.SemaphoreType.DMA,
                        pltpu.SemaphoreType.DMA],
    )
    return pl.pallas_call(
        _dispatch_kernel,
        grid_spec=grid_spec,
        out_shape=jax.ShapeDtypeStruct((n_rows * SUBLANES, LANES), F32),
        compiler_params=_cparams(("arbitrary",)),
        name="moe_dispatch",
    )(dest_flat, zero_rows, hf)


def _expert_kernel(be_ref, nu_ref, xs_ref, wgu_ref, bgu_ref, wdn_ref, bdn_ref, y_ref,
                   wgu_bf, wdn_bf):
    i = pl.program_id(0)
    used = i < nu_ref[0]

    @pl.when(used & ((i == 0) | (be_ref[i] != be_ref[jnp.maximum(i - 1, 0)])))
    def _():
        wgu_bf[...] = wgu_ref[0].astype(BF16)
        wdn_bf[...] = wdn_ref[0].astype(BF16)

    @pl.when(used)
    def _():
        hgu = _dot(_load_token_tiles(xs_ref).astype(BF16), wgu_bf[...]) + bgu_ref[0]
        de = hgu.shape[1] // 2
        glu = jnp.minimum(hgu[:, :de], SWIGLU_LIMIT)
        lin = jnp.clip(hgu[:, de:], -SWIGLU_LIMIT, SWIGLU_LIMIT)
        act = glu * jax.nn.sigmoid(SWIGLU_ALPHA * glu) * (lin + 1.0)
        _store_token_tiles(y_ref, _dot(act.astype(BF16), wdn_bf[...]) + bdn_ref[0])

    @pl.when(jnp.logical_not(used))
    def _():
        y_ref[...] = jnp.zeros(y_ref.shape, y_ref.dtype)


EXPERT_VMEM_LIMIT = 56 * 1024 * 1024


def expert_ffn(layer, blk_exp, n_used, xs, w_gu, b_gu, w_dn, b_dn):
    r = xs.shape[0] // SUBLANES
    d = SUBLANES * LANES
    nl, ne, _, wide = w_gu.shape
    w_gu = w_gu.reshape(nl * ne, d, wide)
    w_dn = w_dn.reshape(nl * ne, wide // 2, d)
    b_gu = b_gu.reshape(nl * ne, wide)
    b_dn = b_dn.reshape(nl * ne, d)
    ne, base = nl * ne, layer * ne
    tm = MOE_TM
    grid_spec = pltpu.PrefetchScalarGridSpec(
        num_scalar_prefetch=2,
        grid=(r // tm,),
        in_specs=[pl.BlockSpec((tm * SUBLANES, LANES), lambda i, be, nu: (i, 0)),
                  pl.BlockSpec((1, d, wide), lambda i, be, nu: (base + be[i], 0, 0)),
                  pl.BlockSpec((1, 1, wide), lambda i, be, nu: (base + be[i], 0, 0)),
                  pl.BlockSpec((1, wide // 2, d), lambda i, be, nu: (base + be[i], 0, 0)),
                  pl.BlockSpec((1, 1, d), lambda i, be, nu: (base + be[i], 0, 0))],
        out_specs=pl.BlockSpec((tm * SUBLANES, LANES), lambda i, be, nu: (i, 0)),
        scratch_shapes=[pltpu.VMEM((d, wide), BF16), pltpu.VMEM((wide // 2, d), BF16)],
    )
    return pl.pallas_call(
        _expert_kernel,
        grid_spec=grid_spec,
        out_shape=jax.ShapeDtypeStruct((r * SUBLANES, LANES), F32),
        compiler_params=pltpu.CompilerParams(dimension_semantics=("arbitrary",),
                                             vmem_limit_bytes=EXPERT_VMEM_LIMIT),
        name="expert_ffn",
    )(blk_exp, n_used, xs, w_gu, b_gu.reshape(ne, 1, wide), w_dn, b_dn.reshape(ne, 1, d))


COMBINE_TT = 256


def _combine_kernel(dest_ref, gate_ref, x_ref, g2_ref, n3_ref, y_ref, o_ref, ybuf, sems):
    tt = x_ref.shape[0]
    n = dest_ref.shape[0] // TOP_K
    i = pl.program_id(0)
    steps = pl.num_programs(0)

    def gather(step, slot):
        base = step * tt

        def issue(t, carry):
            dst = pl.ds(pl.multiple_of(t * SUBLANES, SUBLANES), SUBLANES)
            for k in range(TOP_K):
                row = dest_ref[k * n + base + t]
                src = pl.ds(pl.multiple_of(row * SUBLANES, SUBLANES), SUBLANES)
                pltpu.make_async_copy(y_ref.at[src, :], ybuf.at[slot, k, dst, :], sems.at[slot]).start()
            return carry

        lax.fori_loop(0, tt, issue, 0, unroll=8)

    @pl.when(i == 0)
    def _():
        gather(0, 0)

    @pl.when(i + 1 < steps)
    def _():
        gather(i + 1, (i + 1) % 2)

    slot = i % 2
    for k in range(TOP_K):
        pltpu.make_async_copy(y_ref.at[pl.ds(0, tt * SUBLANES), :], ybuf.at[slot, k], sems.at[slot]).wait()
    gate = gate_ref[...]
    y = gate[:, 0:1] * _load_token_tiles(ybuf.at[slot, 0])
    for k in range(1, TOP_K):
        y = y + gate[:, k:k + 1] * _load_token_tiles(ybuf.at[slot, k])
    o_ref[...] = x_ref[...] + g2_ref[0] * _rms(y, n3_ref[...])


def moe_combine(dest_flat, gate, x, g2, n3, y_rows):
    b, s, d = x.shape
    n = b * s
    tt = COMBINE_TT
    per_batch = s // tt
    grid_spec = pltpu.PrefetchScalarGridSpec(
        num_scalar_prefetch=1,
        grid=(n // tt,),
        in_specs=[pl.BlockSpec((tt, LANES), lambda i, dest: (i, 0)),
                  pl.BlockSpec((tt, d), lambda i, dest: (i, 0)),
                  pl.BlockSpec((1, 1, d), lambda i, dest: (i // per_batch, 0, 0)),
                  pl.BlockSpec((1, d), lambda i, dest: (0, 0)),
                  pl.BlockSpec(memory_space=pl.ANY)],
        out_specs=pl.BlockSpec((tt, d), lambda i, dest: (i, 0)),
        scratch_shapes=[pltpu.VMEM((2, TOP_K, tt * SUBLANES, LANES), F32),
                        pltpu.SemaphoreType.DMA((2,))],
    )
    out = pl.pallas_call(
        _combine_kernel,
        grid_spec=grid_spec,
        out_shape=jax.ShapeDtypeStruct((n, d), F32),
        compiler_params=_cparams(("arbitrary",)),
        name="moe_combine",
    )(dest_flat, gate, x.reshape(n, d), g2, n3, y_rows)
    return out.reshape(b, s, d)


def _dispatch_plan(idx, rank, counts):
    n = idx.shape[1]
    tm = MOE_TM
    padded = (counts + tm - 1) // tm * tm
    pad_end = jnp.cumsum(padded)
    start = pad_end - padded
    experts = jnp.arange(N_EXPERTS, dtype=jnp.int32)
    dest = rank + jnp.sum(jnp.where(idx[..., None] == experts, start, 0), axis=-1)
    n_rows = -(-(n * TOP_K + N_EXPERTS * (tm - 1)) // tm) * tm
    n_blk = n_rows // tm
    blk_start = jnp.arange(n_blk, dtype=jnp.int32) * tm
    blk_exp = jnp.minimum(jnp.sum((pad_end[None, :] <= blk_start[:, None]).astype(jnp.int32), axis=1),
                          N_EXPERTS - 1)
    n_used = (pad_end[-1] // tm).astype(jnp.int32).reshape(1)
    tail = pad_end[-1] + experts * tm
    zero_rows = jnp.concatenate([jnp.where(padded > 0, pad_end - tm, -1),
                                 jnp.where(tail < n_rows, tail, -1)]).astype(jnp.int32)
    return dest.reshape(-1), blk_exp, n_used, zero_rows, n_rows


def _layer(layer, x, mod, norm_g, w_in, cmp_pe, cmp_w1, cmp_w2, grp_g, w_out, router_w, router_b,
           w_gu, b_gu, w_dn, b_dn, tabs, overlap):
    b, s, d = x.shape
    sh1, sc1, g1, sh2, sc2, g2 = (m.reshape(b, 1, d) for m in jnp.split(mod, 6, axis=-1))
    ng = norm_g.reshape(4, 1, d)

    qa, ka, va, qb, kvb, gb, qc, kc, vc = in_projection(x, sc1, sh1, ng[0], _pack_w_in(w_in), tabs)
    oa = moba_attention(qa, ka, va)
    nc = s // NSA_CMP_STRIDE
    kv16 = jnp.stack([kvb[:, :, :HEAD_DIM], kvb[:, :, HEAD_DIM:2 * HEAD_DIM]], axis=1)
    kv16 = kv16.reshape(b, 2, nc, NSA_CMP_STRIDE * HEAD_DIM)
    cmp_kv = nsa_compress(kv16, cmp_pe.reshape(2, 1, NSA_CMP_LEN * HEAD_DIM), cmp_w1, cmp_w2)
    ob = nsa_attention(qb, kvb, gb, cmp_kv, overlap)
    oc = sb_attention(qc, kc, vc)

    rw_hi, rw_lo = _split_bf16(router_w.T)
    rw3 = jnp.concatenate([rw_hi, rw_hi, rw_lo], axis=1)
    x1, hf, logits_t = out_projection(oa, ob, oc, x, grp_g.reshape(1, d), w_out.astype(BF16), g1,
                                      ng[1], ng[2], sc2, sh2, rw3, router_b.reshape(-1, 1))

    n = b * s
    idx, gate, rank, counts = moe_router(logits_t)
    dest, blk_exp, n_used, zero_rows, n_rows = _dispatch_plan(idx, rank, counts[:, 0].astype(jnp.int32))
    xs = moe_dispatch(dest, zero_rows, hf.reshape(n * SUBLANES, LANES), n_rows)
    y_rows = expert_ffn(layer, blk_exp, n_used, xs, w_gu, b_gu, w_dn, b_dn)
    gate = jnp.pad(gate.T, ((0, 0), (0, LANES - TOP_K)))
    return moe_combine(dest, gate, x1, g2, ng[3], y_rows)


def kernel(x, c, ada_w, ada_b, norm_g, w_in, nsa_cmp_pe, nsa_cmp_w1, nsa_cmp_w2, mix_out_g,
           w_out, router_w, router_b, exp_w_gu, exp_b_gu, exp_w_dn, exp_b_dn):
    s = x.shape[1]
    tabs = _rope_tables(s)
    overlap = _nsa_overlap(s)
    mod = ada_modulation(c, ada_w, ada_b)
    for l in range(ada_w.shape[0]):
        x = _layer(l, x, mod[l], norm_g[l], w_in[l], nsa_cmp_pe[l], nsa_cmp_w1[l], nsa_cmp_w2[l],
                   mix_out_g[l], w_out[l], router_w[l], router_b[l], exp_w_gu, exp_b_gu,
                   exp_w_dn, exp_b_dn, tabs, overlap)
    return x
```

```python
import functools

import numpy as np
import jax
import jax.numpy as jnp
from jax import lax
from jax.experimental import pallas as pl
from jax.experimental.pallas import tpu as pltpu

F32 = jnp.float32
BF16 = jnp.bfloat16
HI = lax.Precision.HIGHEST

D_MODEL = 1024
N_HEADS = 16
HEAD_DIM = 64
MOBA_HEADS = 4
NSA_HEADS = 6
SB_HEADS = 6
MOBA_W = MOBA_HEADS * HEAD_DIM
NSA_W = NSA_HEADS * HEAD_DIM
SB_W = SB_HEADS * HEAD_DIM
ROPE_DIM = 16
ROPE_THETA = 500000.0
MOBA_BLOCK = 256
MOBA_TOPK = 3
NSA_CMP_LEN = 32
NSA_CMP_STRIDE = 16
NSA_SLC_BLOCK = 64
NSA_SLC_TOPK = 16
NSA_WINDOW = 512
N_EXPERTS = 32
TOP_K = 4
SWIGLU_LIMIT = 7.0
SWIGLU_ALPHA = 1.702
RMS_EPS = 1e-6
NEG_INF = -1e30
SEL_FORCE = 1e4
SCALE = HEAD_DIM ** -0.5
LOG2E = 1.4426950408889634

LANES = 128
VMEM_LIMIT = 48 * 1024 * 1024


def _cparams(sem):
    return pltpu.CompilerParams(dimension_semantics=sem, vmem_limit_bytes=VMEM_LIMIT)


def _dot(a, b):
    return jnp.dot(a, b, preferred_element_type=F32)


def _dot_nt(a, b):
    return lax.dot_general(a, b, (((1,), (1,)), ((), ())), preferred_element_type=F32)


def _split_bf16(x):
    hi = x.astype(BF16)
    return hi, (x - hi.astype(F32)).astype(BF16)


def _dot_split(a, b):
    (a_hi, a_lo), (b_hi, b_lo) = _split_bf16(a), _split_bf16(b)
    return _dot(jnp.concatenate([a_hi, a_lo, a_hi], axis=1), jnp.concatenate([b_hi, b_hi, b_lo], axis=0))


def _dot_nt_split(a, b):
    (a_hi, a_lo), (b_hi, b_lo) = _split_bf16(a), _split_bf16(b)
    return _dot_nt(jnp.concatenate([a_hi, a_lo, a_hi], axis=1), jnp.concatenate([b_hi, b_hi, b_lo], axis=1))


SB_TILE = 256


SB_GROUP_W = SB_W
SB_LAG = 2


def _sb_kernel(q_ref, k_ref, v_ref, o_ref, ks_ref, vs_ref, run_ref, acc_ref):
    T = SB_TILE
    H = ks_ref.shape[0]
    qi = pl.program_id(2)

    @pl.when(qi == 0)
    def _():
        zeros = jnp.zeros((ks_ref.shape[1], HEAD_DIM), F32)
        for h in range(H):
            ks_ref[h] = k_ref[0, :, h * HEAD_DIM:(h + 1) * HEAD_DIM].astype(BF16)
            vh = v_ref[0, :, h * HEAD_DIM:(h + 1) * HEAD_DIM]
            vs_ref[h] = jnp.concatenate([vh, zeros] if h % 2 == 0 else [zeros, vh], axis=1).astype(BF16)

    run_ref[...] = jnp.zeros(run_ref.shape, F32)
    acc_ref[...] = jnp.zeros(acc_ref.shape, F32)

    row = lax.broadcasted_iota(jnp.int32, (T, T), 0)
    col = lax.broadcasted_iota(jnp.int32, (T, T), 1)
    incl = (row >= col).astype(BF16)
    sum_rhs = jnp.concatenate([incl, incl], axis=0)
    diag_mask = col < row
    qs = [(q_ref[0, :, h * HEAD_DIM:(h + 1) * HEAD_DIM] * (SCALE * LOG2E)).astype(BF16)
          for h in range(H)]
    sign = jnp.uint32(0x80000000)

    def scores(h, j):
        return _dot_nt(qs[h], ks_ref[h, pl.ds(pl.multiple_of(j * T, T), T), :])

    def neg_log2_keep(z, masked):
        neg_abs = lax.bitcast_convert_type(lax.bitcast_convert_type(z, jnp.uint32) | sign, F32)
        nlk = jnp.maximum(z, 0.0) + jnp.log2(1.0 + jnp.exp2(neg_abs))
        return jnp.where(diag_mask, nlk, 0.0) if masked else nlk

    def suffix_sums(nlk):
        hi = nlk.astype(BF16)
        lo = (nlk - hi.astype(F32)).astype(BF16)
        return _dot(jnp.concatenate([hi, lo], axis=1), sum_rhs)

    def weights(z, sums, run, masked):
        x = z - sums - run
        if masked:
            x = jnp.where(diag_mask, x, NEG_INF)
        return jnp.exp2(x.astype(BF16))

    def pv(h, p, j):
        return _dot(p, vs_ref[h, pl.ds(pl.multiple_of(j * T, T), T), :])

    def key_tile(j, masked):
        zs, sums = {0: scores(0, j)}, {}
        for t in range(H + SB_LAG):
            if t + 1 < H:
                zs[t + 1] = scores(t + 1, j)
            if t < H:
                sums[t] = suffix_sums(neg_log2_keep(zs[t], masked))
            h = t - SB_LAG
            if h >= 0:
                run = run_ref[h]
                p = weights(zs.pop(h), sums[h], run, masked)
                run_ref[h] = run + sums.pop(h)[:, 0:1]
                acc_ref[h // 2] = acc_ref[h // 2] + pv(h, p, j)

    key_tile(qi, True)

    def past(i, carry):
        key_tile(qi - 1 - i, False)
        return carry

    lax.fori_loop(0, qi, past, 0)
    for g in range(H // 2):
        o_ref[0, :, g * LANES:(g + 1) * LANES] = acc_ref[g]


def sb_attention(q, k, v):
    b, s, w = q.shape
    T = SB_TILE
    gw = SB_GROUP_W
    nh = gw // HEAD_DIM
    qspec = pl.BlockSpec((1, T, gw), lambda bi, p, i: (bi, i, p))
    kvspec = pl.BlockSpec((1, s, gw), lambda bi, p, i: (bi, 0, p))
    return pl.pallas_call(
        _sb_kernel,
        grid=(b, w // gw, s // T),
        in_specs=[qspec, kvspec, kvspec],
        out_specs=qspec,
        out_shape=jax.ShapeDtypeStruct((b, s, w), F32),
        scratch_shapes=[pltpu.VMEM((nh, s, HEAD_DIM), BF16), pltpu.VMEM((nh, s, LANES), BF16),
                        pltpu.VMEM((nh, T, 1), F32), pltpu.VMEM((nh // 2, T, LANES), F32)],
        compiler_params=_cparams(("arbitrary", "arbitrary", "arbitrary")),
        name="sb_attention",
    )(q, k, v)


def _rank_before_t(vals, n):
    idx = lax.broadcasted_iota(jnp.int32, vals.shape, 0)
    rank = jnp.zeros(vals.shape, F32)
    for j2 in range(n):
        other = vals[j2:j2 + 1, :]
        ahead = (other > vals) | ((other == vals) & (idx > j2))
        rank = rank + jnp.where(ahead, 1.0, 0.0)
    return rank


MOBA_KT = 2 * MOBA_BLOCK
MOBA_GROUP_W = MOBA_W


def _moba_kernel(q_ref, k_ref, v_ref, o_ref, ks_ref, vs_ref, km_ref):
    T, KT = MOBA_BLOCK, MOBA_KT
    H = ks_ref.shape[0]
    nb = km_ref.shape[1]
    s_len = ks_ref.shape[1]
    qi = pl.program_id(2)

    @pl.when(qi == 0)
    def _():
        key_blk = lax.broadcasted_iota(jnp.int32, (s_len, HEAD_DIM), 0) // T
        onehot = (lax.broadcasted_iota(jnp.int32, (s_len, HEAD_DIM), 1) == key_blk).astype(F32)
        ones_col = (lax.broadcasted_iota(jnp.int32, (s_len, HEAD_DIM), 1) == 0).astype(F32)
        for h in range(H):
            kh = k_ref[0, :, h * HEAD_DIM:(h + 1) * HEAD_DIM]
            ks_ref[h] = jnp.concatenate([kh, onehot], axis=1).astype(BF16)
            vs_ref[h] = jnp.concatenate([v_ref[0, :, h * HEAD_DIM:(h + 1) * HEAD_DIM], ones_col],
                                        axis=1).astype(BF16)
            km_ref[h] = jnp.mean(kh.reshape(nb, T, HEAD_DIM), axis=1)

    row = lax.broadcasted_iota(jnp.int32, (T, T), 0)
    col = lax.broadcasted_iota(jnp.int32, (T, T), 1)
    causal = col <= row
    blk_t = lax.broadcasted_iota(jnp.int32, (nb, T), 0)
    start = pl.multiple_of(qi * T, T)

    qfs = [q_ref[0, :, h * HEAD_DIM:(h + 1) * HEAD_DIM] for h in range(H)]
    qss = [qf * SCALE for qf in qfs]
    s_own = [_dot_nt(qss[h].astype(BF16), ks_ref[h, pl.ds(start, T), 0:HEAD_DIM]) for h in range(H)]
    q_aug, state = [], []
    for h in range(H):
        gate = _dot_nt_split(km_ref[h], qfs[h])
        gate = jnp.where(blk_t < qi, gate, NEG_INF)
        sel = (_rank_before_t(gate, nb) < float(MOBA_TOPK)) & (gate > 0.5 * NEG_INF)
        selb = jnp.where(sel, 0.0, NEG_INF)
        selb = jnp.concatenate([selb, jnp.full((LANES - nb, T), NEG_INF, F32)], axis=0).T
        q_aug.append(jnp.concatenate([qss[h], selb[:, :HEAD_DIM]], axis=1).astype(BF16))

        s = jnp.where(causal, s_own[h], NEG_INF)
        m = jnp.max(s, axis=1, keepdims=True)
        p = jnp.exp((s - m).astype(BF16))
        state += [m, _dot(p, vs_ref[h, pl.ds(start, T), :])]

    def past(i, carry):
        st = pl.multiple_of(i * KT, KT)
        ss = [_dot_nt(q_aug[h], ks_ref[h, pl.ds(st, KT), :]) for h in range(H)]
        out = []
        for h in range(H):
            m, acc = carry[2 * h:2 * h + 2]
            m_new = jnp.maximum(m, jnp.max(ss[h], axis=1, keepdims=True))
            alpha = jnp.exp(m - m_new)
            p = jnp.exp((ss[h] - m_new).astype(BF16))
            out += [m_new, alpha * acc + _dot(p, vs_ref[h, pl.ds(st, KT), :])]
        return tuple(out)

    per = KT // T
    state = lax.fori_loop(0, (qi + per - 1) // per, past, tuple(state))
    outs = [state[2 * h + 1][:, :HEAD_DIM] / jnp.maximum(state[2 * h + 1][:, HEAD_DIM:HEAD_DIM + 1], 1e-30)
            for h in range(H)]
    o_ref[0] = jnp.concatenate(outs, axis=1)


def moba_attention(q, k, v):
    b, s, w = q.shape
    T = MOBA_BLOCK
    nb = s // T
    assert nb <= HEAD_DIM and s % MOBA_KT == 0
    gw = MOBA_GROUP_W
    nh = gw // HEAD_DIM
    qspec = pl.BlockSpec((1, T, gw), lambda bi, p, i: (bi, i, p))
    kvspec = pl.BlockSpec((1, s, gw), lambda bi, p, i: (bi, 0, p))
    return pl.pallas_call(
        _moba_kernel,
        grid=(b, w // gw, nb),
        in_specs=[qspec, kvspec, kvspec],
        out_specs=qspec,
        out_shape=jax.ShapeDtypeStruct((b, s, w), F32),
        scratch_shapes=[pltpu.VMEM((nh, s, LANES), BF16), pltpu.VMEM((nh, s, LANES), BF16),
                        pltpu.VMEM((nh, nb, HEAD_DIM), F32)],
        compiler_params=_cparams(("arbitrary", "arbitrary", "arbitrary")),
        name="moba_attention",
    )(q, k, v)


def _nsa_compress_kernel(x_ref, pe_ref, w1_ref, w2_ref, o_ref):
    nc = x_ref.shape[2]
    half = NSA_CMP_STRIDE * HEAD_DIM
    x = x_ref[0, 0]
    w1 = w1_ref[0]
    first = _dot_split(x, w1[:half])
    second = _dot_split(x, w1[half:])
    pe = jnp.broadcast_to(pe_ref[0], (8, 2 * half))
    peb = _dot_split(pe, w1)[0:1]
    pre = first + pltpu.roll(second, nc - 1, 0) + peb
    hid = pre * jax.nn.sigmoid(pre)
    o_ref[0, 0] = _dot_split(hid, w2_ref[0])


def nsa_compress(kv16, pe, w1, w2):
    b, _, nc, wide = kv16.shape
    return pl.pallas_call(
        _nsa_compress_kernel,
        grid=(b, 2),
        in_specs=[pl.BlockSpec((1, 1, nc, wide), lambda bi, i: (bi, i, 0, 0)),
                  pl.BlockSpec((1, 1, 2 * wide), lambda bi, i: (i, 0, 0)),
                  pl.BlockSpec((1, 2 * wide, HEAD_DIM), lambda bi, i: (i, 0, 0)),
                  pl.BlockSpec((1, HEAD_DIM, HEAD_DIM), lambda bi, i: (i, 0, 0))],
        out_specs=pl.BlockSpec((1, 1, nc, HEAD_DIM), lambda bi, i: (bi, i, 0, 0)),
        out_shape=jax.ShapeDtypeStruct((b, 2, nc, HEAD_DIM), F32),
        compiler_params=_cparams(("arbitrary", "arbitrary")),
        name="nsa_compress",
    )(kv16, pe, w1, w2)


NSA_TQ = 256
NSA_KT = 512
NSA_SPAN = NSA_WINDOW + NSA_TQ
NSA_CHAIN_HEADS = 2
NSA_WINDOW_HEADS = 2


def _softmax_rows(s, mask):
    s = jnp.where(mask, s, NEG_INF)
    m = jnp.max(s, axis=-1, keepdims=True)
    e = jnp.where(mask, jnp.exp(s - m), 0.0)
    return e / jnp.maximum(jnp.sum(e, axis=-1, keepdims=True), 1e-30)


def _nsa_kernel(q_ref, cmp_ref, slc_ref, win_ref, g_ref, ov_ref, o_ref,
                ksl_ref, vsl_ref, kw_ref, vw_ref, kc_ref):
    TQ, KT, H = NSA_TQ, NSA_KT, NSA_HEADS
    ns, nc = ov_ref.shape
    s_len = ksl_ref.shape[0]
    qi = pl.program_id(1)
    t0 = qi * TQ

    @pl.when(qi == 0)
    def _():
        lane = lax.broadcasted_iota(jnp.int32, (s_len, HEAD_DIM), 1)
        key_blk = lax.broadcasted_iota(jnp.int32, (s_len, HEAD_DIM), 0) // NSA_SLC_BLOCK
        onehot = (lane == key_blk).astype(F32)
        ones_col = (lane == 0).astype(F32)
        ksl_ref[...] = jnp.concatenate([slc_ref[0, :, :HEAD_DIM], onehot], axis=1).astype(BF16)
        vsl_ref[...] = jnp.concatenate([slc_ref[0, :, HEAD_DIM:], ones_col], axis=1).astype(BF16)
        kw_ref[...] = win_ref[0, :, :HEAD_DIM].astype(BF16)
        vw_ref[...] = jnp.concatenate([win_ref[0, :, HEAD_DIM:], ones_col], axis=1).astype(BF16)
        k_hi, k_lo = _split_bf16(cmp_ref[0, 0])
        kc_ref[...] = jnp.concatenate([k_hi, k_hi, k_lo], axis=1)

    qf = jnp.concatenate([q_ref[0, :, h * HEAD_DIM:(h + 1) * HEAD_DIM] for h in range(H)],
                         axis=0) * SCALE
    q, q_lo = _split_bf16(qf)

    def normalized(acc):
        return acc[:, :HEAD_DIM] / jnp.maximum(acc[:, HEAD_DIM:HEAD_DIM + 1], 1e-30)

    w0 = pl.multiple_of(jnp.maximum(t0 - NSA_WINDOW, 0), TQ)
    wpos = w0 + lax.broadcasted_iota(jnp.int32, (TQ, NSA_SPAN), 1)
    tq_w = t0 + lax.broadcasted_iota(jnp.int32, (TQ, NSA_SPAN), 0)
    mask_w = ((wpos <= tq_w) & (wpos > tq_w - NSA_WINDOW))[None]
    o_w = []
    for c in range(H // NSA_WINDOW_HEADS):
        rows = slice(c * NSA_WINDOW_HEADS * TQ, (c + 1) * NSA_WINDOW_HEADS * TQ)
        s_w = _dot_nt(q[rows], kw_ref[pl.ds(w0, NSA_SPAN), :]).reshape(NSA_WINDOW_HEADS, TQ, NSA_SPAN)
        s_w = jnp.where(mask_w, s_w, NEG_INF)
        p_w = jnp.exp((s_w - jnp.max(s_w, axis=-1, keepdims=True)).astype(BF16))
        o_w.append(normalized(_dot(p_w.reshape(NSA_WINDOW_HEADS * TQ, NSA_SPAN),
                                   vw_ref[pl.ds(w0, NSA_SPAN), :])))
    o_w = jnp.concatenate(o_w, axis=0)

    tq_c = t0 + lax.broadcasted_iota(jnp.int32, (TQ, nc), 0)
    n_c = lax.broadcasted_iota(jnp.int32, (TQ, nc), 1)
    mask_c = (n_c * NSA_CMP_STRIDE + (NSA_CMP_LEN - 1) <= tq_c) & (n_c < nc - 1)
    s_c = _dot_nt(jnp.concatenate([q, q_lo, q], axis=1), kc_ref[...])
    p_c = _softmax_rows(s_c.reshape(H, TQ, nc), mask_c[None])
    o_c = _dot(p_c.reshape(H * TQ, nc).astype(BF16), cmp_ref[0, 1].astype(BF16))

    p_sum = jnp.sum(p_c, axis=0)
    p1, p2 = _split_bf16(p_sum)
    p3 = (p_sum - p1.astype(F32) - p2.astype(F32)).astype(BF16)
    ov = ov_ref[...].astype(BF16)
    imp = _dot_nt(jnp.concatenate([ov, ov, ov], axis=1), jnp.concatenate([p1, p2, p3], axis=1))
    tq_s = t0 + lax.broadcasted_iota(jnp.int32, (ns, TQ), 1)
    blk = lax.broadcasted_iota(jnp.int32, (ns, TQ), 0)
    own = tq_s // NSA_SLC_BLOCK
    forced = (blk == 0) | (blk == own) | (blk == own - 1)
    imp = jnp.where(forced, SEL_FORCE, imp)
    imp = jnp.where(blk <= own, imp, NEG_INF)
    sel = (_rank_before_t(imp, ns) < float(min(NSA_SLC_TOPK, ns))) & (imp > 0.5 * NEG_INF)
    selb = jnp.where(sel, 0.0, NEG_INF)
    if ns < LANES:
        selb = jnp.concatenate([selb, jnp.full((LANES - ns, TQ), NEG_INF, F32)], axis=0)
    selb = selb.T[:, :HEAD_DIM]
    q_aug = jnp.concatenate([qf, jnp.concatenate([selb] * H, axis=0)], axis=1).astype(BF16)

    jd = t0 // KT
    start = pl.multiple_of(jd * KT, KT)
    kpos = start + lax.broadcasted_iota(jnp.int32, (TQ, KT), 1)
    tq_k = t0 + lax.broadcasted_iota(jnp.int32, (TQ, KT), 0)
    HC = NSA_CHAIN_HEADS
    G, R = H // HC, HC * TQ
    qa = [q_aug[c * R:(c + 1) * R] for c in range(G)]
    causal_k = (kpos <= tq_k)[None]

    def scores(c, st):
        return _dot_nt(qa[c], ksl_ref[pl.ds(st, KT), :]).reshape(HC, TQ, KT)

    s_own = [scores(c, start) for c in range(G)]
    state = []
    for c in range(G):
        s = jnp.where(causal_k, s_own[c], NEG_INF)
        m = jnp.max(s, axis=-1, keepdims=True)
        p = jnp.exp((s - m).astype(BF16))
        state += [m, _dot(p.reshape(R, KT), vsl_ref[pl.ds(start, KT), :])]

    def past(j, carry):
        st = pl.multiple_of(j * KT, KT)
        ss = {0: scores(0, st)}
        out = []
        for c in range(G):
            if c + 1 < G:
                ss[c + 1] = scores(c + 1, st)
            m, acc = carry[2 * c:2 * c + 2]
            m_new = jnp.maximum(m, jnp.max(ss[c], axis=-1, keepdims=True))
            alpha = jnp.exp(m - m_new)
            p = jnp.exp((ss[c] - m_new).astype(BF16))
            pv = _dot(p.reshape(R, KT), vsl_ref[pl.ds(st, KT), :])
            out += [m_new, alpha.reshape(R, 1) * acc + pv]
        return tuple(out)

    state = lax.fori_loop(0, jd, past, tuple(state))
    o_s = jnp.concatenate([normalized(state[2 * c + 1]) for c in range(G)], axis=0)

    g = g_ref[0]
    outs = []
    for h in range(H):
        rows = slice(h * TQ, (h + 1) * TQ)
        outs.append(g[:, 3 * h:3 * h + 1] * o_c[rows] + g[:, 3 * h + 1:3 * h + 2] * o_s[rows]
                    + g[:, 3 * h + 2:3 * h + 3] * o_w[rows])
    o_ref[0] = jnp.concatenate(outs, axis=1)


def nsa_attention(q, kv, gates, cmp_kv, overlap):
    b, s, w = q.shape
    ns, nc = overlap.shape
    assert ns <= HEAD_DIM
    TQ = NSA_TQ
    return pl.pallas_call(
        _nsa_kernel,
        grid=(b, s // TQ),
        in_specs=[pl.BlockSpec((1, TQ, w), lambda bi, i: (bi, i, 0)),
                  pl.BlockSpec((1, 2, nc, HEAD_DIM), lambda bi, i: (bi, 0, 0, 0)),
                  pl.BlockSpec((1, s, LANES), lambda bi, i: (bi, 0, 1)),
                  pl.BlockSpec((1, s, LANES), lambda bi, i: (bi, 0, 2)),
                  pl.BlockSpec((1, TQ, LANES), lambda bi, i: (bi, i, 0)),
                  pl.BlockSpec((ns, nc), lambda bi, i: (0, 0))],
        out_specs=pl.BlockSpec((1, TQ, w), lambda bi, i: (bi, i, 0)),
        out_shape=jax.ShapeDtypeStruct((b, s, w), F32),
        scratch_shapes=[pltpu.VMEM((s, LANES), BF16), pltpu.VMEM((s, LANES), BF16),
                        pltpu.VMEM((s, HEAD_DIM), BF16), pltpu.VMEM((s, LANES), BF16),
                        pltpu.VMEM((nc, 3 * HEAD_DIM), BF16)],
        compiler_params=_cparams(("arbitrary", "arbitrary")),
        name="nsa_attention",
    )(q, cmp_kv, kv, kv, gates, overlap)


def _nsa_overlap(s):
    nc = s // NSA_CMP_STRIDE
    ns = s // NSA_SLC_BLOCK
    cstart = np.arange(nc) * NSA_CMP_STRIDE
    cend = cstart + NSA_CMP_LEN - 1
    sstart = np.arange(ns) * NSA_SLC_BLOCK
    ov = (cstart[:, None] <= sstart[None, :] + NSA_SLC_BLOCK - 1) & (cend[:, None] >= sstart[None, :])
    ov[nc - 1] = False
    return jnp.asarray(ov.T.astype(np.float32))


def _mod_kernel(c_ref, w_ref, b_ref, o_ref):
    c = c_ref[...]
    act = c * jax.nn.sigmoid(c)
    o_ref[0] = jnp.dot(act, w_ref[0], precision=HI, preferred_element_type=F32) + b_ref[0]


def ada_modulation(c, ada_w, ada_b):
    nl, d, wide = ada_w.shape
    b = c.shape[0]
    tn = D_MODEL
    return pl.pallas_call(
        _mod_kernel,
        grid=(nl, wide // tn),
        in_specs=[pl.BlockSpec((b, d), lambda l, j: (0, 0)),
                  pl.BlockSpec((1, d, tn), lambda l, j: (l, 0, j)),
                  pl.BlockSpec((1, 1, tn), lambda l, j: (l, 0, j))],
        out_specs=pl.BlockSpec((1, b, tn), lambda l, j: (l, 0, j)),
        out_shape=jax.ShapeDtypeStruct((nl, b, wide), F32),
        compiler_params=_cparams(("arbitrary", "arbitrary")),
        name="ada_modulation",
    )(c, ada_w, ada_b.reshape(nl, 1, wide))


def _rms(x, g):
    return x * lax.rsqrt(jnp.mean(x * x, axis=-1, keepdims=True) + RMS_EPS) * g


_GATE_PAD = LANES
_COLS = {}
_off = 0
for _name, _w in (("qa", MOBA_W), ("ka", MOBA_W), ("va", MOBA_W), ("qb", NSA_W), ("kvb", NSA_W),
                  ("gb", _GATE_PAD), ("qc", SB_W), ("kc", SB_W), ("vc", SB_W)):
    _COLS[_name] = (_off, _w)
    _off += _w
IN_W_PACKED = _off
PROJ_TM = 1024


def _rope_block(p, cs, sm, sp):
    return p * cs + pltpu.roll(p, LANES - ROPE_DIM // 2, 1) * sm + pltpu.roll(p, ROPE_DIM // 2, 1) * sp


def _in_proj_kernel(x_ref, sc_ref, sh_ref, g_ref, w_ref, cqk_ref, mqk_ref, pqk_ref,
                    ckv_ref, mkv_ref, pkv_ref,
                    qa_ref, ka_ref, va_ref, qb_ref, kvb_ref, gb_ref, qc_ref, kc_ref, vc_ref):
    hm = _rms(x_ref[0], g_ref[...]) * (1.0 + sc_ref[0]) + sh_ref[0]
    p = _dot(hm.astype(BF16), w_ref[...])
    qk = (cqk_ref[...], mqk_ref[...], pqk_ref[...])
    kv = (ckv_ref[...], mkv_ref[...], pkv_ref[...])

    def emit(ref, name, tabs):
        off, w = _COLS[name]
        for j in range(w // LANES):
            blk = p[:, off + j * LANES: off + (j + 1) * LANES]
            if tabs is not None:
                blk = _rope_block(blk, *tabs)
            ref[0, :, j * LANES:(j + 1) * LANES] = blk

    emit(qa_ref, "qa", qk)
    emit(ka_ref, "ka", qk)
    emit(va_ref, "va", None)
    emit(qb_ref, "qb", qk)
    emit(kvb_ref, "kvb", kv)
    emit(qc_ref, "qc", None)
    emit(kc_ref, "kc", None)
    emit(vc_ref, "vc", None)
    off, w = _COLS["gb"]
    gb_ref[0] = jax.nn.sigmoid(p[:, off:off + w])


def in_projection(x, sc, sh, g, w_packed, tabs):
    b, s, d = x.shape
    tm = PROJ_TM
    row = lambda w: pl.BlockSpec((1, tm, w), lambda bi, i: (bi, i, 0))
    vec = pl.BlockSpec((1, 1, d), lambda bi, i: (bi, 0, 0))
    tab = pl.BlockSpec((tm, LANES), lambda bi, i: (i, 0))
    names = ("qa", "ka", "va", "qb", "kvb", "gb", "qc", "kc", "vc")
    return pl.pallas_call(
        _in_proj_kernel,
        grid=(b, s // tm),
        in_specs=[row(d), vec, vec, pl.BlockSpec((1, d), lambda bi, i: (0, 0)),
                  pl.BlockSpec((d, IN_W_PACKED), lambda bi, i: (0, 0))] + [tab] * 6,
        out_specs=[row(_COLS[n][1]) for n in names],
        out_shape=[jax.ShapeDtypeStruct((b, s, _COLS[n][1]), F32) for n in names],
        compiler_params=_cparams(("arbitrary", "arbitrary")),
        name="in_projection",
    )(x, sc, sh, g, w_packed, *tabs)


def _pack_w_in(w_in):
    widths = (MOBA_W, MOBA_W, MOBA_W, NSA_W, NSA_W, 3 * NSA_HEADS, SB_W, SB_W, SB_W)
    offs = np.cumsum((0,) + widths)
    parts = []
    for i, w in enumerate(widths):
        blk = w_in[:, offs[i]:offs[i + 1]]
        if w == 3 * NSA_HEADS:
            blk = jnp.pad(blk, ((0, 0), (0, _GATE_PAD - w)))
        parts.append(blk)
    return jnp.concatenate(parts, axis=1).astype(BF16)


def _rope_tables(s):
    half = ROPE_DIM // 2
    inv_freq = ROPE_THETA ** (-jnp.arange(0, ROPE_DIM, 2, dtype=F32) / ROPE_DIM)
    ang = jnp.arange(s, dtype=F32)[:, None] * inv_freq[None, :]
    cos, sin = jnp.cos(ang), jnp.sin(ang)
    zeros = jnp.zeros((s, HEAD_DIM - ROPE_DIM), F32)
    z8 = jnp.zeros((s, half), F32)
    cs_h = jnp.concatenate([cos, cos, zeros + 1.0], axis=1)
    sm_h = jnp.concatenate([-sin, z8, zeros], axis=1)
    sp_h = jnp.concatenate([z8, sin, zeros], axis=1)
    ident = (jnp.ones((s, HEAD_DIM), F32), jnp.zeros((s, HEAD_DIM), F32), jnp.zeros((s, HEAD_DIM), F32))
    qk = tuple(jnp.concatenate([t, t], axis=1) for t in (cs_h, sm_h, sp_h))
    kv = tuple(jnp.concatenate([t, i], axis=1) for t, i in zip((cs_h, sm_h, sp_h), ident))
    return qk + kv


OUT_TM = 1024
SUBLANES = 8


def _store_token_tiles(ref, val):
    rows = val.shape[0]
    for j in range(SUBLANES):
        ref[pl.ds(j, rows, stride=SUBLANES), :] = val[:, j * LANES:(j + 1) * LANES]


def _load_token_tiles(ref):
    rows = ref.shape[0] // SUBLANES
    return jnp.concatenate([ref[pl.ds(j, rows, stride=SUBLANES), :] for j in range(SUBLANES)], axis=1)


def _out_proj_kernel(oa_ref, ob_ref, oc_ref, x_ref, gg_ref, w_ref, g1_ref, n1_ref, n2_ref,
                     sc_ref, sh_ref, rw_ref, rb_ref, x1_ref, hf_ref, lg_ref):
    gg = gg_ref[...]
    y = jnp.concatenate([_rms(oa_ref[0], gg[:, :MOBA_W]),
                         _rms(ob_ref[0], gg[:, MOBA_W:MOBA_W + NSA_W]),
                         _rms(oc_ref[0], gg[:, MOBA_W + NSA_W:])], axis=1)
    y = _dot(y.astype(BF16), w_ref[...])
    x1 = x_ref[0] + g1_ref[0] * _rms(y, n1_ref[...])
    x1_ref[0] = x1
    hf = _rms(x1, n2_ref[...]) * (1.0 + sc_ref[0]) + sh_ref[0]
    _store_token_tiles(hf_ref.at[0], hf)
    h_hi, h_lo = _split_bf16(hf)
    lg_ref[...] = _dot_nt(rw_ref[...], jnp.concatenate([h_hi, h_lo, h_hi], axis=1)) + rb_ref[...]


def out_projection(oa, ob, oc, x, grp_g, w_out, g1, n1, n2, sc2, sh2, rw_t, rb):
    b, s, d = x.shape
    assert d == SUBLANES * LANES
    ne = rw_t.shape[0]
    tm = OUT_TM
    steps = s // tm
    row = lambda w: pl.BlockSpec((1, tm, w), lambda bi, i: (bi, i, 0))
    vec = pl.BlockSpec((1, 1, d), lambda bi, i: (bi, 0, 0))
    cst = lambda r, w: pl.BlockSpec((r, w), lambda bi, i: (0, 0))
    return pl.pallas_call(
        _out_proj_kernel,
        grid=(b, steps),
        in_specs=[row(MOBA_W), row(NSA_W), row(SB_W), row(d), cst(1, d), cst(d, d), vec,
                  cst(1, d), cst(1, d), vec, vec, cst(ne, 3 * d), cst(ne, 1)],
        out_specs=[row(d), pl.BlockSpec((1, tm * SUBLANES, LANES), lambda bi, i: (bi, i, 0)),
                   pl.BlockSpec((ne, tm), lambda bi, i: (0, bi * steps + i))],
        out_shape=[jax.ShapeDtypeStruct((b, s, d), F32),
                   jax.ShapeDtypeStruct((b, s * SUBLANES, LANES), F32),
                   jax.ShapeDtypeStruct((ne, b * s), F32)],
        compiler_params=_cparams(("arbitrary", "arbitrary")),
        name="out_projection",
    )(oa, ob, oc, x, grp_g, w_out, g1, n1, n2, sc2, sh2, rw_t, rb)


ROUTE_TT = 512


def _router_kernel(lg_ref, idx_ref, gate_ref, rank_ref, cnt_ref):
    ne, tt = lg_ref.shape

    @pl.when(pl.program_id(0) == 0)
    def _():
        cnt_ref[...] = jnp.zeros(cnt_ref.shape, cnt_ref.dtype)

    v = lg_ref[...]
    erow = lax.broadcasted_iota(jnp.int32, (ne, tt), 0)
    vals, hots, firsts = [], [], []
    for _ in range(TOP_K):
        m = jnp.max(v, axis=0, keepdims=True)
        first = jnp.min(jnp.where(v == m, erow, ne), axis=0, keepdims=True)
        hot = erow == first
        v = jnp.where(hot, -jnp.inf, v)
        vals.append(m)
        hots.append(hot)
        firsts.append(first)
    exps = [jnp.exp(val - vals[0]) for val in vals]
    den = exps[0] + exps[1] + exps[2] + exps[3]
    gate_ref[...] = jnp.concatenate([e / den for e in exps], axis=0)
    idx_ref[...] = jnp.concatenate(firsts, axis=0)

    cnt = jnp.zeros((ne, tt), F32)
    for hot in hots:
        cnt = cnt + jnp.where(hot, 1.0, 0.0)
    before = (lax.broadcasted_iota(jnp.int32, (tt, tt), 0)
              < lax.broadcasted_iota(jnp.int32, (tt, tt), 1)).astype(BF16)
    excl = _dot(cnt.astype(BF16), before) + cnt_ref[:, 0:1]
    ranks = [jnp.sum(jnp.where(hot, excl, 0.0), axis=0, keepdims=True) for hot in hots]
    rank_ref[...] = jnp.concatenate(ranks, axis=0).astype(jnp.int32)
    cnt_ref[...] = cnt_ref[...] + jnp.sum(cnt, axis=1, keepdims=True)


def moe_router(logits_t):
    ne, n = logits_t.shape
    tt = ROUTE_TT
    slot = pl.BlockSpec((TOP_K, tt), lambda i: (0, i))
    return pl.pallas_call(
        _router_kernel,
        grid=(n // tt,),
        in_specs=[pl.BlockSpec((ne, tt), lambda i: (0, i))],
        out_specs=[slot, slot, slot, pl.BlockSpec((ne, LANES), lambda i: (0, 0))],
        out_shape=[jax.ShapeDtypeStruct((TOP_K, n), jnp.int32), jax.ShapeDtypeStruct((TOP_K, n), F32),
                   jax.ShapeDtypeStruct((TOP_K, n), jnp.int32), jax.ShapeDtypeStruct((ne, LANES), F32)],
        compiler_params=_cparams(("arbitrary",)),
        name="moe_router",
    )(logits_t)


MOE_TM = 512


def _dispatch_kernel(dest_ref, zrow_ref, hf_ref, xs_ref, zbuf, sem, zsem):
    tt = hf_ref.shape[0] // SUBLANES
    n = dest_ref.shape[0] // TOP_K
    base = pl.program_id(0) * tt

    def tile(ref, row):
        return ref.at[pl.ds(pl.multiple_of(row * SUBLANES, SUBLANES), SUBLANES), :]

    @pl.when(pl.program_id(0) == 0)
    def _():
        zbuf[...] = jnp.zeros(zbuf.shape, zbuf.dtype)

        def fill(j):
            first = pl.multiple_of(jnp.maximum(zrow_ref[j], 0) * SUBLANES, MOE_TM * SUBLANES)
            return pltpu.make_async_copy(zbuf, xs_ref.at[pl.ds(first, MOE_TM * SUBLANES), :], zsem)

        for j in range(zrow_ref.shape[0]):
            pl.when(zrow_ref[j] >= 0)(fill(j).start)
        for j in range(zrow_ref.shape[0]):
            pl.when(zrow_ref[j] >= 0)(fill(j).wait)

    def issue(t, carry):
        for k in range(TOP_K):
            pltpu.make_async_copy(tile(hf_ref, t), tile(xs_ref, dest_ref[k * n + base + t]),
                                  sem).start(priority=k % 2)
        return carry

    lax.fori_loop(0, tt, issue, 0, unroll=8)
    for _ in range(TOP_K):
        pltpu.make_async_copy(hf_ref, xs_ref.at[pl.ds(0, tt * SUBLANES), :], sem).wait()


DISPATCH_TT = 512


def moe_dispatch(dest_flat, zero_rows, hf, n_rows):
    n = hf.shape[0] // SUBLANES
    tt = DISPATCH_TT
    grid_spec = pltpu.PrefetchScalarGridSpec(
        num_scalar_prefetch=2,
        grid=(n // tt,),
        in_specs=[pl.BlockSpec((tt * SUBLANES, LANES), lambda i, dest, zrow: (i, 0))],
        out_specs=pl.BlockSpec(memory_space=pl.ANY),
        scratch_shapes=[pltpu.VMEM((MOE_TM * SUBLANES, LANES), F32), pltpu.SemaphoreType.DMA,
                        pltpu.SemaphoreType.DMA],
    )
    return pl.pallas_call(
        _dispatch_kernel,
        grid_spec=grid_spec,
        out_shape=jax.ShapeDtypeStruct((n_rows * SUBLANES, LANES), F32),
        compiler_params=_cparams(("arbitrary",)),
        name="moe_dispatch",
    )(dest_flat, zero_rows, hf)


def _expert_kernel(be_ref, nu_ref, xs_ref, wgu_ref, bgu_ref, wdn_ref, bdn_ref, y_ref,
                   wgu_bf, wdn_bf):
    i = pl.program_id(0)
    used = i < nu_ref[0]

    @pl.when(used & ((i == 0) | (be_ref[i] != be_ref[jnp.maximum(i - 1, 0)])))
    def _():
        wgu_bf[...] = wgu_ref[0].astype(BF16)
        wdn_bf[...] = wdn_ref[0].astype(BF16)

    @pl.when(used)
    def _():
        hgu = _dot(_load_token_tiles(xs_ref).astype(BF16), wgu_bf[...]) + bgu_ref[0]
        de = hgu.shape[1] // 2
        glu = jnp.minimum(hgu[:, :de], SWIGLU_LIMIT)
        lin = jnp.clip(hgu[:, de:], -SWIGLU_LIMIT, SWIGLU_LIMIT)
        act = glu * jax.nn.sigmoid(SWIGLU_ALPHA * glu) * (lin + 1.0)
        _store_token_tiles(y_ref, _dot(act.astype(BF16), wdn_bf[...]) + bdn_ref[0])

    @pl.when(jnp.logical_not(used))
    def _():
        y_ref[...] = jnp.zeros(y_ref.shape, y_ref.dtype)


EXPERT_VMEM_LIMIT = 56 * 1024 * 1024


def expert_ffn(layer, blk_exp, n_used, xs, w_gu, b_gu, w_dn, b_dn):
    r = xs.shape[0] // SUBLANES
    d = SUBLANES * LANES
    nl, ne, _, wide = w_gu.shape
    w_gu = w_gu.reshape(nl * ne, d, wide)
    w_dn = w_dn.reshape(nl * ne, wide // 2, d)
    b_gu = b_gu.reshape(nl * ne, wide)
    b_dn = b_dn.reshape(nl * ne, d)
    ne, base = nl * ne, layer * ne
    tm = MOE_TM
    grid_spec = pltpu.PrefetchScalarGridSpec(
        num_scalar_prefetch=2,
        grid=(r // tm,),
        in_specs=[pl.BlockSpec((tm * SUBLANES, LANES), lambda i, be, nu: (i, 0)),
                  pl.BlockSpec((1, d, wide), lambda i, be, nu: (base + be[i], 0, 0)),
                  pl.BlockSpec((1, 1, wide), lambda i, be, nu: (base + be[i], 0, 0)),
                  pl.BlockSpec((1, wide // 2, d), lambda i, be, nu: (base + be[i], 0, 0)),
                  pl.BlockSpec((1, 1, d), lambda i, be, nu: (base + be[i], 0, 0))],
        out_specs=pl.BlockSpec((tm * SUBLANES, LANES), lambda i, be, nu: (i, 0)),
        scratch_shapes=[pltpu.VMEM((d, wide), BF16), pltpu.VMEM((wide // 2, d), BF16)],
    )
    return pl.pallas_call(
        _expert_kernel,
        grid_spec=grid_spec,
        out_shape=jax.ShapeDtypeStruct((r * SUBLANES, LANES), F32),
        compiler_params=pltpu.CompilerParams(dimension_semantics=("arbitrary",),
                                             vmem_limit_bytes=EXPERT_VMEM_LIMIT),
        name="expert_ffn",
    )(blk_exp, n_used, xs, w_gu, b_gu.reshape(ne, 1, wide), w_dn, b_dn.reshape(ne, 1, d))


COMBINE_TT = 256


def _combine_kernel(dest_ref, gate_ref, x_ref, g2_ref, n3_ref, y_ref, o_ref, ybuf, sems):
    tt = x_ref.shape[0]
    n = dest_ref.shape[0] // TOP_K
    i = pl.program_id(0)
    steps = pl.num_programs(0)

    def gather(step, slot):
        base = step * tt

        def issue(t, carry):
            dst = pl.ds(pl.multiple_of(t * SUBLANES, SUBLANES), SUBLANES)
            for k in range(TOP_K):
                row = dest_ref[k * n + base + t]
                src = pl.ds(pl.multiple_of(row * SUBLANES, SUBLANES), SUBLANES)
                pltpu.make_async_copy(y_ref.at[src, :], ybuf.at[slot, k, dst, :],
                                      sems.at[slot]).start(priority=k % 2)
            return carry

        lax.fori_loop(0, tt, issue, 0, unroll=8)

    @pl.when(i == 0)
    def _():
        gather(0, 0)

    @pl.when(i + 1 < steps)
    def _():
        gather(i + 1, (i + 1) % 2)

    slot = i % 2
    for k in range(TOP_K):
        pltpu.make_async_copy(y_ref.at[pl.ds(0, tt * SUBLANES), :], ybuf.at[slot, k], sems.at[slot]).wait()
    gate = gate_ref[...]
    y = gate[:, 0:1] * _load_token_tiles(ybuf.at[slot, 0])
    for k in range(1, TOP_K):
        y = y + gate[:, k:k + 1] * _load_token_tiles(ybuf.at[slot, k])
    o_ref[...] = x_ref[...] + g2_ref[0] * _rms(y, n3_ref[...])


def moe_combine(dest_flat, gate, x, g2, n3, y_rows):
    b, s, d = x.shape
    n = b * s
    tt = COMBINE_TT
    per_batch = s // tt
    grid_spec = pltpu.PrefetchScalarGridSpec(
        num_scalar_prefetch=1,
        grid=(n // tt,),
        in_specs=[pl.BlockSpec((tt, LANES), lambda i, dest: (i, 0)),
                  pl.BlockSpec((tt, d), lambda i, dest: (i, 0)),
                  pl.BlockSpec((1, 1, d), lambda i, dest: (i // per_batch, 0, 0)),
                  pl.BlockSpec((1, d), lambda i, dest: (0, 0)),
                  pl.BlockSpec(memory_space=pl.ANY)],
        out_specs=pl.BlockSpec((tt, d), lambda i, dest: (i, 0)),
        scratch_shapes=[pltpu.VMEM((2, TOP_K, tt * SUBLANES, LANES), F32),
                        pltpu.SemaphoreType.DMA((2,))],
    )
    out = pl.pallas_call(
        _combine_kernel,
        grid_spec=grid_spec,
        out_shape=jax.ShapeDtypeStruct((n, d), F32),
        compiler_params=_cparams(("arbitrary",)),
        name="moe_combine",
    )(dest_flat, gate, x.reshape(n, d), g2, n3, y_rows)
    return out.reshape(b, s, d)


def _dispatch_plan(idx, rank, counts):
    n = idx.shape[1]
    tm = MOE_TM
    padded = (counts + tm - 1) // tm * tm
    pad_end = jnp.cumsum(padded)
    start = pad_end - padded
    experts = jnp.arange(N_EXPERTS, dtype=jnp.int32)
    dest = rank + jnp.sum(jnp.where(idx[..., None] == experts, start, 0), axis=-1)
    n_rows = -(-(n * TOP_K + N_EXPERTS * (tm - 1)) // tm) * tm
    n_blk = n_rows // tm
    blk_start = jnp.arange(n_blk, dtype=jnp.int32) * tm
    blk_exp = jnp.minimum(jnp.sum((pad_end[None, :] <= blk_start[:, None]).astype(jnp.int32), axis=1),
                          N_EXPERTS - 1)
    n_used = (pad_end[-1] // tm).astype(jnp.int32).reshape(1)
    tail = pad_end[-1] + experts * tm
    zero_rows = jnp.concatenate([jnp.where(padded > 0, pad_end - tm, -1),
                                 jnp.where(tail < n_rows, tail, -1)]).astype(jnp.int32)
    return dest.reshape(-1), blk_exp, n_used, zero_rows, n_rows


def _layer(layer, x, mod, norm_g, w_in, cmp_pe, cmp_w1, cmp_w2, grp_g, w_out, router_w, router_b,
           w_gu, b_gu, w_dn, b_dn, tabs, overlap):
    b, s, d = x.shape
    sh1, sc1, g1, sh2, sc2, g2 = (m.reshape(b, 1, d) for m in jnp.split(mod, 6, axis=-1))
    ng = norm_g.reshape(4, 1, d)

    qa, ka, va, qb, kvb, gb, qc, kc, vc = in_projection(x, sc1, sh1, ng[0], _pack_w_in(w_in), tabs)
    oa = moba_attention(qa, ka, va)
    nc = s // NSA_CMP_STRIDE
    kv16 = jnp.stack([kvb[:, :, :HEAD_DIM], kvb[:, :, HEAD_DIM:2 * HEAD_DIM]], axis=1)
    kv16 = kv16.reshape(b, 2, nc, NSA_CMP_STRIDE * HEAD_DIM)
    cmp_kv = nsa_compress(kv16, cmp_pe.reshape(2, 1, NSA_CMP_LEN * HEAD_DIM), cmp_w1, cmp_w2)
    ob = nsa_attention(qb, kvb, gb, cmp_kv, overlap)
    oc = sb_attention(qc, kc, vc)

    rw_hi, rw_lo = _split_bf16(router_w.T)
    rw3 = jnp.concatenate([rw_hi, rw_hi, rw_lo], axis=1)
    x1, hf, logits_t = out_projection(oa, ob, oc, x, grp_g.reshape(1, d), w_out.astype(BF16), g1,
                                      ng[1], ng[2], sc2, sh2, rw3, router_b.reshape(-1, 1))

    n = b * s
    idx, gate, rank, counts = moe_router(logits_t)
    dest, blk_exp, n_used, zero_rows, n_rows = _dispatch_plan(idx, rank, counts[:, 0].astype(jnp.int32))
    xs = moe_dispatch(dest, zero_rows, hf.reshape(n * SUBLANES, LANES), n_rows)
    y_rows = expert_ffn(layer, blk_exp, n_used, xs, w_gu, b_gu, w_dn, b_dn)
    gate = jnp.pad(gate.T, ((0, 0), (0, LANES - TOP_K)))
    return moe_combine(dest, gate, x1, g2, ng[3], y_rows)


def kernel(x, c, ada_w, ada_b, norm_g, w_in, nsa_cmp_pe, nsa_cmp_w1, nsa_cmp_w2, mix_out_g,
           w_out, router_w, router_b, exp_w_gu, exp_b_gu, exp_w_dn, exp_b_dn):
    s = x.shape[1]
    tabs = _rope_tables(s)
    overlap = _nsa_overlap(s)
    mod = ada_modulation(c, ada_w, ada_b)
    for l in range(ada_w.shape[0]):
        x = _layer(l, x, mod[l], norm_g[l], w_in[l], nsa_cmp_pe[l], nsa_cmp_w1[l], nsa_cmp_w2[l],
                   mix_out_g[l], w_out[l], router_w[l], router_b[l], exp_w_gu, exp_b_gu,
                   exp_w_dn, exp_b_dn, tabs, overlap)
    return x
```
